```python
import jax, jax.numpy as jnp
from jax import lax
import numpy as np

D_MODEL = 1024
BATCH = 8
SEQ = 8192
DEPTH = 4

N_META = 16
LRU_WIDTH = D_MODEL
LRU_HEADS = 8
LRU_HEAD_DIM = LRU_WIDTH // LRU_HEADS
LRU_CONV = 4
LRU_C = 8.0
CONV_WIDTH = D_MODEL // 2
CONV_GROUPS = 4
CONV_KERNEL = 31
MIX_WIDTH = LRU_WIDTH + CONV_WIDTH
IN_WIDTH = 2 * LRU_WIDTH + 2 * CONV_WIDTH
D_FF = 3 * D_MODEL
FFN_CONV = 3
EPS = 1e-6

kernel_name = 'hybrid_rglru_conformer_conv_trunk'


def rms_norm(x, g):
    xf = x.astype(jnp.float32)
    y = xf * lax.rsqrt(jnp.mean(xf * xf, axis=-1, keepdims=True) + EPS)
    return (y * g.astype(jnp.float32)).astype(x.dtype)


def group_layer_norm(x, g, b, groups):
    shp = x.shape
    xf = x.astype(jnp.float32).reshape(shp[:-1] + (groups, shp[-1] // groups))
    mu = jnp.mean(xf, axis=-1, keepdims=True)
    var = jnp.mean(jnp.square(xf - mu), axis=-1, keepdims=True)
    y = ((xf - mu) * lax.rsqrt(var + EPS)).reshape(shp)
    return (y * g.astype(jnp.float32) + b.astype(jnp.float32)).astype(x.dtype)


def causal_dwconv(x, w, b):
    K, C = w.shape
    y = lax.conv_general_dilated(
        x, w[:, None, :].astype(x.dtype), window_strides=(1,),
        padding=[(K - 1, 0)], dimension_numbers=('NWC', 'WIO', 'NWC'),
        feature_group_count=C)
    return y + b.astype(x.dtype)


def rg_lru(x, w_a, b_a, w_x, b_x, lam):
    B, S, W = x.shape
    xh = x.reshape(B, S, LRU_HEADS, LRU_HEAD_DIM)
    r = jax.nn.sigmoid(jnp.einsum('bshi,hij->bshj', xh, w_a).reshape(B, S, W) + b_a)
    i = jax.nn.sigmoid(jnp.einsum('bshi,hij->bshj', xh, w_x).reshape(B, S, W) + b_x)
    log_a = -LRU_C * r.astype(jnp.float32) * jax.nn.softplus(-lam.astype(jnp.float32))
    a = jnp.exp(log_a)
    mult = jnp.sqrt(-jnp.expm1(2.0 * log_a))
    u = mult * (i * x).astype(jnp.float32)

    def combine(left, right):
        a_l, b_l = left
        a_r, b_r = right
        return a_l * a_r, a_r * b_l + b_r

    _, h = lax.associative_scan(combine, (a, u), axis=1)
    return h.astype(x.dtype)


def hybrid_layer(h, g_pre_mix, w_in, lru_conv_w, lru_conv_b, lru_wa, lru_ba, lru_wx, lru_bx,
                 lru_lambda, conv_w, conv_b, conv_ln_g, conv_ln_b, g_out_lru, g_out_conv,
                 w_out, g_post_mix, g_pre_ffn, w_up, ffn_conv_w, ffn_conv_b, w_down, g_post_ffn):
    z = rms_norm(h, g_pre_mix)
    proj = jnp.einsum('bsd,de->bse', z, w_in)
    x_lru, g_lru, c_a, c_b = jnp.split(
        proj, [LRU_WIDTH, 2 * LRU_WIDTH, 2 * LRU_WIDTH + CONV_WIDTH], axis=-1)
    x_lru = causal_dwconv(x_lru, lru_conv_w, lru_conv_b)
    y_a = rg_lru(x_lru, lru_wa, lru_ba, lru_wx, lru_bx, lru_lambda) * jax.nn.gelu(g_lru)
    c = c_a * jax.nn.sigmoid(c_b)
    c = causal_dwconv(c, conv_w, conv_b)
    y_b = jax.nn.silu(group_layer_norm(c, conv_ln_g, conv_ln_b, CONV_GROUPS))
    y = jnp.concatenate([rms_norm(y_a, g_out_lru), rms_norm(y_b, g_out_conv)], axis=-1)
    h = h + rms_norm(jnp.einsum('bse,ed->bsd', y, w_out), g_post_mix)
    z = rms_norm(h, g_pre_ffn)
    u = causal_dwconv(jnp.einsum('bsd,df->bsf', z, w_up), ffn_conv_w, ffn_conv_b)
    gate, up = jnp.split(u, 2, axis=-1)
    f = jnp.einsum('bsf,fd->bsd', jax.nn.gelu(gate) * up, w_down)
    return h + rms_norm(f, g_post_ffn)


def _normal(k, shape, scale):
    return jax.random.normal(k, shape, jnp.float32) * scale


def _fwd_setup_inputs(seed: int = 0) -> dict:
    key = jax.random.key(seed)
    ks = jax.random.split(key, 32)
    L, D = DEPTH, D_MODEL

    def gain(k, n):
        return 1.0 + _normal(k, (L, n), 0.02)

    u = jax.random.uniform(ks[10], (L, LRU_WIDTH), jnp.float32, minval=0.9, maxval=0.999)
    a_base = u ** (1.0 / LRU_C)
    lam = jnp.log(a_base) - jnp.log1p(-a_base)
    return {
        'x': _normal(ks[0], (BATCH, SEQ, D), 1.0),
        'meta_tokens': _normal(ks[1], (N_META, D), 1.0),
        'g_pre_mix': gain(ks[2], D),
        'w_in': _normal(ks[3], (L, D, IN_WIDTH), D ** -0.5),
        'lru_conv_w': _normal(ks[4], (L, LRU_CONV, LRU_WIDTH), LRU_CONV ** -0.5),
        'lru_conv_b': _normal(ks[5], (L, LRU_WIDTH), 0.01),
        'lru_wa': _normal(ks[6], (L, LRU_HEADS, LRU_HEAD_DIM, LRU_HEAD_DIM), LRU_HEAD_DIM ** -0.5),
        'lru_ba': _normal(ks[7], (L, LRU_WIDTH), 0.01),
        'lru_wx': _normal(ks[8], (L, LRU_HEADS, LRU_HEAD_DIM, LRU_HEAD_DIM), LRU_HEAD_DIM ** -0.5),
        'lru_bx': _normal(ks[9], (L, LRU_WIDTH), 0.01),
        'lru_lambda': lam,
        'conv_w': _normal(ks[11], (L, CONV_KERNEL, CONV_WIDTH), CONV_KERNEL ** -0.5),
        'conv_b': _normal(ks[12], (L, CONV_WIDTH), 0.01),
        'conv_ln_g': gain(ks[13], CONV_WIDTH),
        'conv_ln_b': _normal(ks[14], (L, CONV_WIDTH), 0.01),
        'g_out_lru': gain(ks[15], LRU_WIDTH),
        'g_out_conv': gain(ks[16], CONV_WIDTH),
        'w_out': _normal(ks[17], (L, MIX_WIDTH, D), MIX_WIDTH ** -0.5),
        'g_post_mix': gain(ks[18], D),
        'g_pre_ffn': gain(ks[19], D),
        'w_up': _normal(ks[20], (L, D, 2 * D_FF), D ** -0.5),
        'ffn_conv_w': _normal(ks[21], (L, FFN_CONV, 2 * D_FF), FFN_CONV ** -0.5),
        'ffn_conv_b': _normal(ks[22], (L, 2 * D_FF), 0.01),
        'w_down': _normal(ks[23], (L, D_FF, D), D_FF ** -0.5),
        'g_post_ffn': gain(ks[24], D),
    }


def _fwd_reference(x, meta_tokens, g_pre_mix, w_in, lru_conv_w, lru_conv_b, lru_wa, lru_ba, lru_wx,
              lru_bx, lru_lambda, conv_w, conv_b, conv_ln_g, conv_ln_b, g_out_lru, g_out_conv,
              w_out, g_post_mix, g_pre_ffn, w_up, ffn_conv_w, ffn_conv_b, w_down, g_post_ffn):
    B = x.shape[0]
    meta = jnp.broadcast_to(meta_tokens[None].astype(x.dtype), (B, N_META, D_MODEL))
    h = jnp.concatenate([meta, x], axis=1)
    for l in range(DEPTH):
        h = hybrid_layer(
            h, g_pre_mix[l], w_in[l], lru_conv_w[l], lru_conv_b[l], lru_wa[l], lru_ba[l],
            lru_wx[l], lru_bx[l], lru_lambda[l], conv_w[l], conv_b[l], conv_ln_g[l],
            conv_ln_b[l], g_out_lru[l], g_out_conv[l], w_out[l], g_post_mix[l],
            g_pre_ffn[l], w_up[l], ffn_conv_w[l], ffn_conv_b[l], w_down[l], g_post_ffn[l])
    return h[:, N_META:, :]


import jax as _jax
import jax.numpy as _jnp

TWIN_FORMAT = 'train_step'
FWD_PARAMS = ['x', 'meta_tokens', 'g_pre_mix', 'w_in', 'lru_conv_w', 'lru_conv_b', 'lru_wa', 'lru_ba', 'lru_wx', 'lru_bx', 'lru_lambda', 'conv_w', 'conv_b', 'conv_ln_g', 'conv_ln_b', 'g_out_lru', 'g_out_conv', 'w_out', 'g_post_mix', 'g_pre_ffn', 'w_up', 'ffn_conv_w', 'ffn_conv_b', 'w_down', 'g_post_ffn']
TWIN_WEIGHTS = ['meta_tokens', 'g_pre_mix', 'w_in', 'lru_conv_w', 'lru_conv_b', 'lru_wa', 'lru_ba', 'lru_wx', 'lru_bx', 'lru_lambda', 'conv_w', 'conv_b', 'conv_ln_g', 'conv_ln_b', 'g_out_lru', 'g_out_conv', 'w_out', 'g_post_mix', 'g_pre_ffn', 'w_up', 'ffn_conv_w', 'ffn_conv_b', 'w_down', 'g_post_ffn']
TWIN_DIFF_INPUT = 'x'
TWIN_INPUTS = ['x', 'meta_tokens', 'g_pre_mix', 'w_in', 'lru_conv_w', 'lru_conv_b', 'lru_wa', 'lru_ba', 'lru_wx', 'lru_bx', 'lru_lambda', 'conv_w', 'conv_b', 'conv_ln_g', 'conv_ln_b', 'g_out_lru', 'g_out_conv', 'w_out', 'g_post_mix', 'g_pre_ffn', 'w_up', 'ffn_conv_w', 'ffn_conv_b', 'w_down', 'g_post_ffn', 'loss_target', 'm_meta_tokens', 'm_g_pre_mix', 'm_w_in', 'm_lru_conv_w', 'm_lru_conv_b', 'm_lru_wa', 'm_lru_ba', 'm_lru_wx', 'm_lru_bx', 'm_lru_lambda', 'm_conv_w', 'm_conv_b', 'm_conv_ln_g', 'm_conv_ln_b', 'm_g_out_lru', 'm_g_out_conv', 'm_w_out', 'm_g_post_mix', 'm_g_pre_ffn', 'm_w_up', 'm_ffn_conv_w', 'm_ffn_conv_b', 'm_w_down', 'm_g_post_ffn', 'v_meta_tokens', 'v_g_pre_mix', 'v_w_in', 'v_lru_conv_w', 'v_lru_conv_b', 'v_lru_wa', 'v_lru_ba', 'v_lru_wx', 'v_lru_bx', 'v_lru_lambda', 'v_conv_w', 'v_conv_b', 'v_conv_ln_g', 'v_conv_ln_b', 'v_g_out_lru', 'v_g_out_conv', 'v_w_out', 'v_g_post_mix', 'v_g_pre_ffn', 'v_w_up', 'v_ffn_conv_w', 'v_ffn_conv_b', 'v_w_down', 'v_g_post_ffn']
TWIN_OUTPUTS = ['loss', 'grad_x', 'grad_meta_tokens', 'grad_g_pre_mix', 'grad_w_in', 'grad_lru_conv_w', 'grad_lru_conv_b', 'grad_lru_wa', 'grad_lru_ba', 'grad_lru_wx', 'grad_lru_bx', 'grad_lru_lambda', 'grad_conv_w', 'grad_conv_b', 'grad_conv_ln_g', 'grad_conv_ln_b', 'grad_g_out_lru', 'grad_g_out_conv', 'grad_w_out', 'grad_g_post_mix', 'grad_g_pre_ffn', 'grad_w_up', 'grad_ffn_conv_w', 'grad_ffn_conv_b', 'grad_w_down', 'grad_g_post_ffn', 'delta_meta_tokens', 'delta_g_pre_mix', 'delta_w_in', 'delta_lru_conv_w', 'delta_lru_conv_b', 'delta_lru_wa', 'delta_lru_ba', 'delta_lru_wx', 'delta_lru_bx', 'delta_lru_lambda', 'delta_conv_w', 'delta_conv_b', 'delta_conv_ln_g', 'delta_conv_ln_b', 'delta_g_out_lru', 'delta_g_out_conv', 'delta_w_out', 'delta_g_post_mix', 'delta_g_pre_ffn', 'delta_w_up', 'delta_ffn_conv_w', 'delta_ffn_conv_b', 'delta_w_down', 'delta_g_post_ffn', 'new_m_meta_tokens', 'new_m_g_pre_mix', 'new_m_w_in', 'new_m_lru_conv_w', 'new_m_lru_conv_b', 'new_m_lru_wa', 'new_m_lru_ba', 'new_m_lru_wx', 'new_m_lru_bx', 'new_m_lru_lambda', 'new_m_conv_w', 'new_m_conv_b', 'new_m_conv_ln_g', 'new_m_conv_ln_b', 'new_m_g_out_lru', 'new_m_g_out_conv', 'new_m_w_out', 'new_m_g_post_mix', 'new_m_g_pre_ffn', 'new_m_w_up', 'new_m_ffn_conv_w', 'new_m_ffn_conv_b', 'new_m_w_down', 'new_m_g_post_ffn', 'new_v_meta_tokens', 'new_v_g_pre_mix', 'new_v_w_in', 'new_v_lru_conv_w', 'new_v_lru_conv_b', 'new_v_lru_wa', 'new_v_lru_ba', 'new_v_lru_wx', 'new_v_lru_bx', 'new_v_lru_lambda', 'new_v_conv_w', 'new_v_conv_b', 'new_v_conv_ln_g', 'new_v_conv_ln_b', 'new_v_g_out_lru', 'new_v_g_out_conv', 'new_v_w_out', 'new_v_g_post_mix', 'new_v_g_pre_ffn', 'new_v_w_up', 'new_v_ffn_conv_w', 'new_v_ffn_conv_b', 'new_v_w_down', 'new_v_g_post_ffn']
TWIN_LEAF_KINDS = {'loss': 'loss', 'grad_x': 'grad_x', 'grad_meta_tokens': 'grad_w', 'grad_g_pre_mix': 'grad_w', 'grad_w_in': 'grad_w', 'grad_lru_conv_w': 'grad_w', 'grad_lru_conv_b': 'grad_w', 'grad_lru_wa': 'grad_w', 'grad_lru_ba': 'grad_w', 'grad_lru_wx': 'grad_w', 'grad_lru_bx': 'grad_w', 'grad_lru_lambda': 'grad_w', 'grad_conv_w': 'grad_w', 'grad_conv_b': 'grad_w', 'grad_conv_ln_g': 'grad_w', 'grad_conv_ln_b': 'grad_w', 'grad_g_out_lru': 'grad_w', 'grad_g_out_conv': 'grad_w', 'grad_w_out': 'grad_w', 'grad_g_post_mix': 'grad_w', 'grad_g_pre_ffn': 'grad_w', 'grad_w_up': 'grad_w', 'grad_ffn_conv_w': 'grad_w', 'grad_ffn_conv_b': 'grad_w', 'grad_w_down': 'grad_w', 'grad_g_post_ffn': 'grad_w', 'delta_meta_tokens': 'delta_w', 'delta_g_pre_mix': 'delta_w', 'delta_w_in': 'delta_w', 'delta_lru_conv_w': 'delta_w', 'delta_lru_conv_b': 'delta_w', 'delta_lru_wa': 'delta_w', 'delta_lru_ba': 'delta_w', 'delta_lru_wx': 'delta_w', 'delta_lru_bx': 'delta_w', 'delta_lru_lambda': 'delta_w', 'delta_conv_w': 'delta_w', 'delta_conv_b': 'delta_w', 'delta_conv_ln_g': 'delta_w', 'delta_conv_ln_b': 'delta_w', 'delta_g_out_lru': 'delta_w', 'delta_g_out_conv': 'delta_w', 'delta_w_out': 'delta_w', 'delta_g_post_mix': 'delta_w', 'delta_g_pre_ffn': 'delta_w', 'delta_w_up': 'delta_w', 'delta_ffn_conv_w': 'delta_w', 'delta_ffn_conv_b': 'delta_w', 'delta_w_down': 'delta_w', 'delta_g_post_ffn': 'delta_w', 'new_m_meta_tokens': 'new_m', 'new_m_g_pre_mix': 'new_m', 'new_m_w_in': 'new_m', 'new_m_lru_conv_w': 'new_m', 'new_m_lru_conv_b': 'new_m', 'new_m_lru_wa': 'new_m', 'new_m_lru_ba': 'new_m', 'new_m_lru_wx': 'new_m', 'new_m_lru_bx': 'new_m', 'new_m_lru_lambda': 'new_m', 'new_m_conv_w': 'new_m', 'new_m_conv_b': 'new_m', 'new_m_conv_ln_g': 'new_m', 'new_m_conv_ln_b': 'new_m', 'new_m_g_out_lru': 'new_m', 'new_m_g_out_conv': 'new_m', 'new_m_w_out': 'new_m', 'new_m_g_post_mix': 'new_m', 'new_m_g_pre_ffn': 'new_m', 'new_m_w_up': 'new_m', 'new_m_ffn_conv_w': 'new_m', 'new_m_ffn_conv_b': 'new_m', 'new_m_w_down': 'new_m', 'new_m_g_post_ffn': 'new_m', 'new_v_meta_tokens': 'new_v', 'new_v_g_pre_mix': 'new_v', 'new_v_w_in': 'new_v', 'new_v_lru_conv_w': 'new_v', 'new_v_lru_conv_b': 'new_v', 'new_v_lru_wa': 'new_v', 'new_v_lru_ba': 'new_v', 'new_v_lru_wx': 'new_v', 'new_v_lru_bx': 'new_v', 'new_v_lru_lambda': 'new_v', 'new_v_conv_w': 'new_v', 'new_v_conv_b': 'new_v', 'new_v_conv_ln_g': 'new_v', 'new_v_conv_ln_b': 'new_v', 'new_v_g_out_lru': 'new_v', 'new_v_g_out_conv': 'new_v', 'new_v_w_out': 'new_v', 'new_v_g_post_mix': 'new_v', 'new_v_g_pre_ffn': 'new_v', 'new_v_w_up': 'new_v', 'new_v_ffn_conv_w': 'new_v', 'new_v_ffn_conv_b': 'new_v', 'new_v_w_down': 'new_v', 'new_v_g_post_ffn': 'new_v'}


def _forward(args):
    return _fwd_reference(*[args[k] for k in FWD_PARAMS])


def _output_shape():
    out = _jax.eval_shape(lambda: _forward(_fwd_setup_inputs(0)))
    return out.shape, out.dtype

N_MICROBATCH = 1
ADAM_LR = 0.001
ADAM_B1 = 0.9
ADAM_B2 = 0.999
ADAM_EPS = 1e-08
ADAM_WD = 0.01
ADAM_STEP = 10
PER_EXAMPLE_BATCH_AXIS = {'x': 0, 'loss_target': 0}
SHARED_INPUTS = []
_WEIGHT_DTYPES = {'meta_tokens': _jnp.float32, 'g_pre_mix': _jnp.float32, 'w_in': _jnp.float32, 'lru_conv_w': _jnp.float32, 'lru_conv_b': _jnp.float32, 'lru_wa': _jnp.float32, 'lru_ba': _jnp.float32, 'lru_wx': _jnp.float32, 'lru_bx': _jnp.float32, 'lru_lambda': _jnp.float32, 'conv_w': _jnp.float32, 'conv_b': _jnp.float32, 'conv_ln_g': _jnp.float32, 'conv_ln_b': _jnp.float32, 'g_out_lru': _jnp.float32, 'g_out_conv': _jnp.float32, 'w_out': _jnp.float32, 'g_post_mix': _jnp.float32, 'g_pre_ffn': _jnp.float32, 'w_up': _jnp.float32, 'ffn_conv_w': _jnp.float32, 'ffn_conv_b': _jnp.float32, 'w_down': _jnp.float32, 'g_post_ffn': _jnp.float32}
MOMENT_SCALE = {'meta_tokens': 4.628626e-01, 'g_pre_mix': 8.768077e+00, 'w_in': 4.991864e+00, 'lru_conv_w': 1.122728e+01, 'lru_conv_b': 1.453275e+02, 'lru_wa': 3.841702e+00, 'lru_ba': 3.201147e+00, 'lru_wx': 7.488360e+00, 'lru_bx': 4.637013e+00, 'lru_lambda': 6.186807e+00, 'conv_w': 4.660281e+00, 'conv_b': 7.816891e+01, 'conv_ln_g': 3.028682e+01, 'conv_ln_b': 4.439602e+01, 'g_out_lru': 1.442977e+01, 'g_out_conv': 1.730104e+01, 'w_out': 1.654200e+01, 'g_post_mix': 6.651489e+01, 'g_pre_ffn': 4.265834e+00, 'w_up': 1.635995e+00, 'ffn_conv_w': 2.145956e+00, 'ffn_conv_b': 1.267526e+01, 'w_down': 4.125243e+00, 'g_post_ffn': 6.413363e+01}


def _to_microbatches(a, axis):
    t = _jnp.moveaxis(a, axis, 0)
    t = t.reshape((N_MICROBATCH, t.shape[0] // N_MICROBATCH) + t.shape[1:])
    return _jnp.moveaxis(t, 1, axis + 1)


def setup_inputs(seed: int = 0) -> dict:
    inp = _fwd_setup_inputs(seed)
    key = _jax.random.fold_in(_jax.random.key(seed), 7919)
    shape, _ = _output_shape()
    out = dict(inp)
    out["loss_target"] = _jax.random.normal(_jax.random.fold_in(key, 0), shape, _jnp.float32)
    for i, name in enumerate(TWIN_WEIGHTS):
        w = inp[name].astype(_jnp.float32)
        if MOMENT_SCALE is None:
            s = _jnp.sqrt(_jnp.mean(_jnp.square(w)) + 1e-30)
        else:
            s = MOMENT_SCALE[name]
        km, kv = _jax.random.split(_jax.random.fold_in(key, i + 1))
        out[name] = w
        out["m_" + name] = s * _jax.random.normal(km, w.shape, _jnp.float32)
        out["v_" + name] = (s * s) * _jax.random.uniform(kv, w.shape, _jnp.float32, 0.5, 1.5)
    if N_MICROBATCH > 1:
        for name, axis in PER_EXAMPLE_BATCH_AXIS.items():
            out[name] = _to_microbatches(out[name], axis)
    return {'x': out['x'], 'meta_tokens': out['meta_tokens'], 'g_pre_mix': out['g_pre_mix'], 'w_in': out['w_in'], 'lru_conv_w': out['lru_conv_w'], 'lru_conv_b': out['lru_conv_b'], 'lru_wa': out['lru_wa'], 'lru_ba': out['lru_ba'], 'lru_wx': out['lru_wx'], 'lru_bx': out['lru_bx'], 'lru_lambda': out['lru_lambda'], 'conv_w': out['conv_w'], 'conv_b': out['conv_b'], 'conv_ln_g': out['conv_ln_g'], 'conv_ln_b': out['conv_ln_b'], 'g_out_lru': out['g_out_lru'], 'g_out_conv': out['g_out_conv'], 'w_out': out['w_out'], 'g_post_mix': out['g_post_mix'], 'g_pre_ffn': out['g_pre_ffn'], 'w_up': out['w_up'], 'ffn_conv_w': out['ffn_conv_w'], 'ffn_conv_b': out['ffn_conv_b'], 'w_down': out['w_down'], 'g_post_ffn': out['g_post_ffn'], 'loss_target': out['loss_target'], 'm_meta_tokens': out['m_meta_tokens'], 'm_g_pre_mix': out['m_g_pre_mix'], 'm_w_in': out['m_w_in'], 'm_lru_conv_w': out['m_lru_conv_w'], 'm_lru_conv_b': out['m_lru_conv_b'], 'm_lru_wa': out['m_lru_wa'], 'm_lru_ba': out['m_lru_ba'], 'm_lru_wx': out['m_lru_wx'], 'm_lru_bx': out['m_lru_bx'], 'm_lru_lambda': out['m_lru_lambda'], 'm_conv_w': out['m_conv_w'], 'm_conv_b': out['m_conv_b'], 'm_conv_ln_g': out['m_conv_ln_g'], 'm_conv_ln_b': out['m_conv_ln_b'], 'm_g_out_lru': out['m_g_out_lru'], 'm_g_out_conv': out['m_g_out_conv'], 'm_w_out': out['m_w_out'], 'm_g_post_mix': out['m_g_post_mix'], 'm_g_pre_ffn': out['m_g_pre_ffn'], 'm_w_up': out['m_w_up'], 'm_ffn_conv_w': out['m_ffn_conv_w'], 'm_ffn_conv_b': out['m_ffn_conv_b'], 'm_w_down': out['m_w_down'], 'm_g_post_ffn': out['m_g_post_ffn'], 'v_meta_tokens': out['v_meta_tokens'], 'v_g_pre_mix': out['v_g_pre_mix'], 'v_w_in': out['v_w_in'], 'v_lru_conv_w': out['v_lru_conv_w'], 'v_lru_conv_b': out['v_lru_conv_b'], 'v_lru_wa': out['v_lru_wa'], 'v_lru_ba': out['v_lru_ba'], 'v_lru_wx': out['v_lru_wx'], 'v_lru_bx': out['v_lru_bx'], 'v_lru_lambda': out['v_lru_lambda'], 'v_conv_w': out['v_conv_w'], 'v_conv_b': out['v_conv_b'], 'v_conv_ln_g': out['v_conv_ln_g'], 'v_conv_ln_b': out['v_conv_ln_b'], 'v_g_out_lru': out['v_g_out_lru'], 'v_g_out_conv': out['v_g_out_conv'], 'v_w_out': out['v_w_out'], 'v_g_post_mix': out['v_g_post_mix'], 'v_g_pre_ffn': out['v_g_pre_ffn'], 'v_w_up': out['v_w_up'], 'v_ffn_conv_w': out['v_ffn_conv_w'], 'v_ffn_conv_b': out['v_ffn_conv_b'], 'v_w_down': out['v_w_down'], 'v_g_post_ffn': out['v_g_post_ffn']}


def _loss(weights, diff, rest, loss_target):
    with _jax.named_scope("forward"):
        args = {**rest, TWIN_DIFF_INPUT: diff, **{k: w.astype(_WEIGHT_DTYPES[k]) for k, w in weights.items()}}
        y = _forward(args)
    with _jax.named_scope("loss_head"):
        err = _jnp.square(y.astype(_jnp.float32) - loss_target)
        return 0.5 * _jnp.sum(_jnp.mean(err, axis=-1)) if err.ndim else 0.5 * err


def _adamw(w, g, m, v):
    m = ADAM_B1 * m + (1.0 - ADAM_B1) * g
    v = ADAM_B2 * v + (1.0 - ADAM_B2) * _jnp.square(g)
    m_hat = m / (1.0 - ADAM_B1 ** ADAM_STEP)
    v_hat = v / (1.0 - ADAM_B2 ** ADAM_STEP)
    delta = -ADAM_LR * (m_hat / (_jnp.sqrt(v_hat) + ADAM_EPS) + ADAM_WD * w)
    return delta, m, v


def reference(x, meta_tokens, g_pre_mix, w_in, lru_conv_w, lru_conv_b, lru_wa, lru_ba, lru_wx, lru_bx, lru_lambda, conv_w, conv_b, conv_ln_g, conv_ln_b, g_out_lru, g_out_conv, w_out, g_post_mix, g_pre_ffn, w_up, ffn_conv_w, ffn_conv_b, w_down, g_post_ffn, loss_target, m_meta_tokens, m_g_pre_mix, m_w_in, m_lru_conv_w, m_lru_conv_b, m_lru_wa, m_lru_ba, m_lru_wx, m_lru_bx, m_lru_lambda, m_conv_w, m_conv_b, m_conv_ln_g, m_conv_ln_b, m_g_out_lru, m_g_out_conv, m_w_out, m_g_post_mix, m_g_pre_ffn, m_w_up, m_ffn_conv_w, m_ffn_conv_b, m_w_down, m_g_post_ffn, v_meta_tokens, v_g_pre_mix, v_w_in, v_lru_conv_w, v_lru_conv_b, v_lru_wa, v_lru_ba, v_lru_wx, v_lru_bx, v_lru_lambda, v_conv_w, v_conv_b, v_conv_ln_g, v_conv_ln_b, v_g_out_lru, v_g_out_conv, v_w_out, v_g_post_mix, v_g_pre_ffn, v_w_up, v_ffn_conv_w, v_ffn_conv_b, v_w_down, v_g_post_ffn):
    given = dict(x=x, meta_tokens=meta_tokens, g_pre_mix=g_pre_mix, w_in=w_in, lru_conv_w=lru_conv_w, lru_conv_b=lru_conv_b, lru_wa=lru_wa, lru_ba=lru_ba, lru_wx=lru_wx, lru_bx=lru_bx, lru_lambda=lru_lambda, conv_w=conv_w, conv_b=conv_b, conv_ln_g=conv_ln_g, conv_ln_b=conv_ln_b, g_out_lru=g_out_lru, g_out_conv=g_out_conv, w_out=w_out, g_post_mix=g_post_mix, g_pre_ffn=g_pre_ffn, w_up=w_up, ffn_conv_w=ffn_conv_w, ffn_conv_b=ffn_conv_b, w_down=w_down, g_post_ffn=g_post_ffn, loss_target=loss_target, m_meta_tokens=m_meta_tokens, m_g_pre_mix=m_g_pre_mix, m_w_in=m_w_in, m_lru_conv_w=m_lru_conv_w, m_lru_conv_b=m_lru_conv_b, m_lru_wa=m_lru_wa, m_lru_ba=m_lru_ba, m_lru_wx=m_lru_wx, m_lru_bx=m_lru_bx, m_lru_lambda=m_lru_lambda, m_conv_w=m_conv_w, m_conv_b=m_conv_b, m_conv_ln_g=m_conv_ln_g, m_conv_ln_b=m_conv_ln_b, m_g_out_lru=m_g_out_lru, m_g_out_conv=m_g_out_conv, m_w_out=m_w_out, m_g_post_mix=m_g_post_mix, m_g_pre_ffn=m_g_pre_ffn, m_w_up=m_w_up, m_ffn_conv_w=m_ffn_conv_w, m_ffn_conv_b=m_ffn_conv_b, m_w_down=m_w_down, m_g_post_ffn=m_g_post_ffn, v_meta_tokens=v_meta_tokens, v_g_pre_mix=v_g_pre_mix, v_w_in=v_w_in, v_lru_conv_w=v_lru_conv_w, v_lru_conv_b=v_lru_conv_b, v_lru_wa=v_lru_wa, v_lru_ba=v_lru_ba, v_lru_wx=v_lru_wx, v_lru_bx=v_lru_bx, v_lru_lambda=v_lru_lambda, v_conv_w=v_conv_w, v_conv_b=v_conv_b, v_conv_ln_g=v_conv_ln_g, v_conv_ln_b=v_conv_ln_b, v_g_out_lru=v_g_out_lru, v_g_out_conv=v_g_out_conv, v_w_out=v_w_out, v_g_post_mix=v_g_post_mix, v_g_pre_ffn=v_g_pre_ffn, v_w_up=v_w_up, v_ffn_conv_w=v_ffn_conv_w, v_ffn_conv_b=v_ffn_conv_b, v_w_down=v_w_down, v_g_post_ffn=v_g_post_ffn)
    weights = {n: given[n] for n in TWIN_WEIGHTS}
    shared = {n: given[n] for n in SHARED_INPUTS}
    per_example = {n: given[n] for n in ['x']}
    grad_fn = _jax.value_and_grad(_loss, argnums=(0, 1))

    def one_microbatch(ex, loss_target):
        ex = dict(ex)
        diff = ex.pop(TWIN_DIFF_INPUT)
        return grad_fn(weights, diff, {**shared, **ex}, loss_target)

    if N_MICROBATCH == 1:
        loss, (grad_w, grad_x) = one_microbatch(per_example, given["loss_target"])
    else:
        def body(carry, xs):
            loss_sum, grad_sum = carry
            l_k, (gw_k, gx_k) = one_microbatch(xs[0], xs[1])
            with _jax.named_scope("update"):
                return (loss_sum + l_k, _jax.tree.map(_jnp.add, grad_sum, gw_k)), gx_k

        init = (_jnp.zeros((), _jnp.float32), _jax.tree.map(_jnp.zeros_like, weights))
        (loss, grad_w), grad_x = _jax.lax.scan(body, init, (per_example, given["loss_target"]))
    with _jax.named_scope("update"):
        delta_w, new_m, new_v = {}, {}, {}
        for n in TWIN_WEIGHTS:
            delta_w[n], new_m[n], new_v[n] = _adamw(weights[n], grad_w[n], given["m_" + n], given["v_" + n])
    return (loss, grad_x, *[grad_w[n] for n in TWIN_WEIGHTS], *[delta_w[n] for n in TWIN_WEIGHTS],
            *[new_m[n] for n in TWIN_WEIGHTS], *[new_v[n] for n in TWIN_WEIGHTS])
```

```python
import functools

import jax
import jax.numpy as jnp
from jax import lax
from jax.experimental import pallas as pl
from jax.experimental.pallas import tpu as pltpu

F32 = jnp.float32
BF16 = jnp.bfloat16
EPS = 1e-6
N_META = 16
LRU_C = 8.0
CONV_K = 31
LRU_K = 4
FFN_K = 3
LANES = 128
VMEM_LIMIT = 56 * 1024 * 1024
ADAM_LR, ADAM_B1, ADAM_B2, ADAM_EPS, ADAM_WD, ADAM_STEP = 0.001, 0.9, 0.999, 1e-08, 0.01, 10
MESH_T = pl.DeviceIdType.MESH
NT_DIMS = (((1,), (1,)), ((), ()))
TN_DIMS = (((0,), (0,)), ((), ()))

REP_SMALL = ['g_pre_mix', 'lru_conv_b', 'lru_wa', 'lru_ba', 'lru_wx', 'lru_bx', 'lru_lambda', 'conv_b', 'conv_ln_g',
             'conv_ln_b', 'g_out_lru', 'g_out_conv', 'g_post_mix', 'g_pre_ffn', 'ffn_conv_b', 'g_post_ffn']
SH_SMALL = ['meta_tokens', 'lru_conv_w', 'conv_w', 'ffn_conv_w']
BIG = ['w_in', 'w_out', 'w_up', 'w_down']
BIG_AXIS = {'w_in': 2, 'w_out': 1, 'w_up': 2, 'w_down': 1}
WEIGHTS = ['meta_tokens', 'g_pre_mix', 'w_in', 'lru_conv_w', 'lru_conv_b', 'lru_wa', 'lru_ba', 'lru_wx', 'lru_bx',
           'lru_lambda', 'conv_w', 'conv_b', 'conv_ln_g', 'conv_ln_b', 'g_out_lru', 'g_out_conv', 'w_out',
           'g_post_mix', 'g_pre_ffn', 'w_up', 'ffn_conv_w', 'ffn_conv_b', 'w_down', 'g_post_ffn']


def _pcall(body, **kw):
    return pl.pallas_call(body, **kw)


def _params(n_grid=1):
    return pltpu.CompilerParams(dimension_semantics=("arbitrary",) * n_grid, vmem_limit_bytes=VMEM_LIMIT)


def _tiles(t):
    if t % 432 == 0:
        return 432, 144
    assert t % 48 == 0
    return 48, 48


def _row(tm, n):
    return pl.BlockSpec((tm, n), lambda i: (i, 0))


def _rrow(tm, n, nt):
    return pl.BlockSpec((tm, n), lambda i: (nt - 1 - i, 0))


def _halo(hb, n, tm, nt):
    return pl.BlockSpec((hb, n), lambda i: (jnp.maximum((nt - 1 - i) * (tm // hb) - 1, 0), 0))


def _const(shape):
    nd = len(shape)
    return pl.BlockSpec(shape, lambda *_: (0,) * nd, pipeline_mode=pl.Buffered(1))


def _const_out(shape):
    nd = len(shape)
    return pl.BlockSpec(shape, lambda *_: (0,) * nd)


def _sigmoid(x):
    return 1.0 / (1.0 + jnp.exp(-x))


def _gelu(x):
    return 0.5 * x * (1.0 + jnp.tanh(0.7978845608028654 * (x + 0.044715 * (x * x * x))))


def _gelu_and_grad(x):
    k = 0.7978845608028654
    x2 = x * x
    th = jnp.tanh(k * (x + 0.044715 * (x2 * x)))
    return 0.5 * x * (1.0 + th), 0.5 * (1.0 + th) + 0.5 * x * (1.0 - th * th) * (k * (1.0 + 0.134145 * x2))


def _expm1(x):
    return jnp.where(jnp.abs(x) < 1e-2, x * (1.0 + x * (0.5 + x * (1.0 / 6.0 + x * (1.0 / 24.0)))), jnp.exp(x) - 1.0)


def _softplus(x):
    e = jnp.exp(-jnp.abs(x))
    return jnp.maximum(x, 0.0) + jnp.where(e < 1e-4, e * (1.0 - 0.5 * e), jnp.log(1.0 + e))


def _lru_gates(pa, px, sp):
    r = _sigmoid(pa)
    ig = _sigmoid(px)
    la = (-LRU_C * r) * sp
    return r, ig, jnp.exp(la), jnp.sqrt(-_expm1(2.0 * la))


def _rms(x):
    return lax.rsqrt(jnp.mean(x * x, axis=-1, keepdims=True) + EPS)


def _rms_bwd(x, g, dy):
    r = _rms(x)
    xr = x * r
    dyg = dy * g
    return r * (dyg - xr * jnp.mean(dyg * xr, axis=-1, keepdims=True)), xr


def _col_chunk(n):
    return 1536 if n % 1536 == 0 else 1024


def _rms_matmul(h, g, w, name):
    t, d = h.shape
    n = w.shape[1]
    tm, _ = _tiles(t)
    cn = _col_chunk(n)

    def body(h_ref, g_ref, w_ref, p_ref, zb_ref):
        x = h_ref[...]
        zb = ((x * _rms(x)) * g_ref[...]).astype(BF16)
        zb_ref[...] = zb
        for c in range(n // cn):
            p_ref[:, c * cn:(c + 1) * cn] = jnp.dot(zb, w_ref[:, c * cn:(c + 1) * cn], preferred_element_type=F32)

    return _pcall(body, name=name, grid=(t // tm,),
                  in_specs=[_row(tm, d), _const((1, d)), _const((d, n))],
                  out_specs=[_row(tm, n), _row(tm, d)],
                  out_shape=[jax.ShapeDtypeStruct((t, n), F32), jax.ShapeDtypeStruct((t, d), BF16)],
                  compiler_params=_params())(h, g, w)


def _matmul_rms_res(a, w, h, g, name):
    t, k = a.shape
    d = w.shape[1]
    tm, _ = _tiles(t)

    def body(a_ref, w_ref, h_ref, g_ref, o_ref, hn_ref):
        o = jnp.dot(a_ref[...], w_ref[...], preferred_element_type=F32)
        o_ref[...] = o
        hn_ref[...] = h_ref[...] + (o * _rms(o)) * g_ref[...]

    return _pcall(body, name=name, grid=(t // tm,),
                  in_specs=[_row(tm, k), _const((k, d)), _row(tm, d), _const((1, d))],
                  out_specs=[_row(tm, d), _row(tm, d)],
                  out_shape=[jax.ShapeDtypeStruct((t, d), F32), jax.ShapeDtypeStruct((t, d), F32)],
                  compiler_params=_params())(a, w, h, g)


def _rmsbwd_matmul_nt(x, g, dy, w, name):
    t, d = x.shape
    n = w.shape[0]
    tm, _ = _tiles(t)
    cn = _col_chunk(n)

    def body(x_ref, g_ref, dy_ref, w_ref, da_ref, dxb_ref, dg_ref):
        @pl.when(pl.program_id(0) == 0)
        def _():
            dg_ref[...] = jnp.zeros((1, d), F32)

        dy = dy_ref[...]
        dx, xr = _rms_bwd(x_ref[...], g_ref[...], dy)
        dg_ref[...] += jnp.sum(dy * xr, axis=0, keepdims=True)
        dxb = dx.astype(BF16)
        dxb_ref[...] = dxb
        for c in range(n // cn):
            da_ref[:, c * cn:(c + 1) * cn] = lax.dot_general(dxb, w_ref[c * cn:(c + 1) * cn, :], NT_DIMS,
                                                             preferred_element_type=F32)

    return _pcall(body, name=name, grid=(t // tm,),
                  in_specs=[_row(tm, d), _const((1, d)), _row(tm, d), _const((n, d))],
                  out_specs=[_row(tm, n), _row(tm, d), _const_out((1, d))],
                  out_shape=[jax.ShapeDtypeStruct((t, n), F32), jax.ShapeDtypeStruct((t, d), BF16),
                             jax.ShapeDtypeStruct((1, d), F32)],
                  compiler_params=_params())(x, g, dy, w)


def _matmul_nt_rmsbwd_res(dp, w, h, g, dh, name):
    t, n = dp.shape
    d = w.shape[0]
    tm, _ = _tiles(t)

    def body(dp_ref, w_ref, h_ref, g_ref, dh_ref, out_ref, dg_ref):
        @pl.when(pl.program_id(0) == 0)
        def _():
            dg_ref[...] = jnp.zeros((1, d), F32)

        dz = lax.dot_general(dp_ref[...], w_ref[...], NT_DIMS, preferred_element_type=F32)
        dx, xr = _rms_bwd(h_ref[...], g_ref[...], dz)
        dg_ref[...] += jnp.sum(dz * xr, axis=0, keepdims=True)
        out_ref[...] = dh_ref[...] + dx

    return _pcall(body, name=name, grid=(t // tm,),
                  in_specs=[_row(tm, n), _const((d, n)), _row(tm, d), _const((1, d)), _row(tm, d)],
                  out_specs=[_row(tm, d), _const_out((1, d))],
                  out_shape=[jax.ShapeDtypeStruct((t, d), F32), jax.ShapeDtypeStruct((1, d), F32)],
                  compiler_params=_params())(dp, w, h, g, dh)


def _matmul_tn(a, b, name):
    t, k = a.shape
    n = b.shape[1]
    tm, _ = _tiles(t)
    bn = min(n, (1536 * 1024) // k)
    assert n % bn == 0 and bn % LANES == 0

    def body(a_ref, b_ref, o_ref):
        @pl.when(pl.program_id(1) == 0)
        def _():
            o_ref[...] = jnp.zeros((k, bn), F32)

        o_ref[...] += lax.dot_general(a_ref[...], b_ref[...], TN_DIMS, preferred_element_type=F32)

    return _pcall(body, name=name, grid=(n // bn, t // tm),
                  in_specs=[pl.BlockSpec((tm, k), lambda j, i: (i, 0)), pl.BlockSpec((tm, bn), lambda j, i: (i, j))],
                  out_specs=pl.BlockSpec((k, bn), lambda j, i: (0, j)),
                  out_shape=jax.ShapeDtypeStruct((k, n), F32),
                  compiler_params=_params(2))(a, b)


MIX_PARAMS = ['lru_conv_w', 'lru_conv_b', 'lru_wa', 'lru_ba', 'lru_wx', 'lru_bx', 'lru_lambda', 'conv_w', 'conv_b',
              'conv_ln_g', 'conv_ln_b', 'g_out_lru', 'g_out_conv']


def _head_gates(xcb, wa_ref, wx_ref, pa_s, px_s, ba, bx, heads):
    for hd in range(heads):
        sl = slice(LANES * hd, LANES * (hd + 1))
        pa_s[:, sl] = jnp.dot(xcb[:, sl], wa_ref[hd], preferred_element_type=F32) + ba[:, sl]
        px_s[:, sl] = jnp.dot(xcb[:, sl], wx_ref[hd], preferred_element_type=F32) + bx[:, sl]


def _lru_conv_chunk(exta, cw4_ref, cb4, r0):
    win = exta[pl.ds(r0, 16), :]
    acc = cw4_ref[3:4, :] * win[8:16]
    for k in range(LRU_K - 1):
        acc = acc + cw4_ref[k:k + 1, :] * pltpu.roll(win, LRU_K - 1 - k, 0)[8:16]
    return acc + cb4


def _mix_fwd(proj, p, name):
    t = proj.shape[0]
    w = p['lru_conv_b'].shape[1]
    cw = p['conv_b'].shape[1]
    heads = p['lru_wa'].shape[0]
    _, tm = _tiles(t)
    nch = tm // 8

    def body(proj_ref, cw4_ref, cb4_ref, wa_ref, ba_ref, wx_ref, bx_ref, lam_ref, cw31_ref, cb31_ref, lng_ref,
             lnb_ref, ga_ref, gb_ref, y_ref, hs_ref, c1_ref, exta, xc_s, pa_s, px_s, extb, nbuf, hcar):
        @pl.when(pl.program_id(0) == 0)
        def _():
            exta[0:8, :] = jnp.zeros((8, w), F32)
            extb[0:32, :] = jnp.zeros((32, cw), F32)
            hcar[...] = jnp.zeros((8, w), F32)

        exta[8:8 + tm, :] = proj_ref[:, 0:w]
        cb4 = cb4_ref[...]

        def conv_a(c, carry):
            r0 = pl.multiple_of(c * 8, 8)
            xc_s[pl.ds(r0, 8), :] = _lru_conv_chunk(exta, cw4_ref, cb4, r0)
            return carry

        lax.fori_loop(0, nch, conv_a, 0)
        _head_gates(xc_s[...].astype(BF16), wa_ref, wx_ref, pa_s, px_s, ba_ref[...], bx_ref[...], heads)

        sp = _softplus(-lam_ref[...])
        ga = ga_ref[...]
        row = lax.broadcasted_iota(jnp.int32, (8, w), 0)

        def scan_c(c, hprev):
            r0 = pl.multiple_of(c * 8, 8)
            xc = xc_s[pl.ds(r0, 8), :]
            _, ig, a, m = _lru_gates(pa_s[pl.ds(r0, 8), :], px_s[pl.ds(r0, 8), :], sp)
            aa, bb = a, m * (ig * xc)
            for d in (1, 2, 4):
                a_s = pltpu.roll(aa, d, 0)
                b_s = pltpu.roll(bb, d, 0)
                msk = row >= d
                bb = jnp.where(msk, aa * b_s + bb, bb)
                aa = jnp.where(msk, aa * a_s, aa)
            hs = aa * hprev + bb
            hs_ref[pl.ds(r0, 8), :] = hs
            ya = hs * _gelu(proj_ref[pl.ds(r0, 8), w:2 * w])
            nbuf[pl.ds(r0, 8), 0:w] = (ya * _rms(ya)) * ga
            return hs[7:8, :]

        hcar[0:1, :] = lax.fori_loop(0, nch, scan_c, hcar[0:1, :])

        extb[32:32 + tm, :] = proj_ref[:, 2 * w:2 * w + cw] * _sigmoid(proj_ref[:, 2 * w + cw:2 * w + 2 * cw])

        def conv_b(c, carry):
            r0 = pl.multiple_of(c * 8, 8)
            ybs = []
            ssq = jnp.zeros((8, 1), F32)
            for lb in range(cw // LANES):
                sl = slice(LANES * lb, LANES * (lb + 1))
                win = extb[pl.ds(r0, 40), sl]
                rolled = [win] + [pltpu.roll(win, rr, 0) for rr in range(1, 8)]
                acc = jnp.zeros((8, LANES), F32)
                for k in range(CONV_K):
                    q, rr = divmod(CONV_K - 1 - k, 8)
                    acc = acc + cw31_ref[k:k + 1, sl] * rolled[rr][32 - 8 * q:40 - 8 * q]
                acc = acc + cb31_ref[:, sl]
                c1_ref[pl.ds(r0, 8), sl] = acc
                dlt = acc - jnp.mean(acc, axis=-1, keepdims=True)
                c2 = dlt * lax.rsqrt(jnp.mean(dlt * dlt, axis=-1, keepdims=True) + EPS)
                yb0 = c2 * lng_ref[:, sl] + lnb_ref[:, sl]
                yb = yb0 * _sigmoid(yb0)
                ybs.append(yb)
                ssq = ssq + jnp.sum(yb * yb, axis=-1, keepdims=True)
            rb = lax.rsqrt(ssq / cw + EPS)
            for lb in range(cw // LANES):
                sl = slice(LANES * lb, LANES * (lb + 1))
                nbuf[pl.ds(r0, 8), w + LANES * lb:w + LANES * (lb + 1)] = (ybs[lb] * rb) * gb_ref[:, sl]
            return carry

        lax.fori_loop(0, nch, conv_b, 0)
        exta[0:8, :] = exta[tm:tm + 8, :]
        extb[0:32, :] = extb[tm:tm + 32, :]
        y_ref[...] = nbuf[...].astype(BF16)

    consts = [p[k] for k in MIX_PARAMS]
    return _pcall(body, name=name, grid=(t // tm,),
                  in_specs=[_row(tm, 2 * w + 2 * cw)] + [_const(c.shape) for c in consts],
                  out_specs=[_row(tm, w + cw), _row(tm, w), _row(tm, cw)],
                  out_shape=[jax.ShapeDtypeStruct((t, w + cw), BF16), jax.ShapeDtypeStruct((t, w), F32),
                             jax.ShapeDtypeStruct((t, cw), F32)],
                  scratch_shapes=[pltpu.VMEM((8 + tm, w), F32), pltpu.VMEM((tm, w), F32), pltpu.VMEM((tm, w), F32),
                                  pltpu.VMEM((tm, w), F32), pltpu.VMEM((32 + tm, cw), F32),
                                  pltpu.VMEM((tm, w + cw), F32), pltpu.VMEM((8, w), F32)],
                  compiler_params=_params())(proj, *consts)


def _mix_bwd(dy, proj, hs, c1, p, name):
    t = proj.shape[0]
    w = p['lru_conv_b'].shape[1]
    cw = p['conv_b'].shape[1]
    heads = p['lru_wa'].shape[0]
    _, tm = _tiles(t)
    nt = t // tm
    nch = tm // 8
    nlb = cw // LANES
    G_CB4, G_CW4, G_BA, G_BX, G_SP, G_GA, NGW = 0, 1, 5, 6, 7, 8, 9
    G_CB31, G_LNG, G_LNB, G_GB, G_CW31, NGC = 0, 1, 2, 3, 4, 4 + CONV_K

    def body(dy_ref, proj_ref, projh_ref, hs_ref, hsh_ref, c1_ref, cw4_ref, cb4_ref, wa_ref, ba_ref, wx_ref, bx_ref,
             lam_ref, cw31_ref, cb31_ref, lng_ref, lnb_ref, ga_ref, gb_ref,
             dproj_ref, dcw4_ref, dcb4_ref, dwa_ref, dba_ref, dwx_ref, dbx_ref, dlam_ref, dcw31_ref, dcb31_ref,
             dlng_ref, dlnb_ref, dga_ref, dgb_ref,
             exta, exth, xc_s, pa_s, px_s, dpa_s, dpx_s, dxce, extb, dc1e, dpf, cp_s, acc_w, acc_c):
        i = pl.program_id(0)

        @pl.when(i == 0)
        def _():
            acc_w[...] = jnp.zeros((8 * NGW, w), F32)
            acc_c[...] = jnp.zeros((8 * NGC, cw), F32)
            dwa_ref[...] = jnp.zeros(dwa_ref.shape, F32)
            dwx_ref[...] = jnp.zeros(dwx_ref.shape, F32)
            cp_s[...] = jnp.zeros((8, w), F32)
            dxce[tm:tm + 8, :] = jnp.zeros((8, w), F32)
            dc1e[tm:tm + 32, :] = jnp.zeros((32, cw), F32)

        nf = jnp.where(i < nt - 1, 1.0, 0.0).astype(F32)
        exta[0:8, :] = projh_ref[40:48, 0:w] * nf
        exta[8:8 + tm, :] = proj_ref[:, 0:w]
        exth[0:8, :] = hsh_ref[...] * nf
        exth[8:8 + tm, :] = hs_ref[...]
        extb[0:48, :] = (projh_ref[:, 2 * w:2 * w + cw] * _sigmoid(projh_ref[:, 2 * w + cw:2 * w + 2 * cw])) * nf
        extb[48:48 + tm, :] = proj_ref[:, 2 * w:2 * w + cw] * _sigmoid(proj_ref[:, 2 * w + cw:2 * w + 2 * cw])
        cb4 = cb4_ref[...]

        def conv_a(c, carry):
            r0 = pl.multiple_of(c * 8, 8)
            xc_s[pl.ds(r0, 8), :] = _lru_conv_chunk(exta, cw4_ref, cb4, r0)
            return carry

        lax.fori_loop(0, nch, conv_a, 0)
        xcb = xc_s[...].astype(BF16)
        _head_gates(xcb, wa_ref, wx_ref, pa_s, px_s, ba_ref[...], bx_ref[...], heads)

        sp = _softplus(-lam_ref[...])
        ga = ga_ref[...]
        row = lax.broadcasted_iota(jnp.int32, (8, w), 0)

        def acc_add(ref, g, val, sl=slice(None)):
            ref[8 * g:8 * g + 8, sl] = ref[8 * g:8 * g + 8, sl] + val

        def rscan(cc, cp):
            r0 = pl.multiple_of((nch - 1 - cc) * 8, 8)
            xc = xc_s[pl.ds(r0, 8), :]
            r, ig, a, m = _lru_gates(pa_s[pl.ds(r0, 8), :], px_s[pl.ds(r0, 8), :], sp)
            hwin = exth[pl.ds(r0, 16), :]
            hcur = hwin[8:16]
            hprev = pltpu.roll(hwin, 1, 0)[8:16]
            ge, dge = _gelu_and_grad(proj_ref[pl.ds(r0, 8), w:2 * w])
            dna = dy_ref[pl.ds(r0, 8), 0:w]
            dya, yar = _rms_bwd(hcur * ge, ga, dna)
            acc_add(acc_w, G_GA, dna * yar)
            dpf[pl.ds(r0, 8), w:2 * w] = (dya * hcur) * dge
            aa = jnp.where(row == 7, 1.0, pltpu.roll(a, 7, 0))
            bb = dya * ge
            for d in (1, 2, 4):
                a_s = pltpu.roll(aa, 8 - d, 0)
                b_s = pltpu.roll(bb, 8 - d, 0)
                msk = row < 8 - d
                bb = jnp.where(msk, aa * b_s + bb, bb)
                aa = jnp.where(msk, aa * a_s, aa)
            lamb = bb + aa * cp
            dm = lamb * (ig * xc)
            di = lamb * (m * xc)
            dxce[pl.ds(r0, 8), :] = lamb * (m * ig)
            dla = a * (lamb * hprev - dm * (a / m))
            acc_add(acc_w, G_SP, dla * (-LRU_C * r))
            dpa = (dla * (-LRU_C * sp)) * (r * (1.0 - r))
            dpx = di * (ig * (1.0 - ig))
            acc_add(acc_w, G_BA, dpa)
            acc_add(acc_w, G_BX, dpx)
            dpa_s[pl.ds(r0, 8), :] = dpa
            dpx_s[pl.ds(r0, 8), :] = dpx
            return a[0:1, :] * lamb[0:1, :]

        cp_s[0:1, :] = lax.fori_loop(0, nch, rscan, cp_s[0:1, :])

        dpab = dpa_s[...].astype(BF16)
        dpxb = dpx_s[...].astype(BF16)
        for hd in range(heads):
            sl = slice(LANES * hd, LANES * (hd + 1))
            dxce[0:tm, sl] = (dxce[0:tm, sl]
                              + lax.dot_general(dpab[:, sl], wa_ref[hd], NT_DIMS, preferred_element_type=F32)
                              + lax.dot_general(dpxb[:, sl], wx_ref[hd], NT_DIMS, preferred_element_type=F32))
            dwa_ref[hd] = dwa_ref[hd] + lax.dot_general(xcb[:, sl], dpab[:, sl], TN_DIMS, preferred_element_type=F32)
            dwx_ref[hd] = dwx_ref[hd] + lax.dot_general(xcb[:, sl], dpxb[:, sl], TN_DIMS, preferred_element_type=F32)

        def conv_a_bwd(c, carry):
            r0 = pl.multiple_of(c * 8, 8)
            win = dxce[pl.ds(r0, 16), :]
            dxc = win[0:8]
            xwin = exta[pl.ds(r0, 16), :]
            dxl = cw4_ref[3:4, :] * dxc
            acc_add(acc_w, G_CB4, dxc)
            acc_add(acc_w, G_CW4 + 3, dxc * xwin[8:16])
            for k in range(LRU_K - 1):
                s = LRU_K - 1 - k
                dxl = dxl + cw4_ref[k:k + 1, :] * pltpu.roll(win, 16 - s, 0)[0:8]
                acc_add(acc_w, G_CW4 + k, dxc * pltpu.roll(xwin, s, 0)[8:16])
            dpf[pl.ds(r0, 8), 0:w] = dxl
            return carry

        lax.fori_loop(0, nch, conv_a_bwd, 0)
        dxce[tm:tm + 8, :] = dxce[0:8, :]

        def mixb(c, carry):
            r0 = pl.multiple_of(c * 8, 8)
            st = []
            ssq = jnp.zeros((8, 1), F32)
            for lb in range(nlb):
                sl = slice(LANES * lb, LANES * (lb + 1))
                c1v = c1_ref[pl.ds(r0, 8), sl]
                dlt = c1v - jnp.mean(c1v, axis=-1, keepdims=True)
                rs = lax.rsqrt(jnp.mean(dlt * dlt, axis=-1, keepdims=True) + EPS)
                c2 = dlt * rs
                yb0 = c2 * lng_ref[:, sl] + lnb_ref[:, sl]
                sg = _sigmoid(yb0)
                yb = yb0 * sg
                ssq = ssq + jnp.sum(yb * yb, axis=-1, keepdims=True)
                st.append((rs, c2, yb0, sg, yb))
            rb = lax.rsqrt(ssq / cw + EPS)
            tsum = jnp.zeros((8, 1), F32)
            dngs = []
            for lb in range(nlb):
                sl = slice(LANES * lb, LANES * (lb + 1))
                dnb = dy_ref[pl.ds(r0, 8), w + LANES * lb:w + LANES * (lb + 1)]
                ybr = st[lb][4] * rb
                acc_add(acc_c, G_GB, dnb * ybr, sl)
                dng = dnb * gb_ref[:, sl]
                dngs.append((dng, ybr))
                tsum = tsum + jnp.sum(dng * ybr, axis=-1, keepdims=True)
            tsum = tsum / cw
            for lb in range(nlb):
                sl = slice(LANES * lb, LANES * (lb + 1))
                rs, c2, yb0, sg, _ = st[lb]
                dng, ybr = dngs[lb]
                dyb0 = (rb * (dng - ybr * tsum)) * (sg * (1.0 + yb0 * (1.0 - sg)))
                acc_add(acc_c, G_LNG, dyb0 * c2, sl)
                acc_add(acc_c, G_LNB, dyb0, sl)
                dc2 = dyb0 * lng_ref[:, sl]
                dc1 = rs * (dc2 - jnp.mean(dc2, axis=-1, keepdims=True)
                            - c2 * jnp.mean(dc2 * c2, axis=-1, keepdims=True))
                acc_add(acc_c, G_CB31, dc1, sl)
                dc1e[pl.ds(r0, 8), sl] = dc1
            return carry

        lax.fori_loop(0, nch, mixb, 0)

        def conv_b_bwd(c, carry):
            r0 = pl.multiple_of(c * 8, 8)
            for lb in range(nlb):
                sl = slice(LANES * lb, LANES * (lb + 1))
                win = dc1e[pl.ds(r0, 40), sl]
                ups = [win] + [pltpu.roll(win, 40 - rr, 0) for rr in range(1, 8)]
                dc1 = win[0:8]
                dc0 = jnp.zeros((8, LANES), F32)
                for k in range(CONV_K):
                    q, rr = divmod(CONV_K - 1 - k, 8)
                    dc0 = dc0 + cw31_ref[k:k + 1, sl] * ups[rr][8 * q:8 * q + 8]
                cav = proj_ref[pl.ds(r0, 8), 2 * w + LANES * lb:2 * w + LANES * (lb + 1)]
                sg = _sigmoid(proj_ref[pl.ds(r0, 8), 2 * w + cw + LANES * lb:2 * w + cw + LANES * (lb + 1)])
                dpf[pl.ds(r0, 8), 2 * w + LANES * lb:2 * w + LANES * (lb + 1)] = dc0 * sg
                dpf[pl.ds(r0, 8), 2 * w + cw + LANES * lb:2 * w + cw + LANES * (lb + 1)] = (dc0 * cav) * (sg * (1.0 - sg))
                xwin = extb[pl.ds(pl.multiple_of(r0 + 16, 8), 40), sl]
                xr = [xwin] + [pltpu.roll(xwin, rr, 0) for rr in range(1, 8)]
                for k in range(CONV_K):
                    q, rr = divmod(CONV_K - 1 - k, 8)
                    acc_add(acc_c, G_CW31 + k, dc1 * xr[rr][32 - 8 * q:40 - 8 * q], sl)
            return carry

        lax.fori_loop(0, nch, conv_b_bwd, 0)
        dc1e[tm:tm + 32, :] = dc1e[0:32, :]
        dproj_ref[...] = dpf[...].astype(BF16)

        @pl.when(i == nt - 1)
        def _():
            def fold(ref, g):
                return jnp.sum(ref[8 * g:8 * g + 8, :], axis=0, keepdims=True)

            dcb4_ref[...] = fold(acc_w, G_CB4)
            for k in range(LRU_K):
                dcw4_ref[k:k + 1, :] = fold(acc_w, G_CW4 + k)
            dba_ref[...] = fold(acc_w, G_BA)
            dbx_ref[...] = fold(acc_w, G_BX)
            dlam_ref[...] = fold(acc_w, G_SP) * (-_sigmoid(-lam_ref[...]))
            dga_ref[...] = fold(acc_w, G_GA)
            dcb31_ref[...] = fold(acc_c, G_CB31)
            dlng_ref[...] = fold(acc_c, G_LNG)
            dlnb_ref[...] = fold(acc_c, G_LNB)
            dgb_ref[...] = fold(acc_c, G_GB)
            for k in range(CONV_K):
                dcw31_ref[k:k + 1, :] = fold(acc_c, G_CW31 + k)

    consts = [p[k] for k in MIX_PARAMS]
    outs = _pcall(body, name=name, grid=(nt,),
                  in_specs=[_rrow(tm, w + cw, nt), _rrow(tm, 2 * w + 2 * cw, nt), _halo(48, 2 * w + 2 * cw, tm, nt),
                            _rrow(tm, w, nt), _halo(8, w, tm, nt), _rrow(tm, cw, nt)] + [_const(c.shape) for c in consts],
                  out_specs=[_rrow(tm, 2 * w + 2 * cw, nt)] + [_const_out(c.shape) for c in consts],
                  out_shape=[jax.ShapeDtypeStruct((t, 2 * w + 2 * cw), BF16)]
                  + [jax.ShapeDtypeStruct(c.shape, F32) for c in consts],
                  scratch_shapes=[pltpu.VMEM((8 + tm, w), F32), pltpu.VMEM((8 + tm, w), F32), pltpu.VMEM((tm, w), F32),
                                  pltpu.VMEM((tm, w), F32), pltpu.VMEM((tm, w), F32), pltpu.VMEM((tm, w), F32),
                                  pltpu.VMEM((tm, w), F32), pltpu.VMEM((tm + 8, w), F32),
                                  pltpu.VMEM((48 + tm, cw), F32), pltpu.VMEM((tm + 32, cw), F32),
                                  pltpu.VMEM((tm, 2 * w + 2 * cw), F32), pltpu.VMEM((8, w), F32),
                                  pltpu.VMEM((8 * NGW, w), F32), pltpu.VMEM((8 * NGC, cw), F32)],
                  compiler_params=_params())(dy, proj, proj, hs, hs, c1, *consts)
    return outs[0], dict(zip(MIX_PARAMS, outs[1:]))


def _ffn_window(u_ref, halo, c, col):
    if isinstance(c, int) and c == 0:
        return jnp.concatenate([halo[:, col:col + LANES], u_ref[0:16, col:col + LANES]], axis=0)
    return u_ref[pl.ds(pl.multiple_of(c * 16 - 8, 8), 24), col:col + LANES]


def _ffn_conv(win, w3_ref, b3_ref, col):
    sl = slice(col, col + LANES)
    x1 = pltpu.roll(win, 1, 0)[8:24]
    x2 = pltpu.roll(win, 2, 0)[8:24]
    u = w3_ref[2:3, sl] * win[8:24] + w3_ref[1:2, sl] * x1 + w3_ref[0:1, sl] * x2 + b3_ref[:, sl]
    return u, (x2, x1, win[8:24])


def _ffn_act_fwd(u0, w3, b3, name):
    t, f2 = u0.shape
    ff = f2 // 2
    _, tm = _tiles(t)
    nch = tm // 16

    def body(u_ref, w3_ref, b3_ref, act_ref, car):
        @pl.when(pl.program_id(0) == 0)
        def _():
            car[...] = jnp.zeros((8, f2), F32)

        def chunk(c):
            halo = car[...] if isinstance(c, int) else None
            r0 = 0 if isinstance(c, int) else pl.multiple_of(c * 16, 16)
            for j in range(ff // LANES):
                gate, _ = _ffn_conv(_ffn_window(u_ref, halo, c, LANES * j), w3_ref, b3_ref, LANES * j)
                up, _ = _ffn_conv(_ffn_window(u_ref, halo, c, ff + LANES * j), w3_ref, b3_ref, ff + LANES * j)
                act_ref[pl.ds(r0, 16), LANES * j:LANES * (j + 1)] = (_gelu(gate) * up).astype(BF16)

        chunk(0)

        def loop(c, carry):
            chunk(c)
            return carry

        lax.fori_loop(1, nch, loop, 0)
        car[...] = u_ref[tm - 8:tm, :]

    return _pcall(body, name=name, grid=(t // tm,),
                  in_specs=[_row(tm, f2), _const(w3.shape), _const(b3.shape)],
                  out_specs=_row(tm, ff), out_shape=jax.ShapeDtypeStruct((t, ff), BF16),
                  scratch_shapes=[pltpu.VMEM((8, f2), F32)],
                  compiler_params=_params())(u0, w3, b3)


def _ffn_act_bwd(dact, u0, w3, b3, name):
    t, f2 = u0.shape
    ff = f2 // 2
    _, tm = _tiles(t)
    nt = t // tm
    nch = tm // 16

    def body(dact_ref, u_ref, uh_ref, w3_ref, b3_ref, du0_ref, dw3_ref, db3_ref, dub, acc):
        i = pl.program_id(0)

        @pl.when(i == 0)
        def _():
            dub[tm:tm + 8, :] = jnp.zeros((8, f2), F32)
            acc[...] = jnp.zeros((32, f2), F32)

        nf = jnp.where(i < nt - 1, 1.0, 0.0).astype(F32)

        def acc_add(g, val, sl):
            acc[8 * g:8 * g + 8, sl] = acc[8 * g:8 * g + 8, sl] + (val[0:8] + val[8:16])

        def chunk(c):
            halo = uh_ref[...] * nf if isinstance(c, int) else None
            r0 = 0 if isinstance(c, int) else pl.multiple_of(c * 16, 16)
            for j in range(ff // LANES):
                cg, cu = LANES * j, ff + LANES * j
                gate, xg = _ffn_conv(_ffn_window(u_ref, halo, c, cg), w3_ref, b3_ref, cg)
                up, xu = _ffn_conv(_ffn_window(u_ref, halo, c, cu), w3_ref, b3_ref, cu)
                ge, dge = _gelu_and_grad(gate)
                da = dact_ref[pl.ds(r0, 16), cg:cg + LANES]
                for col, du, xs in ((cg, (da * up) * dge, xg), (cu, da * ge, xu)):
                    sl = slice(col, col + LANES)
                    dub[pl.ds(r0, 16), sl] = du
                    acc_add(0, du, sl)
                    for k in range(FFN_K):
                        acc_add(1 + k, du * xs[k], sl)

        chunk(0)

        def loop1(c, carry):
            chunk(c)
            return carry

        lax.fori_loop(1, nch, loop1, 0)

        def loop2(c, carry):
            r0 = pl.multiple_of(c * 16, 16)
            for j in range(f2 // LANES):
                sl = slice(LANES * j, LANES * (j + 1))
                win = dub[pl.ds(r0, 24), sl]
                du0 = (w3_ref[2:3, sl] * win[0:16] + w3_ref[1:2, sl] * pltpu.roll(win, 23, 0)[0:16]
                       + w3_ref[0:1, sl] * pltpu.roll(win, 22, 0)[0:16])
                du0_ref[pl.ds(r0, 16), sl] = du0.astype(BF16)
            return carry

        lax.fori_loop(0, nch, loop2, 0)
        dub[tm:tm + 8, :] = dub[0:8, :]

        @pl.when(i == nt - 1)
        def _():
            db3_ref[...] = jnp.sum(acc[0:8, :], axis=0, keepdims=True)
            for k in range(FFN_K):
                dw3_ref[k:k + 1, :] = jnp.sum(acc[8 + 8 * k:16 + 8 * k, :], axis=0, keepdims=True)

    return _pcall(body, name=name, grid=(nt,),
                  in_specs=[_rrow(tm, ff, nt), _rrow(tm, f2, nt), _halo(8, f2, tm, nt), _const(w3.shape),
                            _const(b3.shape)],
                  out_specs=[_rrow(tm, f2, nt), _const_out(w3.shape), _const_out(b3.shape)],
                  out_shape=[jax.ShapeDtypeStruct((t, f2), BF16), jax.ShapeDtypeStruct(w3.shape, F32),
                             jax.ShapeDtypeStruct(b3.shape, F32)],
                  scratch_shapes=[pltpu.VMEM((tm + 8, f2), F32), pltpu.VMEM((32, f2), F32)],
                  compiler_params=_params())(dact, u0, u0, w3, b3)


def _loss_head(h, tgt, name):
    t, d = h.shape
    tm, _ = _tiles(t)

    def body(h_ref, t_ref, dh_ref, s_ref):
        i = pl.program_id(0)

        @pl.when(i == 0)
        def _():
            s_ref[...] = jnp.zeros((1, d), F32)

        row = lax.broadcasted_iota(jnp.int32, (tm, d), 0) + i * tm
        err = jnp.where(row >= N_META, h_ref[...] - t_ref[...], 0.0)
        dh_ref[...] = err / d
        s_ref[...] += jnp.sum(err * err, axis=0, keepdims=True)

    return _pcall(body, name=name, grid=(t // tm,), in_specs=[_row(tm, d), _row(tm, d)],
                  out_specs=[_row(tm, d), _const_out((1, d))],
                  out_shape=[jax.ShapeDtypeStruct((t, d), F32), jax.ShapeDtypeStruct((1, d), F32)],
                  compiler_params=_params())(h, tgt)


def _row_tile(rows, row_bytes, budget):
    best = None
    for tr in range(16, rows + 1, 16):
        if rows % tr == 0 and tr * row_bytes <= budget:
            best = tr
    assert best is not None, (rows, row_bytes)
    return best


def _cast_bf16(a, name):
    r, c = a.shape
    tr = _row_tile(r, c * 4, 4 << 20)

    def body(a_ref, o_ref):
        o_ref[...] = a_ref[...].astype(BF16)

    return _pcall(body, name=name, grid=(r // tr,), in_specs=[_row(tr, c)], out_specs=_row(tr, c),
                  out_shape=jax.ShapeDtypeStruct((r, c), BF16), compiler_params=_params())(a)


def _sum_slots(r, name):
    s, rows, c = r.shape
    tr = _row_tile(rows, s * c * 4, 8 << 20)

    def body(r_ref, o_ref):
        acc = r_ref[0]
        for k in range(1, s):
            acc = acc + r_ref[k]
        o_ref[...] = acc

    return _pcall(body, name=name, grid=(rows // tr,),
                  in_specs=[pl.BlockSpec((s, tr, c), lambda i: (0, i, 0))], out_specs=_row(tr, c),
                  out_shape=jax.ShapeDtypeStruct((rows, c), F32), compiler_params=_params())(r)


def _adamw(g, w, m, v, name):
    r, c = g.shape
    tr = _row_tile(r, c * 4, 1 << 20)

    def body(g_ref, w_ref, m_ref, v_ref, d_ref, m2_ref, v2_ref):
        gv = g_ref[...]
        m2 = ADAM_B1 * m_ref[...] + (1.0 - ADAM_B1) * gv
        v2 = ADAM_B2 * v_ref[...] + (1.0 - ADAM_B2) * (gv * gv)
        m_hat = m2 / (1.0 - ADAM_B1 ** ADAM_STEP)
        v_hat = v2 / (1.0 - ADAM_B2 ** ADAM_STEP)
        d_ref[...] = -ADAM_LR * (m_hat / (jnp.sqrt(v_hat) + ADAM_EPS) + ADAM_WD * w_ref[...])
        m2_ref[...] = m2
        v2_ref[...] = v2

    return _pcall(body, name=name, grid=(r // tr,), in_specs=[_row(tr, c)] * 4, out_specs=[_row(tr, c)] * 3,
                  out_shape=[jax.ShapeDtypeStruct((r, c), F32)] * 3, compiler_params=_params())(g, w, m, v)


ANY = pl.BlockSpec(memory_space=pl.ANY)


def _coords():
    return lax.axis_index("x"), lax.axis_index("y"), lax.axis_index("c")


def _window(ref, lead, axis, k, width):
    idx = [slice(None)] * len(ref.shape)
    idx[0] = lead
    idx[axis] = pl.ds(pl.multiple_of(k * width, LANES if axis == len(ref.shape) - 1 else 8), width)
    return ref.at[tuple(idx)]


def _gather_xy(arrs, axes, name):
    n = len(arrs)
    out_shape = []
    for a, ax in zip(arrs, axes):
        s = list(a.shape)
        s[ax] *= 4
        out_shape.append(jax.ShapeDtypeStruct(tuple(s), a.dtype))

    def body(*refs):
        ins, outs = refs[:n], refs[n:2 * n]
        send_sems, recv_sems, loc_sems = refs[2 * n:]
        x, y, c = _coords()
        k_me = 2 * x + y
        chips = [(1 - x, y), (x, 1 - y), (1 - x, 1 - y)]
        sib = (x, y, 1 - c)

        def half(i, which):
            hl = arrs[i].shape[0] // 2
            return pl.ds(which * hl, hl)

        def win(i, kk, which):
            return _window(outs[i], half(i, which), axes[i], kk, arrs[i].shape[axes[i]])

        def copy(i, s, src, dst, to):
            return pltpu.make_async_remote_copy(src_ref=src, dst_ref=dst, send_sem=send_sems.at[i, s],
                                                recv_sem=recv_sems.at[i, s], device_id=to, device_id_type=MESH_T)

        locs = []
        for i in range(n):
            lc = pltpu.make_async_copy(ins[i], _window(outs[i], slice(None), axes[i], k_me, arrs[i].shape[axes[i]]),
                                       loc_sems.at[i])
            lc.start()
            locs.append(lc)
        started = []
        for i in range(n):
            for j, chip in enumerate(chips):
                cp = copy(i, j, ins[i].at[half(i, c)], win(i, k_me, c), (*chip, c))
                cp.start()
                started.append(cp)
        for i in range(n):
            for j, chip in enumerate(chips):
                kk = 2 * chip[0] + chip[1]
                copy(i, j, win(i, kk, c), win(i, kk, c), (*chip, c)).wait_recv()
                fw = copy(i, 3 + j, win(i, kk, c), win(i, kk, c), sib)
                fw.start()
                started.append(fw)
        for i in range(n):
            for j, chip in enumerate(chips):
                kk = 2 * chip[0] + chip[1]
                copy(i, 3 + j, win(i, kk, 1 - c), win(i, kk, 1 - c), sib).wait_recv()
        for cp in started:
            cp.wait_send()
        for lc in locs:
            lc.wait()

    return _pcall(body, name=name, in_specs=[ANY] * n, out_specs=[ANY] * n, out_shape=out_shape,
                  scratch_shapes=[pltpu.SemaphoreType.DMA((n, 6)), pltpu.SemaphoreType.DMA((n, 6)),
                                  pltpu.SemaphoreType.DMA((n,))],
                  compiler_params=pltpu.CompilerParams(has_side_effects=True))(*arrs)


def _peer(x, y, c, mask):
    bx, by, bc = (mask >> 2) & 1, (mask >> 1) & 1, mask & 1
    return (1 - x if bx else x, 1 - y if by else y, 1 - c if bc else c)


def _grad_scatter(grads, axes, name):
    n = len(grads)
    out_shape = []
    widths = []
    for a, ax in zip(grads, axes):
        s = list(a.shape)
        s[ax] //= 4
        s[0] //= 2
        widths.append(s[ax])
        out_shape.append(jax.ShapeDtypeStruct((8, *s), F32))

    def body(*refs):
        ins, outs = refs[:n], refs[n:2 * n]
        send_sems, recv_sems, loc_sems = refs[2 * n:]
        x, y, c = _coords()

        def piece(i, px, py, pc):
            hl = grads[i].shape[0] // 2
            return _window(ins[i], pl.ds(pc * hl, hl), axes[i], 2 * px + py, widths[i])

        locs = []
        for i in range(n):
            lc = pltpu.make_async_copy(piece(i, x, y, c), outs[i].at[7], loc_sems.at[i])
            lc.start()
            locs.append(lc)
        started = []
        for i in range(n):
            for mask in range(1, 8):
                px, py, pc = _peer(x, y, c, mask)
                cp = pltpu.make_async_remote_copy(src_ref=piece(i, px, py, pc), dst_ref=outs[i].at[mask - 1],
                                                  send_sem=send_sems.at[i, mask - 1], recv_sem=recv_sems.at[i, mask - 1],
                                                  device_id=(px, py, pc), device_id_type=MESH_T)
                cp.start()
                started.append(cp)
        for cp in started:
            cp.wait()
        for lc in locs:
            lc.wait()

    return _pcall(body, name=name, in_specs=[ANY] * n, out_specs=[ANY] * n, out_shape=out_shape,
                  scratch_shapes=[pltpu.SemaphoreType.DMA((n, 7)), pltpu.SemaphoreType.DMA((n, 7)),
                                  pltpu.SemaphoreType.DMA((n,))],
                  compiler_params=pltpu.CompilerParams(has_side_effects=True))(*grads)


def _sibling_swap(halves, name):
    n = len(halves)
    out_shape = [jax.ShapeDtypeStruct((2 * a.shape[0], *a.shape[1:]), a.dtype) for a in halves]

    def body(*refs):
        ins, outs = refs[:n], refs[n:2 * n]
        send_sems, recv_sems, loc_sems = refs[2 * n:]
        x, y, c = _coords()
        locs, started = [], []
        for i in range(n):
            hl = halves[i].shape[0]
            lc = pltpu.make_async_copy(ins[i], outs[i].at[pl.ds(c * hl, hl)], loc_sems.at[i])
            lc.start()
            locs.append(lc)
            cp = pltpu.make_async_remote_copy(src_ref=ins[i], dst_ref=outs[i].at[pl.ds(c * hl, hl)],
                                              send_sem=send_sems.at[i], recv_sem=recv_sems.at[i],
                                              device_id=(x, y, 1 - c), device_id_type=MESH_T)
            cp.start()
            started.append(cp)
        for cp in started:
            cp.wait()
        for lc in locs:
            lc.wait()

    return _pcall(body, name=name, in_specs=[ANY] * n, out_specs=[ANY] * n, out_shape=out_shape,
                  scratch_shapes=[pltpu.SemaphoreType.DMA((n,)), pltpu.SemaphoreType.DMA((n,)),
                                  pltpu.SemaphoreType.DMA((n,))],
                  compiler_params=pltpu.CompilerParams(has_side_effects=True))(*halves)


def _gather_all(pk, name):
    r, cdim = pk.shape

    def body(p_ref, o_ref, send_sems, recv_sems, loc_sem):
        x, y, c = _coords()
        lc = pltpu.make_async_copy(p_ref, o_ref.at[4 * x + 2 * y + c], loc_sem)
        lc.start()
        started = []
        for mask in range(1, 8):
            px, py, pc = _peer(x, y, c, mask)
            cp = pltpu.make_async_remote_copy(src_ref=p_ref, dst_ref=o_ref.at[4 * x + 2 * y + c],
                                              send_sem=send_sems.at[mask - 1], recv_sem=recv_sems.at[mask - 1],
                                              device_id=(px, py, pc), device_id_type=MESH_T)
            cp.start()
            started.append(cp)
        for cp in started:
            cp.wait()
        lc.wait()

    return _pcall(body, name=name, in_specs=[ANY], out_specs=ANY, out_shape=jax.ShapeDtypeStruct((8, r, cdim), F32),
                  scratch_shapes=[pltpu.SemaphoreType.DMA((7,)), pltpu.SemaphoreType.DMA((7,)),
                                  pltpu.SemaphoreType.DMA],
                  compiler_params=pltpu.CompilerParams(has_side_effects=True))(pk)


PACK_C = 1024


def _pack(arrs, row_mult):
    parts = []
    for a in arrs:
        flat = a.reshape(-1)
        parts.append(jnp.pad(flat, (0, (-flat.shape[0]) % PACK_C)))
    flat = jnp.concatenate(parts)
    flat = jnp.pad(flat, (0, (-flat.shape[0]) % (PACK_C * row_mult)))
    return flat.reshape(-1, PACK_C)


def _unpack(pk, shapes):
    flat = pk.reshape(-1)
    out, off = [], 0
    for s in shapes:
        size = 1
        for dd in s:
            size *= dd
        out.append(flat[off:off + size].reshape(s))
        off += size + (-size) % PACK_C
    return out


def _layer_params(full, l):
    p = {}
    for k in ['g_pre_mix', 'lru_conv_b', 'lru_ba', 'lru_bx', 'lru_lambda', 'conv_b', 'conv_ln_g', 'conv_ln_b', 'g_out_lru',
              'g_out_conv', 'g_post_mix', 'g_pre_ffn', 'ffn_conv_b', 'g_post_ffn']:
        p[k] = full[k][l][None, :]
    for k in ['lru_conv_w', 'conv_w', 'ffn_conv_w', 'w_in', 'w_out', 'w_up', 'w_down']:
        p[k] = full[k][l]
    p['lru_wa'] = full['lru_wa_bf'][l]
    p['lru_wx'] = full['lru_wx_bf'][l]
    return p


def _step(x, loss_target, w, m, v):
    depth = w['w_in'].shape[0]
    d = x.shape[2]
    xk, yk, _ = _coords()
    k_me = 2 * xk + yk

    big_bf = {k: _cast_bf16(w[k].reshape(-1, w[k].shape[-1]), "cast_" + k).reshape(w[k].shape) for k in BIG}
    sh_pad = [w['meta_tokens']] + [jnp.pad(w[k], ((0, 0), (0, (-w[k].shape[1]) % 8), (0, 0))) for k in SH_SMALL[1:]]
    gath = _gather_xy([big_bf[k] for k in BIG] + sh_pad, [BIG_AXIS[k] for k in BIG] + [1, 2, 2, 2], "gather_weights")
    full = dict(w)
    full.update(dict(zip(BIG, gath[:len(BIG)])))
    full['meta_tokens'] = gath[len(BIG)]
    for k, a in zip(SH_SMALL[1:], gath[len(BIG) + 1:]):
        full[k] = a[:, :w[k].shape[1]]
    for k in ('lru_wa', 'lru_wx'):
        full[k + '_bf'] = _cast_bf16(w[k].reshape(-1, LANES), "cast_" + k).reshape(w[k].shape)

    h = jnp.concatenate([full['meta_tokens'], x[0]], axis=0)
    tgt = jnp.pad(loss_target[0], ((N_META, 0), (0, 0)))
    sq, dh, gl = _fwd_bwd(h, tgt, full, depth)
    loss = lax.psum(0.5 * jnp.sum(sq) / d, ("x", "y", "c"))
    return _reduce_update(loss, dh, gl, w, m, v, depth, k_me)


def _fwd_bwd(h, tgt, full, depth):
    saved = []
    for l in range(depth):
        p = _layer_params(full, l)
        proj, zb1 = _rms_matmul(h, p['g_pre_mix'], p['w_in'], "in_proj")
        y, hs, c1 = _mix_fwd(proj, p, "mix_fwd")
        o, h1 = _matmul_rms_res(y, p['w_out'], h, p['g_post_mix'], "out_proj")
        u0, zb2 = _rms_matmul(h1, p['g_pre_ffn'], p['w_up'], "up_proj")
        act = _ffn_act_fwd(u0, p['ffn_conv_w'], p['ffn_conv_b'], "ffn_act_fwd")
        f, h2 = _matmul_rms_res(act, p['w_down'], h1, p['g_post_ffn'], "down_proj")
        saved.append((p, h, zb1, proj, y, hs, c1, o, h1, zb2, u0, act, f))
        h = h2

    dh, sq = _loss_head(h, tgt, "loss_head")

    gl = [None] * depth
    for l in reversed(range(depth)):
        p, h0, zb1, proj, y, hs, c1, o, h1, zb2, u0, act, f = saved[l]
        g = {}
        dact, dfb, g['g_post_ffn'] = _rmsbwd_matmul_nt(f, p['g_post_ffn'], dh, p['w_down'], "down_bwd")
        g['w_down'] = _matmul_tn(act, dfb, "down_dw")
        du0, g['ffn_conv_w'], g['ffn_conv_b'] = _ffn_act_bwd(dact, u0, p['ffn_conv_w'], p['ffn_conv_b'], "ffn_act_bwd")
        g['w_up'] = _matmul_tn(zb2, du0, "up_dw")
        dh1, g['g_pre_ffn'] = _matmul_nt_rmsbwd_res(du0, p['w_up'], h1, p['g_pre_ffn'], dh, "up_bwd")
        dy, dob, g['g_post_mix'] = _rmsbwd_matmul_nt(o, p['g_post_mix'], dh1, p['w_out'], "out_bwd")
        g['w_out'] = _matmul_tn(y, dob, "out_dw")
        dproj, gm = _mix_bwd(dy, proj, hs, c1, p, "mix_bwd")
        g.update(gm)
        g['w_in'] = _matmul_tn(zb1, dproj, "in_dw")
        dh, g['g_pre_mix'] = _matmul_nt_rmsbwd_res(dproj, p['w_in'], h0, p['g_pre_mix'], dh1, "in_bwd")
        gl[l] = g
    return sq, dh, gl


def _reduce_update(loss, dh, gl, w, m, v, depth, k_me):
    grad_x = dh[N_META:][None]

    def stacked(k):
        return jnp.stack([gl[l][k].reshape(w[k].shape[1:]) if k not in SH_SMALL + BIG else gl[l][k]
                          for l in range(depth)])

    big_full = [stacked(k) for k in BIG]
    slots = _grad_scatter(big_full, [BIG_AXIS[k] for k in BIG], "grad_scatter")
    halves = []
    for k, s in zip(BIG, slots):
        red = _sum_slots(s.reshape(8, -1, s.shape[-1]), "grad_sum")
        halves.append(red.reshape(s.shape[1:]))
    big_red = _sibling_swap(halves, "grad_swap")

    out = {}
    for k, gk in zip(BIG, big_red):
        c2 = gk.shape[-1]
        dl, m2, v2 = _adamw(gk.reshape(-1, c2), w[k].reshape(-1, c2), m[k].reshape(-1, c2), v[k].reshape(-1, c2),
                            "adamw_big")
        out[k] = (gk, dl.reshape(gk.shape), m2.reshape(gk.shape), v2.reshape(gk.shape))

    rep_g = [stacked(k) for k in REP_SMALL]
    sh_g = [dh[:N_META]] + [stacked(k) for k in SH_SMALL[1:]]
    n_rep_rows = _pack(rep_g, 1).shape[0]
    pk = jnp.concatenate([_pack(rep_g, 1), _pack(sh_g, 1)])
    pk = jnp.pad(pk, ((0, (-pk.shape[0]) % 256), (0, 0)))
    red = _sum_slots(_gather_all(pk, "small_gather"), "small_sum")
    rep_red = _unpack(red[:n_rep_rows], [a.shape for a in rep_g])
    sh_red = []
    for a in _unpack(red[n_rep_rows:], [a.shape for a in sh_g]):
        wd = a.shape[-1] // 4
        sh_red.append(lax.dynamic_slice_in_dim(a, k_me * wd, wd, axis=a.ndim - 1))

    for names, grads_ in ((REP_SMALL, rep_red), (SH_SMALL, sh_red)):
        res = _adamw(_pack(grads_, 16), _pack([w[k] for k in names], 16), _pack([m[k] for k in names], 16),
                     _pack([v[k] for k in names], 16), "adamw_small")
        shapes = [w[k].shape for k in names]
        un = [_unpack(r, shapes) for r in res]
        for j, k in enumerate(names):
            out[k] = (grads_[j].reshape(w[k].shape), un[0][j], un[1][j], un[2][j])

    return (loss, grad_x, *[out[k][0] for k in WEIGHTS], *[out[k][1] for k in WEIGHTS],
            *[out[k][2] for k in WEIGHTS], *[out[k][3] for k in WEIGHTS])


def kernel(x, meta_tokens, g_pre_mix, w_in, lru_conv_w, lru_conv_b, lru_wa, lru_ba, lru_wx, lru_bx, lru_lambda, conv_w, conv_b, conv_ln_g, conv_ln_b, g_out_lru, g_out_conv, w_out, g_post_mix, g_pre_ffn, w_up, ffn_conv_w, ffn_conv_b, w_down, g_post_ffn, loss_target, m_meta_tokens, m_g_pre_mix, m_w_in, m_lru_conv_w, m_lru_conv_b, m_lru_wa, m_lru_ba, m_lru_wx, m_lru_bx, m_lru_lambda, m_conv_w, m_conv_b, m_conv_ln_g, m_conv_ln_b, m_g_out_lru, m_g_out_conv, m_w_out, m_g_post_mix, m_g_pre_ffn, m_w_up, m_ffn_conv_w, m_ffn_conv_b, m_w_down, m_g_post_ffn, v_meta_tokens, v_g_pre_mix, v_w_in, v_lru_conv_w, v_lru_conv_b, v_lru_wa, v_lru_ba, v_lru_wx, v_lru_bx, v_lru_lambda, v_conv_w, v_conv_b, v_conv_ln_g, v_conv_ln_b, v_g_out_lru, v_g_out_conv, v_w_out, v_g_post_mix, v_g_pre_ffn, v_w_up, v_ffn_conv_w, v_ffn_conv_b, v_w_down, v_g_post_ffn):
    w = dict(meta_tokens=meta_tokens, g_pre_mix=g_pre_mix, w_in=w_in, lru_conv_w=lru_conv_w, lru_conv_b=lru_conv_b,
             lru_wa=lru_wa, lru_ba=lru_ba, lru_wx=lru_wx, lru_bx=lru_bx, lru_lambda=lru_lambda, conv_w=conv_w,
             conv_b=conv_b, conv_ln_g=conv_ln_g, conv_ln_b=conv_ln_b, g_out_lru=g_out_lru, g_out_conv=g_out_conv,
             w_out=w_out, g_post_mix=g_post_mix, g_pre_ffn=g_pre_ffn, w_up=w_up, ffn_conv_w=ffn_conv_w,
             ffn_conv_b=ffn_conv_b, w_down=w_down, g_post_ffn=g_post_ffn)
    m = dict(meta_tokens=m_meta_tokens, g_pre_mix=m_g_pre_mix, w_in=m_w_in, lru_conv_w=m_lru_conv_w,
             lru_conv_b=m_lru_conv_b, lru_wa=m_lru_wa, lru_ba=m_lru_ba, lru_wx=m_lru_wx, lru_bx=m_lru_bx,
             lru_lambda=m_lru_lambda, conv_w=m_conv_w, conv_b=m_conv_b, conv_ln_g=m_conv_ln_g, conv_ln_b=m_conv_ln_b,
             g_out_lru=m_g_out_lru, g_out_conv=m_g_out_conv, w_out=m_w_out, g_post_mix=m_g_post_mix,
             g_pre_ffn=m_g_pre_ffn, w_up=m_w_up, ffn_conv_w=m_ffn_conv_w, ffn_conv_b=m_ffn_conv_b, w_down=m_w_down,
             g_post_ffn=m_g_post_ffn)
    v = dict(meta_tokens=v_meta_tokens, g_pre_mix=v_g_pre_mix, w_in=v_w_in, lru_conv_w=v_lru_conv_w,
             lru_conv_b=v_lru_conv_b, lru_wa=v_lru_wa, lru_ba=v_lru_ba, lru_wx=v_lru_wx, lru_bx=v_lru_bx,
             lru_lambda=v_lru_lambda, conv_w=v_conv_w, conv_b=v_conv_b, conv_ln_g=v_conv_ln_g, conv_ln_b=v_conv_ln_b,
             g_out_lru=v_g_out_lru, g_out_conv=v_g_out_conv, w_out=v_w_out, g_post_mix=v_g_post_mix,
             g_pre_ffn=v_g_pre_ffn, w_up=v_w_up, ffn_conv_w=v_ffn_conv_w, ffn_conv_b=v_ffn_conv_b, w_down=v_w_down,
             g_post_ffn=v_g_post_ffn)
    return _step(x, loss_target, w, m, v)
```

```python
import functools

import jax
import jax.numpy as jnp
from jax import lax
from jax.experimental import pallas as pl
from jax.experimental.pallas import tpu as pltpu

F32 = jnp.float32
BF16 = jnp.bfloat16
EPS = 1e-6
N_META = 16
LRU_C = 8.0
CONV_K = 31
LRU_K = 4
FFN_K = 3
SCAN_U = 3
RSCAN_U = 2
LANES = 128
VMEM_LIMIT = 56 * 1024 * 1024
ADAM_LR, ADAM_B1, ADAM_B2, ADAM_EPS, ADAM_WD, ADAM_STEP = 0.001, 0.9, 0.999, 1e-08, 0.01, 10
MESH_T = pl.DeviceIdType.MESH
NT_DIMS = (((1,), (1,)), ((), ()))
TN_DIMS = (((0,), (0,)), ((), ()))

REP_SMALL = ['g_pre_mix', 'lru_conv_b', 'lru_wa', 'lru_ba', 'lru_wx', 'lru_bx', 'lru_lambda', 'conv_b', 'conv_ln_g',
             'conv_ln_b', 'g_out_lru', 'g_out_conv', 'g_post_mix', 'g_pre_ffn', 'ffn_conv_b', 'g_post_ffn']
SH_SMALL = ['meta_tokens', 'lru_conv_w', 'conv_w', 'ffn_conv_w']
BIG = ['w_in', 'w_out', 'w_up', 'w_down']
BIG_AXIS = {'w_in': 2, 'w_out': 1, 'w_up': 2, 'w_down': 1}
WEIGHTS = ['meta_tokens', 'g_pre_mix', 'w_in', 'lru_conv_w', 'lru_conv_b', 'lru_wa', 'lru_ba', 'lru_wx', 'lru_bx',
           'lru_lambda', 'conv_w', 'conv_b', 'conv_ln_g', 'conv_ln_b', 'g_out_lru', 'g_out_conv', 'w_out',
           'g_post_mix', 'g_pre_ffn', 'w_up', 'ffn_conv_w', 'ffn_conv_b', 'w_down', 'g_post_ffn']


def _pcall(body, **kw):
    return pl.pallas_call(body, **kw)


def _params(n_grid=1):
    return pltpu.CompilerParams(dimension_semantics=("arbitrary",) * n_grid, vmem_limit_bytes=VMEM_LIMIT)


def _tiles(t):
    if t % 432 == 0:
        return 432, 144
    assert t % 48 == 0
    return 48, 48


def _row(tm, n):
    return pl.BlockSpec((tm, n), lambda i: (i, 0))


def _rrow(tm, n, nt):
    return pl.BlockSpec((tm, n), lambda i: (nt - 1 - i, 0))


def _halo(hb, n, tm, nt):
    return pl.BlockSpec((hb, n), lambda i: (jnp.maximum((nt - 1 - i) * (tm // hb) - 1, 0), 0))


def _const(shape):
    nd = len(shape)
    return pl.BlockSpec(shape, lambda *_: (0,) * nd, pipeline_mode=pl.Buffered(1))


def _const_out(shape):
    nd = len(shape)
    return pl.BlockSpec(shape, lambda *_: (0,) * nd)


def _sigmoid(x):
    return 1.0 / (1.0 + jnp.exp(-x))


def _gelu(x):
    return 0.5 * x * (1.0 + jnp.tanh(0.7978845608028654 * (x + 0.044715 * (x * x * x))))


def _gelu_and_grad(x):
    k = 0.7978845608028654
    x2 = x * x
    th = jnp.tanh(k * (x + 0.044715 * (x2 * x)))
    return 0.5 * x * (1.0 + th), 0.5 * (1.0 + th) + 0.5 * x * (1.0 - th * th) * (k * (1.0 + 0.134145 * x2))


def _expm1(x):
    return jnp.where(jnp.abs(x) < 1e-2, x * (1.0 + x * (0.5 + x * (1.0 / 6.0 + x * (1.0 / 24.0)))), jnp.exp(x) - 1.0)


def _softplus(x):
    e = jnp.exp(-jnp.abs(x))
    return jnp.maximum(x, 0.0) + jnp.where(e < 1e-4, e * (1.0 - 0.5 * e), jnp.log(1.0 + e))


def _lru_gates(pa, px, sp):
    r = _sigmoid(pa)
    ig = _sigmoid(px)
    la = (-LRU_C * r) * sp
    return r, ig, jnp.exp(la), jnp.sqrt(-_expm1(2.0 * la))


def _rms(x):
    return lax.rsqrt(jnp.mean(x * x, axis=-1, keepdims=True) + EPS)


def _rms_bwd(x, g, dy):
    r = _rms(x)
    xr = x * r
    dyg = dy * g
    return r * (dyg - xr * jnp.mean(dyg * xr, axis=-1, keepdims=True)), xr


def _col_chunk(n):
    return 1536 if n % 1536 == 0 else 1024


def _rms_matmul(h, g, w, name):
    t, d = h.shape
    n = w.shape[1]
    tm, _ = _tiles(t)
    cn = _col_chunk(n)

    def body(h_ref, g_ref, w_ref, p_ref, zb_ref):
        x = h_ref[...]
        zb = ((x * _rms(x)) * g_ref[...]).astype(BF16)
        zb_ref[...] = zb
        for c in range(n // cn):
            p_ref[:, c * cn:(c + 1) * cn] = jnp.dot(zb, w_ref[:, c * cn:(c + 1) * cn], preferred_element_type=F32)

    return _pcall(body, name=name, grid=(t // tm,),
                  in_specs=[_row(tm, d), _const((1, d)), _const((d, n))],
                  out_specs=[_row(tm, n), _row(tm, d)],
                  out_shape=[jax.ShapeDtypeStruct((t, n), F32), jax.ShapeDtypeStruct((t, d), BF16)],
                  compiler_params=_params())(h, g, w)


def _matmul_rms_res(a, w, h, g, name):
    t, k = a.shape
    d = w.shape[1]
    tm, _ = _tiles(t)

    def body(a_ref, w_ref, h_ref, g_ref, o_ref, hn_ref):
        o = jnp.dot(a_ref[...], w_ref[...], preferred_element_type=F32)
        o_ref[...] = o
        hn_ref[...] = h_ref[...] + (o * _rms(o)) * g_ref[...]

    return _pcall(body, name=name, grid=(t // tm,),
                  in_specs=[_row(tm, k), _const((k, d)), _row(tm, d), _const((1, d))],
                  out_specs=[_row(tm, d), _row(tm, d)],
                  out_shape=[jax.ShapeDtypeStruct((t, d), F32), jax.ShapeDtypeStruct((t, d), F32)],
                  compiler_params=_params())(a, w, h, g)


def _rmsbwd_matmul_nt(x, g, dy, w, name):
    t, d = x.shape
    n = w.shape[0]
    tm, _ = _tiles(t)
    cn = _col_chunk(n)

    def body(x_ref, g_ref, dy_ref, w_ref, da_ref, dxb_ref, dg_ref):
        @pl.when(pl.program_id(0) == 0)
        def _():
            dg_ref[...] = jnp.zeros((1, d), F32)

        dy = dy_ref[...]
        dx, xr = _rms_bwd(x_ref[...], g_ref[...], dy)
        dg_ref[...] += jnp.sum(dy * xr, axis=0, keepdims=True)
        dxb = dx.astype(BF16)
        dxb_ref[...] = dxb
        for c in range(n // cn):
            da_ref[:, c * cn:(c + 1) * cn] = lax.dot_general(dxb, w_ref[c * cn:(c + 1) * cn, :], NT_DIMS,
                                                             preferred_element_type=F32)

    return _pcall(body, name=name, grid=(t // tm,),
                  in_specs=[_row(tm, d), _const((1, d)), _row(tm, d), _const((n, d))],
                  out_specs=[_row(tm, n), _row(tm, d), _const_out((1, d))],
                  out_shape=[jax.ShapeDtypeStruct((t, n), F32), jax.ShapeDtypeStruct((t, d), BF16),
                             jax.ShapeDtypeStruct((1, d), F32)],
                  compiler_params=_params())(x, g, dy, w)


def _matmul_nt_rmsbwd_res(dp, w, h, g, dh, name):
    t, n = dp.shape
    d = w.shape[0]
    tm, _ = _tiles(t)

    def body(dp_ref, w_ref, h_ref, g_ref, dh_ref, out_ref, dg_ref):
        @pl.when(pl.program_id(0) == 0)
        def _():
            dg_ref[...] = jnp.zeros((1, d), F32)

        dz = lax.dot_general(dp_ref[...], w_ref[...], NT_DIMS, preferred_element_type=F32)
        dx, xr = _rms_bwd(h_ref[...], g_ref[...], dz)
        dg_ref[...] += jnp.sum(dz * xr, axis=0, keepdims=True)
        out_ref[...] = dh_ref[...] + dx

    return _pcall(body, name=name, grid=(t // tm,),
                  in_specs=[_row(tm, n), _const((d, n)), _row(tm, d), _const((1, d)), _row(tm, d)],
                  out_specs=[_row(tm, d), _const_out((1, d))],
                  out_shape=[jax.ShapeDtypeStruct((t, d), F32), jax.ShapeDtypeStruct((1, d), F32)],
                  compiler_params=_params())(dp, w, h, g, dh)


def _matmul_tn(a, b, name):
    t, k = a.shape
    n = b.shape[1]
    tm, _ = _tiles(t)
    nt = t // tm
    bn = min(n, (1536 * 1024) // k)
    assert n % bn == 0 and bn % LANES == 0

    def body(a_ref, b_ref, o_ref, acc):
        @pl.when(pl.program_id(1) == 0)
        def _():
            acc[...] = jnp.zeros((k, bn), F32)

        acc[...] += lax.dot_general(a_ref[...], b_ref[...], TN_DIMS, preferred_element_type=F32)

        @pl.when(pl.program_id(1) == nt - 1)
        def _():
            o_ref[...] = acc[...].astype(BF16)

    return _pcall(body, name=name, grid=(n // bn, nt),
                  in_specs=[pl.BlockSpec((tm, k), lambda j, i: (i, 0)), pl.BlockSpec((tm, bn), lambda j, i: (i, j))],
                  out_specs=pl.BlockSpec((k, bn), lambda j, i: (0, j)),
                  out_shape=jax.ShapeDtypeStruct((k, n), BF16),
                  scratch_shapes=[pltpu.VMEM((k, bn), F32)],
                  compiler_params=_params(2))(a, b)


MIX_PARAMS = ['lru_conv_w', 'lru_conv_b', 'lru_wa', 'lru_ba', 'lru_wx', 'lru_bx', 'lru_lambda', 'conv_w', 'conv_b',
              'conv_ln_g', 'conv_ln_b', 'g_out_lru', 'g_out_conv']


def _head_gates(xcb, wa_ref, wx_ref, pa_s, px_s, ba, bx, heads):
    for hd in range(heads):
        sl = slice(LANES * hd, LANES * (hd + 1))
        pa_s[:, sl] = jnp.dot(xcb[:, sl], wa_ref[hd], preferred_element_type=F32) + ba[:, sl]
        px_s[:, sl] = jnp.dot(xcb[:, sl], wx_ref[hd], preferred_element_type=F32) + bx[:, sl]


def _lru_conv_chunk(exta, cw4_ref, cb4, r0):
    win = exta[pl.ds(r0, 16), :]
    acc = cw4_ref[3:4, :] * win[8:16]
    for k in range(LRU_K - 1):
        acc = acc + cw4_ref[k:k + 1, :] * pltpu.roll(win, LRU_K - 1 - k, 0)[8:16]
    return acc + cb4


def _mix_fwd(proj, p, name):
    t = proj.shape[0]
    w = p['lru_conv_b'].shape[1]
    cw = p['conv_b'].shape[1]
    heads = p['lru_wa'].shape[0]
    _, tm = _tiles(t)
    nch = tm // 8

    def body(proj_ref, cw4_ref, cb4_ref, wa_ref, ba_ref, wx_ref, bx_ref, lam_ref, cw31_ref, cb31_ref, lng_ref,
             lnb_ref, ga_ref, gb_ref, y_ref, hs_ref, c1_ref, exta, xc_s, pa_s, px_s, extb, nbuf, hcar):
        @pl.when(pl.program_id(0) == 0)
        def _():
            exta[0:8, :] = jnp.zeros((8, w), F32)
            extb[0:32, :] = jnp.zeros((32, cw), F32)
            hcar[...] = jnp.zeros((8, w), F32)

        exta[8:8 + tm, :] = proj_ref[:, 0:w]
        cb4 = cb4_ref[...]

        def conv_a(c, carry):
            r0 = pl.multiple_of(c * 8, 8)
            xc_s[pl.ds(r0, 8), :] = _lru_conv_chunk(exta, cw4_ref, cb4, r0)
            return carry

        lax.fori_loop(0, nch, conv_a, 0, unroll=3)
        _head_gates(xc_s[...].astype(BF16), wa_ref, wx_ref, pa_s, px_s, ba_ref[...], bx_ref[...], heads)

        sp = _softplus(-lam_ref[...])
        ga = ga_ref[...]
        row = lax.broadcasted_iota(jnp.int32, (8, w), 0)

        def scan_c(cg, hprev):
            part = []
            for u in range(SCAN_U):
                r0 = pl.multiple_of((cg * SCAN_U + u) * 8, 8)
                xc = xc_s[pl.ds(r0, 8), :]
                _, ig, a, m = _lru_gates(pa_s[pl.ds(r0, 8), :], px_s[pl.ds(r0, 8), :], sp)
                aa, bb = a, m * (ig * xc)
                for d in (1, 2, 4):
                    a_s = pltpu.roll(aa, d, 0)
                    b_s = pltpu.roll(bb, d, 0)
                    msk = row >= d
                    bb = jnp.where(msk, aa * b_s + bb, bb)
                    aa = jnp.where(msk, aa * a_s, aa)
                part.append((r0, aa, bb, _gelu(proj_ref[pl.ds(r0, 8), w:2 * w])))
            for r0, aa, bb, ge in part:
                hs = aa * hprev + bb
                hs_ref[pl.ds(r0, 8), :] = hs
                ya = hs * ge
                nbuf[pl.ds(r0, 8), 0:w] = (ya * _rms(ya)) * ga
                hprev = hs[7:8, :]
            return hprev

        hcar[0:1, :] = lax.fori_loop(0, nch // SCAN_U, scan_c, hcar[0:1, :])

        extb[32:32 + tm, :] = proj_ref[:, 2 * w:2 * w + cw] * _sigmoid(proj_ref[:, 2 * w + cw:2 * w + 2 * cw])

        def conv_b(c, carry):
            r0 = pl.multiple_of(c * 8, 8)
            ybs = []
            ssq = jnp.zeros((8, 1), F32)
            for lb in range(cw // LANES):
                sl = slice(LANES * lb, LANES * (lb + 1))
                win = extb[pl.ds(r0, 40), sl]
                rolled = [win] + [pltpu.roll(win, rr, 0) for rr in range(1, 8)]
                parts = [None] * 4
                for k in range(CONV_K):
                    q, rr = divmod(CONV_K - 1 - k, 8)
                    term = cw31_ref[k:k + 1, sl] * rolled[rr][32 - 8 * q:40 - 8 * q]
                    parts[k % 4] = term if parts[k % 4] is None else parts[k % 4] + term
                acc = ((parts[0] + parts[1]) + (parts[2] + parts[3])) + cb31_ref[:, sl]
                c1_ref[pl.ds(r0, 8), sl] = acc
                dlt = acc - jnp.mean(acc, axis=-1, keepdims=True)
                c2 = dlt * lax.rsqrt(jnp.mean(dlt * dlt, axis=-1, keepdims=True) + EPS)
                yb0 = c2 * lng_ref[:, sl] + lnb_ref[:, sl]
                yb = yb0 * _sigmoid(yb0)
                ybs.append(yb)
                ssq = ssq + jnp.sum(yb * yb, axis=-1, keepdims=True)
            rb = lax.rsqrt(ssq / cw + EPS)
            for lb in range(cw // LANES):
                sl = slice(LANES * lb, LANES * (lb + 1))
                nbuf[pl.ds(r0, 8), w + LANES * lb:w + LANES * (lb + 1)] = (ybs[lb] * rb) * gb_ref[:, sl]
            return carry

        lax.fori_loop(0, nch, conv_b, 0, unroll=6)
        exta[0:8, :] = exta[tm:tm + 8, :]
        extb[0:32, :] = extb[tm:tm + 32, :]
        y_ref[...] = nbuf[...].astype(BF16)

    consts = [p[k] for k in MIX_PARAMS]
    return _pcall(body, name=name, grid=(t // tm,),
                  in_specs=[_row(tm, 2 * w + 2 * cw)] + [_const(c.shape) for c in consts],
                  out_specs=[_row(tm, w + cw), _row(tm, w), _row(tm, cw)],
                  out_shape=[jax.ShapeDtypeStruct((t, w + cw), BF16), jax.ShapeDtypeStruct((t, w), F32),
                             jax.ShapeDtypeStruct((t, cw), F32)],
                  scratch_shapes=[pltpu.VMEM((8 + tm, w), F32), pltpu.VMEM((tm, w), F32), pltpu.VMEM((tm, w), F32),
                                  pltpu.VMEM((tm, w), F32), pltpu.VMEM((32 + tm, cw), F32),
                                  pltpu.VMEM((tm, w + cw), F32), pltpu.VMEM((8, w), F32)],
                  compiler_params=_params())(proj, *consts)


def _mix_bwd(dy, proj, hs, c1, p, name):
    t = proj.shape[0]
    w = p['lru_conv_b'].shape[1]
    cw = p['conv_b'].shape[1]
    heads = p['lru_wa'].shape[0]
    _, tm = _tiles(t)
    nt = t // tm
    nch = tm // 8
    nlb = cw // LANES
    G_CB4, G_CW4, G_BA, G_BX, G_SP, G_GA, NGW = 0, 1, 5, 6, 7, 8, 9
    G_CB31, G_LNG, G_LNB, G_GB, G_CW31, NGC = 0, 1, 2, 3, 4, 4 + CONV_K

    def body(dy_ref, proj_ref, projh_ref, hs_ref, hsh_ref, c1_ref, cw4_ref, cb4_ref, wa_ref, ba_ref, wx_ref, bx_ref,
             lam_ref, cw31_ref, cb31_ref, lng_ref, lnb_ref, ga_ref, gb_ref,
             dproj_ref, dcw4_ref, dcb4_ref, dwa_ref, dba_ref, dwx_ref, dbx_ref, dlam_ref, dcw31_ref, dcb31_ref,
             dlng_ref, dlnb_ref, dga_ref, dgb_ref,
             exta, exth, xc_s, pa_s, px_s, dpa_s, dpx_s, dxce, extb, dc1e, dpf, cp_s, acc_w, acc_c):
        i = pl.program_id(0)

        @pl.when(i == 0)
        def _():
            acc_w[...] = jnp.zeros((8 * NGW, w), F32)
            acc_c[...] = jnp.zeros((8 * NGC, cw), F32)
            dwa_ref[...] = jnp.zeros(dwa_ref.shape, F32)
            dwx_ref[...] = jnp.zeros(dwx_ref.shape, F32)
            cp_s[...] = jnp.zeros((8, w), F32)
            dxce[tm:tm + 8, :] = jnp.zeros((8, w), F32)
            dc1e[tm:tm + 32, :] = jnp.zeros((32, cw), F32)

        nf = jnp.where(i < nt - 1, 1.0, 0.0).astype(F32)
        exta[0:8, :] = projh_ref[40:48, 0:w] * nf
        exta[8:8 + tm, :] = proj_ref[:, 0:w]
        exth[0:8, :] = hsh_ref[...] * nf
        exth[8:8 + tm, :] = hs_ref[...]
        extb[0:48, :] = (projh_ref[:, 2 * w:2 * w + cw] * _sigmoid(projh_ref[:, 2 * w + cw:2 * w + 2 * cw])) * nf
        extb[48:48 + tm, :] = proj_ref[:, 2 * w:2 * w + cw] * _sigmoid(proj_ref[:, 2 * w + cw:2 * w + 2 * cw])
        cb4 = cb4_ref[...]

        def conv_a(c, carry):
            r0 = pl.multiple_of(c * 8, 8)
            xc_s[pl.ds(r0, 8), :] = _lru_conv_chunk(exta, cw4_ref, cb4, r0)
            return carry

        lax.fori_loop(0, nch, conv_a, 0, unroll=3)
        xcb = xc_s[...].astype(BF16)
        _head_gates(xcb, wa_ref, wx_ref, pa_s, px_s, ba_ref[...], bx_ref[...], heads)

        sp = _softplus(-lam_ref[...])
        ga = ga_ref[...]
        row = lax.broadcasted_iota(jnp.int32, (8, w), 0)

        def acc_add(ref, g, val, sl=slice(None)):
            ref[8 * g:8 * g + 8, sl] = ref[8 * g:8 * g + 8, sl] + val

        def rscan(cg, cp):
            part = []
            for u in range(RSCAN_U):
                r0 = pl.multiple_of((nch - 1 - (cg * RSCAN_U + u)) * 8, 8)
                xc = xc_s[pl.ds(r0, 8), :]
                r, ig, a, m = _lru_gates(pa_s[pl.ds(r0, 8), :], px_s[pl.ds(r0, 8), :], sp)
                hwin = exth[pl.ds(r0, 16), :]
                hcur = hwin[8:16]
                hprev = pltpu.roll(hwin, 1, 0)[8:16]
                ge, dge = _gelu_and_grad(proj_ref[pl.ds(r0, 8), w:2 * w])
                dna = dy_ref[pl.ds(r0, 8), 0:w]
                dya, yar = _rms_bwd(hcur * ge, ga, dna)
                acc_add(acc_w, G_GA, dna * yar)
                dpf[pl.ds(r0, 8), w:2 * w] = (dya * hcur) * dge
                aa = jnp.where(row == 7, 1.0, pltpu.roll(a, 7, 0))
                bb = dya * ge
                for d in (1, 2, 4):
                    a_s = pltpu.roll(aa, 8 - d, 0)
                    b_s = pltpu.roll(bb, 8 - d, 0)
                    msk = row < 8 - d
                    bb = jnp.where(msk, aa * b_s + bb, bb)
                    aa = jnp.where(msk, aa * a_s, aa)
                part.append((r0, aa, bb, xc, r, ig, a, m, hprev))
            for r0, aa, bb, xc, r, ig, a, m, hprev in part:
                lamb = bb + aa * cp
                cp = a[0:1, :] * lamb[0:1, :]
                dm = lamb * (ig * xc)
                di = lamb * (m * xc)
                dxce[pl.ds(r0, 8), :] = lamb * (m * ig)
                dla = a * (lamb * hprev - dm * (a / m))
                acc_add(acc_w, G_SP, dla * (-LRU_C * r))
                dpa = (dla * (-LRU_C * sp)) * (r * (1.0 - r))
                dpx = di * (ig * (1.0 - ig))
                acc_add(acc_w, G_BA, dpa)
                acc_add(acc_w, G_BX, dpx)
                dpa_s[pl.ds(r0, 8), :] = dpa
                dpx_s[pl.ds(r0, 8), :] = dpx
            return cp

        cp_s[0:1, :] = lax.fori_loop(0, nch // RSCAN_U, rscan, cp_s[0:1, :])

        dpab = dpa_s[...].astype(BF16)
        dpxb = dpx_s[...].astype(BF16)
        for hd in range(heads):
            sl = slice(LANES * hd, LANES * (hd + 1))
            dxce[0:tm, sl] = (dxce[0:tm, sl]
                              + lax.dot_general(dpab[:, sl], wa_ref[hd], NT_DIMS, preferred_element_type=F32)
                              + lax.dot_general(dpxb[:, sl], wx_ref[hd], NT_DIMS, preferred_element_type=F32))
            dwa_ref[hd] = dwa_ref[hd] + lax.dot_general(xcb[:, sl], dpab[:, sl], TN_DIMS, preferred_element_type=F32)
            dwx_ref[hd] = dwx_ref[hd] + lax.dot_general(xcb[:, sl], dpxb[:, sl], TN_DIMS, preferred_element_type=F32)

        def conv_a_bwd(c, carry):
            r0 = pl.multiple_of(c * 8, 8)
            win = dxce[pl.ds(r0, 16), :]
            dxc = win[0:8]
            xwin = exta[pl.ds(r0, 16), :]
            dxl = cw4_ref[3:4, :] * dxc
            acc_add(acc_w, G_CB4, dxc)
            acc_add(acc_w, G_CW4 + 3, dxc * xwin[8:16])
            for k in range(LRU_K - 1):
                s = LRU_K - 1 - k
                dxl = dxl + cw4_ref[k:k + 1, :] * pltpu.roll(win, 16 - s, 0)[0:8]
                acc_add(acc_w, G_CW4 + k, dxc * pltpu.roll(xwin, s, 0)[8:16])
            dpf[pl.ds(r0, 8), 0:w] = dxl
            return carry

        lax.fori_loop(0, nch, conv_a_bwd, 0, unroll=3)
        dxce[tm:tm + 8, :] = dxce[0:8, :]

        def mixb(c, carry):
            r0 = pl.multiple_of(c * 8, 8)
            st = []
            ssq = jnp.zeros((8, 1), F32)
            for lb in range(nlb):
                sl = slice(LANES * lb, LANES * (lb + 1))
                c1v = c1_ref[pl.ds(r0, 8), sl]
                dlt = c1v - jnp.mean(c1v, axis=-1, keepdims=True)
                rs = lax.rsqrt(jnp.mean(dlt * dlt, axis=-1, keepdims=True) + EPS)
                c2 = dlt * rs
                yb0 = c2 * lng_ref[:, sl] + lnb_ref[:, sl]
                sg = _sigmoid(yb0)
                yb = yb0 * sg
                ssq = ssq + jnp.sum(yb * yb, axis=-1, keepdims=True)
                st.append((rs, c2, yb0, sg, yb))
            rb = lax.rsqrt(ssq / cw + EPS)
            tsum = jnp.zeros((8, 1), F32)
            dngs = []
            for lb in range(nlb):
                sl = slice(LANES * lb, LANES * (lb + 1))
                dnb = dy_ref[pl.ds(r0, 8), w + LANES * lb:w + LANES * (lb + 1)]
                ybr = st[lb][4] * rb
                acc_add(acc_c, G_GB, dnb * ybr, sl)
                dng = dnb * gb_ref[:, sl]
                dngs.append((dng, ybr))
                tsum = tsum + jnp.sum(dng * ybr, axis=-1, keepdims=True)
            tsum = tsum / cw
            for lb in range(nlb):
                sl = slice(LANES * lb, LANES * (lb + 1))
                rs, c2, yb0, sg, _ = st[lb]
                dng, ybr = dngs[lb]
                dyb0 = (rb * (dng - ybr * tsum)) * (sg * (1.0 + yb0 * (1.0 - sg)))
                acc_add(acc_c, G_LNG, dyb0 * c2, sl)
                acc_add(acc_c, G_LNB, dyb0, sl)
                dc2 = dyb0 * lng_ref[:, sl]
                dc1 = rs * (dc2 - jnp.mean(dc2, axis=-1, keepdims=True)
                            - c2 * jnp.mean(dc2 * c2, axis=-1, keepdims=True))
                acc_add(acc_c, G_CB31, dc1, sl)
                dc1e[pl.ds(r0, 8), sl] = dc1
            return carry

        lax.fori_loop(0, nch, mixb, 0, unroll=6)

        def conv_b_bwd(c, carry):
            r0 = pl.multiple_of(c * 8, 8)
            for lb in range(nlb):
                sl = slice(LANES * lb, LANES * (lb + 1))
                win = dc1e[pl.ds(r0, 40), sl]
                ups = [win] + [pltpu.roll(win, 40 - rr, 0) for rr in range(1, 8)]
                dc1 = win[0:8]
                parts = [None] * 4
                for k in range(CONV_K):
                    q, rr = divmod(CONV_K - 1 - k, 8)
                    term = cw31_ref[k:k + 1, sl] * ups[rr][8 * q:8 * q + 8]
                    parts[k % 4] = term if parts[k % 4] is None else parts[k % 4] + term
                dc0 = (parts[0] + parts[1]) + (parts[2] + parts[3])
                cav = proj_ref[pl.ds(r0, 8), 2 * w + LANES * lb:2 * w + LANES * (lb + 1)]
                sg = _sigmoid(proj_ref[pl.ds(r0, 8), 2 * w + cw + LANES * lb:2 * w + cw + LANES * (lb + 1)])
                dpf[pl.ds(r0, 8), 2 * w + LANES * lb:2 * w + LANES * (lb + 1)] = dc0 * sg
                dpf[pl.ds(r0, 8), 2 * w + cw + LANES * lb:2 * w + cw + LANES * (lb + 1)] = (dc0 * cav) * (sg * (1.0 - sg))
                xwin = extb[pl.ds(pl.multiple_of(r0 + 16, 8), 40), sl]
                xr = [xwin] + [pltpu.roll(xwin, rr, 0) for rr in range(1, 8)]
                for k in range(CONV_K):
                    q, rr = divmod(CONV_K - 1 - k, 8)
                    acc_add(acc_c, G_CW31 + k, dc1 * xr[rr][32 - 8 * q:40 - 8 * q], sl)
            return carry

        lax.fori_loop(0, nch, conv_b_bwd, 0, unroll=3)
        dc1e[tm:tm + 32, :] = dc1e[0:32, :]
        dproj_ref[...] = dpf[...].astype(BF16)

        @pl.when(i == nt - 1)
        def _():
            def fold(ref, g):
                return jnp.sum(ref[8 * g:8 * g + 8, :], axis=0, keepdims=True)

            dcb4_ref[...] = fold(acc_w, G_CB4)
            for k in range(LRU_K):
                dcw4_ref[k:k + 1, :] = fold(acc_w, G_CW4 + k)
            dba_ref[...] = fold(acc_w, G_BA)
            dbx_ref[...] = fold(acc_w, G_BX)
            dlam_ref[...] = fold(acc_w, G_SP) * (-_sigmoid(-lam_ref[...]))
            dga_ref[...] = fold(acc_w, G_GA)
            dcb31_ref[...] = fold(acc_c, G_CB31)
            dlng_ref[...] = fold(acc_c, G_LNG)
            dlnb_ref[...] = fold(acc_c, G_LNB)
            dgb_ref[...] = fold(acc_c, G_GB)
            for k in range(CONV_K):
                dcw31_ref[k:k + 1, :] = fold(acc_c, G_CW31 + k)

    consts = [p[k] for k in MIX_PARAMS]
    outs = _pcall(body, name=name, grid=(nt,),
                  in_specs=[_rrow(tm, w + cw, nt), _rrow(tm, 2 * w + 2 * cw, nt), _halo(48, 2 * w + 2 * cw, tm, nt),
                            _rrow(tm, w, nt), _halo(8, w, tm, nt), _rrow(tm, cw, nt)] + [_const(c.shape) for c in consts],
                  out_specs=[_rrow(tm, 2 * w + 2 * cw, nt)] + [_const_out(c.shape) for c in consts],
                  out_shape=[jax.ShapeDtypeStruct((t, 2 * w + 2 * cw), BF16)]
                  + [jax.ShapeDtypeStruct(c.shape, F32) for c in consts],
                  scratch_shapes=[pltpu.VMEM((8 + tm, w), F32), pltpu.VMEM((8 + tm, w), F32), pltpu.VMEM((tm, w), F32),
                                  pltpu.VMEM((tm, w), F32), pltpu.VMEM((tm, w), F32), pltpu.VMEM((tm, w), F32),
                                  pltpu.VMEM((tm, w), F32), pltpu.VMEM((tm + 8, w), F32),
                                  pltpu.VMEM((48 + tm, cw), F32), pltpu.VMEM((tm + 32, cw), F32),
                                  pltpu.VMEM((tm, 2 * w + 2 * cw), F32), pltpu.VMEM((8, w), F32),
                                  pltpu.VMEM((8 * NGW, w), F32), pltpu.VMEM((8 * NGC, cw), F32)],
                  compiler_params=_params())(dy, proj, proj, hs, hs, c1, *consts)
    return outs[0], dict(zip(MIX_PARAMS, outs[1:]))


def _ffn_window(u_ref, halo, c, col):
    if isinstance(c, int) and c == 0:
        return jnp.concatenate([halo[:, col:col + LANES], u_ref[0:16, col:col + LANES]], axis=0)
    return u_ref[pl.ds(pl.multiple_of(c * 16 - 8, 8), 24), col:col + LANES]


def _ffn_conv(win, w3_ref, b3_ref, col):
    sl = slice(col, col + LANES)
    x1 = pltpu.roll(win, 1, 0)[8:24]
    x2 = pltpu.roll(win, 2, 0)[8:24]
    u = w3_ref[2:3, sl] * win[8:24] + w3_ref[1:2, sl] * x1 + w3_ref[0:1, sl] * x2 + b3_ref[:, sl]
    return u, (x2, x1, win[8:24])


def _ffn_act_fwd(u0, w3, b3, name):
    t, f2 = u0.shape
    ff = f2 // 2
    _, tm = _tiles(t)
    nch = tm // 16

    def body(u_ref, w3_ref, b3_ref, act_ref, car):
        @pl.when(pl.program_id(0) == 0)
        def _():
            car[...] = jnp.zeros((8, f2), F32)

        def chunk(c):
            halo = car[...] if isinstance(c, int) else None
            r0 = 0 if isinstance(c, int) else pl.multiple_of(c * 16, 16)
            for j in range(ff // LANES):
                gate, _ = _ffn_conv(_ffn_window(u_ref, halo, c, LANES * j), w3_ref, b3_ref, LANES * j)
                up, _ = _ffn_conv(_ffn_window(u_ref, halo, c, ff + LANES * j), w3_ref, b3_ref, ff + LANES * j)
                act_ref[pl.ds(r0, 16), LANES * j:LANES * (j + 1)] = (_gelu(gate) * up).astype(BF16)

        chunk(0)

        def loop(c, carry):
            chunk(c)
            return carry

        lax.fori_loop(1, nch, loop, 0)
        car[...] = u_ref[tm - 8:tm, :]

    return _pcall(body, name=name, grid=(t // tm,),
                  in_specs=[_row(tm, f2), _const(w3.shape), _const(b3.shape)],
                  out_specs=_row(tm, ff), out_shape=jax.ShapeDtypeStruct((t, ff), BF16),
                  scratch_shapes=[pltpu.VMEM((8, f2), F32)],
                  compiler_params=_params())(u0, w3, b3)


def _ffn_act_bwd(dact, u0, w3, b3, name):
    t, f2 = u0.shape
    ff = f2 // 2
    _, tm = _tiles(t)
    nt = t // tm
    nch = tm // 16

    def body(dact_ref, u_ref, uh_ref, w3_ref, b3_ref, du0_ref, dw3_ref, db3_ref, dub, acc):
        i = pl.program_id(0)

        @pl.when(i == 0)
        def _():
            dub[tm:tm + 8, :] = jnp.zeros((8, f2), F32)
            acc[...] = jnp.zeros((32, f2), F32)

        nf = jnp.where(i < nt - 1, 1.0, 0.0).astype(F32)

        def acc_add(g, val, sl):
            acc[8 * g:8 * g + 8, sl] = acc[8 * g:8 * g + 8, sl] + (val[0:8] + val[8:16])

        def chunk(c):
            halo = uh_ref[...] * nf if isinstance(c, int) else None
            r0 = 0 if isinstance(c, int) else pl.multiple_of(c * 16, 16)
            for j in range(ff // LANES):
                cg, cu = LANES * j, ff + LANES * j
                gate, xg = _ffn_conv(_ffn_window(u_ref, halo, c, cg), w3_ref, b3_ref, cg)
                up, xu = _ffn_conv(_ffn_window(u_ref, halo, c, cu), w3_ref, b3_ref, cu)
                ge, dge = _gelu_and_grad(gate)
                da = dact_ref[pl.ds(r0, 16), cg:cg + LANES]
                for col, du, xs in ((cg, (da * up) * dge, xg), (cu, da * ge, xu)):
                    sl = slice(col, col + LANES)
                    dub[pl.ds(r0, 16), sl] = du
                    acc_add(0, du, sl)
                    for k in range(FFN_K):
                        acc_add(1 + k, du * xs[k], sl)

        chunk(0)

        def loop1(c, carry):
            chunk(c)
            return carry

        lax.fori_loop(1, nch, loop1, 0)

        def loop2(c, carry):
            r0 = pl.multiple_of(c * 16, 16)
            for j in range(f2 // LANES):
                sl = slice(LANES * j, LANES * (j + 1))
                win = dub[pl.ds(r0, 24), sl]
                du0 = (w3_ref[2:3, sl] * win[0:16] + w3_ref[1:2, sl] * pltpu.roll(win, 23, 0)[0:16]
                       + w3_ref[0:1, sl] * pltpu.roll(win, 22, 0)[0:16])
                du0_ref[pl.ds(r0, 16), sl] = du0.astype(BF16)
            return carry

        lax.fori_loop(0, nch, loop2, 0)
        dub[tm:tm + 8, :] = dub[0:8, :]

        @pl.when(i == nt - 1)
        def _():
            db3_ref[...] = jnp.sum(acc[0:8, :], axis=0, keepdims=True)
            for k in range(FFN_K):
                dw3_ref[k:k + 1, :] = jnp.sum(acc[8 + 8 * k:16 + 8 * k, :], axis=0, keepdims=True)

    return _pcall(body, name=name, grid=(nt,),
                  in_specs=[_rrow(tm, ff, nt), _rrow(tm, f2, nt), _halo(8, f2, tm, nt), _const(w3.shape),
                            _const(b3.shape)],
                  out_specs=[_rrow(tm, f2, nt), _const_out(w3.shape), _const_out(b3.shape)],
                  out_shape=[jax.ShapeDtypeStruct((t, f2), BF16), jax.ShapeDtypeStruct(w3.shape, F32),
                             jax.ShapeDtypeStruct(b3.shape, F32)],
                  scratch_shapes=[pltpu.VMEM((tm + 8, f2), F32), pltpu.VMEM((32, f2), F32)],
                  compiler_params=_params())(dact, u0, u0, w3, b3)


def _loss_head(h, tgt, name):
    t, d = h.shape
    tm, _ = _tiles(t)

    def body(h_ref, t_ref, dh_ref, s_ref):
        i = pl.program_id(0)

        @pl.when(i == 0)
        def _():
            s_ref[...] = jnp.zeros((1, d), F32)

        row = lax.broadcasted_iota(jnp.int32, (tm, d), 0) + i * tm
        err = jnp.where(row >= N_META, h_ref[...] - t_ref[...], 0.0)
        dh_ref[...] = err / d
        s_ref[...] += jnp.sum(err * err, axis=0, keepdims=True)

    return _pcall(body, name=name, grid=(t // tm,), in_specs=[_row(tm, d), _row(tm, d)],
                  out_specs=[_row(tm, d), _const_out((1, d))],
                  out_shape=[jax.ShapeDtypeStruct((t, d), F32), jax.ShapeDtypeStruct((1, d), F32)],
                  compiler_params=_params())(h, tgt)


def _row_tile(rows, row_bytes, budget):
    best = None
    for tr in range(16, rows + 1, 16):
        if rows % tr == 0 and tr * row_bytes <= budget:
            best = tr
    assert best is not None, (rows, row_bytes)
    return best


def _cast_bf16(a, name):
    r, c = a.shape
    tr = _row_tile(r, c * 4, 4 << 20)

    def body(a_ref, o_ref):
        o_ref[...] = a_ref[...].astype(BF16)

    return _pcall(body, name=name, grid=(r // tr,), in_specs=[_row(tr, c)], out_specs=_row(tr, c),
                  out_shape=jax.ShapeDtypeStruct((r, c), BF16), compiler_params=_params())(a)


def _sum_slots(r, name):
    s, rows, c = r.shape
    tr = _row_tile(rows, s * c * 4, 8 << 20)

    def body(r_ref, o_ref):
        acc = r_ref[0].astype(F32)
        for k in range(1, s):
            acc = acc + r_ref[k].astype(F32)
        o_ref[...] = acc

    return _pcall(body, name=name, grid=(rows // tr,),
                  in_specs=[pl.BlockSpec((s, tr, c), lambda i: (0, i, 0))], out_specs=_row(tr, c),
                  out_shape=jax.ShapeDtypeStruct((rows, c), F32), compiler_params=_params())(r)


def _adamw(g, w, m, v, name):
    r, c = g.shape
    tr = _row_tile(r, c * 4, 1 << 20)

    def body(g_ref, w_ref, m_ref, v_ref, d_ref, m2_ref, v2_ref):
        gv = g_ref[...]
        m2 = ADAM_B1 * m_ref[...] + (1.0 - ADAM_B1) * gv
        v2 = ADAM_B2 * v_ref[...] + (1.0 - ADAM_B2) * (gv * gv)
        m_hat = m2 / (1.0 - ADAM_B1 ** ADAM_STEP)
        v_hat = v2 / (1.0 - ADAM_B2 ** ADAM_STEP)
        d_ref[...] = -ADAM_LR * (m_hat / (jnp.sqrt(v_hat) + ADAM_EPS) + ADAM_WD * w_ref[...])
        m2_ref[...] = m2
        v2_ref[...] = v2

    return _pcall(body, name=name, grid=(r // tr,), in_specs=[_row(tr, c)] * 4, out_specs=[_row(tr, c)] * 3,
                  out_shape=[jax.ShapeDtypeStruct((r, c), F32)] * 3, compiler_params=_params())(g, w, m, v)


ANY = pl.BlockSpec(memory_space=pl.ANY)


def _coords():
    return lax.axis_index("x"), lax.axis_index("y"), lax.axis_index("c")


def _window(ref, lead, axis, k, width):
    idx = [slice(None)] * len(ref.shape)
    idx[0] = lead
    idx[axis] = pl.ds(pl.multiple_of(k * width, LANES if axis == len(ref.shape) - 1 else 8), width)
    return ref.at[tuple(idx)]


def _gather_xy(arrs, axes, name):
    n = len(arrs)
    out_shape = []
    for a, ax in zip(arrs, axes):
        s = list(a.shape)
        s[ax] *= 4
        out_shape.append(jax.ShapeDtypeStruct(tuple(s), a.dtype))

    def body(*refs):
        ins, outs = refs[:n], refs[n:2 * n]
        send_sems, recv_sems, loc_sems = refs[2 * n:]
        x, y, c = _coords()
        k_me = 2 * x + y
        chips = [(1 - x, y), (x, 1 - y), (1 - x, 1 - y)]
        sib = (x, y, 1 - c)

        def half(i, which):
            hl = arrs[i].shape[0] // 2
            return pl.ds(which * hl, hl)

        def win(i, kk, which):
            return _window(outs[i], half(i, which), axes[i], kk, arrs[i].shape[axes[i]])

        def copy(i, s, src, dst, to):
            return pltpu.make_async_remote_copy(src_ref=src, dst_ref=dst, send_sem=send_sems.at[i, s],
                                                recv_sem=recv_sems.at[i, s], device_id=to, device_id_type=MESH_T)

        locs = []
        for i in range(n):
            lc = pltpu.make_async_copy(ins[i], _window(outs[i], slice(None), axes[i], k_me, arrs[i].shape[axes[i]]),
                                       loc_sems.at[i])
            lc.start()
            locs.append(lc)
        started = []
        for i in range(n):
            for j, chip in enumerate(chips):
                cp = copy(i, j, ins[i].at[half(i, c)], win(i, k_me, c), (*chip, c))
                cp.start()
                started.append(cp)
        for i in range(n):
            for j, chip in enumerate(chips):
                kk = 2 * chip[0] + chip[1]
                copy(i, j, win(i, kk, c), win(i, kk, c), (*chip, c)).wait_recv()
                fw = copy(i, 3 + j, win(i, kk, c), win(i, kk, c), sib)
                fw.start()
                started.append(fw)
        for i in range(n):
            for j, chip in enumerate(chips):
                kk = 2 * chip[0] + chip[1]
                copy(i, 3 + j, win(i, kk, 1 - c), win(i, kk, 1 - c), sib).wait_recv()
        for cp in started:
            cp.wait_send()
        for lc in locs:
            lc.wait()

    return _pcall(body, name=name, in_specs=[ANY] * n, out_specs=[ANY] * n, out_shape=out_shape,
                  scratch_shapes=[pltpu.SemaphoreType.DMA((n, 6)), pltpu.SemaphoreType.DMA((n, 6)),
                                  pltpu.SemaphoreType.DMA((n,))],
                  compiler_params=pltpu.CompilerParams(has_side_effects=True))(*arrs)


def _peer(x, y, c, mask):
    bx, by, bc = (mask >> 2) & 1, (mask >> 1) & 1, mask & 1
    return (1 - x if bx else x, 1 - y if by else y, 1 - c if bc else c)


def _grad_scatter(grads, axes, name):
    n = len(grads)
    out_shape = []
    widths = []
    for a, ax in zip(grads, axes):
        s = list(a.shape)
        s[ax] //= 4
        s[0] //= 2
        widths.append(s[ax])
        out_shape.append(jax.ShapeDtypeStruct((8, *s), a.dtype))

    def body(*refs):
        ins, outs = refs[:n], refs[n:2 * n]
        send_sems, recv_sems, loc_sems = refs[2 * n:]
        x, y, c = _coords()

        def piece(i, px, py, pc):
            hl = grads[i].shape[0] // 2
            return _window(ins[i], pl.ds(pc * hl, hl), axes[i], 2 * px + py, widths[i])

        locs = []
        for i in range(n):
            lc = pltpu.make_async_copy(piece(i, x, y, c), outs[i].at[7], loc_sems.at[i])
            lc.start()
            locs.append(lc)
        started = []
        for i in range(n):
            for mask in range(1, 8):
                px, py, pc = _peer(x, y, c, mask)
                cp = pltpu.make_async_remote_copy(src_ref=piece(i, px, py, pc), dst_ref=outs[i].at[mask - 1],
                                                  send_sem=send_sems.at[i, mask - 1], recv_sem=recv_sems.at[i, mask - 1],
                                                  device_id=(px, py, pc), device_id_type=MESH_T)
                cp.start()
                started.append(cp)
        for cp in started:
            cp.wait()
        for lc in locs:
            lc.wait()

    return _pcall(body, name=name, in_specs=[ANY] * n, out_specs=[ANY] * n, out_shape=out_shape,
                  scratch_shapes=[pltpu.SemaphoreType.DMA((n, 7)), pltpu.SemaphoreType.DMA((n, 7)),
                                  pltpu.SemaphoreType.DMA((n,))],
                  compiler_params=pltpu.CompilerParams(has_side_effects=True))(*grads)


SWAP_CHUNKS = 8


def _sibling_swap(halves, name):
    n = len(halves)
    out_shape = [jax.ShapeDtypeStruct((2 * a.shape[0], *a.shape[1:]), a.dtype) for a in halves]

    def body(*refs):
        ins, outs = refs[:n], refs[n:2 * n]
        send_sems, recv_sems, loc_sems = refs[2 * n:]
        x, y, c = _coords()
        locs, started = [], []
        for i in range(n):
            hl, rows = halves[i].shape[0], halves[i].shape[1]
            rc = rows // SWAP_CHUNKS
            for q in range(SWAP_CHUNKS):
                src = ins[i].at[:, pl.ds(q * rc, rc)]
                dst = outs[i].at[pl.ds(c * hl, hl), pl.ds(q * rc, rc)]
                lc = pltpu.make_async_copy(src, dst, loc_sems.at[i, q])
                lc.start()
                locs.append(lc)
                cp = pltpu.make_async_remote_copy(src_ref=src, dst_ref=dst, send_sem=send_sems.at[i, q],
                                                  recv_sem=recv_sems.at[i, q], device_id=(x, y, 1 - c),
                                                  device_id_type=MESH_T)
                cp.start()
                started.append(cp)
        for cp in started:
            cp.wait()
        for lc in locs:
            lc.wait()

    for a in halves:
        assert a.shape[1] % (16 * SWAP_CHUNKS) == 0, a.shape
    return _pcall(body, name=name, in_specs=[ANY] * n, out_specs=[ANY] * n, out_shape=out_shape,
                  scratch_shapes=[pltpu.SemaphoreType.DMA((n, SWAP_CHUNKS)), pltpu.SemaphoreType.DMA((n, SWAP_CHUNKS)),
                                  pltpu.SemaphoreType.DMA((n, SWAP_CHUNKS))],
                  compiler_params=pltpu.CompilerParams(has_side_effects=True))(*halves)


def _scatter8(pk, name):
    r, cdim = pk.shape
    pr = r // 8
    assert pr % 8 == 0

    def body(p_ref, o_ref, send_sems, recv_sems, loc_sem):
        x, y, c = _coords()

        def piece(px, py, pc):
            return p_ref.at[pl.ds(pl.multiple_of((4 * px + 2 * py + pc) * pr, 8), pr)]

        lc = pltpu.make_async_copy(piece(x, y, c), o_ref.at[7], loc_sem)
        lc.start()
        started = []
        for mask in range(1, 8):
            px, py, pc = _peer(x, y, c, mask)
            cp = pltpu.make_async_remote_copy(src_ref=piece(px, py, pc), dst_ref=o_ref.at[mask - 1],
                                              send_sem=send_sems.at[mask - 1], recv_sem=recv_sems.at[mask - 1],
                                              device_id=(px, py, pc), device_id_type=MESH_T)
            cp.start()
            started.append(cp)
        for cp in started:
            cp.wait()
        lc.wait()

    return _pcall(body, name=name, in_specs=[ANY], out_specs=ANY, out_shape=jax.ShapeDtypeStruct((8, pr, cdim), F32),
                  scratch_shapes=[pltpu.SemaphoreType.DMA((7,)), pltpu.SemaphoreType.DMA((7,)),
                                  pltpu.SemaphoreType.DMA],
                  compiler_params=pltpu.CompilerParams(has_side_effects=True))(pk)


def _gather_all(pk, name):
    r, cdim = pk.shape

    def body(p_ref, o_ref, send_sems, recv_sems, loc_sem):
        x, y, c = _coords()
        lc = pltpu.make_async_copy(p_ref, o_ref.at[4 * x + 2 * y + c], loc_sem)
        lc.start()
        started = []
        for mask in range(1, 8):
            px, py, pc = _peer(x, y, c, mask)
            cp = pltpu.make_async_remote_copy(src_ref=p_ref, dst_ref=o_ref.at[4 * x + 2 * y + c],
                                              send_sem=send_sems.at[mask - 1], recv_sem=recv_sems.at[mask - 1],
                                              device_id=(px, py, pc), device_id_type=MESH_T)
            cp.start()
            started.append(cp)
        for cp in started:
            cp.wait()
        lc.wait()

    return _pcall(body, name=name, in_specs=[ANY], out_specs=ANY, out_shape=jax.ShapeDtypeStruct((8, r, cdim), F32),
                  scratch_shapes=[pltpu.SemaphoreType.DMA((7,)), pltpu.SemaphoreType.DMA((7,)),
                                  pltpu.SemaphoreType.DMA],
                  compiler_params=pltpu.CompilerParams(has_side_effects=True))(pk)


PACK_C = 1024


def _pack(arrs, row_mult):
    parts = []
    for a in arrs:
        flat = a.reshape(-1)
        parts.append(jnp.pad(flat, (0, (-flat.shape[0]) % PACK_C)))
    flat = jnp.concatenate(parts)
    flat = jnp.pad(flat, (0, (-flat.shape[0]) % (PACK_C * row_mult)))
    return flat.reshape(-1, PACK_C)


def _unpack(pk, shapes):
    flat = pk.reshape(-1)
    out, off = [], 0
    for s in shapes:
        size = 1
        for dd in s:
            size *= dd
        out.append(flat[off:off + size].reshape(s))
        off += size + (-size) % PACK_C
    return out


def _layer_params(full, l):
    p = {}
    for k in ['g_pre_mix', 'lru_conv_b', 'lru_ba', 'lru_bx', 'lru_lambda', 'conv_b', 'conv_ln_g', 'conv_ln_b', 'g_out_lru',
              'g_out_conv', 'g_post_mix', 'g_pre_ffn', 'ffn_conv_b', 'g_post_ffn']:
        p[k] = full[k][l][None, :]
    for k in ['lru_conv_w', 'conv_w', 'ffn_conv_w', 'w_in', 'w_out', 'w_up', 'w_down']:
        p[k] = full[k][l]
    p['lru_wa'] = full['lru_wa_bf'][l]
    p['lru_wx'] = full['lru_wx_bf'][l]
    return p


def _step(x, loss_target, w, m, v):
    depth = w['w_in'].shape[0]
    d = x.shape[2]
    xk, yk, _ = _coords()
    k_me = 2 * xk + yk

    big_bf = {k: _cast_bf16(w[k].reshape(-1, w[k].shape[-1]), "cast_" + k).reshape(w[k].shape) for k in BIG}
    sh_pad = [w['meta_tokens']] + [jnp.pad(w[k], ((0, 0), (0, (-w[k].shape[1]) % 8), (0, 0))) for k in SH_SMALL[1:]]
    gath = _gather_xy([big_bf[k] for k in BIG] + sh_pad, [BIG_AXIS[k] for k in BIG] + [1, 2, 2, 2], "gather_weights")
    full = dict(w)
    full.update(dict(zip(BIG, gath[:len(BIG)])))
    full['meta_tokens'] = gath[len(BIG)]
    for k, a in zip(SH_SMALL[1:], gath[len(BIG) + 1:]):
        full[k] = a[:, :w[k].shape[1]]
    for k in ('lru_wa', 'lru_wx'):
        full[k + '_bf'] = _cast_bf16(w[k].reshape(-1, LANES), "cast_" + k).reshape(w[k].shape)

    h = jnp.concatenate([full['meta_tokens'], x[0]], axis=0)
    tgt = jnp.pad(loss_target[0], ((N_META, 0), (0, 0)))
    sq, dh, gl = _fwd_bwd(h, tgt, full, depth)
    loss = lax.psum(0.5 * jnp.sum(sq) / d, ("x", "y", "c"))
    return _reduce_update(loss, dh, gl, w, m, v, depth, k_me)


def _fwd_bwd(h, tgt, full, depth):
    saved = []
    for l in range(depth):
        p = _layer_params(full, l)
        proj, zb1 = _rms_matmul(h, p['g_pre_mix'], p['w_in'], "in_proj")
        y, hs, c1 = _mix_fwd(proj, p, "mix_fwd")
        o, h1 = _matmul_rms_res(y, p['w_out'], h, p['g_post_mix'], "out_proj")
        u0, zb2 = _rms_matmul(h1, p['g_pre_ffn'], p['w_up'], "up_proj")
        act = _ffn_act_fwd(u0, p['ffn_conv_w'], p['ffn_conv_b'], "ffn_act_fwd")
        f, h2 = _matmul_rms_res(act, p['w_down'], h1, p['g_post_ffn'], "down_proj")
        saved.append((p, h, zb1, proj, y, hs, c1, o, h1, zb2, u0, act, f))
        h = h2

    dh, sq = _loss_head(h, tgt, "loss_head")

    gl = [None] * depth
    for l in reversed(range(depth)):
        p, h0, zb1, proj, y, hs, c1, o, h1, zb2, u0, act, f = saved[l]
        g = {}
        dact, dfb, g['g_post_ffn'] = _rmsbwd_matmul_nt(f, p['g_post_ffn'], dh, p['w_down'], "down_bwd")
        g['w_down'] = _matmul_tn(act, dfb, "down_dw")
        du0, g['ffn_conv_w'], g['ffn_conv_b'] = _ffn_act_bwd(dact, u0, p['ffn_conv_w'], p['ffn_conv_b'], "ffn_act_bwd")
        g['w_up'] = _matmul_tn(zb2, du0, "up_dw")
        dh1, g['g_pre_ffn'] = _matmul_nt_rmsbwd_res(du0, p['w_up'], h1, p['g_pre_ffn'], dh, "up_bwd")
        dy, dob, g['g_post_mix'] = _rmsbwd_matmul_nt(o, p['g_post_mix'], dh1, p['w_out'], "out_bwd")
        g['w_out'] = _matmul_tn(y, dob, "out_dw")
        dproj, gm = _mix_bwd(dy, proj, hs, c1, p, "mix_bwd")
        g.update(gm)
        g['w_in'] = _matmul_tn(zb1, dproj, "in_dw")
        dh, g['g_pre_mix'] = _matmul_nt_rmsbwd_res(dproj, p['w_in'], h0, p['g_pre_mix'], dh1, "in_bwd")
        gl[l] = g
    return sq, dh, gl


def _reduce_update(loss, dh, gl, w, m, v, depth, k_me):
    grad_x = dh[N_META:][None]

    def stacked(k):
        return jnp.stack([gl[l][k].reshape(w[k].shape[1:]) if k not in SH_SMALL + BIG else gl[l][k]
                          for l in range(depth)])

    big_full = [stacked(k) for k in BIG]
    slots = _grad_scatter(big_full, [BIG_AXIS[k] for k in BIG], "grad_scatter")
    halves = []
    for k, s in zip(BIG, slots):
        red = _sum_slots(s.reshape(8, -1, s.shape[-1]), "grad_sum")
        halves.append(red.reshape(s.shape[1:]))
    big_red = _sibling_swap(halves, "grad_swap")

    out = {}
    for k, gk in zip(BIG, big_red):
        c2 = gk.shape[-1]
        dl, m2, v2 = _adamw(gk.reshape(-1, c2), w[k].reshape(-1, c2), m[k].reshape(-1, c2), v[k].reshape(-1, c2),
                            "adamw_big")
        out[k] = (gk, dl.reshape(gk.shape), m2.reshape(gk.shape), v2.reshape(gk.shape))

    rep_g = [stacked(k) for k in REP_SMALL]
    sh_g = [dh[:N_META]] + [stacked(k) for k in SH_SMALL[1:]]
    n_rep_rows = _pack(rep_g, 1).shape[0]
    pk = jnp.concatenate([_pack(rep_g, 1), _pack(sh_g, 1)])
    pk = jnp.pad(pk, ((0, (-pk.shape[0]) % 256), (0, 0)))
    part = _sum_slots(_scatter8(pk, "small_scatter"), "small_sum")
    red = _gather_all(part, "small_gather").reshape(pk.shape)
    rep_red = _unpack(red[:n_rep_rows], [a.shape for a in rep_g])
    sh_red = []
    for a in _unpack(red[n_rep_rows:], [a.shape for a in sh_g]):
        wd = a.shape[-1] // 4
        sh_red.append(lax.dynamic_slice_in_dim(a, k_me * wd, wd, axis=a.ndim - 1))

    for names, grads_ in ((REP_SMALL, rep_red), (SH_SMALL, sh_red)):
        res = _adamw(_pack(grads_, 16), _pack([w[k] for k in names], 16), _pack([m[k] for k in names], 16),
                     _pack([v[k] for k in names], 16), "adamw_small")
        shapes = [w[k].shape for k in names]
        un = [_unpack(r, shapes) for r in res]
        for j, k in enumerate(names):
            out[k] = (grads_[j].reshape(w[k].shape), un[0][j], un[1][j], un[2][j])

    return (loss, grad_x, *[out[k][0] for k in WEIGHTS], *[out[k][1] for k in WEIGHTS],
            *[out[k][2] for k in WEIGHTS], *[out[k][3] for k in WEIGHTS])


def kernel(x, meta_tokens, g_pre_mix, w_in, lru_conv_w, lru_conv_b, lru_wa, lru_ba, lru_wx, lru_bx, lru_lambda, conv_w, conv_b, conv_ln_g, conv_ln_b, g_out_lru, g_out_conv, w_out, g_post_mix, g_pre_ffn, w_up, ffn_conv_w, ffn_conv_b, w_down, g_post_ffn, loss_target, m_meta_tokens, m_g_pre_mix, m_w_in, m_lru_conv_w, m_lru_conv_b, m_lru_wa, m_lru_ba, m_lru_wx, m_lru_bx, m_lru_lambda, m_conv_w, m_conv_b, m_conv_ln_g, m_conv_ln_b, m_g_out_lru, m_g_out_conv, m_w_out, m_g_post_mix, m_g_pre_ffn, m_w_up, m_ffn_conv_w, m_ffn_conv_b, m_w_down, m_g_post_ffn, v_meta_tokens, v_g_pre_mix, v_w_in, v_lru_conv_w, v_lru_conv_b, v_lru_wa, v_lru_ba, v_lru_wx, v_lru_bx, v_lru_lambda, v_conv_w, v_conv_b, v_conv_ln_g, v_conv_ln_b, v_g_out_lru, v_g_out_conv, v_w_out, v_g_post_mix, v_g_pre_ffn, v_w_up, v_ffn_conv_w, v_ffn_conv_b, v_w_down, v_g_post_ffn):
    w = dict(meta_tokens=meta_tokens, g_pre_mix=g_pre_mix, w_in=w_in, lru_conv_w=lru_conv_w, lru_conv_b=lru_conv_b,
             lru_wa=lru_wa, lru_ba=lru_ba, lru_wx=lru_wx, lru_bx=lru_bx, lru_lambda=lru_lambda, conv_w=conv_w,
             conv_b=conv_b, conv_ln_g=conv_ln_g, conv_ln_b=conv_ln_b, g_out_lru=g_out_lru, g_out_conv=g_out_conv,
             w_out=w_out, g_post_mix=g_post_mix, g_pre_ffn=g_pre_ffn, w_up=w_up, ffn_conv_w=ffn_conv_w,
             ffn_conv_b=ffn_conv_b, w_down=w_down, g_post_ffn=g_post_ffn)
    m = dict(meta_tokens=m_meta_tokens, g_pre_mix=m_g_pre_mix, w_in=m_w_in, lru_conv_w=m_lru_conv_w,
             lru_conv_b=m_lru_conv_b, lru_wa=m_lru_wa, lru_ba=m_lru_ba, lru_wx=m_lru_wx, lru_bx=m_lru_bx,
             lru_lambda=m_lru_lambda, conv_w=m_conv_w, conv_b=m_conv_b, conv_ln_g=m_conv_ln_g, conv_ln_b=m_conv_ln_b,
             g_out_lru=m_g_out_lru, g_out_conv=m_g_out_conv, w_out=m_w_out, g_post_mix=m_g_post_mix,
             g_pre_ffn=m_g_pre_ffn, w_up=m_w_up, ffn_conv_w=m_ffn_conv_w, ffn_conv_b=m_ffn_conv_b, w_down=m_w_down,
             g_post_ffn=m_g_post_ffn)
    v = dict(meta_tokens=v_meta_tokens, g_pre_mix=v_g_pre_mix, w_in=v_w_in, lru_conv_w=v_lru_conv_w,
             lru_conv_b=v_lru_conv_b, lru_wa=v_lru_wa, lru_ba=v_lru_ba, lru_wx=v_lru_wx, lru_bx=v_lru_bx,
             lru_lambda=v_lru_lambda, conv_w=v_conv_w, conv_b=v_conv_b, conv_ln_g=v_conv_ln_g, conv_ln_b=v_conv_ln_b,
             g_out_lru=v_g_out_lru, g_out_conv=v_g_out_conv, w_out=v_w_out, g_post_mix=v_g_post_mix,
             g_pre_ffn=v_g_pre_ffn, w_up=v_w_up, ffn_conv_w=v_ffn_conv_w, ffn_conv_b=v_ffn_conv_b, w_down=v_w_down,
             g_post_ffn=v_g_post_ffn)
    return _step(x, loss_target, w, m, v)
```

```python
import functools

import jax
import jax.numpy as jnp
from jax import lax
from jax.experimental import pallas as pl
from jax.experimental.pallas import tpu as pltpu

F32 = jnp.float32
BF16 = jnp.bfloat16
EPS = 1e-6
N_META = 16
LRU_C = 8.0
CONV_K = 31
LRU_K = 4
FFN_K = 3
CONV_ROWS = 24
SCAN_U = 3
RSCAN_U = 2
LANES = 128
VMEM_LIMIT = 56 * 1024 * 1024
ADAM_LR, ADAM_B1, ADAM_B2, ADAM_EPS, ADAM_WD, ADAM_STEP = 0.001, 0.9, 0.999, 1e-08, 0.01, 10
MESH_T = pl.DeviceIdType.MESH
NT_DIMS = (((1,), (1,)), ((), ()))
TN_DIMS = (((0,), (0,)), ((), ()))

REP_SMALL = ['g_pre_mix', 'lru_conv_b', 'lru_wa', 'lru_ba', 'lru_wx', 'lru_bx', 'lru_lambda', 'conv_b', 'conv_ln_g',
             'conv_ln_b', 'g_out_lru', 'g_out_conv', 'g_post_mix', 'g_pre_ffn', 'ffn_conv_b', 'g_post_ffn']
SH_SMALL = ['meta_tokens', 'lru_conv_w', 'conv_w', 'ffn_conv_w']
BIG = ['w_in', 'w_out', 'w_up', 'w_down']
BIG_AXIS = {'w_in': 2, 'w_out': 1, 'w_up': 2, 'w_down': 1}
WEIGHTS = ['meta_tokens', 'g_pre_mix', 'w_in', 'lru_conv_w', 'lru_conv_b', 'lru_wa', 'lru_ba', 'lru_wx', 'lru_bx',
           'lru_lambda', 'conv_w', 'conv_b', 'conv_ln_g', 'conv_ln_b', 'g_out_lru', 'g_out_conv', 'w_out',
           'g_post_mix', 'g_pre_ffn', 'w_up', 'ffn_conv_w', 'ffn_conv_b', 'w_down', 'g_post_ffn']


def _pcall(body, **kw):
    return pl.pallas_call(body, **kw)


def _params(n_grid=1):
    return pltpu.CompilerParams(dimension_semantics=("arbitrary",) * n_grid, vmem_limit_bytes=VMEM_LIMIT)


def _tiles(t):
    if t % 432 == 0:
        return 432, 144
    assert t % 48 == 0
    return 48, 48


def _row(tm, n):
    return pl.BlockSpec((tm, n), lambda i: (i, 0))


def _rrow(tm, n, nt):
    return pl.BlockSpec((tm, n), lambda i: (nt - 1 - i, 0))


def _halo(hb, n, tm, nt):
    return pl.BlockSpec((hb, n), lambda i: (jnp.maximum((nt - 1 - i) * (tm // hb) - 1, 0), 0))


def _const(shape):
    nd = len(shape)
    return pl.BlockSpec(shape, lambda *_: (0,) * nd, pipeline_mode=pl.Buffered(1))


def _const_out(shape):
    nd = len(shape)
    return pl.BlockSpec(shape, lambda *_: (0,) * nd)


def _sigmoid(x):
    return 1.0 / (1.0 + jnp.exp(-x))


def _gelu(x):
    return 0.5 * x * (1.0 + jnp.tanh(0.7978845608028654 * (x + 0.044715 * (x * x * x))))


def _gelu_and_grad(x):
    k = 0.7978845608028654
    x2 = x * x
    th = jnp.tanh(k * (x + 0.044715 * (x2 * x)))
    return 0.5 * x * (1.0 + th), 0.5 * (1.0 + th) + 0.5 * x * (1.0 - th * th) * (k * (1.0 + 0.134145 * x2))


def _expm1(x):
    return jnp.where(jnp.abs(x) < 1e-2, x * (1.0 + x * (0.5 + x * (1.0 / 6.0 + x * (1.0 / 24.0)))), jnp.exp(x) - 1.0)


def _softplus(x):
    e = jnp.exp(-jnp.abs(x))
    return jnp.maximum(x, 0.0) + jnp.where(e < 1e-4, e * (1.0 - 0.5 * e), jnp.log(1.0 + e))


def _lru_gates(pa, px, sp):
    r = _sigmoid(pa)
    ig = _sigmoid(px)
    la = (-LRU_C * r) * sp
    return r, ig, jnp.exp(la), jnp.sqrt(-_expm1(2.0 * la))


def _rms(x):
    return lax.rsqrt(jnp.mean(x * x, axis=-1, keepdims=True) + EPS)


def _rms_bwd(x, g, dy):
    r = _rms(x)
    xr = x * r
    dyg = dy * g
    return r * (dyg - xr * jnp.mean(dyg * xr, axis=-1, keepdims=True)), xr


def _col_chunk(n):
    return 1536 if n % 1536 == 0 else 1024


def _rms_matmul(h, g, w, name):
    t, d = h.shape
    n = w.shape[1]
    tm, _ = _tiles(t)
    cn = _col_chunk(n)

    def body(h_ref, g_ref, w_ref, p_ref, zb_ref):
        x = h_ref[...]
        zb = ((x * _rms(x)) * g_ref[...]).astype(BF16)
        zb_ref[...] = zb
        for c in range(n // cn):
            p_ref[:, c * cn:(c + 1) * cn] = jnp.dot(zb, w_ref[:, c * cn:(c + 1) * cn], preferred_element_type=F32)

    return _pcall(body, name=name, grid=(t // tm,),
                  in_specs=[_row(tm, d), _const((1, d)), _const((d, n))],
                  out_specs=[_row(tm, n), _row(tm, d)],
                  out_shape=[jax.ShapeDtypeStruct((t, n), F32), jax.ShapeDtypeStruct((t, d), BF16)],
                  compiler_params=_params())(h, g, w)


def _matmul_rms_res(a, w, h, g, name):
    t, k = a.shape
    d = w.shape[1]
    tm, _ = _tiles(t)

    def body(a_ref, w_ref, h_ref, g_ref, o_ref, hn_ref):
        o = jnp.dot(a_ref[...], w_ref[...], preferred_element_type=F32)
        o_ref[...] = o
        hn_ref[...] = h_ref[...] + (o * _rms(o)) * g_ref[...]

    return _pcall(body, name=name, grid=(t // tm,),
                  in_specs=[_row(tm, k), _const((k, d)), _row(tm, d), _const((1, d))],
                  out_specs=[_row(tm, d), _row(tm, d)],
                  out_shape=[jax.ShapeDtypeStruct((t, d), F32), jax.ShapeDtypeStruct((t, d), F32)],
                  compiler_params=_params())(a, w, h, g)


def _rmsbwd_matmul_nt(x, g, dy, w, name):
    t, d = x.shape
    n = w.shape[0]
    tm, _ = _tiles(t)
    cn = _col_chunk(n)

    def body(x_ref, g_ref, dy_ref, w_ref, da_ref, dxb_ref, dg_ref):
        @pl.when(pl.program_id(0) == 0)
        def _():
            dg_ref[...] = jnp.zeros((1, d), F32)

        dy = dy_ref[...]
        dx, xr = _rms_bwd(x_ref[...], g_ref[...], dy)
        dg_ref[...] += jnp.sum(dy * xr, axis=0, keepdims=True)
        dxb = dx.astype(BF16)
        dxb_ref[...] = dxb
        for c in range(n // cn):
            da_ref[:, c * cn:(c + 1) * cn] = lax.dot_general(dxb, w_ref[c * cn:(c + 1) * cn, :], NT_DIMS,
                                                             preferred_element_type=F32)

    return _pcall(body, name=name, grid=(t // tm,),
                  in_specs=[_row(tm, d), _const((1, d)), _row(tm, d), _const((n, d))],
                  out_specs=[_row(tm, n), _row(tm, d), _const_out((1, d))],
                  out_shape=[jax.ShapeDtypeStruct((t, n), F32), jax.ShapeDtypeStruct((t, d), BF16),
                             jax.ShapeDtypeStruct((1, d), F32)],
                  compiler_params=_params())(x, g, dy, w)


def _matmul_nt_rmsbwd_res(dp, w, h, g, dh, name):
    t, n = dp.shape
    d = w.shape[0]
    tm, _ = _tiles(t)

    def body(dp_ref, w_ref, h_ref, g_ref, dh_ref, out_ref, dg_ref):
        @pl.when(pl.program_id(0) == 0)
        def _():
            dg_ref[...] = jnp.zeros((1, d), F32)

        dz = lax.dot_general(dp_ref[...], w_ref[...], NT_DIMS, preferred_element_type=F32)
        dx, xr = _rms_bwd(h_ref[...], g_ref[...], dz)
        dg_ref[...] += jnp.sum(dz * xr, axis=0, keepdims=True)
        out_ref[...] = dh_ref[...] + dx

    return _pcall(body, name=name, grid=(t // tm,),
                  in_specs=[_row(tm, n), _const((d, n)), _row(tm, d), _const((1, d)), _row(tm, d)],
                  out_specs=[_row(tm, d), _const_out((1, d))],
                  out_shape=[jax.ShapeDtypeStruct((t, d), F32), jax.ShapeDtypeStruct((1, d), F32)],
                  compiler_params=_params())(dp, w, h, g, dh)


def _matmul_tn(a, b, name):
    t, k = a.shape
    n = b.shape[1]
    tm, _ = _tiles(t)
    nt = t // tm
    bn = min(n, (1536 * 1024) // k)
    assert n % bn == 0 and bn % LANES == 0

    def body(a_ref, b_ref, o_ref, acc):
        @pl.when(pl.program_id(1) == 0)
        def _():
            acc[...] = jnp.zeros((k, bn), F32)

        acc[...] += lax.dot_general(a_ref[...], b_ref[...], TN_DIMS, preferred_element_type=F32)

        @pl.when(pl.program_id(1) == nt - 1)
        def _():
            o_ref[...] = acc[...].astype(BF16)

    return _pcall(body, name=name, grid=(n // bn, nt),
                  in_specs=[pl.BlockSpec((tm, k), lambda j, i: (i, 0)), pl.BlockSpec((tm, bn), lambda j, i: (i, j))],
                  out_specs=pl.BlockSpec((k, bn), lambda j, i: (0, j)),
                  out_shape=jax.ShapeDtypeStruct((k, n), BF16),
                  scratch_shapes=[pltpu.VMEM((k, bn), F32)],
                  compiler_params=_params(2))(a, b)


MIX_PARAMS = ['lru_conv_w', 'lru_conv_b', 'lru_wa', 'lru_ba', 'lru_wx', 'lru_bx', 'lru_lambda', 'conv_w', 'conv_b',
              'conv_ln_g', 'conv_ln_b', 'g_out_lru', 'g_out_conv']


def _head_gates(xcb, wa_ref, wx_ref, pa_s, px_s, ba, bx, heads):
    for hd in range(heads):
        sl = slice(LANES * hd, LANES * (hd + 1))
        pa_s[:, sl] = jnp.dot(xcb[:, sl], wa_ref[hd], preferred_element_type=F32) + ba[:, sl]
        px_s[:, sl] = jnp.dot(xcb[:, sl], wx_ref[hd], preferred_element_type=F32) + bx[:, sl]


def _lru_conv_chunk(exta, cw4_ref, cb4, r0):
    win = exta[pl.ds(r0, 16), :]
    acc = cw4_ref[3:4, :] * win[8:16]
    for k in range(LRU_K - 1):
        acc = acc + cw4_ref[k:k + 1, :] * pltpu.roll(win, LRU_K - 1 - k, 0)[8:16]
    return acc + cb4


def _mix_fwd(proj, p, name):
    t = proj.shape[0]
    w = p['lru_conv_b'].shape[1]
    cw = p['conv_b'].shape[1]
    heads = p['lru_wa'].shape[0]
    _, tm = _tiles(t)
    nch = tm // 8

    def body(proj_ref, cw4_ref, cb4_ref, wa_ref, ba_ref, wx_ref, bx_ref, lam_ref, cw31_ref, cb31_ref, lng_ref,
             lnb_ref, ga_ref, gb_ref, y_ref, hs_ref, c1_ref, exta, xc_s, pa_s, px_s, extb, nbuf, hcar):
        @pl.when(pl.program_id(0) == 0)
        def _():
            exta[0:8, :] = jnp.zeros((8, w), F32)
            extb[0:32, :] = jnp.zeros((32, cw), F32)
            hcar[...] = jnp.zeros((8, w), F32)

        exta[8:8 + tm, :] = proj_ref[:, 0:w]
        cb4 = cb4_ref[...]

        def conv_a(c, carry):
            r0 = pl.multiple_of(c * 8, 8)
            xc_s[pl.ds(r0, 8), :] = _lru_conv_chunk(exta, cw4_ref, cb4, r0)
            return carry

        lax.fori_loop(0, nch, conv_a, 0, unroll=3)
        _head_gates(xc_s[...].astype(BF16), wa_ref, wx_ref, pa_s, px_s, ba_ref[...], bx_ref[...], heads)

        sp = _softplus(-lam_ref[...])
        ga = ga_ref[...]
        row = lax.broadcasted_iota(jnp.int32, (8, w), 0)

        def scan_c(cg, hprev):
            part = []
            for u in range(SCAN_U):
                r0 = pl.multiple_of((cg * SCAN_U + u) * 8, 8)
                xc = xc_s[pl.ds(r0, 8), :]
                _, ig, a, m = _lru_gates(pa_s[pl.ds(r0, 8), :], px_s[pl.ds(r0, 8), :], sp)
                aa, bb = a, m * (ig * xc)
                for d in (1, 2, 4):
                    a_s = pltpu.roll(aa, d, 0)
                    b_s = pltpu.roll(bb, d, 0)
                    msk = row >= d
                    bb = jnp.where(msk, aa * b_s + bb, bb)
                    aa = jnp.where(msk, aa * a_s, aa)
                part.append((r0, aa, bb, _gelu(proj_ref[pl.ds(r0, 8), w:2 * w])))
            for r0, aa, bb, ge in part:
                hs = aa * hprev + bb
                hs_ref[pl.ds(r0, 8), :] = hs
                ya = hs * ge
                nbuf[pl.ds(r0, 8), 0:w] = (ya * _rms(ya)) * ga
                hprev = hs[7:8, :]
            return hprev

        hcar[0:1, :] = lax.fori_loop(0, nch // SCAN_U, scan_c, hcar[0:1, :])

        extb[32:32 + tm, :] = proj_ref[:, 2 * w:2 * w + cw] * _sigmoid(proj_ref[:, 2 * w + cw:2 * w + 2 * cw])

        def conv_b(c, carry):
            r0 = pl.multiple_of(c * CONV_ROWS, 8)
            ybs = []
            ssq = jnp.zeros((CONV_ROWS, 1), F32)
            for lb in range(cw // LANES):
                sl = slice(LANES * lb, LANES * (lb + 1))
                win = extb[pl.ds(r0, CONV_ROWS + 32), sl]
                rolled = [win] + [pltpu.roll(win, rr, 0) for rr in range(1, 8)]
                parts = [None] * 4
                for k in range(CONV_K):
                    q, rr = divmod(CONV_K - 1 - k, 8)
                    term = cw31_ref[k:k + 1, sl] * rolled[rr][32 - 8 * q:32 - 8 * q + CONV_ROWS]
                    parts[k % 4] = term if parts[k % 4] is None else parts[k % 4] + term
                acc = ((parts[0] + parts[1]) + (parts[2] + parts[3])) + cb31_ref[:, sl]
                c1_ref[pl.ds(r0, CONV_ROWS), sl] = acc
                dlt = acc - jnp.mean(acc, axis=-1, keepdims=True)
                c2 = dlt * lax.rsqrt(jnp.mean(dlt * dlt, axis=-1, keepdims=True) + EPS)
                yb0 = c2 * lng_ref[:, sl] + lnb_ref[:, sl]
                yb = yb0 * _sigmoid(yb0)
                ybs.append(yb)
                ssq = ssq + jnp.sum(yb * yb, axis=-1, keepdims=True)
            rb = lax.rsqrt(ssq / cw + EPS)
            for lb in range(cw // LANES):
                sl = slice(LANES * lb, LANES * (lb + 1))
                nbuf[pl.ds(r0, CONV_ROWS), w + LANES * lb:w + LANES * (lb + 1)] = (ybs[lb] * rb) * gb_ref[:, sl]
            return carry

        lax.fori_loop(0, tm // CONV_ROWS, conv_b, 0, unroll=2)
        exta[0:8, :] = exta[tm:tm + 8, :]
        extb[0:32, :] = extb[tm:tm + 32, :]
        y_ref[...] = nbuf[...].astype(BF16)

    consts = [p[k] for k in MIX_PARAMS]
    return _pcall(body, name=name, grid=(t // tm,),
                  in_specs=[_row(tm, 2 * w + 2 * cw)] + [_const(c.shape) for c in consts],
                  out_specs=[_row(tm, w + cw), _row(tm, w), _row(tm, cw)],
                  out_shape=[jax.ShapeDtypeStruct((t, w + cw), BF16), jax.ShapeDtypeStruct((t, w), F32),
                             jax.ShapeDtypeStruct((t, cw), F32)],
                  scratch_shapes=[pltpu.VMEM((8 + tm, w), F32), pltpu.VMEM((tm, w), F32), pltpu.VMEM((tm, w), F32),
                                  pltpu.VMEM((tm, w), F32), pltpu.VMEM((32 + tm, cw), F32),
                                  pltpu.VMEM((tm, w + cw), F32), pltpu.VMEM((8, w), F32)],
                  compiler_params=_params())(proj, *consts)


def _mix_bwd(dy, proj, hs, c1, p, name):
    t = proj.shape[0]
    w = p['lru_conv_b'].shape[1]
    cw = p['conv_b'].shape[1]
    heads = p['lru_wa'].shape[0]
    _, tm = _tiles(t)
    nt = t // tm
    nch = tm // 8
    nlb = cw // LANES
    G_CB4, G_CW4, G_BA, G_BX, G_SP, G_GA, NGW = 0, 1, 5, 6, 7, 8, 9
    G_CB31, G_LNG, G_LNB, G_GB, G_CW31, NGC = 0, 1, 2, 3, 4, 4 + CONV_K

    def body(dy_ref, proj_ref, projh_ref, hs_ref, hsh_ref, c1_ref, cw4_ref, cb4_ref, wa_ref, ba_ref, wx_ref, bx_ref,
             lam_ref, cw31_ref, cb31_ref, lng_ref, lnb_ref, ga_ref, gb_ref,
             dproj_ref, dcw4_ref, dcb4_ref, dwa_ref, dba_ref, dwx_ref, dbx_ref, dlam_ref, dcw31_ref, dcb31_ref,
             dlng_ref, dlnb_ref, dga_ref, dgb_ref,
             exta, exth, xc_s, pa_s, px_s, dpa_s, dpx_s, dxce, extb, dc1e, dpf, cp_s, acc_w, acc_c):
        i = pl.program_id(0)

        @pl.when(i == 0)
        def _():
            acc_w[...] = jnp.zeros((8 * NGW, w), F32)
            acc_c[...] = jnp.zeros((8 * NGC, cw), F32)
            dwa_ref[...] = jnp.zeros(dwa_ref.shape, F32)
            dwx_ref[...] = jnp.zeros(dwx_ref.shape, F32)
            cp_s[...] = jnp.zeros((8, w), F32)
            dxce[tm:tm + 8, :] = jnp.zeros((8, w), F32)
            dc1e[tm:tm + 32, :] = jnp.zeros((32, cw), F32)

        nf = jnp.where(i < nt - 1, 1.0, 0.0).astype(F32)
        exta[0:8, :] = projh_ref[40:48, 0:w] * nf
        exta[8:8 + tm, :] = proj_ref[:, 0:w]
        exth[0:8, :] = hsh_ref[...] * nf
        exth[8:8 + tm, :] = hs_ref[...]
        extb[0:48, :] = (projh_ref[:, 2 * w:2 * w + cw] * _sigmoid(projh_ref[:, 2 * w + cw:2 * w + 2 * cw])) * nf
        extb[48:48 + tm, :] = proj_ref[:, 2 * w:2 * w + cw] * _sigmoid(proj_ref[:, 2 * w + cw:2 * w + 2 * cw])
        cb4 = cb4_ref[...]

        def conv_a(c, carry):
            r0 = pl.multiple_of(c * 8, 8)
            xc_s[pl.ds(r0, 8), :] = _lru_conv_chunk(exta, cw4_ref, cb4, r0)
            return carry

        lax.fori_loop(0, nch, conv_a, 0, unroll=3)
        xcb = xc_s[...].astype(BF16)
        _head_gates(xcb, wa_ref, wx_ref, pa_s, px_s, ba_ref[...], bx_ref[...], heads)

        sp = _softplus(-lam_ref[...])
        ga = ga_ref[...]
        row = lax.broadcasted_iota(jnp.int32, (8, w), 0)

        def acc_add(ref, g, val, sl=slice(None)):
            for j in range(val.shape[0] // 8):
                ref[8 * g:8 * g + 8, sl] = ref[8 * g:8 * g + 8, sl] + val[8 * j:8 * j + 8]

        def rscan(cg, cp):
            part = []
            for u in range(RSCAN_U):
                r0 = pl.multiple_of((nch - 1 - (cg * RSCAN_U + u)) * 8, 8)
                xc = xc_s[pl.ds(r0, 8), :]
                r, ig, a, m = _lru_gates(pa_s[pl.ds(r0, 8), :], px_s[pl.ds(r0, 8), :], sp)
                hwin = exth[pl.ds(r0, 16), :]
                hcur = hwin[8:16]
                hprev = pltpu.roll(hwin, 1, 0)[8:16]
                ge, dge = _gelu_and_grad(proj_ref[pl.ds(r0, 8), w:2 * w])
                dna = dy_ref[pl.ds(r0, 8), 0:w]
                dya, yar = _rms_bwd(hcur * ge, ga, dna)
                acc_add(acc_w, G_GA, dna * yar)
                dpf[pl.ds(r0, 8), w:2 * w] = (dya * hcur) * dge
                aa = jnp.where(row == 7, 1.0, pltpu.roll(a, 7, 0))
                bb = dya * ge
                for d in (1, 2, 4):
                    a_s = pltpu.roll(aa, 8 - d, 0)
                    b_s = pltpu.roll(bb, 8 - d, 0)
                    msk = row < 8 - d
                    bb = jnp.where(msk, aa * b_s + bb, bb)
                    aa = jnp.where(msk, aa * a_s, aa)
                part.append((r0, aa, bb, xc, r, ig, a, m, hprev))
            for r0, aa, bb, xc, r, ig, a, m, hprev in part:
                lamb = bb + aa * cp
                cp = a[0:1, :] * lamb[0:1, :]
                dm = lamb * (ig * xc)
                di = lamb * (m * xc)
                dxce[pl.ds(r0, 8), :] = lamb * (m * ig)
                dla = a * (lamb * hprev - dm * (a / m))
                acc_add(acc_w, G_SP, dla * (-LRU_C * r))
                dpa = (dla * (-LRU_C * sp)) * (r * (1.0 - r))
                dpx = di * (ig * (1.0 - ig))
                acc_add(acc_w, G_BA, dpa)
                acc_add(acc_w, G_BX, dpx)
                dpa_s[pl.ds(r0, 8), :] = dpa
                dpx_s[pl.ds(r0, 8), :] = dpx
            return cp

        cp_s[0:1, :] = lax.fori_loop(0, nch // RSCAN_U, rscan, cp_s[0:1, :])

        dpab = dpa_s[...].astype(BF16)
        dpxb = dpx_s[...].astype(BF16)
        for hd in range(heads):
            sl = slice(LANES * hd, LANES * (hd + 1))
            dxce[0:tm, sl] = (dxce[0:tm, sl]
                              + lax.dot_general(dpab[:, sl], wa_ref[hd], NT_DIMS, preferred_element_type=F32)
                              + lax.dot_general(dpxb[:, sl], wx_ref[hd], NT_DIMS, preferred_element_type=F32))
            dwa_ref[hd] = dwa_ref[hd] + lax.dot_general(xcb[:, sl], dpab[:, sl], TN_DIMS, preferred_element_type=F32)
            dwx_ref[hd] = dwx_ref[hd] + lax.dot_general(xcb[:, sl], dpxb[:, sl], TN_DIMS, preferred_element_type=F32)

        def conv_a_bwd(c, carry):
            r0 = pl.multiple_of(c * 8, 8)
            win = dxce[pl.ds(r0, 16), :]
            dxc = win[0:8]
            xwin = exta[pl.ds(r0, 16), :]
            dxl = cw4_ref[3:4, :] * dxc
            acc_add(acc_w, G_CB4, dxc)
            acc_add(acc_w, G_CW4 + 3, dxc * xwin[8:16])
            for k in range(LRU_K - 1):
                s = LRU_K - 1 - k
                dxl = dxl + cw4_ref[k:k + 1, :] * pltpu.roll(win, 16 - s, 0)[0:8]
                acc_add(acc_w, G_CW4 + k, dxc * pltpu.roll(xwin, s, 0)[8:16])
            dpf[pl.ds(r0, 8), 0:w] = dxl
            return carry

        lax.fori_loop(0, nch, conv_a_bwd, 0, unroll=3)
        dxce[tm:tm + 8, :] = dxce[0:8, :]

        def mixb(c, carry):
            r0 = pl.multiple_of(c * 8, 8)
            st = []
            ssq = jnp.zeros((8, 1), F32)
            for lb in range(nlb):
                sl = slice(LANES * lb, LANES * (lb + 1))
                c1v = c1_ref[pl.ds(r0, 8), sl]
                dlt = c1v - jnp.mean(c1v, axis=-1, keepdims=True)
                rs = lax.rsqrt(jnp.mean(dlt * dlt, axis=-1, keepdims=True) + EPS)
                c2 = dlt * rs
                yb0 = c2 * lng_ref[:, sl] + lnb_ref[:, sl]
                sg = _sigmoid(yb0)
                yb = yb0 * sg
                ssq = ssq + jnp.sum(yb * yb, axis=-1, keepdims=True)
                st.append((rs, c2, yb0, sg, yb))
            rb = lax.rsqrt(ssq / cw + EPS)
            tsum = jnp.zeros((8, 1), F32)
            dngs = []
            for lb in range(nlb):
                sl = slice(LANES * lb, LANES * (lb + 1))
                dnb = dy_ref[pl.ds(r0, 8), w + LANES * lb:w + LANES * (lb + 1)]
                ybr = st[lb][4] * rb
                acc_add(acc_c, G_GB, dnb * ybr, sl)
                dng = dnb * gb_ref[:, sl]
                dngs.append((dng, ybr))
                tsum = tsum + jnp.sum(dng * ybr, axis=-1, keepdims=True)
            tsum = tsum / cw
            for lb in range(nlb):
                sl = slice(LANES * lb, LANES * (lb + 1))
                rs, c2, yb0, sg, _ = st[lb]
                dng, ybr = dngs[lb]
                dyb0 = (rb * (dng - ybr * tsum)) * (sg * (1.0 + yb0 * (1.0 - sg)))
                acc_add(acc_c, G_LNG, dyb0 * c2, sl)
                acc_add(acc_c, G_LNB, dyb0, sl)
                dc2 = dyb0 * lng_ref[:, sl]
                dc1 = rs * (dc2 - jnp.mean(dc2, axis=-1, keepdims=True)
                            - c2 * jnp.mean(dc2 * c2, axis=-1, keepdims=True))
                acc_add(acc_c, G_CB31, dc1, sl)
                dc1e[pl.ds(r0, 8), sl] = dc1
            return carry

        lax.fori_loop(0, nch, mixb, 0, unroll=6)

        def conv_b_bwd(c, carry):
            r0 = pl.multiple_of(c * CONV_ROWS, 8)
            nwin = CONV_ROWS + 32
            for lb in range(nlb):
                sl = slice(LANES * lb, LANES * (lb + 1))
                win = dc1e[pl.ds(r0, nwin), sl]
                ups = [win] + [pltpu.roll(win, nwin - rr, 0) for rr in range(1, 8)]
                dc1 = win[0:CONV_ROWS]
                parts = [None] * 4
                for k in range(CONV_K):
                    q, rr = divmod(CONV_K - 1 - k, 8)
                    term = cw31_ref[k:k + 1, sl] * ups[rr][8 * q:8 * q + CONV_ROWS]
                    parts[k % 4] = term if parts[k % 4] is None else parts[k % 4] + term
                dc0 = (parts[0] + parts[1]) + (parts[2] + parts[3])
                cav = proj_ref[pl.ds(r0, CONV_ROWS), 2 * w + LANES * lb:2 * w + LANES * (lb + 1)]
                sg = _sigmoid(proj_ref[pl.ds(r0, CONV_ROWS), 2 * w + cw + LANES * lb:2 * w + cw + LANES * (lb + 1)])
                dpf[pl.ds(r0, CONV_ROWS), 2 * w + LANES * lb:2 * w + LANES * (lb + 1)] = dc0 * sg
                dpf[pl.ds(r0, CONV_ROWS), 2 * w + cw + LANES * lb:2 * w + cw + LANES * (lb + 1)] = (
                    (dc0 * cav) * (sg * (1.0 - sg)))
                xwin = extb[pl.ds(pl.multiple_of(r0 + 16, 8), nwin), sl]
                xr = [xwin] + [pltpu.roll(xwin, rr, 0) for rr in range(1, 8)]
                for k in range(CONV_K):
                    q, rr = divmod(CONV_K - 1 - k, 8)
                    acc_add(acc_c, G_CW31 + k, dc1 * xr[rr][32 - 8 * q:32 - 8 * q + CONV_ROWS], sl)
            return carry

        lax.fori_loop(0, tm // CONV_ROWS, conv_b_bwd, 0, unroll=2)
        dc1e[tm:tm + 32, :] = dc1e[0:32, :]
        dproj_ref[...] = dpf[...].astype(BF16)

        @pl.when(i == nt - 1)
        def _():
            def fold(ref, g):
                return jnp.sum(ref[8 * g:8 * g + 8, :], axis=0, keepdims=True)

            dcb4_ref[...] = fold(acc_w, G_CB4)
            for k in range(LRU_K):
                dcw4_ref[k:k + 1, :] = fold(acc_w, G_CW4 + k)
            dba_ref[...] = fold(acc_w, G_BA)
            dbx_ref[...] = fold(acc_w, G_BX)
            dlam_ref[...] = fold(acc_w, G_SP) * (-_sigmoid(-lam_ref[...]))
            dga_ref[...] = fold(acc_w, G_GA)
            dcb31_ref[...] = fold(acc_c, G_CB31)
            dlng_ref[...] = fold(acc_c, G_LNG)
            dlnb_ref[...] = fold(acc_c, G_LNB)
            dgb_ref[...] = fold(acc_c, G_GB)
            for k in range(CONV_K):
                dcw31_ref[k:k + 1, :] = fold(acc_c, G_CW31 + k)

    consts = [p[k] for k in MIX_PARAMS]
    outs = _pcall(body, name=name, grid=(nt,),
                  in_specs=[_rrow(tm, w + cw, nt), _rrow(tm, 2 * w + 2 * cw, nt), _halo(48, 2 * w + 2 * cw, tm, nt),
                            _rrow(tm, w, nt), _halo(8, w, tm, nt), _rrow(tm, cw, nt)] + [_const(c.shape) for c in consts],
                  out_specs=[_rrow(tm, 2 * w + 2 * cw, nt)] + [_const_out(c.shape) for c in consts],
                  out_shape=[jax.ShapeDtypeStruct((t, 2 * w + 2 * cw), BF16)]
                  + [jax.ShapeDtypeStruct(c.shape, F32) for c in consts],
                  scratch_shapes=[pltpu.VMEM((8 + tm, w), F32), pltpu.VMEM((8 + tm, w), F32), pltpu.VMEM((tm, w), F32),
                                  pltpu.VMEM((tm, w), F32), pltpu.VMEM((tm, w), F32), pltpu.VMEM((tm, w), F32),
                                  pltpu.VMEM((tm, w), F32), pltpu.VMEM((tm + 8, w), F32),
                                  pltpu.VMEM((48 + tm, cw), F32), pltpu.VMEM((tm + 32, cw), F32),
                                  pltpu.VMEM((tm, 2 * w + 2 * cw), F32), pltpu.VMEM((8, w), F32),
                                  pltpu.VMEM((8 * NGW, w), F32), pltpu.VMEM((8 * NGC, cw), F32)],
                  compiler_params=_params())(dy, proj, proj, hs, hs, c1, *consts)
    return outs[0], dict(zip(MIX_PARAMS, outs[1:]))


def _ffn_window(u_ref, halo, c, col):
    if isinstance(c, int) and c == 0:
        return jnp.concatenate([halo[:, col:col + LANES], u_ref[0:16, col:col + LANES]], axis=0)
    return u_ref[pl.ds(pl.multiple_of(c * 16 - 8, 8), 24), col:col + LANES]


def _ffn_conv(win, w3_ref, b3_ref, col):
    sl = slice(col, col + LANES)
    x1 = pltpu.roll(win, 1, 0)[8:24]
    x2 = pltpu.roll(win, 2, 0)[8:24]
    u = w3_ref[2:3, sl] * win[8:24] + w3_ref[1:2, sl] * x1 + w3_ref[0:1, sl] * x2 + b3_ref[:, sl]
    return u, (x2, x1, win[8:24])


def _ffn_act_fwd(u0, w3, b3, name):
    t, f2 = u0.shape
    ff = f2 // 2
    _, tm = _tiles(t)
    nch = tm // 16

    def body(u_ref, w3_ref, b3_ref, act_ref, car):
        @pl.when(pl.program_id(0) == 0)
        def _():
            car[...] = jnp.zeros((8, f2), F32)

        def chunk(c):
            halo = car[...] if isinstance(c, int) else None
            r0 = 0 if isinstance(c, int) else pl.multiple_of(c * 16, 16)
            for j in range(ff // LANES):
                gate, _ = _ffn_conv(_ffn_window(u_ref, halo, c, LANES * j), w3_ref, b3_ref, LANES * j)
                up, _ = _ffn_conv(_ffn_window(u_ref, halo, c, ff + LANES * j), w3_ref, b3_ref, ff + LANES * j)
                act_ref[pl.ds(r0, 16), LANES * j:LANES * (j + 1)] = (_gelu(gate) * up).astype(BF16)

        chunk(0)

        def loop(c, carry):
            chunk(c)
            return carry

        lax.fori_loop(1, nch, loop, 0)
        car[...] = u_ref[tm - 8:tm, :]

    return _pcall(body, name=name, grid=(t // tm,),
                  in_specs=[_row(tm, f2), _const(w3.shape), _const(b3.shape)],
                  out_specs=_row(tm, ff), out_shape=jax.ShapeDtypeStruct((t, ff), BF16),
                  scratch_shapes=[pltpu.VMEM((8, f2), F32)],
                  compiler_params=_params())(u0, w3, b3)


def _ffn_act_bwd(dact, u0, w3, b3, name):
    t, f2 = u0.shape
    ff = f2 // 2
    _, tm = _tiles(t)
    nt = t // tm
    nch = tm // 16

    def body(dact_ref, u_ref, uh_ref, w3_ref, b3_ref, du0_ref, dw3_ref, db3_ref, dub, acc):
        i = pl.program_id(0)

        @pl.when(i == 0)
        def _():
            dub[tm:tm + 8, :] = jnp.zeros((8, f2), F32)
            acc[...] = jnp.zeros((32, f2), F32)

        nf = jnp.where(i < nt - 1, 1.0, 0.0).astype(F32)

        def acc_add(g, val, sl):
            acc[8 * g:8 * g + 8, sl] = acc[8 * g:8 * g + 8, sl] + (val[0:8] + val[8:16])

        def chunk(c):
            halo = uh_ref[...] * nf if isinstance(c, int) else None
            r0 = 0 if isinstance(c, int) else pl.multiple_of(c * 16, 16)
            for j in range(ff // LANES):
                cg, cu = LANES * j, ff + LANES * j
                gate, xg = _ffn_conv(_ffn_window(u_ref, halo, c, cg), w3_ref, b3_ref, cg)
                up, xu = _ffn_conv(_ffn_window(u_ref, halo, c, cu), w3_ref, b3_ref, cu)
                ge, dge = _gelu_and_grad(gate)
                da = dact_ref[pl.ds(r0, 16), cg:cg + LANES]
                for col, du, xs in ((cg, (da * up) * dge, xg), (cu, da * ge, xu)):
                    sl = slice(col, col + LANES)
                    dub[pl.ds(r0, 16), sl] = du
                    acc_add(0, du, sl)
                    for k in range(FFN_K):
                        acc_add(1 + k, du * xs[k], sl)

        chunk(0)

        def loop1(c, carry):
            chunk(c)
            return carry

        lax.fori_loop(1, nch, loop1, 0)

        def loop2(c, carry):
            r0 = pl.multiple_of(c * 16, 16)
            for j in range(f2 // LANES):
                sl = slice(LANES * j, LANES * (j + 1))
                win = dub[pl.ds(r0, 24), sl]
                du0 = (w3_ref[2:3, sl] * win[0:16] + w3_ref[1:2, sl] * pltpu.roll(win, 23, 0)[0:16]
                       + w3_ref[0:1, sl] * pltpu.roll(win, 22, 0)[0:16])
                du0_ref[pl.ds(r0, 16), sl] = du0.astype(BF16)
            return carry

        lax.fori_loop(0, nch, loop2, 0)
        dub[tm:tm + 8, :] = dub[0:8, :]

        @pl.when(i == nt - 1)
        def _():
            db3_ref[...] = jnp.sum(acc[0:8, :], axis=0, keepdims=True)
            for k in range(FFN_K):
                dw3_ref[k:k + 1, :] = jnp.sum(acc[8 + 8 * k:16 + 8 * k, :], axis=0, keepdims=True)

    return _pcall(body, name=name, grid=(nt,),
                  in_specs=[_rrow(tm, ff, nt), _rrow(tm, f2, nt), _halo(8, f2, tm, nt), _const(w3.shape),
                            _const(b3.shape)],
                  out_specs=[_rrow(tm, f2, nt), _const_out(w3.shape), _const_out(b3.shape)],
                  out_shape=[jax.ShapeDtypeStruct((t, f2), BF16), jax.ShapeDtypeStruct(w3.shape, F32),
                             jax.ShapeDtypeStruct(b3.shape, F32)],
                  scratch_shapes=[pltpu.VMEM((tm + 8, f2), F32), pltpu.VMEM((32, f2), F32)],
                  compiler_params=_params())(dact, u0, u0, w3, b3)


def _loss_head(h, tgt, name):
    t, d = h.shape
    tm, _ = _tiles(t)

    def body(h_ref, t_ref, dh_ref, s_ref):
        i = pl.program_id(0)

        @pl.when(i == 0)
        def _():
            s_ref[...] = jnp.zeros((1, d), F32)

        row = lax.broadcasted_iota(jnp.int32, (tm, d), 0) + i * tm
        err = jnp.where(row >= N_META, h_ref[...] - t_ref[...], 0.0)
        dh_ref[...] = err / d
        s_ref[...] += jnp.sum(err * err, axis=0, keepdims=True)

    return _pcall(body, name=name, grid=(t // tm,), in_specs=[_row(tm, d), _row(tm, d)],
                  out_specs=[_row(tm, d), _const_out((1, d))],
                  out_shape=[jax.ShapeDtypeStruct((t, d), F32), jax.ShapeDtypeStruct((1, d), F32)],
                  compiler_params=_params())(h, tgt)


def _row_tile(rows, row_bytes, budget):
    best = None
    for tr in range(16, rows + 1, 16):
        if rows % tr == 0 and tr * row_bytes <= budget:
            best = tr
    assert best is not None, (rows, row_bytes)
    return best


def _cast_bf16(a, name):
    r, c = a.shape
    tr = _row_tile(r, c * 4, 4 << 20)

    def body(a_ref, o_ref):
        o_ref[...] = a_ref[...].astype(BF16)

    return _pcall(body, name=name, grid=(r // tr,), in_specs=[_row(tr, c)], out_specs=_row(tr, c),
                  out_shape=jax.ShapeDtypeStruct((r, c), BF16), compiler_params=_params())(a)


def _sum_slots(r, name):
    s, rows, c = r.shape
    tr = _row_tile(rows, s * c * 4, 8 << 20)

    def body(r_ref, o_ref):
        acc = r_ref[0].astype(F32)
        for k in range(1, s):
            acc = acc + r_ref[k].astype(F32)
        o_ref[...] = acc

    return _pcall(body, name=name, grid=(rows // tr,),
                  in_specs=[pl.BlockSpec((s, tr, c), lambda i: (0, i, 0))], out_specs=_row(tr, c),
                  out_shape=jax.ShapeDtypeStruct((rows, c), F32), compiler_params=_params())(r)


def _sum_slots_into_half(r, c1, name):
    s, rows, c = r.shape
    tr = _row_tile(rows, s * c * 4, 8 << 20)
    nb = rows // tr

    def body(c_ref, r_ref, o_ref):
        acc = r_ref[0].astype(F32)
        for k in range(1, s):
            acc = acc + r_ref[k].astype(F32)
        o_ref[...] = acc

    gs = pltpu.PrefetchScalarGridSpec(
        num_scalar_prefetch=1, grid=(nb,),
        in_specs=[pl.BlockSpec((s, tr, c), lambda i, cr: (0, i, 0))],
        out_specs=pl.BlockSpec((tr, c), lambda i, cr: (cr[0] * nb + i, 0)))
    return _pcall(body, name=name, grid_spec=gs, out_shape=jax.ShapeDtypeStruct((2 * rows, c), F32),
                  compiler_params=_params())(c1, r)


def _cast_into_window(a, axis, k1, name):
    l, r, c = a.shape
    shape = (l, 4 * r, c) if axis == 1 else (l, r, 4 * c)

    def body(k_ref, a_ref, o_ref):
        o_ref[...] = a_ref[...].astype(BF16)

    omap = (lambda i, k: (i, k[0], 0)) if axis == 1 else (lambda i, k: (i, 0, k[0]))
    gs = pltpu.PrefetchScalarGridSpec(num_scalar_prefetch=1, grid=(l,),
                                      in_specs=[pl.BlockSpec((1, r, c), lambda i, k: (i, 0, 0))],
                                      out_specs=pl.BlockSpec((1, r, c), omap))
    return _pcall(body, name=name, grid_spec=gs, out_shape=jax.ShapeDtypeStruct(shape, BF16),
                  compiler_params=_params())(k1, a)


def _adamw(g, w, m, v, name):
    r, c = g.shape
    tr = _row_tile(r, c * 4, 1 << 20)

    def body(g_ref, w_ref, m_ref, v_ref, d_ref, m2_ref, v2_ref):
        gv = g_ref[...]
        m2 = ADAM_B1 * m_ref[...] + (1.0 - ADAM_B1) * gv
        v2 = ADAM_B2 * v_ref[...] + (1.0 - ADAM_B2) * (gv * gv)
        m_hat = m2 / (1.0 - ADAM_B1 ** ADAM_STEP)
        v_hat = v2 / (1.0 - ADAM_B2 ** ADAM_STEP)
        d_ref[...] = -ADAM_LR * (m_hat / (jnp.sqrt(v_hat) + ADAM_EPS) + ADAM_WD * w_ref[...])
        m2_ref[...] = m2
        v2_ref[...] = v2

    return _pcall(body, name=name, grid=(r // tr,), in_specs=[_row(tr, c)] * 4, out_specs=[_row(tr, c)] * 3,
                  out_shape=[jax.ShapeDtypeStruct((r, c), F32)] * 3, compiler_params=_params())(g, w, m, v)


ANY = pl.BlockSpec(memory_space=pl.ANY)


def _coords():
    return lax.axis_index("x"), lax.axis_index("y"), lax.axis_index("c")


def _window(ref, lead, axis, k, width):
    idx = [slice(None)] * len(ref.shape)
    idx[0] = lead
    idx[axis] = pl.ds(pl.multiple_of(k * width, LANES if axis == len(ref.shape) - 1 else 8), width)
    return ref.at[tuple(idx)]


def _gather_xy(arrs, axes, n_inplace, name):
    n = len(arrs)
    out_shape, widths = [], []
    for i, (a, ax) in enumerate(zip(arrs, axes)):
        s = list(a.shape)
        if i < n_inplace:
            widths.append(s[ax] // 4)
        else:
            widths.append(s[ax])
            s[ax] *= 4
        out_shape.append(jax.ShapeDtypeStruct(tuple(s), a.dtype))

    def body(*refs):
        ins, outs = refs[:n], refs[n:2 * n]
        send_sems, recv_sems, loc_sems = refs[2 * n:]
        x, y, c = _coords()
        k_me = 2 * x + y
        chips = [(1 - x, y), (x, 1 - y), (1 - x, 1 - y)]
        sib = (x, y, 1 - c)

        def half(i, which):
            hl = arrs[i].shape[0] // 2
            return pl.ds(which * hl, hl)

        def win(i, kk, which):
            return _window(outs[i], half(i, which), axes[i], kk, widths[i])

        def copy(i, s, src, dst, to):
            return pltpu.make_async_remote_copy(src_ref=src, dst_ref=dst, send_sem=send_sems.at[i, s],
                                                recv_sem=recv_sems.at[i, s], device_id=to, device_id_type=MESH_T)

        locs = []
        for i in range(n_inplace, n):
            lc = pltpu.make_async_copy(ins[i], _window(outs[i], slice(None), axes[i], k_me, widths[i]), loc_sems.at[i])
            lc.start()
            locs.append(lc)
        started = []
        for i in range(n):
            for j, chip in enumerate(chips):
                src = win(i, k_me, c) if i < n_inplace else ins[i].at[half(i, c)]
                cp = copy(i, j, src, win(i, k_me, c), (*chip, c))
                cp.start()
                started.append(cp)
        for i in range(n):
            for j, chip in enumerate(chips):
                kk = 2 * chip[0] + chip[1]
                copy(i, j, win(i, kk, c), win(i, kk, c), (*chip, c)).wait_recv()
                fw = copy(i, 3 + j, win(i, kk, c), win(i, kk, c), sib)
                fw.start()
                started.append(fw)
        for i in range(n):
            for j, chip in enumerate(chips):
                kk = 2 * chip[0] + chip[1]
                copy(i, 3 + j, win(i, kk, 1 - c), win(i, kk, 1 - c), sib).wait_recv()
        for cp in started:
            cp.wait_send()
        for lc in locs:
            lc.wait()

    return _pcall(body, name=name, in_specs=[ANY] * n, out_specs=[ANY] * n, out_shape=out_shape,
                  input_output_aliases={i: i for i in range(n_inplace)},
                  scratch_shapes=[pltpu.SemaphoreType.DMA((n, 6)), pltpu.SemaphoreType.DMA((n, 6)),
                                  pltpu.SemaphoreType.DMA((n,))],
                  compiler_params=pltpu.CompilerParams(has_side_effects=True))(*arrs)


def _peer(x, y, c, mask):
    bx, by, bc = (mask >> 2) & 1, (mask >> 1) & 1, mask & 1
    return (1 - x if bx else x, 1 - y if by else y, 1 - c if bc else c)


def _grad_scatter(grads, axes, name):
    n = len(grads)
    out_shape = []
    widths = []
    for a, ax in zip(grads, axes):
        s = list(a.shape)
        s[ax] //= 4
        s[0] //= 2
        widths.append(s[ax])
        out_shape.append(jax.ShapeDtypeStruct((8, *s), a.dtype))

    def body(*refs):
        ins, outs = refs[:n], refs[n:2 * n]
        send_sems, recv_sems, loc_sems = refs[2 * n:]
        x, y, c = _coords()

        def piece(i, px, py, pc):
            hl = grads[i].shape[0] // 2
            return _window(ins[i], pl.ds(pc * hl, hl), axes[i], 2 * px + py, widths[i])

        locs = []
        for i in range(n):
            lc = pltpu.make_async_copy(piece(i, x, y, c), outs[i].at[7], loc_sems.at[i])
            lc.start()
            locs.append(lc)
        started = []
        for i in range(n):
            for mask in range(1, 8):
                px, py, pc = _peer(x, y, c, mask)
                cp = pltpu.make_async_remote_copy(src_ref=piece(i, px, py, pc), dst_ref=outs[i].at[mask - 1],
                                                  send_sem=send_sems.at[i, mask - 1], recv_sem=recv_sems.at[i, mask - 1],
                                                  device_id=(px, py, pc), device_id_type=MESH_T)
                cp.start()
                started.append(cp)
        for cp in started:
            cp.wait()
        for lc in locs:
            lc.wait()

    return _pcall(body, name=name, in_specs=[ANY] * n, out_specs=[ANY] * n, out_shape=out_shape,
                  scratch_shapes=[pltpu.SemaphoreType.DMA((n, 7)), pltpu.SemaphoreType.DMA((n, 7)),
                                  pltpu.SemaphoreType.DMA((n,))],
                  compiler_params=pltpu.CompilerParams(has_side_effects=True))(*grads)


SWAP_CHUNKS = 8


def _sibling_swap(fulls, name):
    n = len(fulls)
    out_shape = [jax.ShapeDtypeStruct(a.shape, a.dtype) for a in fulls]

    def body(*refs):
        outs = refs[n:2 * n]
        send_sems, recv_sems = refs[2 * n:]
        x, y, c = _coords()

        def chunk(i, which, q):
            hl, rc = fulls[i].shape[0] // 2, fulls[i].shape[1] // SWAP_CHUNKS
            ref = outs[i].at[pl.ds(which * hl, hl), pl.ds(q * rc, rc)]
            return pltpu.make_async_remote_copy(src_ref=ref, dst_ref=ref, send_sem=send_sems.at[i, q],
                                                recv_sem=recv_sems.at[i, q], device_id=(x, y, 1 - c),
                                                device_id_type=MESH_T)

        started = []
        for i in range(n):
            for q in range(SWAP_CHUNKS):
                cp = chunk(i, c, q)
                cp.start()
                started.append(cp)
        for cp in started:
            cp.wait_send()
        for i in range(n):
            for q in range(SWAP_CHUNKS):
                chunk(i, 1 - c, q).wait_recv()

    for a in fulls:
        assert a.shape[1] % (16 * SWAP_CHUNKS) == 0, a.shape
    return _pcall(body, name=name, in_specs=[ANY] * n, out_specs=[ANY] * n, out_shape=out_shape,
                  input_output_aliases={i: i for i in range(n)},
                  scratch_shapes=[pltpu.SemaphoreType.DMA((n, SWAP_CHUNKS)), pltpu.SemaphoreType.DMA((n, SWAP_CHUNKS))],
                  compiler_params=pltpu.CompilerParams(has_side_effects=True))(*fulls)


def _scatter8(pk, name):
    r, cdim = pk.shape
    pr = r // 8
    assert pr % 8 == 0

    def body(p_ref, o_ref, send_sems, recv_sems, loc_sem):
        x, y, c = _coords()

        def piece(px, py, pc):
            return p_ref.at[pl.ds(pl.multiple_of((4 * px + 2 * py + pc) * pr, 8), pr)]

        lc = pltpu.make_async_copy(piece(x, y, c), o_ref.at[7], loc_sem)
        lc.start()
        started = []
        for mask in range(1, 8):
            px, py, pc = _peer(x, y, c, mask)
            cp = pltpu.make_async_remote_copy(src_ref=piece(px, py, pc), dst_ref=o_ref.at[mask - 1],
                                              send_sem=send_sems.at[mask - 1], recv_sem=recv_sems.at[mask - 1],
                                              device_id=(px, py, pc), device_id_type=MESH_T)
            cp.start()
            started.append(cp)
        for cp in started:
            cp.wait()
        lc.wait()

    return _pcall(body, name=name, in_specs=[ANY], out_specs=ANY, out_shape=jax.ShapeDtypeStruct((8, pr, cdim), F32),
                  scratch_shapes=[pltpu.SemaphoreType.DMA((7,)), pltpu.SemaphoreType.DMA((7,)),
                                  pltpu.SemaphoreType.DMA],
                  compiler_params=pltpu.CompilerParams(has_side_effects=True))(pk)


def _gather_all(pk, name):
    r, cdim = pk.shape

    def body(p_ref, o_ref, send_sems, recv_sems, loc_sem):
        x, y, c = _coords()
        lc = pltpu.make_async_copy(p_ref, o_ref.at[4 * x + 2 * y + c], loc_sem)
        lc.start()
        started = []
        for mask in range(1, 8):
            px, py, pc = _peer(x, y, c, mask)
            cp = pltpu.make_async_remote_copy(src_ref=p_ref, dst_ref=o_ref.at[4 * x + 2 * y + c],
                                              send_sem=send_sems.at[mask - 1], recv_sem=recv_sems.at[mask - 1],
                                              device_id=(px, py, pc), device_id_type=MESH_T)
            cp.start()
            started.append(cp)
        for cp in started:
            cp.wait()
        lc.wait()

    return _pcall(body, name=name, in_specs=[ANY], out_specs=ANY, out_shape=jax.ShapeDtypeStruct((8, r, cdim), F32),
                  scratch_shapes=[pltpu.SemaphoreType.DMA((7,)), pltpu.SemaphoreType.DMA((7,)),
                                  pltpu.SemaphoreType.DMA],
                  compiler_params=pltpu.CompilerParams(has_side_effects=True))(pk)


PACK_C = 1024


def _pack(arrs, row_mult):
    parts = []
    for a in arrs:
        flat = a.reshape(-1)
        parts.append(jnp.pad(flat, (0, (-flat.shape[0]) % PACK_C)))
    flat = jnp.concatenate(parts)
    flat = jnp.pad(flat, (0, (-flat.shape[0]) % (PACK_C * row_mult)))
    return flat.reshape(-1, PACK_C)


def _unpack(pk, shapes):
    flat = pk.reshape(-1)
    out, off = [], 0
    for s in shapes:
        size = 1
        for dd in s:
            size *= dd
        out.append(flat[off:off + size].reshape(s))
        off += size + (-size) % PACK_C
    return out


def _layer_params(full, l):
    p = {}
    for k in ['g_pre_mix', 'lru_conv_b', 'lru_ba', 'lru_bx', 'lru_lambda', 'conv_b', 'conv_ln_g', 'conv_ln_b', 'g_out_lru',
              'g_out_conv', 'g_post_mix', 'g_pre_ffn', 'ffn_conv_b', 'g_post_ffn']:
        p[k] = full[k][l][None, :]
    for k in ['lru_conv_w', 'conv_w', 'ffn_conv_w', 'w_in', 'w_out', 'w_up', 'w_down']:
        p[k] = full[k][l]
    p['lru_wa'] = full['lru_wa_bf'][l]
    p['lru_wx'] = full['lru_wx_bf'][l]
    return p


def _step(x, loss_target, w, m, v):
    depth = w['w_in'].shape[0]
    d = x.shape[2]
    xk, yk, _ = _coords()
    k_me = 2 * xk + yk

    k1 = jnp.reshape(k_me, (1,)).astype(jnp.int32)
    big_bf = {k: _cast_into_window(w[k], BIG_AXIS[k], k1, "cast_" + k) for k in BIG}
    sh_pad = [w['meta_tokens']] + [jnp.pad(w[k], ((0, 0), (0, (-w[k].shape[1]) % 8), (0, 0))) for k in SH_SMALL[1:]]
    gath = _gather_xy([big_bf[k] for k in BIG] + sh_pad, [BIG_AXIS[k] for k in BIG] + [1, 2, 2, 2], len(BIG),
                      "gather_weights")
    full = dict(w)
    full.update(dict(zip(BIG, gath[:len(BIG)])))
    full['meta_tokens'] = gath[len(BIG)]
    for k, a in zip(SH_SMALL[1:], gath[len(BIG) + 1:]):
        full[k] = a[:, :w[k].shape[1]]
    for k in ('lru_wa', 'lru_wx'):
        full[k + '_bf'] = _cast_bf16(w[k].reshape(-1, LANES), "cast_" + k).reshape(w[k].shape)

    h = jnp.concatenate([full['meta_tokens'], x[0]], axis=0)
    tgt = jnp.pad(loss_target[0], ((N_META, 0), (0, 0)))
    sq, dh, gl = _fwd_bwd(h, tgt, full, depth)
    loss = lax.psum(0.5 * jnp.sum(sq) / d, ("x", "y", "c"))
    return _reduce_update(loss, dh, gl, w, m, v, depth, k_me)


def _fwd_bwd(h, tgt, full, depth):
    saved = []
    for l in range(depth):
        p = _layer_params(full, l)
        proj, zb1 = _rms_matmul(h, p['g_pre_mix'], p['w_in'], "in_proj")
        y, hs, c1 = _mix_fwd(proj, p, "mix_fwd")
        o, h1 = _matmul_rms_res(y, p['w_out'], h, p['g_post_mix'], "out_proj")
        u0, zb2 = _rms_matmul(h1, p['g_pre_ffn'], p['w_up'], "up_proj")
        act = _ffn_act_fwd(u0, p['ffn_conv_w'], p['ffn_conv_b'], "ffn_act_fwd")
        f, h2 = _matmul_rms_res(act, p['w_down'], h1, p['g_post_ffn'], "down_proj")
        saved.append((p, h, zb1, proj, y, hs, c1, o, h1, zb2, u0, act, f))
        h = h2

    dh, sq = _loss_head(h, tgt, "loss_head")

    gl = [None] * depth
    for l in reversed(range(depth)):
        p, h0, zb1, proj, y, hs, c1, o, h1, zb2, u0, act, f = saved[l]
        g = {}
        dact, dfb, g['g_post_ffn'] = _rmsbwd_matmul_nt(f, p['g_post_ffn'], dh, p['w_down'], "down_bwd")
        g['w_down'] = _matmul_tn(act, dfb, "down_dw")
        du0, g['ffn_conv_w'], g['ffn_conv_b'] = _ffn_act_bwd(dact, u0, p['ffn_conv_w'], p['ffn_conv_b'], "ffn_act_bwd")
        g['w_up'] = _matmul_tn(zb2, du0, "up_dw")
        dh1, g['g_pre_ffn'] = _matmul_nt_rmsbwd_res(du0, p['w_up'], h1, p['g_pre_ffn'], dh, "up_bwd")
        dy, dob, g['g_post_mix'] = _rmsbwd_matmul_nt(o, p['g_post_mix'], dh1, p['w_out'], "out_bwd")
        g['w_out'] = _matmul_tn(y, dob, "out_dw")
        dproj, gm = _mix_bwd(dy, proj, hs, c1, p, "mix_bwd")
        g.update(gm)
        g['w_in'] = _matmul_tn(zb1, dproj, "in_dw")
        dh, g['g_pre_mix'] = _matmul_nt_rmsbwd_res(dproj, p['w_in'], h0, p['g_pre_mix'], dh1, "in_bwd")
        gl[l] = g
    return sq, dh, gl


def _reduce_update(loss, dh, gl, w, m, v, depth, k_me):
    grad_x = dh[N_META:][None]

    def stacked(k):
        return jnp.stack([gl[l][k].reshape(w[k].shape[1:]) if k not in SH_SMALL + BIG else gl[l][k]
                          for l in range(depth)])

    big_full = [stacked(k) for k in BIG]
    slots = _grad_scatter(big_full, [BIG_AXIS[k] for k in BIG], "grad_scatter")
    c1 = jnp.reshape(lax.axis_index("c"), (1,)).astype(jnp.int32)
    fulls = []
    for k, s in zip(BIG, slots):
        red = _sum_slots_into_half(s.reshape(8, -1, s.shape[-1]), c1, "grad_sum")
        fulls.append(red.reshape(2 * s.shape[1], *s.shape[2:]))
    big_red = _sibling_swap(fulls, "grad_swap")

    out = {}
    for k, gk in zip(BIG, big_red):
        c2 = gk.shape[-1]
        dl, m2, v2 = _adamw(gk.reshape(-1, c2), w[k].reshape(-1, c2), m[k].reshape(-1, c2), v[k].reshape(-1, c2),
                            "adamw_big")
        out[k] = (gk, dl.reshape(gk.shape), m2.reshape(gk.shape), v2.reshape(gk.shape))

    rep_g = [stacked(k) for k in REP_SMALL]
    sh_g = [dh[:N_META]] + [stacked(k) for k in SH_SMALL[1:]]
    n_rep_rows = _pack(rep_g, 1).shape[0]
    pk = jnp.concatenate([_pack(rep_g, 1), _pack(sh_g, 1)])
    pk = jnp.pad(pk, ((0, (-pk.shape[0]) % 256), (0, 0)))
    part = _sum_slots(_scatter8(pk, "small_scatter"), "small_sum")
    red = _gather_all(part, "small_gather").reshape(pk.shape)
    rep_red = _unpack(red[:n_rep_rows], [a.shape for a in rep_g])
    sh_red = []
    for a in _unpack(red[n_rep_rows:], [a.shape for a in sh_g]):
        wd = a.shape[-1] // 4
        sh_red.append(lax.dynamic_slice_in_dim(a, k_me * wd, wd, axis=a.ndim - 1))

    for names, grads_ in ((REP_SMALL, rep_red), (SH_SMALL, sh_red)):
        res = _adamw(_pack(grads_, 16), _pack([w[k] for k in names], 16), _pack([m[k] for k in names], 16),
                     _pack([v[k] for k in names], 16), "adamw_small")
        shapes = [w[k].shape for k in names]
        un = [_unpack(r, shapes) for r in res]
        for j, k in enumerate(names):
            out[k] = (grads_[j].reshape(w[k].shape), un[0][j], un[1][j], un[2][j])

    return (loss, grad_x, *[out[k][0] for k in WEIGHTS], *[out[k][1] for k in WEIGHTS],
            *[out[k][2] for k in WEIGHTS], *[out[k][3] for k in WEIGHTS])


def kernel(x, meta_tokens, g_pre_mix, w_in, lru_conv_w, lru_conv_b, lru_wa, lru_ba, lru_wx, lru_bx, lru_lambda, conv_w, conv_b, conv_ln_g, conv_ln_b, g_out_lru, g_out_conv, w_out, g_post_mix, g_pre_ffn, w_up, ffn_conv_w, ffn_conv_b, w_down, g_post_ffn, loss_target, m_meta_tokens, m_g_pre_mix, m_w_in, m_lru_conv_w, m_lru_conv_b, m_lru_wa, m_lru_ba, m_lru_wx, m_lru_bx, m_lru_lambda, m_conv_w, m_conv_b, m_conv_ln_g, m_conv_ln_b, m_g_out_lru, m_g_out_conv, m_w_out, m_g_post_mix, m_g_pre_ffn, m_w_up, m_ffn_conv_w, m_ffn_conv_b, m_w_down, m_g_post_ffn, v_meta_tokens, v_g_pre_mix, v_w_in, v_lru_conv_w, v_lru_conv_b, v_lru_wa, v_lru_ba, v_lru_wx, v_lru_bx, v_lru_lambda, v_conv_w, v_conv_b, v_conv_ln_g, v_conv_ln_b, v_g_out_lru, v_g_out_conv, v_w_out, v_g_post_mix, v_g_pre_ffn, v_w_up, v_ffn_conv_w, v_ffn_conv_b, v_w_down, v_g_post_ffn):
    w = dict(meta_tokens=meta_tokens, g_pre_mix=g_pre_mix, w_in=w_in, lru_conv_w=lru_conv_w, lru_conv_b=lru_conv_b,
             lru_wa=lru_wa, lru_ba=lru_ba, lru_wx=lru_wx, lru_bx=lru_bx, lru_lambda=lru_lambda, conv_w=conv_w,
             conv_b=conv_b, conv_ln_g=conv_ln_g, conv_ln_b=conv_ln_b, g_out_lru=g_out_lru, g_out_conv=g_out_conv,
             w_out=w_out, g_post_mix=g_post_mix, g_pre_ffn=g_pre_ffn, w_up=w_up, ffn_conv_w=ffn_conv_w,
             ffn_conv_b=ffn_conv_b, w_down=w_down, g_post_ffn=g_post_ffn)
    m = dict(meta_tokens=m_meta_tokens, g_pre_mix=m_g_pre_mix, w_in=m_w_in, lru_conv_w=m_lru_conv_w,
             lru_conv_b=m_lru_conv_b, lru_wa=m_lru_wa, lru_ba=m_lru_ba, lru_wx=m_lru_wx, lru_bx=m_lru_bx,
             lru_lambda=m_lru_lambda, conv_w=m_conv_w, conv_b=m_conv_b, conv_ln_g=m_conv_ln_g, conv_ln_b=m_conv_ln_b,
             g_out_lru=m_g_out_lru, g_out_conv=m_g_out_conv, w_out=m_w_out, g_post_mix=m_g_post_mix,
             g_pre_ffn=m_g_pre_ffn, w_up=m_w_up, ffn_conv_w=m_ffn_conv_w, ffn_conv_b=m_ffn_conv_b, w_down=m_w_down,
             g_post_ffn=m_g_post_ffn)
    v = dict(meta_tokens=v_meta_tokens, g_pre_mix=v_g_pre_mix, w_in=v_w_in, lru_conv_w=v_lru_conv_w,
             lru_conv_b=v_lru_conv_b, lru_wa=v_lru_wa, lru_ba=v_lru_ba, lru_wx=v_lru_wx, lru_bx=v_lru_bx,
             lru_lambda=v_lru_lambda, conv_w=v_conv_w, conv_b=v_conv_b, conv_ln_g=v_conv_ln_g, conv_ln_b=v_conv_ln_b,
             g_out_lru=v_g_out_lru, g_out_conv=v_g_out_conv, w_out=v_w_out, g_post_mix=v_g_post_mix,
             g_pre_ffn=v_g_pre_ffn, w_up=v_w_up, ffn_conv_w=v_ffn_conv_w, ffn_conv_b=v_ffn_conv_b, w_down=v_w_down,
             g_post_ffn=v_g_post_ffn)
    return _step(x, loss_target, w, m, v)
```

```python
import functools

import jax
import jax.numpy as jnp
from jax import lax
from jax.experimental import pallas as pl
from jax.experimental.pallas import tpu as pltpu

F32 = jnp.float32
BF16 = jnp.bfloat16
EPS = 1e-6
N_META = 16
LRU_C = 8.0
CONV_K = 31
LRU_K = 4
FFN_K = 3
CONV_ROWS = 24
SCAN_U = 3
RSCAN_U = 2
LANES = 128
VMEM_LIMIT = 56 * 1024 * 1024
ADAM_LR, ADAM_B1, ADAM_B2, ADAM_EPS, ADAM_WD, ADAM_STEP = 0.001, 0.9, 0.999, 1e-08, 0.01, 10
MESH_T = pl.DeviceIdType.MESH
NT_DIMS = (((1,), (1,)), ((), ()))
TN_DIMS = (((0,), (0,)), ((), ()))

REP_SMALL = ['g_pre_mix', 'lru_conv_b', 'lru_wa', 'lru_ba', 'lru_wx', 'lru_bx', 'lru_lambda', 'conv_b', 'conv_ln_g',
             'conv_ln_b', 'g_out_lru', 'g_out_conv', 'g_post_mix', 'g_pre_ffn', 'ffn_conv_b', 'g_post_ffn']
SH_SMALL = ['meta_tokens', 'lru_conv_w', 'conv_w', 'ffn_conv_w']
BIG = ['w_in', 'w_out', 'w_up', 'w_down']
BIG_AXIS = {'w_in': 2, 'w_out': 1, 'w_up': 2, 'w_down': 1}
WEIGHTS = ['meta_tokens', 'g_pre_mix', 'w_in', 'lru_conv_w', 'lru_conv_b', 'lru_wa', 'lru_ba', 'lru_wx', 'lru_bx',
           'lru_lambda', 'conv_w', 'conv_b', 'conv_ln_g', 'conv_ln_b', 'g_out_lru', 'g_out_conv', 'w_out',
           'g_post_mix', 'g_pre_ffn', 'w_up', 'ffn_conv_w', 'ffn_conv_b', 'w_down', 'g_post_ffn']


def _pcall(body, **kw):
    return pl.pallas_call(body, **kw)


def _params(n_grid=1):
    return pltpu.CompilerParams(dimension_semantics=("arbitrary",) * n_grid, vmem_limit_bytes=VMEM_LIMIT)


def _tiles(t):
    if t % 432 == 0:
        return 432, 144
    assert t % 48 == 0
    return 48, 48


def _row(tm, n):
    return pl.BlockSpec((tm, n), lambda i: (i, 0))


def _rrow(tm, n, nt):
    return pl.BlockSpec((tm, n), lambda i: (nt - 1 - i, 0))


def _halo(hb, n, tm, nt):
    return pl.BlockSpec((hb, n), lambda i: (jnp.maximum((nt - 1 - i) * (tm // hb) - 1, 0), 0))


def _const(shape):
    nd = len(shape)
    return pl.BlockSpec(shape, lambda *_: (0,) * nd, pipeline_mode=pl.Buffered(1))


def _const_out(shape):
    nd = len(shape)
    return pl.BlockSpec(shape, lambda *_: (0,) * nd)


def _sigmoid(x):
    return 1.0 / (1.0 + jnp.exp(-x))


def _gelu(x):
    return 0.5 * x * (1.0 + jnp.tanh(0.7978845608028654 * (x + 0.044715 * (x * x * x))))


def _gelu_and_grad(x):
    k = 0.7978845608028654
    x2 = x * x
    th = jnp.tanh(k * (x + 0.044715 * (x2 * x)))
    return 0.5 * x * (1.0 + th), 0.5 * (1.0 + th) + 0.5 * x * (1.0 - th * th) * (k * (1.0 + 0.134145 * x2))


def _expm1(x):
    return jnp.where(jnp.abs(x) < 1e-2, x * (1.0 + x * (0.5 + x * (1.0 / 6.0 + x * (1.0 / 24.0)))), jnp.exp(x) - 1.0)


def _softplus(x):
    e = jnp.exp(-jnp.abs(x))
    return jnp.maximum(x, 0.0) + jnp.where(e < 1e-4, e * (1.0 - 0.5 * e), jnp.log(1.0 + e))


def _lru_gates(pa, px, sp):
    r = _sigmoid(pa)
    ig = _sigmoid(px)
    la = (-LRU_C * r) * sp
    return r, ig, jnp.exp(la), jnp.sqrt(-_expm1(2.0 * la))


def _rms(x):
    return lax.rsqrt(jnp.mean(x * x, axis=-1, keepdims=True) + EPS)


def _rms_bwd(x, g, dy):
    r = _rms(x)
    xr = x * r
    dyg = dy * g
    return r * (dyg - xr * jnp.mean(dyg * xr, axis=-1, keepdims=True)), xr


def _col_chunk(n):
    return 1536 if n % 1536 == 0 else 1024


def _rms_matmul(h, g, w, name):
    t, d = h.shape
    n = w.shape[1]
    tm, _ = _tiles(t)
    cn = _col_chunk(n)

    def body(h_ref, g_ref, w_ref, p_ref, zb_ref):
        x = h_ref[...]
        zb = ((x * _rms(x)) * g_ref[...]).astype(BF16)
        zb_ref[...] = zb
        for c in range(n // cn):
            p_ref[:, c * cn:(c + 1) * cn] = jnp.dot(zb, w_ref[:, c * cn:(c + 1) * cn], preferred_element_type=F32)

    return _pcall(body, name=name, grid=(t // tm,),
                  in_specs=[_row(tm, d), _const((1, d)), _const((d, n))],
                  out_specs=[_row(tm, n), _row(tm, d)],
                  out_shape=[jax.ShapeDtypeStruct((t, n), F32), jax.ShapeDtypeStruct((t, d), BF16)],
                  compiler_params=_params())(h, g, w)


def _matmul_rms_res(a, w, h, g, name):
    t, k = a.shape
    d = w.shape[1]
    tm, _ = _tiles(t)

    def body(a_ref, w_ref, h_ref, g_ref, o_ref, hn_ref):
        o = jnp.dot(a_ref[...], w_ref[...], preferred_element_type=F32)
        o_ref[...] = o
        hn_ref[...] = h_ref[...] + (o * _rms(o)) * g_ref[...]

    return _pcall(body, name=name, grid=(t // tm,),
                  in_specs=[_row(tm, k), _const((k, d)), _row(tm, d), _const((1, d))],
                  out_specs=[_row(tm, d), _row(tm, d)],
                  out_shape=[jax.ShapeDtypeStruct((t, d), F32), jax.ShapeDtypeStruct((t, d), F32)],
                  compiler_params=_params())(a, w, h, g)


def _rmsbwd_matmul_nt(x, g, dy, w, name):
    t, d = x.shape
    n = w.shape[0]
    tm, _ = _tiles(t)
    cn = _col_chunk(n)

    def body(x_ref, g_ref, dy_ref, w_ref, da_ref, dxb_ref, dg_ref):
        @pl.when(pl.program_id(0) == 0)
        def _():
            dg_ref[...] = jnp.zeros((1, d), F32)

        dy = dy_ref[...]
        dx, xr = _rms_bwd(x_ref[...], g_ref[...], dy)
        dg_ref[...] += jnp.sum(dy * xr, axis=0, keepdims=True)
        dxb = dx.astype(BF16)
        dxb_ref[...] = dxb
        for c in range(n // cn):
            da_ref[:, c * cn:(c + 1) * cn] = lax.dot_general(dxb, w_ref[c * cn:(c + 1) * cn, :], NT_DIMS,
                                                             preferred_element_type=F32)

    return _pcall(body, name=name, grid=(t // tm,),
                  in_specs=[_row(tm, d), _const((1, d)), _row(tm, d), _const((n, d))],
                  out_specs=[_row(tm, n), _row(tm, d), _const_out((1, d))],
                  out_shape=[jax.ShapeDtypeStruct((t, n), F32), jax.ShapeDtypeStruct((t, d), BF16),
                             jax.ShapeDtypeStruct((1, d), F32)],
                  compiler_params=_params())(x, g, dy, w)


def _matmul_nt_rmsbwd_res(dp, w, h, g, dh, name):
    t, n = dp.shape
    d = w.shape[0]
    tm, _ = _tiles(t)

    def body(dp_ref, w_ref, h_ref, g_ref, dh_ref, out_ref, dg_ref):
        @pl.when(pl.program_id(0) == 0)
        def _():
            dg_ref[...] = jnp.zeros((1, d), F32)

        dz = lax.dot_general(dp_ref[...], w_ref[...], NT_DIMS, preferred_element_type=F32)
        dx, xr = _rms_bwd(h_ref[...], g_ref[...], dz)
        dg_ref[...] += jnp.sum(dz * xr, axis=0, keepdims=True)
        out_ref[...] = dh_ref[...] + dx

    return _pcall(body, name=name, grid=(t // tm,),
                  in_specs=[_row(tm, n), _const((d, n)), _row(tm, d), _const((1, d)), _row(tm, d)],
                  out_specs=[_row(tm, d), _const_out((1, d))],
                  out_shape=[jax.ShapeDtypeStruct((t, d), F32), jax.ShapeDtypeStruct((1, d), F32)],
                  compiler_params=_params())(dp, w, h, g, dh)


def _matmul_tn(a, b, name):
    t, k = a.shape
    n = b.shape[1]
    tm, _ = _tiles(t)
    nt = t // tm
    bn = min(n, (1536 * 1024) // k)
    assert n % bn == 0 and bn % LANES == 0

    def body(a_ref, b_ref, o_ref, acc):
        @pl.when(pl.program_id(1) == 0)
        def _():
            acc[...] = jnp.zeros((k, bn), F32)

        acc[...] += lax.dot_general(a_ref[...], b_ref[...], TN_DIMS, preferred_element_type=F32)

        @pl.when(pl.program_id(1) == nt - 1)
        def _():
            o_ref[...] = acc[...].astype(BF16)

    return _pcall(body, name=name, grid=(n // bn, nt),
                  in_specs=[pl.BlockSpec((tm, k), lambda j, i: (i, 0)), pl.BlockSpec((tm, bn), lambda j, i: (i, j))],
                  out_specs=pl.BlockSpec((k, bn), lambda j, i: (0, j)),
                  out_shape=jax.ShapeDtypeStruct((k, n), BF16),
                  scratch_shapes=[pltpu.VMEM((k, bn), F32)],
                  compiler_params=_params(2))(a, b)


MIX_PARAMS = ['lru_conv_w', 'lru_conv_b', 'lru_wa', 'lru_ba', 'lru_wx', 'lru_bx', 'lru_lambda', 'conv_w', 'conv_b',
              'conv_ln_g', 'conv_ln_b', 'g_out_lru', 'g_out_conv']


def _head_gates(xcb, wa_ref, wx_ref, pa_s, px_s, ba, bx, heads):
    for hd in range(heads):
        sl = slice(LANES * hd, LANES * (hd + 1))
        pa_s[:, sl] = jnp.dot(xcb[:, sl], wa_ref[hd], preferred_element_type=F32) + ba[:, sl]
        px_s[:, sl] = jnp.dot(xcb[:, sl], wx_ref[hd], preferred_element_type=F32) + bx[:, sl]


def _lru_conv_chunk(exta, cw4_ref, cb4, r0):
    win = exta[pl.ds(r0, 16), :]
    acc = cw4_ref[3:4, :] * win[8:16]
    for k in range(LRU_K - 1):
        acc = acc + cw4_ref[k:k + 1, :] * pltpu.roll(win, LRU_K - 1 - k, 0)[8:16]
    return acc + cb4


def _mix_fwd(proj, p, name):
    t = proj.shape[0]
    w = p['lru_conv_b'].shape[1]
    cw = p['conv_b'].shape[1]
    heads = p['lru_wa'].shape[0]
    _, tm = _tiles(t)
    nch = tm // 8

    def body(proj_ref, cw4_ref, cb4_ref, wa_ref, ba_ref, wx_ref, bx_ref, lam_ref, cw31_ref, cb31_ref, lng_ref,
             lnb_ref, ga_ref, gb_ref, y_ref, hs_ref, c1_ref, exta, xc_s, pa_s, px_s, extb, nbuf, hcar):
        @pl.when(pl.program_id(0) == 0)
        def _():
            exta[0:8, :] = jnp.zeros((8, w), F32)
            extb[0:32, :] = jnp.zeros((32, cw), F32)
            hcar[...] = jnp.zeros((8, w), F32)

        exta[8:8 + tm, :] = proj_ref[:, 0:w]
        cb4 = cb4_ref[...]

        def conv_a(c, carry):
            r0 = pl.multiple_of(c * 8, 8)
            xc_s[pl.ds(r0, 8), :] = _lru_conv_chunk(exta, cw4_ref, cb4, r0)
            return carry

        lax.fori_loop(0, nch, conv_a, 0, unroll=3)
        _head_gates(xc_s[...].astype(BF16), wa_ref, wx_ref, pa_s, px_s, ba_ref[...], bx_ref[...], heads)

        sp = _softplus(-lam_ref[...])
        ga = ga_ref[...]
        row = lax.broadcasted_iota(jnp.int32, (8, w), 0)

        def scan_c(cg, hprev):
            part = []
            for u in range(SCAN_U):
                r0 = pl.multiple_of((cg * SCAN_U + u) * 8, 8)
                xc = xc_s[pl.ds(r0, 8), :]
                _, ig, a, m = _lru_gates(pa_s[pl.ds(r0, 8), :], px_s[pl.ds(r0, 8), :], sp)
                aa, bb = a, m * (ig * xc)
                for d in (1, 2, 4):
                    a_s = pltpu.roll(aa, d, 0)
                    b_s = pltpu.roll(bb, d, 0)
                    msk = row >= d
                    bb = jnp.where(msk, aa * b_s + bb, bb)
                    aa = jnp.where(msk, aa * a_s, aa)
                part.append((r0, aa, bb, _gelu(proj_ref[pl.ds(r0, 8), w:2 * w])))
            for r0, aa, bb, ge in part:
                hs = aa * hprev + bb
                hs_ref[pl.ds(r0, 8), :] = hs
                ya = hs * ge
                nbuf[pl.ds(r0, 8), 0:w] = (ya * _rms(ya)) * ga
                hprev = hs[7:8, :]
            return hprev

        hcar[0:1, :] = lax.fori_loop(0, nch // SCAN_U, scan_c, hcar[0:1, :])

        extb[32:32 + tm, :] = proj_ref[:, 2 * w:2 * w + cw] * _sigmoid(proj_ref[:, 2 * w + cw:2 * w + 2 * cw])

        def conv_b(c, carry):
            r0 = pl.multiple_of(c * CONV_ROWS, 8)
            ybs = []
            ssq = jnp.zeros((CONV_ROWS, 1), F32)
            for lb in range(cw // LANES):
                sl = slice(LANES * lb, LANES * (lb + 1))
                win = extb[pl.ds(r0, CONV_ROWS + 32), sl]
                rolled = [win] + [pltpu.roll(win, rr, 0) for rr in range(1, 8)]
                parts = [None] * 4
                for k in range(CONV_K):
                    q, rr = divmod(CONV_K - 1 - k, 8)
                    term = cw31_ref[k:k + 1, sl] * rolled[rr][32 - 8 * q:32 - 8 * q + CONV_ROWS]
                    parts[k % 4] = term if parts[k % 4] is None else parts[k % 4] + term
                acc = ((parts[0] + parts[1]) + (parts[2] + parts[3])) + cb31_ref[:, sl]
                c1_ref[pl.ds(r0, CONV_ROWS), sl] = acc
                dlt = acc - jnp.mean(acc, axis=-1, keepdims=True)
                c2 = dlt * lax.rsqrt(jnp.mean(dlt * dlt, axis=-1, keepdims=True) + EPS)
                yb0 = c2 * lng_ref[:, sl] + lnb_ref[:, sl]
                yb = yb0 * _sigmoid(yb0)
                ybs.append(yb)
                ssq = ssq + jnp.sum(yb * yb, axis=-1, keepdims=True)
            rb = lax.rsqrt(ssq / cw + EPS)
            for lb in range(cw // LANES):
                sl = slice(LANES * lb, LANES * (lb + 1))
                nbuf[pl.ds(r0, CONV_ROWS), w + LANES * lb:w + LANES * (lb + 1)] = (ybs[lb] * rb) * gb_ref[:, sl]
            return carry

        lax.fori_loop(0, tm // CONV_ROWS, conv_b, 0, unroll=2)
        exta[0:8, :] = exta[tm:tm + 8, :]
        extb[0:32, :] = extb[tm:tm + 32, :]
        y_ref[...] = nbuf[...].astype(BF16)

    consts = [p[k] for k in MIX_PARAMS]
    return _pcall(body, name=name, grid=(t // tm,),
                  in_specs=[_row(tm, 2 * w + 2 * cw)] + [_const(c.shape) for c in consts],
                  out_specs=[_row(tm, w + cw), _row(tm, w), _row(tm, cw)],
                  out_shape=[jax.ShapeDtypeStruct((t, w + cw), BF16), jax.ShapeDtypeStruct((t, w), F32),
                             jax.ShapeDtypeStruct((t, cw), F32)],
                  scratch_shapes=[pltpu.VMEM((8 + tm, w), F32), pltpu.VMEM((tm, w), F32), pltpu.VMEM((tm, w), F32),
                                  pltpu.VMEM((tm, w), F32), pltpu.VMEM((32 + tm, cw), F32),
                                  pltpu.VMEM((tm, w + cw), F32), pltpu.VMEM((8, w), F32)],
                  compiler_params=_params())(proj, *consts)


def _mix_bwd(dy, proj, hs, c1, p, name):
    t = proj.shape[0]
    w = p['lru_conv_b'].shape[1]
    cw = p['conv_b'].shape[1]
    heads = p['lru_wa'].shape[0]
    _, tm = _tiles(t)
    nt = t // tm
    nch = tm // 8
    nlb = cw // LANES
    G_CB4, G_CW4, G_BA, G_BX, G_SP, G_GA, NGW = 0, 1, 5, 6, 7, 8, 9
    G_CB31, G_LNG, G_LNB, G_GB, G_CW31, NGC = 0, 1, 2, 3, 4, 4 + CONV_K

    def body(dy_ref, proj_ref, projh_ref, hs_ref, hsh_ref, c1_ref, cw4_ref, cb4_ref, wa_ref, ba_ref, wx_ref, bx_ref,
             lam_ref, cw31_ref, cb31_ref, lng_ref, lnb_ref, ga_ref, gb_ref,
             dproj_ref, dcw4_ref, dcb4_ref, dwa_ref, dba_ref, dwx_ref, dbx_ref, dlam_ref, dcw31_ref, dcb31_ref,
             dlng_ref, dlnb_ref, dga_ref, dgb_ref,
             exta, exth, xc_s, pa_s, px_s, dpa_s, dpx_s, dxce, extb, dc1e, dpf, cp_s, acc_w, acc_c):
        i = pl.program_id(0)

        @pl.when(i == 0)
        def _():
            acc_w[...] = jnp.zeros((8 * NGW, w), F32)
            acc_c[...] = jnp.zeros((8 * NGC, cw), F32)
            dwa_ref[...] = jnp.zeros(dwa_ref.shape, F32)
            dwx_ref[...] = jnp.zeros(dwx_ref.shape, F32)
            cp_s[...] = jnp.zeros((8, w), F32)
            dxce[tm:tm + 8, :] = jnp.zeros((8, w), F32)
            dc1e[tm:tm + 32, :] = jnp.zeros((32, cw), F32)

        nf = jnp.where(i < nt - 1, 1.0, 0.0).astype(F32)
        exta[0:8, :] = projh_ref[40:48, 0:w] * nf
        exta[8:8 + tm, :] = proj_ref[:, 0:w]
        exth[0:8, :] = hsh_ref[...] * nf
        exth[8:8 + tm, :] = hs_ref[...]
        extb[0:48, :] = (projh_ref[:, 2 * w:2 * w + cw] * _sigmoid(projh_ref[:, 2 * w + cw:2 * w + 2 * cw])) * nf
        extb[48:48 + tm, :] = proj_ref[:, 2 * w:2 * w + cw] * _sigmoid(proj_ref[:, 2 * w + cw:2 * w + 2 * cw])
        cb4 = cb4_ref[...]

        def conv_a(c, carry):
            r0 = pl.multiple_of(c * 8, 8)
            xc_s[pl.ds(r0, 8), :] = _lru_conv_chunk(exta, cw4_ref, cb4, r0)
            return carry

        lax.fori_loop(0, nch, conv_a, 0, unroll=3)
        xcb = xc_s[...].astype(BF16)
        _head_gates(xcb, wa_ref, wx_ref, pa_s, px_s, ba_ref[...], bx_ref[...], heads)

        sp = _softplus(-lam_ref[...])
        ga = ga_ref[...]
        row = lax.broadcasted_iota(jnp.int32, (8, w), 0)

        def acc_add(ref, g, val, sl=slice(None)):
            for j in range(val.shape[0] // 8):
                ref[8 * g:8 * g + 8, sl] = ref[8 * g:8 * g + 8, sl] + val[8 * j:8 * j + 8]

        def rscan(cg, cp):
            part = []
            for u in range(RSCAN_U):
                r0 = pl.multiple_of((nch - 1 - (cg * RSCAN_U + u)) * 8, 8)
                xc = xc_s[pl.ds(r0, 8), :]
                r, ig, a, m = _lru_gates(pa_s[pl.ds(r0, 8), :], px_s[pl.ds(r0, 8), :], sp)
                hwin = exth[pl.ds(r0, 16), :]
                hcur = hwin[8:16]
                hprev = pltpu.roll(hwin, 1, 0)[8:16]
                ge, dge = _gelu_and_grad(proj_ref[pl.ds(r0, 8), w:2 * w])
                dna = dy_ref[pl.ds(r0, 8), 0:w]
                dya, yar = _rms_bwd(hcur * ge, ga, dna)
                acc_add(acc_w, G_GA, dna * yar)
                dpf[pl.ds(r0, 8), w:2 * w] = (dya * hcur) * dge
                aa = jnp.where(row == 7, 1.0, pltpu.roll(a, 7, 0))
                bb = dya * ge
                for d in (1, 2, 4):
                    a_s = pltpu.roll(aa, 8 - d, 0)
                    b_s = pltpu.roll(bb, 8 - d, 0)
                    msk = row < 8 - d
                    bb = jnp.where(msk, aa * b_s + bb, bb)
                    aa = jnp.where(msk, aa * a_s, aa)
                part.append((r0, aa, bb, xc, r, ig, a, m, hprev))
            for r0, aa, bb, xc, r, ig, a, m, hprev in part:
                lamb = bb + aa * cp
                cp = a[0:1, :] * lamb[0:1, :]
                dm = lamb * (ig * xc)
                di = lamb * (m * xc)
                dxce[pl.ds(r0, 8), :] = lamb * (m * ig)
                dla = a * (lamb * hprev - dm * (a / m))
                acc_add(acc_w, G_SP, dla * (-LRU_C * r))
                dpa = (dla * (-LRU_C * sp)) * (r * (1.0 - r))
                dpx = di * (ig * (1.0 - ig))
                acc_add(acc_w, G_BA, dpa)
                acc_add(acc_w, G_BX, dpx)
                dpa_s[pl.ds(r0, 8), :] = dpa
                dpx_s[pl.ds(r0, 8), :] = dpx
            return cp

        cp_s[0:1, :] = lax.fori_loop(0, nch // RSCAN_U, rscan, cp_s[0:1, :])

        dpab = dpa_s[...].astype(BF16)
        dpxb = dpx_s[...].astype(BF16)
        for hd in range(heads):
            sl = slice(LANES * hd, LANES * (hd + 1))
            dxce[0:tm, sl] = (dxce[0:tm, sl]
                              + lax.dot_general(dpab[:, sl], wa_ref[hd], NT_DIMS, preferred_element_type=F32)
                              + lax.dot_general(dpxb[:, sl], wx_ref[hd], NT_DIMS, preferred_element_type=F32))
            dwa_ref[hd] = dwa_ref[hd] + lax.dot_general(xcb[:, sl], dpab[:, sl], TN_DIMS, preferred_element_type=F32)
            dwx_ref[hd] = dwx_ref[hd] + lax.dot_general(xcb[:, sl], dpxb[:, sl], TN_DIMS, preferred_element_type=F32)

        def conv_a_bwd(c, carry):
            r0 = pl.multiple_of(c * 8, 8)
            win = dxce[pl.ds(r0, 16), :]
            dxc = win[0:8]
            xwin = exta[pl.ds(r0, 16), :]
            dxl = cw4_ref[3:4, :] * dxc
            acc_add(acc_w, G_CB4, dxc)
            acc_add(acc_w, G_CW4 + 3, dxc * xwin[8:16])
            for k in range(LRU_K - 1):
                s = LRU_K - 1 - k
                dxl = dxl + cw4_ref[k:k + 1, :] * pltpu.roll(win, 16 - s, 0)[0:8]
                acc_add(acc_w, G_CW4 + k, dxc * pltpu.roll(xwin, s, 0)[8:16])
            dpf[pl.ds(r0, 8), 0:w] = dxl
            return carry

        lax.fori_loop(0, nch, conv_a_bwd, 0, unroll=3)
        dxce[tm:tm + 8, :] = dxce[0:8, :]

        def mixb(c, carry):
            r0 = pl.multiple_of(c * 8, 8)
            st = []
            ssq = jnp.zeros((8, 1), F32)
            for lb in range(nlb):
                sl = slice(LANES * lb, LANES * (lb + 1))
                c1v = c1_ref[pl.ds(r0, 8), sl]
                dlt = c1v - jnp.mean(c1v, axis=-1, keepdims=True)
                rs = lax.rsqrt(jnp.mean(dlt * dlt, axis=-1, keepdims=True) + EPS)
                c2 = dlt * rs
                yb0 = c2 * lng_ref[:, sl] + lnb_ref[:, sl]
                sg = _sigmoid(yb0)
                yb = yb0 * sg
                ssq = ssq + jnp.sum(yb * yb, axis=-1, keepdims=True)
                st.append((rs, c2, yb0, sg, yb))
            rb = lax.rsqrt(ssq / cw + EPS)
            tsum = jnp.zeros((8, 1), F32)
            dngs = []
            for lb in range(nlb):
                sl = slice(LANES * lb, LANES * (lb + 1))
                dnb = dy_ref[pl.ds(r0, 8), w + LANES * lb:w + LANES * (lb + 1)]
                ybr = st[lb][4] * rb
                acc_add(acc_c, G_GB, dnb * ybr, sl)
                dng = dnb * gb_ref[:, sl]
                dngs.append((dng, ybr))
                tsum = tsum + jnp.sum(dng * ybr, axis=-1, keepdims=True)
            tsum = tsum / cw
            for lb in range(nlb):
                sl = slice(LANES * lb, LANES * (lb + 1))
                rs, c2, yb0, sg, _ = st[lb]
                dng, ybr = dngs[lb]
                dyb0 = (rb * (dng - ybr * tsum)) * (sg * (1.0 + yb0 * (1.0 - sg)))
                acc_add(acc_c, G_LNG, dyb0 * c2, sl)
                acc_add(acc_c, G_LNB, dyb0, sl)
                dc2 = dyb0 * lng_ref[:, sl]
                dc1 = rs * (dc2 - jnp.mean(dc2, axis=-1, keepdims=True)
                            - c2 * jnp.mean(dc2 * c2, axis=-1, keepdims=True))
                acc_add(acc_c, G_CB31, dc1, sl)
                dc1e[pl.ds(r0, 8), sl] = dc1
            return carry

        lax.fori_loop(0, nch, mixb, 0, unroll=6)

        def conv_b_bwd(c, carry):
            r0 = pl.multiple_of(c * CONV_ROWS, 8)
            nwin = CONV_ROWS + 32
            for lb in range(nlb):
                sl = slice(LANES * lb, LANES * (lb + 1))
                win = dc1e[pl.ds(r0, nwin), sl]
                ups = [win] + [pltpu.roll(win, nwin - rr, 0) for rr in range(1, 8)]
                dc1 = win[0:CONV_ROWS]
                parts = [None] * 4
                for k in range(CONV_K):
                    q, rr = divmod(CONV_K - 1 - k, 8)
                    term = cw31_ref[k:k + 1, sl] * ups[rr][8 * q:8 * q + CONV_ROWS]
                    parts[k % 4] = term if parts[k % 4] is None else parts[k % 4] + term
                dc0 = (parts[0] + parts[1]) + (parts[2] + parts[3])
                cav = proj_ref[pl.ds(r0, CONV_ROWS), 2 * w + LANES * lb:2 * w + LANES * (lb + 1)]
                sg = _sigmoid(proj_ref[pl.ds(r0, CONV_ROWS), 2 * w + cw + LANES * lb:2 * w + cw + LANES * (lb + 1)])
                dpf[pl.ds(r0, CONV_ROWS), 2 * w + LANES * lb:2 * w + LANES * (lb + 1)] = dc0 * sg
                dpf[pl.ds(r0, CONV_ROWS), 2 * w + cw + LANES * lb:2 * w + cw + LANES * (lb + 1)] = (
                    (dc0 * cav) * (sg * (1.0 - sg)))
                xwin = extb[pl.ds(pl.multiple_of(r0 + 16, 8), nwin), sl]
                xr = [xwin] + [pltpu.roll(xwin, rr, 0) for rr in range(1, 8)]
                for k in range(CONV_K):
                    q, rr = divmod(CONV_K - 1 - k, 8)
                    acc_add(acc_c, G_CW31 + k, dc1 * xr[rr][32 - 8 * q:32 - 8 * q + CONV_ROWS], sl)
            return carry

        lax.fori_loop(0, tm // CONV_ROWS, conv_b_bwd, 0, unroll=2)
        dc1e[tm:tm + 32, :] = dc1e[0:32, :]
        dproj_ref[...] = dpf[...].astype(BF16)

        @pl.when(i == nt - 1)
        def _():
            def fold(ref, g):
                return jnp.sum(ref[8 * g:8 * g + 8, :], axis=0, keepdims=True)

            dcb4_ref[...] = fold(acc_w, G_CB4)
            for k in range(LRU_K):
                dcw4_ref[k:k + 1, :] = fold(acc_w, G_CW4 + k)
            dba_ref[...] = fold(acc_w, G_BA)
            dbx_ref[...] = fold(acc_w, G_BX)
            dlam_ref[...] = fold(acc_w, G_SP) * (-_sigmoid(-lam_ref[...]))
            dga_ref[...] = fold(acc_w, G_GA)
            dcb31_ref[...] = fold(acc_c, G_CB31)
            dlng_ref[...] = fold(acc_c, G_LNG)
            dlnb_ref[...] = fold(acc_c, G_LNB)
            dgb_ref[...] = fold(acc_c, G_GB)
            for k in range(CONV_K):
                dcw31_ref[k:k + 1, :] = fold(acc_c, G_CW31 + k)

    consts = [p[k] for k in MIX_PARAMS]
    outs = _pcall(body, name=name, grid=(nt,),
                  in_specs=[_rrow(tm, w + cw, nt), _rrow(tm, 2 * w + 2 * cw, nt), _halo(48, 2 * w + 2 * cw, tm, nt),
                            _rrow(tm, w, nt), _halo(8, w, tm, nt), _rrow(tm, cw, nt)] + [_const(c.shape) for c in consts],
                  out_specs=[_rrow(tm, 2 * w + 2 * cw, nt)] + [_const_out(c.shape) for c in consts],
                  out_shape=[jax.ShapeDtypeStruct((t, 2 * w + 2 * cw), BF16)]
                  + [jax.ShapeDtypeStruct(c.shape, F32) for c in consts],
                  scratch_shapes=[pltpu.VMEM((8 + tm, w), F32), pltpu.VMEM((8 + tm, w), F32), pltpu.VMEM((tm, w), F32),
                                  pltpu.VMEM((tm, w), F32), pltpu.VMEM((tm, w), F32), pltpu.VMEM((tm, w), F32),
                                  pltpu.VMEM((tm, w), F32), pltpu.VMEM((tm + 8, w), F32),
                                  pltpu.VMEM((48 + tm, cw), F32), pltpu.VMEM((tm + 32, cw), F32),
                                  pltpu.VMEM((tm, 2 * w + 2 * cw), F32), pltpu.VMEM((8, w), F32),
                                  pltpu.VMEM((8 * NGW, w), F32), pltpu.VMEM((8 * NGC, cw), F32)],
                  compiler_params=_params())(dy, proj, proj, hs, hs, c1, *consts)
    return outs[0], dict(zip(MIX_PARAMS, outs[1:]))


def _ffn_window(u_ref, halo, c, col):
    if isinstance(c, int) and c == 0:
        return jnp.concatenate([halo[:, col:col + LANES], u_ref[0:16, col:col + LANES]], axis=0)
    return u_ref[pl.ds(pl.multiple_of(c * 16 - 8, 8), 24), col:col + LANES]


def _ffn_conv(win, w3_ref, b3_ref, col):
    sl = slice(col, col + LANES)
    x1 = pltpu.roll(win, 1, 0)[8:24]
    x2 = pltpu.roll(win, 2, 0)[8:24]
    u = w3_ref[2:3, sl] * win[8:24] + w3_ref[1:2, sl] * x1 + w3_ref[0:1, sl] * x2 + b3_ref[:, sl]
    return u, (x2, x1, win[8:24])


def _ffn_act_fwd(u0, w3, b3, name):
    t, f2 = u0.shape
    ff = f2 // 2
    _, tm = _tiles(t)
    nch = tm // 16

    def body(u_ref, w3_ref, b3_ref, act_ref, car):
        @pl.when(pl.program_id(0) == 0)
        def _():
            car[...] = jnp.zeros((8, f2), F32)

        def chunk(c):
            halo = car[...] if isinstance(c, int) else None
            r0 = 0 if isinstance(c, int) else pl.multiple_of(c * 16, 16)
            for j in range(ff // LANES):
                gate, _ = _ffn_conv(_ffn_window(u_ref, halo, c, LANES * j), w3_ref, b3_ref, LANES * j)
                up, _ = _ffn_conv(_ffn_window(u_ref, halo, c, ff + LANES * j), w3_ref, b3_ref, ff + LANES * j)
                act_ref[pl.ds(r0, 16), LANES * j:LANES * (j + 1)] = (_gelu(gate) * up).astype(BF16)

        chunk(0)

        def loop(c, carry):
            chunk(c)
            return carry

        lax.fori_loop(1, nch, loop, 0)
        car[...] = u_ref[tm - 8:tm, :]

    return _pcall(body, name=name, grid=(t // tm,),
                  in_specs=[_row(tm, f2), _const(w3.shape), _const(b3.shape)],
                  out_specs=_row(tm, ff), out_shape=jax.ShapeDtypeStruct((t, ff), BF16),
                  scratch_shapes=[pltpu.VMEM((8, f2), F32)],
                  compiler_params=_params())(u0, w3, b3)


def _ffn_act_bwd(dact, u0, w3, b3, name):
    t, f2 = u0.shape
    ff = f2 // 2
    _, tm = _tiles(t)
    nt = t // tm
    nch = tm // 16

    def body(dact_ref, u_ref, uh_ref, w3_ref, b3_ref, du0_ref, dw3_ref, db3_ref, dub, acc):
        i = pl.program_id(0)

        @pl.when(i == 0)
        def _():
            dub[tm:tm + 8, :] = jnp.zeros((8, f2), F32)
            acc[...] = jnp.zeros((32, f2), F32)

        nf = jnp.where(i < nt - 1, 1.0, 0.0).astype(F32)

        def acc_add(g, val, sl):
            acc[8 * g:8 * g + 8, sl] = acc[8 * g:8 * g + 8, sl] + (val[0:8] + val[8:16])

        def chunk(c):
            halo = uh_ref[...] * nf if isinstance(c, int) else None
            r0 = 0 if isinstance(c, int) else pl.multiple_of(c * 16, 16)
            for j in range(ff // LANES):
                cg, cu = LANES * j, ff + LANES * j
                gate, xg = _ffn_conv(_ffn_window(u_ref, halo, c, cg), w3_ref, b3_ref, cg)
                up, xu = _ffn_conv(_ffn_window(u_ref, halo, c, cu), w3_ref, b3_ref, cu)
                ge, dge = _gelu_and_grad(gate)
                da = dact_ref[pl.ds(r0, 16), cg:cg + LANES]
                for col, du, xs in ((cg, (da * up) * dge, xg), (cu, da * ge, xu)):
                    sl = slice(col, col + LANES)
                    dub[pl.ds(r0, 16), sl] = du
                    acc_add(0, du, sl)
                    for k in range(FFN_K):
                        acc_add(1 + k, du * xs[k], sl)

        chunk(0)

        def loop1(c, carry):
            chunk(c)
            return carry

        lax.fori_loop(1, nch, loop1, 0)

        def loop2(c, carry):
            r0 = pl.multiple_of(c * 16, 16)
            for j in range(f2 // LANES):
                sl = slice(LANES * j, LANES * (j + 1))
                win = dub[pl.ds(r0, 24), sl]
                du0 = (w3_ref[2:3, sl] * win[0:16] + w3_ref[1:2, sl] * pltpu.roll(win, 23, 0)[0:16]
                       + w3_ref[0:1, sl] * pltpu.roll(win, 22, 0)[0:16])
                du0_ref[pl.ds(r0, 16), sl] = du0.astype(BF16)
            return carry

        lax.fori_loop(0, nch, loop2, 0)
        dub[tm:tm + 8, :] = dub[0:8, :]

        @pl.when(i == nt - 1)
        def _():
            db3_ref[...] = jnp.sum(acc[0:8, :], axis=0, keepdims=True)
            for k in range(FFN_K):
                dw3_ref[k:k + 1, :] = jnp.sum(acc[8 + 8 * k:16 + 8 * k, :], axis=0, keepdims=True)

    return _pcall(body, name=name, grid=(nt,),
                  in_specs=[_rrow(tm, ff, nt), _rrow(tm, f2, nt), _halo(8, f2, tm, nt), _const(w3.shape),
                            _const(b3.shape)],
                  out_specs=[_rrow(tm, f2, nt), _const_out(w3.shape), _const_out(b3.shape)],
                  out_shape=[jax.ShapeDtypeStruct((t, f2), BF16), jax.ShapeDtypeStruct(w3.shape, F32),
                             jax.ShapeDtypeStruct(b3.shape, F32)],
                  scratch_shapes=[pltpu.VMEM((tm + 8, f2), F32), pltpu.VMEM((32, f2), F32)],
                  compiler_params=_params())(dact, u0, u0, w3, b3)


def _loss_head(h, tgt, name):
    t, d = h.shape
    tm, _ = _tiles(t)

    def body(h_ref, t_ref, dh_ref, s_ref):
        i = pl.program_id(0)

        @pl.when(i == 0)
        def _():
            s_ref[...] = jnp.zeros((1, d), F32)

        row = lax.broadcasted_iota(jnp.int32, (tm, d), 0) + i * tm
        err = jnp.where(row >= N_META, h_ref[...] - t_ref[...], 0.0)
        dh_ref[...] = err / d
        s_ref[...] += jnp.sum(err * err, axis=0, keepdims=True)

    return _pcall(body, name=name, grid=(t // tm,), in_specs=[_row(tm, d), _row(tm, d)],
                  out_specs=[_row(tm, d), _const_out((1, d))],
                  out_shape=[jax.ShapeDtypeStruct((t, d), F32), jax.ShapeDtypeStruct((1, d), F32)],
                  compiler_params=_params())(h, tgt)


def _row_tile(rows, row_bytes, budget):
    best = None
    for tr in range(16, rows + 1, 16):
        if rows % tr == 0 and tr * row_bytes <= budget:
            best = tr
    assert best is not None, (rows, row_bytes)
    return best


def _cast_bf16(a, name):
    r, c = a.shape
    tr = _row_tile(r, c * 4, 4 << 20)

    def body(a_ref, o_ref):
        o_ref[...] = a_ref[...].astype(BF16)

    return _pcall(body, name=name, grid=(r // tr,), in_specs=[_row(tr, c)], out_specs=_row(tr, c),
                  out_shape=jax.ShapeDtypeStruct((r, c), BF16), compiler_params=_params())(a)


def _sum_slots(r, name):
    s, rows, c = r.shape
    tr = _row_tile(rows, s * c * 4, 8 << 20)

    def body(r_ref, o_ref):
        acc = r_ref[0].astype(F32)
        for k in range(1, s):
            acc = acc + r_ref[k].astype(F32)
        o_ref[...] = acc

    return _pcall(body, name=name, grid=(rows // tr,),
                  in_specs=[pl.BlockSpec((s, tr, c), lambda i: (0, i, 0))], out_specs=_row(tr, c),
                  out_shape=jax.ShapeDtypeStruct((rows, c), F32), compiler_params=_params())(r)


def _cast_into_window(a, axis, k1, name):
    l, r, c = a.shape
    shape = (l, 4 * r, c) if axis == 1 else (l, r, 4 * c)

    def body(k_ref, a_ref, o_ref):
        o_ref[...] = a_ref[...].astype(BF16)

    omap = (lambda i, k: (i, k[0], 0)) if axis == 1 else (lambda i, k: (i, 0, k[0]))
    gs = pltpu.PrefetchScalarGridSpec(num_scalar_prefetch=1, grid=(l,),
                                      in_specs=[pl.BlockSpec((1, r, c), lambda i, k: (i, 0, 0))],
                                      out_specs=pl.BlockSpec((1, r, c), omap))
    return _pcall(body, name=name, grid_spec=gs, out_shape=jax.ShapeDtypeStruct(shape, BF16),
                  compiler_params=_params())(k1, a)


def _adamw(g, w, m, v, name):
    r, c = g.shape
    tr = _row_tile(r, c * 4, 1 << 20)

    def body(g_ref, w_ref, m_ref, v_ref, d_ref, m2_ref, v2_ref):
        gv = g_ref[...]
        m2 = ADAM_B1 * m_ref[...] + (1.0 - ADAM_B1) * gv
        v2 = ADAM_B2 * v_ref[...] + (1.0 - ADAM_B2) * (gv * gv)
        m_hat = m2 / (1.0 - ADAM_B1 ** ADAM_STEP)
        v_hat = v2 / (1.0 - ADAM_B2 ** ADAM_STEP)
        d_ref[...] = -ADAM_LR * (m_hat / (jnp.sqrt(v_hat) + ADAM_EPS) + ADAM_WD * w_ref[...])
        m2_ref[...] = m2
        v2_ref[...] = v2

    return _pcall(body, name=name, grid=(r // tr,), in_specs=[_row(tr, c)] * 4, out_specs=[_row(tr, c)] * 3,
                  out_shape=[jax.ShapeDtypeStruct((r, c), F32)] * 3, compiler_params=_params())(g, w, m, v)


ANY = pl.BlockSpec(memory_space=pl.ANY)


def _coords():
    return lax.axis_index("x"), lax.axis_index("y"), lax.axis_index("c")


def _window(ref, lead, axis, k, width):
    idx = [slice(None)] * len(ref.shape)
    idx[0] = lead
    idx[axis] = pl.ds(pl.multiple_of(k * width, LANES if axis == len(ref.shape) - 1 else 8), width)
    return ref.at[tuple(idx)]


def _gather_xy(arrs, axes, n_inplace, name):
    n = len(arrs)
    out_shape, widths = [], []
    for i, (a, ax) in enumerate(zip(arrs, axes)):
        s = list(a.shape)
        if i < n_inplace:
            widths.append(s[ax] // 4)
        else:
            widths.append(s[ax])
            s[ax] *= 4
        out_shape.append(jax.ShapeDtypeStruct(tuple(s), a.dtype))

    def body(*refs):
        ins, outs = refs[:n], refs[n:2 * n]
        send_sems, recv_sems, loc_sems = refs[2 * n:]
        x, y, c = _coords()
        k_me = 2 * x + y
        chips = [(1 - x, y), (x, 1 - y), (1 - x, 1 - y)]
        sib = (x, y, 1 - c)

        def half(i, which):
            hl = arrs[i].shape[0] // 2
            return pl.ds(which * hl, hl)

        def win(i, kk, which):
            return _window(outs[i], half(i, which), axes[i], kk, widths[i])

        def copy(i, s, src, dst, to):
            return pltpu.make_async_remote_copy(src_ref=src, dst_ref=dst, send_sem=send_sems.at[i, s],
                                                recv_sem=recv_sems.at[i, s], device_id=to, device_id_type=MESH_T)

        locs = []
        for i in range(n_inplace, n):
            lc = pltpu.make_async_copy(ins[i], _window(outs[i], slice(None), axes[i], k_me, widths[i]), loc_sems.at[i])
            lc.start()
            locs.append(lc)
        started = []
        for i in range(n):
            for j, chip in enumerate(chips):
                src = win(i, k_me, c) if i < n_inplace else ins[i].at[half(i, c)]
                cp = copy(i, j, src, win(i, k_me, c), (*chip, c))
                cp.start()
                started.append(cp)
        for i in range(n):
            for j, chip in enumerate(chips):
                kk = 2 * chip[0] + chip[1]
                copy(i, j, win(i, kk, c), win(i, kk, c), (*chip, c)).wait_recv()
                fw = copy(i, 3 + j, win(i, kk, c), win(i, kk, c), sib)
                fw.start()
                started.append(fw)
        for i in range(n):
            for j, chip in enumerate(chips):
                kk = 2 * chip[0] + chip[1]
                copy(i, 3 + j, win(i, kk, 1 - c), win(i, kk, 1 - c), sib).wait_recv()
        for cp in started:
            cp.wait_send()
        for lc in locs:
            lc.wait()

    return _pcall(body, name=name, in_specs=[ANY] * n, out_specs=[ANY] * n, out_shape=out_shape,
                  input_output_aliases={i: i for i in range(n_inplace)},
                  scratch_shapes=[pltpu.SemaphoreType.DMA((n, 6)), pltpu.SemaphoreType.DMA((n, 6)),
                                  pltpu.SemaphoreType.DMA((n,))],
                  compiler_params=pltpu.CompilerParams(has_side_effects=True))(*arrs)


def _peer(x, y, c, mask):
    bx, by, bc = (mask >> 2) & 1, (mask >> 1) & 1, mask & 1
    return (1 - x if bx else x, 1 - y if by else y, 1 - c if bc else c)


HBM = pl.BlockSpec(memory_space=pltpu.HBM)
SEM = pl.BlockSpec(memory_space=pltpu.SEMAPHORE)


def _piece_shape(shape, wa):
    r, c = shape
    return (r // 2, c // 4) if wa == 1 else (r // 8, c)


def _piece(ref, wa, k, h):
    r, c = ref.shape
    if wa == 1:
        return ref.at[pl.ds(pl.multiple_of(h * (r // 2), 16), r // 2), pl.ds(pl.multiple_of(k * (c // 4), LANES), c // 4)]
    return ref.at[pl.ds(pl.multiple_of((2 * k + h) * (r // 8), 16), r // 8), :]


def _scatter_start(grads, was, name):
    n = len(grads)
    lands = [lax.empty((7, *_piece_shape(g.shape, wa)), g.dtype) for g, wa in zip(grads, was)]

    def body(*refs):
        g_in, land_in = refs[:n], refs[n:2 * n]
        send_sems, recv_sems = refs[2 * n:2 * n + 7 * n], refs[2 * n + 7 * n:2 * n + 14 * n]
        token = refs[-1]
        x, y, c = _coords()
        for i in range(n):
            for mask in range(1, 8):
                px, py, pc = _peer(x, y, c, mask)
                pltpu.make_async_remote_copy(src_ref=_piece(g_in[i], was[i], 2 * px + py, pc),
                                             dst_ref=land_in[i].at[mask - 1], send_sem=send_sems[7 * i + mask - 1],
                                             recv_sem=recv_sems[7 * i + mask - 1], device_id=(px, py, pc),
                                             device_id_type=MESH_T).start()
        token[...] = jnp.zeros(token.shape, F32)

    args = [pltpu.with_memory_space_constraint(a, pltpu.HBM) for a in list(grads) + lands]
    res = _pcall(body, name=name, in_specs=[HBM] * (2 * n),
                 out_specs=[SEM] * (14 * n) + [HBM] * (2 * n) + [pl.BlockSpec(memory_space=pltpu.VMEM)],
                 out_shape=[pltpu.SemaphoreType.DMA(())] * (14 * n)
                 + [pltpu.HBM(a.shape, a.dtype) for a in args] + [jax.ShapeDtypeStruct((8, LANES), F32)],
                 input_output_aliases={i: 14 * n + i for i in range(2 * n)},
                 compiler_params=pltpu.CompilerParams(has_side_effects=pltpu.SideEffectType.DATAFLOW_SIDE_EFFECTING))(*args)
    return res[:7 * n], res[7 * n:14 * n], res[14 * n:15 * n], res[15 * n:16 * n], res[-1]


def _scatter_wait(send_sems, recv_sems, grads, lands, was, after, name):
    n = len(grads)

    def body(*refs):
        g_in, land_in = refs[:n], refs[n:2 * n]
        s_sems, r_sems = refs[2 * n:2 * n + 7 * n], refs[2 * n + 7 * n:2 * n + 14 * n]
        x, y, c = _coords()
        for i in range(n):
            for mask in range(1, 8):
                px, py, pc = _peer(x, y, c, mask)
                cp = pltpu.make_async_remote_copy(src_ref=_piece(g_in[i], was[i], 2 * px + py, pc),
                                                  dst_ref=land_in[i].at[mask - 1], send_sem=s_sems[7 * i + mask - 1],
                                                  recv_sem=r_sems[7 * i + mask - 1], device_id=(px, py, pc),
                                                  device_id_type=MESH_T)
                cp.wait_send()
                cp.wait_recv()

    args = list(grads) + list(lands)
    res = _pcall(body, name=name, in_specs=[HBM] * (2 * n) + [SEM] * (14 * n) + [ANY], out_specs=[HBM] * (2 * n),
                 out_shape=[pltpu.HBM(a.shape, a.dtype) for a in args],
                 input_output_aliases={i: i for i in range(2 * n)},
                 compiler_params=pltpu.CompilerParams(has_side_effects=pltpu.SideEffectType.DATAFLOW_SIDE_EFFECTING))(
                     *args, *send_sems, *recv_sems, after)
    return res[:n], res[n:]


def _sum_pieces(land, g, wa, k1, c1, name):
    s, rp, cp = land.shape
    tr = _row_tile(rp, (s + 1) * cp * 4, 8 << 20)
    nb = rp // tr
    if wa == 1:
        own = pl.BlockSpec((tr, cp), lambda i, k, c: (c[0] * nb + i, k[0]))
    else:
        own = pl.BlockSpec((tr, cp), lambda i, k, c: ((2 * k[0] + c[0]) * nb + i, 0))

    def body(k_ref, c_ref, l_ref, g_ref, o_ref):
        acc = l_ref[0].astype(F32)
        for j in range(1, s):
            acc = acc + l_ref[j].astype(F32)
        o_ref[...] = acc + g_ref[...].astype(F32)

    gs = pltpu.PrefetchScalarGridSpec(
        num_scalar_prefetch=2, grid=(nb,),
        in_specs=[pl.BlockSpec((s, tr, cp), lambda i, k, c: (0, i, 0)), own],
        out_specs=pl.BlockSpec((tr, cp), lambda i, k, c: (c[0] * nb + i, 0)))
    return _pcall(body, name=name, grid_spec=gs, out_shape=jax.ShapeDtypeStruct((2 * rp, cp), F32),
                  compiler_params=_params())(k1, c1, land, g)


SWAP_CHUNKS = 4


def _sibling_swap(fulls, name):
    n = len(fulls)
    out_shape = [jax.ShapeDtypeStruct(a.shape, a.dtype) for a in fulls]

    def body(*refs):
        outs = refs[n:2 * n]
        send_sems, recv_sems = refs[2 * n:]
        x, y, c = _coords()

        def chunk(i, which, q):
            hl, rc = fulls[i].shape[0] // 2, fulls[i].shape[1] // SWAP_CHUNKS
            ref = outs[i].at[pl.ds(which * hl, hl), pl.ds(q * rc, rc)]
            return pltpu.make_async_remote_copy(src_ref=ref, dst_ref=ref, send_sem=send_sems.at[i, q],
                                                recv_sem=recv_sems.at[i, q], device_id=(x, y, 1 - c),
                                                device_id_type=MESH_T)

        started = []
        for i in range(n):
            for q in range(SWAP_CHUNKS):
                cp = chunk(i, c, q)
                cp.start()
                started.append(cp)
        for cp in started:
            cp.wait_send()
        for i in range(n):
            for q in range(SWAP_CHUNKS):
                chunk(i, 1 - c, q).wait_recv()

    for a in fulls:
        assert a.shape[1] % (8 * SWAP_CHUNKS) == 0, a.shape
    return _pcall(body, name=name, in_specs=[ANY] * n, out_specs=[ANY] * n, out_shape=out_shape,
                  input_output_aliases={i: i for i in range(n)},
                  scratch_shapes=[pltpu.SemaphoreType.DMA((n, SWAP_CHUNKS)), pltpu.SemaphoreType.DMA((n, SWAP_CHUNKS))],
                  compiler_params=pltpu.CompilerParams(has_side_effects=True))(*fulls)


def _scatter8(pk, name):
    r, cdim = pk.shape
    pr = r // 8
    assert pr % 8 == 0

    def body(p_ref, o_ref, send_sems, recv_sems, loc_sem):
        x, y, c = _coords()

        def piece(px, py, pc):
            return p_ref.at[pl.ds(pl.multiple_of((4 * px + 2 * py + pc) * pr, 8), pr)]

        lc = pltpu.make_async_copy(piece(x, y, c), o_ref.at[7], loc_sem)
        lc.start()
        started = []
        for mask in range(1, 8):
            px, py, pc = _peer(x, y, c, mask)
            cp = pltpu.make_async_remote_copy(src_ref=piece(px, py, pc), dst_ref=o_ref.at[mask - 1],
                                              send_sem=send_sems.at[mask - 1], recv_sem=recv_sems.at[mask - 1],
                                              device_id=(px, py, pc), device_id_type=MESH_T)
            cp.start()
            started.append(cp)
        for cp in started:
            cp.wait()
        lc.wait()

    return _pcall(body, name=name, in_specs=[ANY], out_specs=ANY, out_shape=jax.ShapeDtypeStruct((8, pr, cdim), F32),
                  scratch_shapes=[pltpu.SemaphoreType.DMA((7,)), pltpu.SemaphoreType.DMA((7,)),
                                  pltpu.SemaphoreType.DMA],
                  compiler_params=pltpu.CompilerParams(has_side_effects=True))(pk)


def _gather_all(pk, name):
    r, cdim = pk.shape

    def body(p_ref, o_ref, send_sems, recv_sems, loc_sem):
        x, y, c = _coords()
        lc = pltpu.make_async_copy(p_ref, o_ref.at[4 * x + 2 * y + c], loc_sem)
        lc.start()
        started = []
        for mask in range(1, 8):
            px, py, pc = _peer(x, y, c, mask)
            cp = pltpu.make_async_remote_copy(src_ref=p_ref, dst_ref=o_ref.at[4 * x + 2 * y + c],
                                              send_sem=send_sems.at[mask - 1], recv_sem=recv_sems.at[mask - 1],
                                              device_id=(px, py, pc), device_id_type=MESH_T)
            cp.start()
            started.append(cp)
        for cp in started:
            cp.wait()
        lc.wait()

    return _pcall(body, name=name, in_specs=[ANY], out_specs=ANY, out_shape=jax.ShapeDtypeStruct((8, r, cdim), F32),
                  scratch_shapes=[pltpu.SemaphoreType.DMA((7,)), pltpu.SemaphoreType.DMA((7,)),
                                  pltpu.SemaphoreType.DMA],
                  compiler_params=pltpu.CompilerParams(has_side_effects=True))(pk)


PACK_C = 1024


def _pack(arrs, row_mult):
    parts = []
    for a in arrs:
        flat = a.reshape(-1)
        parts.append(jnp.pad(flat, (0, (-flat.shape[0]) % PACK_C)))
    flat = jnp.concatenate(parts)
    flat = jnp.pad(flat, (0, (-flat.shape[0]) % (PACK_C * row_mult)))
    return flat.reshape(-1, PACK_C)


def _unpack(pk, shapes):
    flat = pk.reshape(-1)
    out, off = [], 0
    for s in shapes:
        size = 1
        for dd in s:
            size *= dd
        out.append(flat[off:off + size].reshape(s))
        off += size + (-size) % PACK_C
    return out


def _layer_params(full, l):
    p = {}
    for k in ['g_pre_mix', 'lru_conv_b', 'lru_ba', 'lru_bx', 'lru_lambda', 'conv_b', 'conv_ln_g', 'conv_ln_b', 'g_out_lru',
              'g_out_conv', 'g_post_mix', 'g_pre_ffn', 'ffn_conv_b', 'g_post_ffn']:
        p[k] = full[k][l][None, :]
    for k in ['lru_conv_w', 'conv_w', 'ffn_conv_w', 'w_in', 'w_out', 'w_up', 'w_down']:
        p[k] = full[k][l]
    p['lru_wa'] = full['lru_wa_bf'][l]
    p['lru_wx'] = full['lru_wx_bf'][l]
    return p


def _step(x, loss_target, w, m, v):
    depth = w['w_in'].shape[0]
    d = x.shape[2]
    xk, yk, _ = _coords()
    k_me = 2 * xk + yk

    k1 = jnp.reshape(k_me, (1,)).astype(jnp.int32)
    big_bf = {k: _cast_into_window(w[k], BIG_AXIS[k], k1, "cast_" + k) for k in BIG}
    sh_pad = [w['meta_tokens']] + [jnp.pad(w[k], ((0, 0), (0, (-w[k].shape[1]) % 8), (0, 0))) for k in SH_SMALL[1:]]
    gath = _gather_xy([big_bf[k] for k in BIG] + sh_pad, [BIG_AXIS[k] for k in BIG] + [1, 2, 2, 2], len(BIG),
                      "gather_weights")
    full = dict(w)
    full.update(dict(zip(BIG, gath[:len(BIG)])))
    full['meta_tokens'] = gath[len(BIG)]
    for k, a in zip(SH_SMALL[1:], gath[len(BIG) + 1:]):
        full[k] = a[:, :w[k].shape[1]]
    for k in ('lru_wa', 'lru_wx'):
        full[k + '_bf'] = _cast_bf16(w[k].reshape(-1, LANES), "cast_" + k).reshape(w[k].shape)

    h = jnp.concatenate([full['meta_tokens'], x[0]], axis=0)
    tgt = jnp.pad(loss_target[0], ((N_META, 0), (0, 0)))
    pending = {}

    def send_big_grads(l, g):
        res = _scatter_start([g[k] for k in BIG], [BIG_AXIS[k] - 1 for k in BIG], "scatter_start_%d" % l)
        pending[l] = res[:4]
        return res[4]

    sq, dh, gl = _fwd_bwd(h, tgt, full, depth, send_big_grads)
    loss = lax.psum(0.5 * jnp.sum(sq) / d, ("x", "y", "c"))
    return _reduce_update(loss, dh, gl, pending, w, m, v, depth, k_me)


def _fwd_bwd(h, tgt, full, depth, on_grads=None):
    saved = []
    for l in range(depth):
        p = _layer_params(full, l)
        proj, zb1 = _rms_matmul(h, p['g_pre_mix'], p['w_in'], "in_proj")
        y, hs, c1 = _mix_fwd(proj, p, "mix_fwd")
        o, h1 = _matmul_rms_res(y, p['w_out'], h, p['g_post_mix'], "out_proj")
        u0, zb2 = _rms_matmul(h1, p['g_pre_ffn'], p['w_up'], "up_proj")
        act = _ffn_act_fwd(u0, p['ffn_conv_w'], p['ffn_conv_b'], "ffn_act_fwd")
        f, h2 = _matmul_rms_res(act, p['w_down'], h1, p['g_post_ffn'], "down_proj")
        saved.append((p, h, zb1, proj, y, hs, c1, o, h1, zb2, u0, act, f))
        h = h2

    dh, sq = _loss_head(h, tgt, "loss_head")

    gl = [None] * depth
    token = None
    for l in reversed(range(depth)):
        p, h0, zb1, proj, y, hs, c1, o, h1, zb2, u0, act, f = saved[l]
        g = {}
        gpf = p['g_post_ffn'] if token is None else p['g_post_ffn'] + token[0:1, 0:1]
        dact, dfb, g['g_post_ffn'] = _rmsbwd_matmul_nt(f, gpf, dh, p['w_down'], "down_bwd")
        g['w_down'] = _matmul_tn(act, dfb, "down_dw")
        du0, g['ffn_conv_w'], g['ffn_conv_b'] = _ffn_act_bwd(dact, u0, p['ffn_conv_w'], p['ffn_conv_b'], "ffn_act_bwd")
        g['w_up'] = _matmul_tn(zb2, du0, "up_dw")
        dh1, g['g_pre_ffn'] = _matmul_nt_rmsbwd_res(du0, p['w_up'], h1, p['g_pre_ffn'], dh, "up_bwd")
        dy, dob, g['g_post_mix'] = _rmsbwd_matmul_nt(o, p['g_post_mix'], dh1, p['w_out'], "out_bwd")
        g['w_out'] = _matmul_tn(y, dob, "out_dw")
        dproj, gm = _mix_bwd(dy, proj, hs, c1, p, "mix_bwd")
        g.update(gm)
        g['w_in'] = _matmul_tn(zb1, dproj, "in_dw")
        dh, g['g_pre_mix'] = _matmul_nt_rmsbwd_res(dproj, p['w_in'], h0, p['g_pre_mix'], dh1, "in_bwd")
        gl[l] = g
        if on_grads is not None:
            token = on_grads(l, g)
    return sq, dh, gl


def _reduce_update(loss, dh, gl, pending, w, m, v, depth, k_me):
    grad_x = dh[N_META:][None]

    def stacked(k):
        return jnp.stack([gl[l][k].reshape(w[k].shape[1:]) if k not in SH_SMALL + BIG else gl[l][k]
                          for l in range(depth)])

    k1 = jnp.reshape(k_me, (1,)).astype(jnp.int32)
    c1 = jnp.reshape(lax.axis_index("c"), (1,)).astype(jnp.int32)
    was = [BIG_AXIS[k] - 1 for k in BIG]
    halves = []
    for l in range(depth):
        send_sems, recv_sems, g_thru, lands = pending[l]
        g_own, lands = _scatter_wait(send_sems, recv_sems, g_thru, lands, was, dh, "scatter_wait_%d" % l)
        for wa, land, g_k in zip(was, lands, g_own):
            red = _sum_pieces(land, g_k, wa, k1, c1, "grad_sum")
            halves.append(red.reshape(2, land.shape[1], land.shape[2]))
    swapped = _sibling_swap(halves, "grad_swap")
    big_red = [jnp.stack([swapped[l * len(BIG) + j].reshape(w[k].shape[1:]) for l in range(depth)])
               for j, k in enumerate(BIG)]

    out = {}
    for k, gk in zip(BIG, big_red):
        c2 = gk.shape[-1]
        dl, m2, v2 = _adamw(gk.reshape(-1, c2), w[k].reshape(-1, c2), m[k].reshape(-1, c2), v[k].reshape(-1, c2),
                            "adamw_big")
        out[k] = (gk, dl.reshape(gk.shape), m2.reshape(gk.shape), v2.reshape(gk.shape))

    rep_g = [stacked(k) for k in REP_SMALL]
    sh_g = [dh[:N_META]] + [stacked(k) for k in SH_SMALL[1:]]
    n_rep_rows = _pack(rep_g, 1).shape[0]
    pk = jnp.concatenate([_pack(rep_g, 1), _pack(sh_g, 1)])
    pk = jnp.pad(pk, ((0, (-pk.shape[0]) % 256), (0, 0)))
    part = _sum_slots(_scatter8(pk, "small_scatter"), "small_sum")
    red = _gather_all(part, "small_gather").reshape(pk.shape)
    rep_red = _unpack(red[:n_rep_rows], [a.shape for a in rep_g])
    sh_red = []
    for a in _unpack(red[n_rep_rows:], [a.shape for a in sh_g]):
        wd = a.shape[-1] // 4
        sh_red.append(lax.dynamic_slice_in_dim(a, k_me * wd, wd, axis=a.ndim - 1))

    for names, grads_ in ((REP_SMALL, rep_red), (SH_SMALL, sh_red)):
        res = _adamw(_pack(grads_, 16), _pack([w[k] for k in names], 16), _pack([m[k] for k in names], 16),
                     _pack([v[k] for k in names], 16), "adamw_small")
        shapes = [w[k].shape for k in names]
        un = [_unpack(r, shapes) for r in res]
        for j, k in enumerate(names):
            out[k] = (grads_[j].reshape(w[k].shape), un[0][j], un[1][j], un[2][j])

    return (loss, grad_x, *[out[k][0] for k in WEIGHTS], *[out[k][1] for k in WEIGHTS],
            *[out[k][2] for k in WEIGHTS], *[out[k][3] for k in WEIGHTS])


def kernel(x, meta_tokens, g_pre_mix, w_in, lru_conv_w, lru_conv_b, lru_wa, lru_ba, lru_wx, lru_bx, lru_lambda, conv_w, conv_b, conv_ln_g, conv_ln_b, g_out_lru, g_out_conv, w_out, g_post_mix, g_pre_ffn, w_up, ffn_conv_w, ffn_conv_b, w_down, g_post_ffn, loss_target, m_meta_tokens, m_g_pre_mix, m_w_in, m_lru_conv_w, m_lru_conv_b, m_lru_wa, m_lru_ba, m_lru_wx, m_lru_bx, m_lru_lambda, m_conv_w, m_conv_b, m_conv_ln_g, m_conv_ln_b, m_g_out_lru, m_g_out_conv, m_w_out, m_g_post_mix, m_g_pre_ffn, m_w_up, m_ffn_conv_w, m_ffn_conv_b, m_w_down, m_g_post_ffn, v_meta_tokens, v_g_pre_mix, v_w_in, v_lru_conv_w, v_lru_conv_b, v_lru_wa, v_lru_ba, v_lru_wx, v_lru_bx, v_lru_lambda, v_conv_w, v_conv_b, v_conv_ln_g, v_conv_ln_b, v_g_out_lru, v_g_out_conv, v_w_out, v_g_post_mix, v_g_pre_ffn, v_w_up, v_ffn_conv_w, v_ffn_conv_b, v_w_down, v_g_post_ffn):
    w = dict(meta_tokens=meta_tokens, g_pre_mix=g_pre_mix, w_in=w_in, lru_conv_w=lru_conv_w, lru_conv_b=lru_conv_b,
             lru_wa=lru_wa, lru_ba=lru_ba, lru_wx=lru_wx, lru_bx=lru_bx, lru_lambda=lru_lambda, conv_w=conv_w,
             conv_b=conv_b, conv_ln_g=conv_ln_g, conv_ln_b=conv_ln_b, g_out_lru=g_out_lru, g_out_conv=g_out_conv,
             w_out=w_out, g_post_mix=g_post_mix, g_pre_ffn=g_pre_ffn, w_up=w_up, ffn_conv_w=ffn_conv_w,
             ffn_conv_b=ffn_conv_b, w_down=w_down, g_post_ffn=g_post_ffn)
    m = dict(meta_tokens=m_meta_tokens, g_pre_mix=m_g_pre_mix, w_in=m_w_in, lru_conv_w=m_lru_conv_w,
             lru_conv_b=m_lru_conv_b, lru_wa=m_lru_wa, lru_ba=m_lru_ba, lru_wx=m_lru_wx, lru_bx=m_lru_bx,
             lru_lambda=m_lru_lambda, conv_w=m_conv_w, conv_b=m_conv_b, conv_ln_g=m_conv_ln_g, conv_ln_b=m_conv_ln_b,
             g_out_lru=m_g_out_lru, g_out_conv=m_g_out_conv, w_out=m_w_out, g_post_mix=m_g_post_mix,
             g_pre_ffn=m_g_pre_ffn, w_up=m_w_up, ffn_conv_w=m_ffn_conv_w, ffn_conv_b=m_ffn_conv_b, w_down=m_w_down,
             g_post_ffn=m_g_post_ffn)
    v = dict(meta_tokens=v_meta_tokens, g_pre_mix=v_g_pre_mix, w_in=v_w_in, lru_conv_w=v_lru_conv_w,
             lru_conv_b=v_lru_conv_b, lru_wa=v_lru_wa, lru_ba=v_lru_ba, lru_wx=v_lru_wx, lru_bx=v_lru_bx,
             lru_lambda=v_lru_lambda, conv_w=v_conv_w, conv_b=v_conv_b, conv_ln_g=v_conv_ln_g, conv_ln_b=v_conv_ln_b,
             g_out_lru=v_g_out_lru, g_out_conv=v_g_out_conv, w_out=v_w_out, g_post_mix=v_g_post_mix,
             g_pre_ffn=v_g_pre_ffn, w_up=v_w_up, ffn_conv_w=v_ffn_conv_w, ffn_conv_b=v_ffn_conv_b, w_down=v_w_down,
             g_post_ffn=v_g_post_ffn)
    return _step(x, loss_target, w, m, v)
```

```python
import functools

import jax
import jax.numpy as jnp
from jax import lax
from jax.experimental import pallas as pl
from jax.experimental.pallas import tpu as pltpu

F32 = jnp.float32
BF16 = jnp.bfloat16
EPS = 1e-6
N_META = 16
LRU_C = 8.0
CONV_K = 31
LRU_K = 4
FFN_K = 3
CONV_ROWS = 24
SCAN_U = 3
RSCAN_U = 2
LANES = 128
VMEM_LIMIT = 56 * 1024 * 1024
ADAM_LR, ADAM_B1, ADAM_B2, ADAM_EPS, ADAM_WD, ADAM_STEP = 0.001, 0.9, 0.999, 1e-08, 0.01, 10
MESH_T = pl.DeviceIdType.MESH
NT_DIMS = (((1,), (1,)), ((), ()))
TN_DIMS = (((0,), (0,)), ((), ()))

REP_SMALL = ['g_pre_mix', 'lru_conv_b', 'lru_wa', 'lru_ba', 'lru_wx', 'lru_bx', 'lru_lambda', 'conv_b', 'conv_ln_g',
             'conv_ln_b', 'g_out_lru', 'g_out_conv', 'g_post_mix', 'g_pre_ffn', 'ffn_conv_b', 'g_post_ffn']
SH_SMALL = ['meta_tokens', 'lru_conv_w', 'conv_w', 'ffn_conv_w']
BIG = ['w_in', 'w_out', 'w_up', 'w_down']
BIG_AXIS = {'w_in': 2, 'w_out': 1, 'w_up': 2, 'w_down': 1}
WEIGHTS = ['meta_tokens', 'g_pre_mix', 'w_in', 'lru_conv_w', 'lru_conv_b', 'lru_wa', 'lru_ba', 'lru_wx', 'lru_bx',
           'lru_lambda', 'conv_w', 'conv_b', 'conv_ln_g', 'conv_ln_b', 'g_out_lru', 'g_out_conv', 'w_out',
           'g_post_mix', 'g_pre_ffn', 'w_up', 'ffn_conv_w', 'ffn_conv_b', 'w_down', 'g_post_ffn']


def _pcall(body, **kw):
    return pl.pallas_call(body, **kw)


def _params(n_grid=1):
    return pltpu.CompilerParams(dimension_semantics=("arbitrary",) * n_grid, vmem_limit_bytes=VMEM_LIMIT)


def _tiles(t):
    if t % 432 == 0:
        return 432, 144
    assert t % 48 == 0
    return 48, 48


def _row(tm, n):
    return pl.BlockSpec((tm, n), lambda i: (i, 0))


def _rrow(tm, n, nt):
    return pl.BlockSpec((tm, n), lambda i: (nt - 1 - i, 0))


def _halo(hb, n, tm, nt):
    return pl.BlockSpec((hb, n), lambda i: (jnp.maximum((nt - 1 - i) * (tm // hb) - 1, 0), 0))


def _const(shape):
    nd = len(shape)
    return pl.BlockSpec(shape, lambda *_: (0,) * nd, pipeline_mode=pl.Buffered(1))


def _const_out(shape):
    nd = len(shape)
    return pl.BlockSpec(shape, lambda *_: (0,) * nd)


def _sigmoid(x):
    return 1.0 / (1.0 + jnp.exp(-x))


def _gelu(x):
    return 0.5 * x * (1.0 + jnp.tanh(0.7978845608028654 * (x + 0.044715 * (x * x * x))))


def _gelu_and_grad(x):
    k = 0.7978845608028654
    x2 = x * x
    th = jnp.tanh(k * (x + 0.044715 * (x2 * x)))
    return 0.5 * x * (1.0 + th), 0.5 * (1.0 + th) + 0.5 * x * (1.0 - th * th) * (k * (1.0 + 0.134145 * x2))


def _expm1(x):
    return jnp.where(jnp.abs(x) < 1e-2, x * (1.0 + x * (0.5 + x * (1.0 / 6.0 + x * (1.0 / 24.0)))), jnp.exp(x) - 1.0)


def _softplus(x):
    e = jnp.exp(-jnp.abs(x))
    return jnp.maximum(x, 0.0) + jnp.where(e < 1e-4, e * (1.0 - 0.5 * e), jnp.log(1.0 + e))


def _lru_gates(pa, px, sp):
    r = _sigmoid(pa)
    ig = _sigmoid(px)
    la = (-LRU_C * r) * sp
    return r, ig, jnp.exp(la), jnp.sqrt(-_expm1(2.0 * la))


def _rms(x):
    return lax.rsqrt(jnp.mean(x * x, axis=-1, keepdims=True) + EPS)


def _rms_bwd(x, g, dy):
    r = _rms(x)
    xr = x * r
    dyg = dy * g
    return r * (dyg - xr * jnp.mean(dyg * xr, axis=-1, keepdims=True)), xr


def _col_chunk(n):
    return 1536 if n % 1536 == 0 else 1024


def _rms_matmul(h, g, w, name):
    t, d = h.shape
    n = w.shape[1]
    tm, _ = _tiles(t)
    cn = _col_chunk(n)

    def body(h_ref, g_ref, w_ref, p_ref, zb_ref):
        x = h_ref[...]
        zb = ((x * _rms(x)) * g_ref[...]).astype(BF16)
        zb_ref[...] = zb
        for c in range(n // cn):
            p_ref[:, c * cn:(c + 1) * cn] = jnp.dot(zb, w_ref[:, c * cn:(c + 1) * cn], preferred_element_type=F32)

    return _pcall(body, name=name, grid=(t // tm,),
                  in_specs=[_row(tm, d), _const((1, d)), _const((d, n))],
                  out_specs=[_row(tm, n), _row(tm, d)],
                  out_shape=[jax.ShapeDtypeStruct((t, n), F32), jax.ShapeDtypeStruct((t, d), BF16)],
                  compiler_params=_params())(h, g, w)


def _matmul_rms_res(a, w, h, g, name):
    t, k = a.shape
    d = w.shape[1]
    tm, _ = _tiles(t)

    def body(a_ref, w_ref, h_ref, g_ref, o_ref, hn_ref):
        o = jnp.dot(a_ref[...], w_ref[...], preferred_element_type=F32)
        o_ref[...] = o
        hn_ref[...] = h_ref[...] + (o * _rms(o)) * g_ref[...]

    return _pcall(body, name=name, grid=(t // tm,),
                  in_specs=[_row(tm, k), _const((k, d)), _row(tm, d), _const((1, d))],
                  out_specs=[_row(tm, d), _row(tm, d)],
                  out_shape=[jax.ShapeDtypeStruct((t, d), F32), jax.ShapeDtypeStruct((t, d), F32)],
                  compiler_params=_params())(a, w, h, g)


def _rmsbwd_matmul_nt(x, g, dy, w, name):
    t, d = x.shape
    n = w.shape[0]
    tm, _ = _tiles(t)
    cn = _col_chunk(n)

    def body(x_ref, g_ref, dy_ref, w_ref, da_ref, dxb_ref, dg_ref):
        @pl.when(pl.program_id(0) == 0)
        def _():
            dg_ref[...] = jnp.zeros((1, d), F32)

        dy = dy_ref[...]
        dx, xr = _rms_bwd(x_ref[...], g_ref[...], dy)
        dg_ref[...] += jnp.sum(dy * xr, axis=0, keepdims=True)
        dxb = dx.astype(BF16)
        dxb_ref[...] = dxb
        for c in range(n // cn):
            da_ref[:, c * cn:(c + 1) * cn] = lax.dot_general(dxb, w_ref[c * cn:(c + 1) * cn, :], NT_DIMS,
                                                             preferred_element_type=F32)

    return _pcall(body, name=name, grid=(t // tm,),
                  in_specs=[_row(tm, d), _const((1, d)), _row(tm, d), _const((n, d))],
                  out_specs=[_row(tm, n), _row(tm, d), _const_out((1, d))],
                  out_shape=[jax.ShapeDtypeStruct((t, n), F32), jax.ShapeDtypeStruct((t, d), BF16),
                             jax.ShapeDtypeStruct((1, d), F32)],
                  compiler_params=_params())(x, g, dy, w)


def _matmul_nt_rmsbwd_res(dp, w, h, g, dh, name):
    t, n = dp.shape
    d = w.shape[0]
    tm, _ = _tiles(t)

    def body(dp_ref, w_ref, h_ref, g_ref, dh_ref, out_ref, dg_ref):
        @pl.when(pl.program_id(0) == 0)
        def _():
            dg_ref[...] = jnp.zeros((1, d), F32)

        dz = lax.dot_general(dp_ref[...], w_ref[...], NT_DIMS, preferred_element_type=F32)
        dx, xr = _rms_bwd(h_ref[...], g_ref[...], dz)
        dg_ref[...] += jnp.sum(dz * xr, axis=0, keepdims=True)
        out_ref[...] = dh_ref[...] + dx

    return _pcall(body, name=name, grid=(t // tm,),
                  in_specs=[_row(tm, n), _const((d, n)), _row(tm, d), _const((1, d)), _row(tm, d)],
                  out_specs=[_row(tm, d), _const_out((1, d))],
                  out_shape=[jax.ShapeDtypeStruct((t, d), F32), jax.ShapeDtypeStruct((1, d), F32)],
                  compiler_params=_params())(dp, w, h, g, dh)


def _matmul_tn(a, b, name):
    t, k = a.shape
    n = b.shape[1]
    tm, _ = _tiles(t)
    nt = t // tm
    bn = min(n, (1536 * 1024) // k)
    assert n % bn == 0 and bn % LANES == 0

    def body(a_ref, b_ref, o_ref, acc):
        @pl.when(pl.program_id(1) == 0)
        def _():
            acc[...] = jnp.zeros((k, bn), F32)

        acc[...] += lax.dot_general(a_ref[...], b_ref[...], TN_DIMS, preferred_element_type=F32)

        @pl.when(pl.program_id(1) == nt - 1)
        def _():
            o_ref[...] = acc[...].astype(BF16)

    return _pcall(body, name=name, grid=(n // bn, nt),
                  in_specs=[pl.BlockSpec((tm, k), lambda j, i: (i, 0)), pl.BlockSpec((tm, bn), lambda j, i: (i, j))],
                  out_specs=pl.BlockSpec((k, bn), lambda j, i: (0, j)),
                  out_shape=jax.ShapeDtypeStruct((k, n), BF16),
                  scratch_shapes=[pltpu.VMEM((k, bn), F32)],
                  compiler_params=_params(2))(a, b)


MIX_PARAMS = ['lru_conv_w', 'lru_conv_b', 'lru_wa', 'lru_ba', 'lru_wx', 'lru_bx', 'lru_lambda', 'conv_w', 'conv_b',
              'conv_ln_g', 'conv_ln_b', 'g_out_lru', 'g_out_conv']


def _head_gates(xcb, wa_ref, wx_ref, pa_s, px_s, ba, bx, heads):
    for hd in range(heads):
        sl = slice(LANES * hd, LANES * (hd + 1))
        pa_s[:, sl] = jnp.dot(xcb[:, sl], wa_ref[hd], preferred_element_type=F32) + ba[:, sl]
        px_s[:, sl] = jnp.dot(xcb[:, sl], wx_ref[hd], preferred_element_type=F32) + bx[:, sl]


def _lru_conv_chunk(exta, cw4_ref, cb4, r0):
    win = exta[pl.ds(r0, 16), :]
    acc = cw4_ref[3:4, :] * win[8:16]
    for k in range(LRU_K - 1):
        acc = acc + cw4_ref[k:k + 1, :] * pltpu.roll(win, LRU_K - 1 - k, 0)[8:16]
    return acc + cb4


def _mix_fwd(proj, p, name):
    t = proj.shape[0]
    w = p['lru_conv_b'].shape[1]
    cw = p['conv_b'].shape[1]
    heads = p['lru_wa'].shape[0]
    _, tm = _tiles(t)
    nch = tm // 8

    def body(proj_ref, cw4_ref, cb4_ref, wa_ref, ba_ref, wx_ref, bx_ref, lam_ref, cw31_ref, cb31_ref, lng_ref,
             lnb_ref, ga_ref, gb_ref, y_ref, hs_ref, c1_ref, exta, xc_s, pa_s, px_s, extb, nbuf, hcar):
        @pl.when(pl.program_id(0) == 0)
        def _():
            exta[0:8, :] = jnp.zeros((8, w), F32)
            extb[0:32, :] = jnp.zeros((32, cw), F32)
            hcar[...] = jnp.zeros((8, w), F32)

        exta[8:8 + tm, :] = proj_ref[:, 0:w]
        cb4 = cb4_ref[...]

        def conv_a(c, carry):
            r0 = pl.multiple_of(c * 8, 8)
            xc_s[pl.ds(r0, 8), :] = _lru_conv_chunk(exta, cw4_ref, cb4, r0)
            return carry

        lax.fori_loop(0, nch, conv_a, 0, unroll=3)
        _head_gates(xc_s[...].astype(BF16), wa_ref, wx_ref, pa_s, px_s, ba_ref[...], bx_ref[...], heads)

        sp = _softplus(-lam_ref[...])
        ga = ga_ref[...]
        row = lax.broadcasted_iota(jnp.int32, (8, w), 0)

        def scan_c(cg, hprev):
            part = []
            for u in range(SCAN_U):
                r0 = pl.multiple_of((cg * SCAN_U + u) * 8, 8)
                xc = xc_s[pl.ds(r0, 8), :]
                _, ig, a, m = _lru_gates(pa_s[pl.ds(r0, 8), :], px_s[pl.ds(r0, 8), :], sp)
                aa, bb = a, m * (ig * xc)
                for d in (1, 2, 4):
                    a_s = pltpu.roll(aa, d, 0)
                    b_s = pltpu.roll(bb, d, 0)
                    msk = row >= d
                    bb = jnp.where(msk, aa * b_s + bb, bb)
                    aa = jnp.where(msk, aa * a_s, aa)
                part.append((r0, aa, bb, _gelu(proj_ref[pl.ds(r0, 8), w:2 * w])))
            for r0, aa, bb, ge in part:
                hs = aa * hprev + bb
                hs_ref[pl.ds(r0, 8), :] = hs
                ya = hs * ge
                nbuf[pl.ds(r0, 8), 0:w] = (ya * _rms(ya)) * ga
                hprev = hs[7:8, :]
            return hprev

        hcar[0:1, :] = lax.fori_loop(0, nch // SCAN_U, scan_c, hcar[0:1, :])

        extb[32:32 + tm, :] = proj_ref[:, 2 * w:2 * w + cw] * _sigmoid(proj_ref[:, 2 * w + cw:2 * w + 2 * cw])

        def conv_b(c, carry):
            r0 = pl.multiple_of(c * CONV_ROWS, 8)
            ybs = []
            ssq = jnp.zeros((CONV_ROWS, 1), F32)
            for lb in range(cw // LANES):
                sl = slice(LANES * lb, LANES * (lb + 1))
                win = extb[pl.ds(r0, CONV_ROWS + 32), sl]
                rolled = [win] + [pltpu.roll(win, rr, 0) for rr in range(1, 8)]
                parts = [None] * 4
                for k in range(CONV_K):
                    q, rr = divmod(CONV_K - 1 - k, 8)
                    term = cw31_ref[k:k + 1, sl] * rolled[rr][32 - 8 * q:32 - 8 * q + CONV_ROWS]
                    parts[k % 4] = term if parts[k % 4] is None else parts[k % 4] + term
                acc = ((parts[0] + parts[1]) + (parts[2] + parts[3])) + cb31_ref[:, sl]
                c1_ref[pl.ds(r0, CONV_ROWS), sl] = acc
                dlt = acc - jnp.mean(acc, axis=-1, keepdims=True)
                c2 = dlt * lax.rsqrt(jnp.mean(dlt * dlt, axis=-1, keepdims=True) + EPS)
                yb0 = c2 * lng_ref[:, sl] + lnb_ref[:, sl]
                yb = yb0 * _sigmoid(yb0)
                ybs.append(yb)
                ssq = ssq + jnp.sum(yb * yb, axis=-1, keepdims=True)
            rb = lax.rsqrt(ssq / cw + EPS)
            for lb in range(cw // LANES):
                sl = slice(LANES * lb, LANES * (lb + 1))
                nbuf[pl.ds(r0, CONV_ROWS), w + LANES * lb:w + LANES * (lb + 1)] = (ybs[lb] * rb) * gb_ref[:, sl]
            return carry

        lax.fori_loop(0, tm // CONV_ROWS, conv_b, 0, unroll=2)
        exta[0:8, :] = exta[tm:tm + 8, :]
        extb[0:32, :] = extb[tm:tm + 32, :]
        y_ref[...] = nbuf[...].astype(BF16)

    consts = [p[k] for k in MIX_PARAMS]
    return _pcall(body, name=name, grid=(t // tm,),
                  in_specs=[_row(tm, 2 * w + 2 * cw)] + [_const(c.shape) for c in consts],
                  out_specs=[_row(tm, w + cw), _row(tm, w), _row(tm, cw)],
                  out_shape=[jax.ShapeDtypeStruct((t, w + cw), BF16), jax.ShapeDtypeStruct((t, w), F32),
                             jax.ShapeDtypeStruct((t, cw), F32)],
                  scratch_shapes=[pltpu.VMEM((8 + tm, w), F32), pltpu.VMEM((tm, w), F32), pltpu.VMEM((tm, w), F32),
                                  pltpu.VMEM((tm, w), F32), pltpu.VMEM((32 + tm, cw), F32),
                                  pltpu.VMEM((tm, w + cw), F32), pltpu.VMEM((8, w), F32)],
                  compiler_params=_params())(proj, *consts)


def _mix_bwd(dy, proj, hs, c1, p, name):
    t = proj.shape[0]
    w = p['lru_conv_b'].shape[1]
    cw = p['conv_b'].shape[1]
    heads = p['lru_wa'].shape[0]
    _, tm = _tiles(t)
    nt = t // tm
    nch = tm // 8
    nlb = cw // LANES
    G_CB4, G_CW4, G_BA, G_BX, G_SP, G_GA, NGW = 0, 1, 5, 6, 7, 8, 9
    G_CB31, G_LNG, G_LNB, G_GB, G_CW31, NGC = 0, 1, 2, 3, 4, 4 + CONV_K

    def body(dy_ref, proj_ref, projh_ref, hs_ref, hsh_ref, c1_ref, cw4_ref, cb4_ref, wa_ref, ba_ref, wx_ref, bx_ref,
             lam_ref, cw31_ref, cb31_ref, lng_ref, lnb_ref, ga_ref, gb_ref,
             dproj_ref, dcw4_ref, dcb4_ref, dwa_ref, dba_ref, dwx_ref, dbx_ref, dlam_ref, dcw31_ref, dcb31_ref,
             dlng_ref, dlnb_ref, dga_ref, dgb_ref,
             exta, exth, xc_s, pa_s, px_s, dpa_s, dpx_s, dxce, extb, dc1e, dpf, cp_s, acc_w, acc_c):
        i = pl.program_id(0)

        @pl.when(i == 0)
        def _():
            acc_w[...] = jnp.zeros((8 * NGW, w), F32)
            acc_c[...] = jnp.zeros((8 * NGC, cw), F32)
            dwa_ref[...] = jnp.zeros(dwa_ref.shape, F32)
            dwx_ref[...] = jnp.zeros(dwx_ref.shape, F32)
            cp_s[...] = jnp.zeros((8, w), F32)
            dxce[tm:tm + 8, :] = jnp.zeros((8, w), F32)
            dc1e[tm:tm + 32, :] = jnp.zeros((32, cw), F32)

        nf = jnp.where(i < nt - 1, 1.0, 0.0).astype(F32)
        exta[0:8, :] = projh_ref[40:48, 0:w] * nf
        exta[8:8 + tm, :] = proj_ref[:, 0:w]
        exth[0:8, :] = hsh_ref[...] * nf
        exth[8:8 + tm, :] = hs_ref[...]
        extb[0:48, :] = (projh_ref[:, 2 * w:2 * w + cw] * _sigmoid(projh_ref[:, 2 * w + cw:2 * w + 2 * cw])) * nf
        extb[48:48 + tm, :] = proj_ref[:, 2 * w:2 * w + cw] * _sigmoid(proj_ref[:, 2 * w + cw:2 * w + 2 * cw])
        cb4 = cb4_ref[...]

        def conv_a(c, carry):
            r0 = pl.multiple_of(c * 8, 8)
            xc_s[pl.ds(r0, 8), :] = _lru_conv_chunk(exta, cw4_ref, cb4, r0)
            return carry

        lax.fori_loop(0, nch, conv_a, 0, unroll=3)
        xcb = xc_s[...].astype(BF16)
        _head_gates(xcb, wa_ref, wx_ref, pa_s, px_s, ba_ref[...], bx_ref[...], heads)

        sp = _softplus(-lam_ref[...])
        ga = ga_ref[...]
        row = lax.broadcasted_iota(jnp.int32, (8, w), 0)

        def acc_add(ref, g, val, sl=slice(None)):
            for j in range(val.shape[0] // 8):
                ref[8 * g:8 * g + 8, sl] = ref[8 * g:8 * g + 8, sl] + val[8 * j:8 * j + 8]

        def rscan(cg, cp):
            part = []
            for u in range(RSCAN_U):
                r0 = pl.multiple_of((nch - 1 - (cg * RSCAN_U + u)) * 8, 8)
                xc = xc_s[pl.ds(r0, 8), :]
                r, ig, a, m = _lru_gates(pa_s[pl.ds(r0, 8), :], px_s[pl.ds(r0, 8), :], sp)
                hwin = exth[pl.ds(r0, 16), :]
                hcur = hwin[8:16]
                hprev = pltpu.roll(hwin, 1, 0)[8:16]
                ge, dge = _gelu_and_grad(proj_ref[pl.ds(r0, 8), w:2 * w])
                dna = dy_ref[pl.ds(r0, 8), 0:w]
                dya, yar = _rms_bwd(hcur * ge, ga, dna)
                acc_add(acc_w, G_GA, dna * yar)
                dpf[pl.ds(r0, 8), w:2 * w] = (dya * hcur) * dge
                aa = jnp.where(row == 7, 1.0, pltpu.roll(a, 7, 0))
                bb = dya * ge
                for d in (1, 2, 4):
                    a_s = pltpu.roll(aa, 8 - d, 0)
                    b_s = pltpu.roll(bb, 8 - d, 0)
                    msk = row < 8 - d
                    bb = jnp.where(msk, aa * b_s + bb, bb)
                    aa = jnp.where(msk, aa * a_s, aa)
                part.append((r0, aa, bb, xc, r, ig, a, m, hprev))
            for r0, aa, bb, xc, r, ig, a, m, hprev in part:
                lamb = bb + aa * cp
                cp = a[0:1, :] * lamb[0:1, :]
                dm = lamb * (ig * xc)
                di = lamb * (m * xc)
                dxce[pl.ds(r0, 8), :] = lamb * (m * ig)
                dla = a * (lamb * hprev - dm * (a / m))
                acc_add(acc_w, G_SP, dla * (-LRU_C * r))
                dpa = (dla * (-LRU_C * sp)) * (r * (1.0 - r))
                dpx = di * (ig * (1.0 - ig))
                acc_add(acc_w, G_BA, dpa)
                acc_add(acc_w, G_BX, dpx)
                dpa_s[pl.ds(r0, 8), :] = dpa
                dpx_s[pl.ds(r0, 8), :] = dpx
            return cp

        cp_s[0:1, :] = lax.fori_loop(0, nch // RSCAN_U, rscan, cp_s[0:1, :])

        dpab = dpa_s[...].astype(BF16)
        dpxb = dpx_s[...].astype(BF16)
        for hd in range(heads):
            sl = slice(LANES * hd, LANES * (hd + 1))
            dxce[0:tm, sl] = (dxce[0:tm, sl]
                              + lax.dot_general(dpab[:, sl], wa_ref[hd], NT_DIMS, preferred_element_type=F32)
                              + lax.dot_general(dpxb[:, sl], wx_ref[hd], NT_DIMS, preferred_element_type=F32))
            dwa_ref[hd] = dwa_ref[hd] + lax.dot_general(xcb[:, sl], dpab[:, sl], TN_DIMS, preferred_element_type=F32)
            dwx_ref[hd] = dwx_ref[hd] + lax.dot_general(xcb[:, sl], dpxb[:, sl], TN_DIMS, preferred_element_type=F32)

        def conv_a_bwd(c, carry):
            r0 = pl.multiple_of(c * 8, 8)
            win = dxce[pl.ds(r0, 16), :]
            dxc = win[0:8]
            xwin = exta[pl.ds(r0, 16), :]
            dxl = cw4_ref[3:4, :] * dxc
            acc_add(acc_w, G_CB4, dxc)
            acc_add(acc_w, G_CW4 + 3, dxc * xwin[8:16])
            for k in range(LRU_K - 1):
                s = LRU_K - 1 - k
                dxl = dxl + cw4_ref[k:k + 1, :] * pltpu.roll(win, 16 - s, 0)[0:8]
                acc_add(acc_w, G_CW4 + k, dxc * pltpu.roll(xwin, s, 0)[8:16])
            dpf[pl.ds(r0, 8), 0:w] = dxl
            return carry

        lax.fori_loop(0, nch, conv_a_bwd, 0, unroll=3)
        dxce[tm:tm + 8, :] = dxce[0:8, :]

        def mixb(c, carry):
            r0 = pl.multiple_of(c * 8, 8)
            st = []
            ssq = jnp.zeros((8, 1), F32)
            for lb in range(nlb):
                sl = slice(LANES * lb, LANES * (lb + 1))
                c1v = c1_ref[pl.ds(r0, 8), sl]
                dlt = c1v - jnp.mean(c1v, axis=-1, keepdims=True)
                rs = lax.rsqrt(jnp.mean(dlt * dlt, axis=-1, keepdims=True) + EPS)
                c2 = dlt * rs
                yb0 = c2 * lng_ref[:, sl] + lnb_ref[:, sl]
                sg = _sigmoid(yb0)
                yb = yb0 * sg
                ssq = ssq + jnp.sum(yb * yb, axis=-1, keepdims=True)
                st.append((rs, c2, yb0, sg, yb))
            rb = lax.rsqrt(ssq / cw + EPS)
            tsum = jnp.zeros((8, 1), F32)
            dngs = []
            for lb in range(nlb):
                sl = slice(LANES * lb, LANES * (lb + 1))
                dnb = dy_ref[pl.ds(r0, 8), w + LANES * lb:w + LANES * (lb + 1)]
                ybr = st[lb][4] * rb
                acc_add(acc_c, G_GB, dnb * ybr, sl)
                dng = dnb * gb_ref[:, sl]
                dngs.append((dng, ybr))
                tsum = tsum + jnp.sum(dng * ybr, axis=-1, keepdims=True)
            tsum = tsum / cw
            for lb in range(nlb):
                sl = slice(LANES * lb, LANES * (lb + 1))
                rs, c2, yb0, sg, _ = st[lb]
                dng, ybr = dngs[lb]
                dyb0 = (rb * (dng - ybr * tsum)) * (sg * (1.0 + yb0 * (1.0 - sg)))
                acc_add(acc_c, G_LNG, dyb0 * c2, sl)
                acc_add(acc_c, G_LNB, dyb0, sl)
                dc2 = dyb0 * lng_ref[:, sl]
                dc1 = rs * (dc2 - jnp.mean(dc2, axis=-1, keepdims=True)
                            - c2 * jnp.mean(dc2 * c2, axis=-1, keepdims=True))
                acc_add(acc_c, G_CB31, dc1, sl)
                dc1e[pl.ds(r0, 8), sl] = dc1
            return carry

        lax.fori_loop(0, nch, mixb, 0, unroll=6)

        def conv_b_bwd(c, carry):
            r0 = pl.multiple_of(c * CONV_ROWS, 8)
            nwin = CONV_ROWS + 32
            for lb in range(nlb):
                sl = slice(LANES * lb, LANES * (lb + 1))
                win = dc1e[pl.ds(r0, nwin), sl]
                ups = [win] + [pltpu.roll(win, nwin - rr, 0) for rr in range(1, 8)]
                dc1 = win[0:CONV_ROWS]
                parts = [None] * 4
                for k in range(CONV_K):
                    q, rr = divmod(CONV_K - 1 - k, 8)
                    term = cw31_ref[k:k + 1, sl] * ups[rr][8 * q:8 * q + CONV_ROWS]
                    parts[k % 4] = term if parts[k % 4] is None else parts[k % 4] + term
                dc0 = (parts[0] + parts[1]) + (parts[2] + parts[3])
                cav = proj_ref[pl.ds(r0, CONV_ROWS), 2 * w + LANES * lb:2 * w + LANES * (lb + 1)]
                sg = _sigmoid(proj_ref[pl.ds(r0, CONV_ROWS), 2 * w + cw + LANES * lb:2 * w + cw + LANES * (lb + 1)])
                dpf[pl.ds(r0, CONV_ROWS), 2 * w + LANES * lb:2 * w + LANES * (lb + 1)] = dc0 * sg
                dpf[pl.ds(r0, CONV_ROWS), 2 * w + cw + LANES * lb:2 * w + cw + LANES * (lb + 1)] = (
                    (dc0 * cav) * (sg * (1.0 - sg)))
                xwin = extb[pl.ds(pl.multiple_of(r0 + 16, 8), nwin), sl]
                xr = [xwin] + [pltpu.roll(xwin, rr, 0) for rr in range(1, 8)]
                for k in range(CONV_K):
                    q, rr = divmod(CONV_K - 1 - k, 8)
                    acc_add(acc_c, G_CW31 + k, dc1 * xr[rr][32 - 8 * q:32 - 8 * q + CONV_ROWS], sl)
            return carry

        lax.fori_loop(0, tm // CONV_ROWS, conv_b_bwd, 0, unroll=2)
        dc1e[tm:tm + 32, :] = dc1e[0:32, :]
        dproj_ref[...] = dpf[...].astype(BF16)

        @pl.when(i == nt - 1)
        def _():
            def fold(ref, g):
                return jnp.sum(ref[8 * g:8 * g + 8, :], axis=0, keepdims=True)

            dcb4_ref[...] = fold(acc_w, G_CB4)
            for k in range(LRU_K):
                dcw4_ref[k:k + 1, :] = fold(acc_w, G_CW4 + k)
            dba_ref[...] = fold(acc_w, G_BA)
            dbx_ref[...] = fold(acc_w, G_BX)
            dlam_ref[...] = fold(acc_w, G_SP) * (-_sigmoid(-lam_ref[...]))
            dga_ref[...] = fold(acc_w, G_GA)
            dcb31_ref[...] = fold(acc_c, G_CB31)
            dlng_ref[...] = fold(acc_c, G_LNG)
            dlnb_ref[...] = fold(acc_c, G_LNB)
            dgb_ref[...] = fold(acc_c, G_GB)
            for k in range(CONV_K):
                dcw31_ref[k:k + 1, :] = fold(acc_c, G_CW31 + k)

    consts = [p[k] for k in MIX_PARAMS]
    outs = _pcall(body, name=name, grid=(nt,),
                  in_specs=[_rrow(tm, w + cw, nt), _rrow(tm, 2 * w + 2 * cw, nt), _halo(48, 2 * w + 2 * cw, tm, nt),
                            _rrow(tm, w, nt), _halo(8, w, tm, nt), _rrow(tm, cw, nt)] + [_const(c.shape) for c in consts],
                  out_specs=[_rrow(tm, 2 * w + 2 * cw, nt)] + [_const_out(c.shape) for c in consts],
                  out_shape=[jax.ShapeDtypeStruct((t, 2 * w + 2 * cw), BF16)]
                  + [jax.ShapeDtypeStruct(c.shape, F32) for c in consts],
                  scratch_shapes=[pltpu.VMEM((8 + tm, w), F32), pltpu.VMEM((8 + tm, w), F32), pltpu.VMEM((tm, w), F32),
                                  pltpu.VMEM((tm, w), F32), pltpu.VMEM((tm, w), F32), pltpu.VMEM((tm, w), F32),
                                  pltpu.VMEM((tm, w), F32), pltpu.VMEM((tm + 8, w), F32),
                                  pltpu.VMEM((48 + tm, cw), F32), pltpu.VMEM((tm + 32, cw), F32),
                                  pltpu.VMEM((tm, 2 * w + 2 * cw), F32), pltpu.VMEM((8, w), F32),
                                  pltpu.VMEM((8 * NGW, w), F32), pltpu.VMEM((8 * NGC, cw), F32)],
                  compiler_params=_params())(dy, proj, proj, hs, hs, c1, *consts)
    return outs[0], dict(zip(MIX_PARAMS, outs[1:]))


def _ffn_window(u_ref, halo, c, col):
    if isinstance(c, int) and c == 0:
        return jnp.concatenate([halo[:, col:col + LANES], u_ref[0:16, col:col + LANES]], axis=0)
    return u_ref[pl.ds(pl.multiple_of(c * 16 - 8, 8), 24), col:col + LANES]


def _ffn_conv(win, w3_ref, b3_ref, col):
    sl = slice(col, col + LANES)
    x1 = pltpu.roll(win, 1, 0)[8:24]
    x2 = pltpu.roll(win, 2, 0)[8:24]
    u = w3_ref[2:3, sl] * win[8:24] + w3_ref[1:2, sl] * x1 + w3_ref[0:1, sl] * x2 + b3_ref[:, sl]
    return u, (x2, x1, win[8:24])


def _ffn_act_fwd(u0, w3, b3, name):
    t, f2 = u0.shape
    ff = f2 // 2
    _, tm = _tiles(t)
    nch = tm // 16

    def body(u_ref, w3_ref, b3_ref, act_ref, car):
        @pl.when(pl.program_id(0) == 0)
        def _():
            car[...] = jnp.zeros((8, f2), F32)

        def chunk(c):
            halo = car[...] if isinstance(c, int) else None
            r0 = 0 if isinstance(c, int) else pl.multiple_of(c * 16, 16)
            for j in range(ff // LANES):
                gate, _ = _ffn_conv(_ffn_window(u_ref, halo, c, LANES * j), w3_ref, b3_ref, LANES * j)
                up, _ = _ffn_conv(_ffn_window(u_ref, halo, c, ff + LANES * j), w3_ref, b3_ref, ff + LANES * j)
                act_ref[pl.ds(r0, 16), LANES * j:LANES * (j + 1)] = (_gelu(gate) * up).astype(BF16)

        chunk(0)

        def loop(c, carry):
            chunk(c)
            return carry

        lax.fori_loop(1, nch, loop, 0)
        car[...] = u_ref[tm - 8:tm, :]

    return _pcall(body, name=name, grid=(t // tm,),
                  in_specs=[_row(tm, f2), _const(w3.shape), _const(b3.shape)],
                  out_specs=_row(tm, ff), out_shape=jax.ShapeDtypeStruct((t, ff), BF16),
                  scratch_shapes=[pltpu.VMEM((8, f2), F32)],
                  compiler_params=_params())(u0, w3, b3)


def _ffn_act_bwd(dact, u0, w3, b3, name):
    t, f2 = u0.shape
    ff = f2 // 2
    _, tm = _tiles(t)
    nt = t // tm
    nch = tm // 16

    def body(dact_ref, u_ref, uh_ref, w3_ref, b3_ref, du0_ref, dw3_ref, db3_ref, dub, acc):
        i = pl.program_id(0)

        @pl.when(i == 0)
        def _():
            dub[tm:tm + 8, :] = jnp.zeros((8, f2), F32)
            acc[...] = jnp.zeros((32, f2), F32)

        nf = jnp.where(i < nt - 1, 1.0, 0.0).astype(F32)

        def acc_add(g, val, sl):
            acc[8 * g:8 * g + 8, sl] = acc[8 * g:8 * g + 8, sl] + (val[0:8] + val[8:16])

        def chunk(c):
            halo = uh_ref[...] * nf if isinstance(c, int) else None
            r0 = 0 if isinstance(c, int) else pl.multiple_of(c * 16, 16)
            for j in range(ff // LANES):
                cg, cu = LANES * j, ff + LANES * j
                gate, xg = _ffn_conv(_ffn_window(u_ref, halo, c, cg), w3_ref, b3_ref, cg)
                up, xu = _ffn_conv(_ffn_window(u_ref, halo, c, cu), w3_ref, b3_ref, cu)
                ge, dge = _gelu_and_grad(gate)
                da = dact_ref[pl.ds(r0, 16), cg:cg + LANES]
                for col, du, xs in ((cg, (da * up) * dge, xg), (cu, da * ge, xu)):
                    sl = slice(col, col + LANES)
                    dub[pl.ds(r0, 16), sl] = du
                    acc_add(0, du, sl)
                    for k in range(FFN_K):
                        acc_add(1 + k, du * xs[k], sl)

        chunk(0)

        def loop1(c, carry):
            chunk(c)
            return carry

        lax.fori_loop(1, nch, loop1, 0)

        def loop2(c, carry):
            r0 = pl.multiple_of(c * 16, 16)
            for j in range(f2 // LANES):
                sl = slice(LANES * j, LANES * (j + 1))
                win = dub[pl.ds(r0, 24), sl]
                du0 = (w3_ref[2:3, sl] * win[0:16] + w3_ref[1:2, sl] * pltpu.roll(win, 23, 0)[0:16]
                       + w3_ref[0:1, sl] * pltpu.roll(win, 22, 0)[0:16])
                du0_ref[pl.ds(r0, 16), sl] = du0.astype(BF16)
            return carry

        lax.fori_loop(0, nch, loop2, 0)
        dub[tm:tm + 8, :] = dub[0:8, :]

        @pl.when(i == nt - 1)
        def _():
            db3_ref[...] = jnp.sum(acc[0:8, :], axis=0, keepdims=True)
            for k in range(FFN_K):
                dw3_ref[k:k + 1, :] = jnp.sum(acc[8 + 8 * k:16 + 8 * k, :], axis=0, keepdims=True)

    return _pcall(body, name=name, grid=(nt,),
                  in_specs=[_rrow(tm, ff, nt), _rrow(tm, f2, nt), _halo(8, f2, tm, nt), _const(w3.shape),
                            _const(b3.shape)],
                  out_specs=[_rrow(tm, f2, nt), _const_out(w3.shape), _const_out(b3.shape)],
                  out_shape=[jax.ShapeDtypeStruct((t, f2), BF16), jax.ShapeDtypeStruct(w3.shape, F32),
                             jax.ShapeDtypeStruct(b3.shape, F32)],
                  scratch_shapes=[pltpu.VMEM((tm + 8, f2), F32), pltpu.VMEM((32, f2), F32)],
                  compiler_params=_params())(dact, u0, u0, w3, b3)


def _loss_head(h, tgt, name):
    t, d = h.shape
    tm, _ = _tiles(t)

    def body(h_ref, t_ref, dh_ref, s_ref):
        i = pl.program_id(0)

        @pl.when(i == 0)
        def _():
            s_ref[...] = jnp.zeros((1, d), F32)

        row = lax.broadcasted_iota(jnp.int32, (tm, d), 0) + i * tm
        err = jnp.where(row >= N_META, h_ref[...] - t_ref[...], 0.0)
        dh_ref[...] = err / d
        s_ref[...] += jnp.sum(err * err, axis=0, keepdims=True)

    return _pcall(body, name=name, grid=(t // tm,), in_specs=[_row(tm, d), _row(tm, d)],
                  out_specs=[_row(tm, d), _const_out((1, d))],
                  out_shape=[jax.ShapeDtypeStruct((t, d), F32), jax.ShapeDtypeStruct((1, d), F32)],
                  compiler_params=_params())(h, tgt)


def _row_tile(rows, row_bytes, budget):
    best = None
    for tr in range(16, rows + 1, 16):
        if rows % tr == 0 and tr * row_bytes <= budget:
            best = tr
    assert best is not None, (rows, row_bytes)
    return best


def _cast_bf16(a, name):
    r, c = a.shape
    tr = _row_tile(r, c * 4, 4 << 20)

    def body(a_ref, o_ref):
        o_ref[...] = a_ref[...].astype(BF16)

    return _pcall(body, name=name, grid=(r // tr,), in_specs=[_row(tr, c)], out_specs=_row(tr, c),
                  out_shape=jax.ShapeDtypeStruct((r, c), BF16), compiler_params=_params())(a)


def _sum_slots(r, name):
    s, rows, c = r.shape
    tr = _row_tile(rows, s * c * 4, 8 << 20)

    def body(r_ref, o_ref):
        acc = r_ref[0].astype(F32)
        for k in range(1, s):
            acc = acc + r_ref[k].astype(F32)
        o_ref[...] = acc

    return _pcall(body, name=name, grid=(rows // tr,),
                  in_specs=[pl.BlockSpec((s, tr, c), lambda i: (0, i, 0))], out_specs=_row(tr, c),
                  out_shape=jax.ShapeDtypeStruct((rows, c), F32), compiler_params=_params())(r)


def _cast_into_window(a, axis, k1, name):
    l, r, c = a.shape
    shape = (l, 4 * r, c) if axis == 1 else (l, r, 4 * c)

    def body(k_ref, a_ref, o_ref):
        o_ref[...] = a_ref[...].astype(BF16)

    omap = (lambda i, k: (i, k[0], 0)) if axis == 1 else (lambda i, k: (i, 0, k[0]))
    gs = pltpu.PrefetchScalarGridSpec(num_scalar_prefetch=1, grid=(l,),
                                      in_specs=[pl.BlockSpec((1, r, c), lambda i, k: (i, 0, 0))],
                                      out_specs=pl.BlockSpec((1, r, c), omap))
    return _pcall(body, name=name, grid_spec=gs, out_shape=jax.ShapeDtypeStruct(shape, BF16),
                  compiler_params=_params())(k1, a)


def _adamw(g, w, m, v, name):
    r, c = g.shape
    tr = _row_tile(r, c * 4, 1 << 20)

    def body(g_ref, w_ref, m_ref, v_ref, d_ref, m2_ref, v2_ref):
        gv = g_ref[...]
        m2 = ADAM_B1 * m_ref[...] + (1.0 - ADAM_B1) * gv
        v2 = ADAM_B2 * v_ref[...] + (1.0 - ADAM_B2) * (gv * gv)
        m_hat = m2 / (1.0 - ADAM_B1 ** ADAM_STEP)
        v_hat = v2 / (1.0 - ADAM_B2 ** ADAM_STEP)
        d_ref[...] = -ADAM_LR * (m_hat / (jnp.sqrt(v_hat) + ADAM_EPS) + ADAM_WD * w_ref[...])
        m2_ref[...] = m2
        v2_ref[...] = v2

    return _pcall(body, name=name, grid=(r // tr,), in_specs=[_row(tr, c)] * 4, out_specs=[_row(tr, c)] * 3,
                  out_shape=[jax.ShapeDtypeStruct((r, c), F32)] * 3, compiler_params=_params())(g, w, m, v)


ANY = pl.BlockSpec(memory_space=pl.ANY)


def _coords():
    return lax.axis_index("x"), lax.axis_index("y"), lax.axis_index("c")


def _window(ref, lead, axis, k, width):
    idx = [slice(None)] * len(ref.shape)
    idx[0] = lead
    idx[axis] = pl.ds(pl.multiple_of(k * width, LANES if axis == len(ref.shape) - 1 else 8), width)
    return ref.at[tuple(idx)]


def _gather_xy(arrs, axes, n_inplace, name):
    n = len(arrs)
    out_shape, widths = [], []
    for i, (a, ax) in enumerate(zip(arrs, axes)):
        s = list(a.shape)
        if i < n_inplace:
            widths.append(s[ax] // 4)
        else:
            widths.append(s[ax])
            s[ax] *= 4
        out_shape.append(jax.ShapeDtypeStruct(tuple(s), a.dtype))

    def body(*refs):
        ins, outs = refs[:n], refs[n:2 * n]
        send_sems, recv_sems, loc_sems = refs[2 * n:]
        x, y, c = _coords()
        k_me = 2 * x + y
        chips = [(1 - x, y), (x, 1 - y), (1 - x, 1 - y)]
        sib = (x, y, 1 - c)

        def half(i, which):
            hl = arrs[i].shape[0] // 2
            return pl.ds(which * hl, hl)

        def win(i, kk, which):
            return _window(outs[i], half(i, which), axes[i], kk, widths[i])

        def copy(i, s, src, dst, to):
            return pltpu.make_async_remote_copy(src_ref=src, dst_ref=dst, send_sem=send_sems.at[i, s],
                                                recv_sem=recv_sems.at[i, s], device_id=to, device_id_type=MESH_T)

        locs = []
        for i in range(n_inplace, n):
            lc = pltpu.make_async_copy(ins[i], _window(outs[i], slice(None), axes[i], k_me, widths[i]), loc_sems.at[i])
            lc.start()
            locs.append(lc)
        started = []
        for i in range(n):
            for j, chip in enumerate(chips):
                src = win(i, k_me, c) if i < n_inplace else ins[i].at[half(i, c)]
                cp = copy(i, j, src, win(i, k_me, c), (*chip, c))
                cp.start()
                started.append(cp)
        for i in range(n):
            for j, chip in enumerate(chips):
                kk = 2 * chip[0] + chip[1]
                copy(i, j, win(i, kk, c), win(i, kk, c), (*chip, c)).wait_recv()
                fw = copy(i, 3 + j, win(i, kk, c), win(i, kk, c), sib)
                fw.start()
                started.append(fw)
        for i in range(n):
            for j, chip in enumerate(chips):
                kk = 2 * chip[0] + chip[1]
                copy(i, 3 + j, win(i, kk, 1 - c), win(i, kk, 1 - c), sib).wait_recv()
        for cp in started:
            cp.wait_send()
        for lc in locs:
            lc.wait()

    return _pcall(body, name=name, in_specs=[ANY] * n, out_specs=[ANY] * n, out_shape=out_shape,
                  input_output_aliases={i: i for i in range(n_inplace)},
                  scratch_shapes=[pltpu.SemaphoreType.DMA((n, 6)), pltpu.SemaphoreType.DMA((n, 6)),
                                  pltpu.SemaphoreType.DMA((n,))],
                  compiler_params=pltpu.CompilerParams(has_side_effects=True))(*arrs)


def _peer(x, y, c, mask):
    bx, by, bc = (mask >> 2) & 1, (mask >> 1) & 1, mask & 1
    return (1 - x if bx else x, 1 - y if by else y, 1 - c if bc else c)


HBM = pl.BlockSpec(memory_space=pltpu.HBM)
SEM = pl.BlockSpec(memory_space=pltpu.SEMAPHORE)


def _piece_shape(shape, wa):
    r, c = shape
    return (r // 2, c // 4) if wa == 1 else (r // 8, c)


def _piece(ref, wa, k, h):
    r, c = ref.shape
    if wa == 1:
        return ref.at[pl.ds(pl.multiple_of(h * (r // 2), 16), r // 2), pl.ds(pl.multiple_of(k * (c // 4), LANES), c // 4)]
    return ref.at[pl.ds(pl.multiple_of((2 * k + h) * (r // 8), 16), r // 8), :]


def _scatter_start(grads, was, name):
    n = len(grads)
    lands = [lax.empty((7, *_piece_shape(g.shape, wa)), g.dtype) for g, wa in zip(grads, was)]

    def body(*refs):
        g_in, land_in = refs[:n], refs[n:2 * n]
        send_sems, recv_sems = refs[2 * n:2 * n + 7 * n], refs[2 * n + 7 * n:2 * n + 14 * n]
        token = refs[-1]
        x, y, c = _coords()
        for i in range(n):
            for mask in range(1, 8):
                px, py, pc = _peer(x, y, c, mask)
                pltpu.make_async_remote_copy(src_ref=_piece(g_in[i], was[i], 2 * px + py, pc),
                                             dst_ref=land_in[i].at[mask - 1], send_sem=send_sems[7 * i + mask - 1],
                                             recv_sem=recv_sems[7 * i + mask - 1], device_id=(px, py, pc),
                                             device_id_type=MESH_T).start()
        token[...] = jnp.zeros(token.shape, F32)

    args = [pltpu.with_memory_space_constraint(a, pltpu.HBM) for a in list(grads) + lands]
    res = _pcall(body, name=name, in_specs=[HBM] * (2 * n),
                 out_specs=[SEM] * (14 * n) + [HBM] * (2 * n) + [pl.BlockSpec(memory_space=pltpu.VMEM)],
                 out_shape=[pltpu.SemaphoreType.DMA(())] * (14 * n)
                 + [pltpu.HBM(a.shape, a.dtype) for a in args] + [jax.ShapeDtypeStruct((8, LANES), F32)],
                 input_output_aliases={i: 14 * n + i for i in range(2 * n)},
                 compiler_params=pltpu.CompilerParams(has_side_effects=pltpu.SideEffectType.DATAFLOW_SIDE_EFFECTING))(*args)
    return res[:7 * n], res[7 * n:14 * n], res[14 * n:15 * n], res[15 * n:16 * n], res[-1]


def _scatter_wait(send_sems, recv_sems, grads, lands, was, after, name):
    n = len(grads)

    def body(*refs):
        g_in, land_in = refs[:n], refs[n:2 * n]
        s_sems, r_sems = refs[2 * n:2 * n + 7 * n], refs[2 * n + 7 * n:2 * n + 14 * n]
        x, y, c = _coords()
        for i in range(n):
            for mask in range(1, 8):
                px, py, pc = _peer(x, y, c, mask)
                cp = pltpu.make_async_remote_copy(src_ref=_piece(g_in[i], was[i], 2 * px + py, pc),
                                                  dst_ref=land_in[i].at[mask - 1], send_sem=s_sems[7 * i + mask - 1],
                                                  recv_sem=r_sems[7 * i + mask - 1], device_id=(px, py, pc),
                                                  device_id_type=MESH_T)
                cp.wait_send()
                cp.wait_recv()

    args = list(grads) + list(lands)
    res = _pcall(body, name=name, in_specs=[HBM] * (2 * n) + [SEM] * (14 * n) + [ANY], out_specs=[HBM] * (2 * n),
                 out_shape=[pltpu.HBM(a.shape, a.dtype) for a in args],
                 input_output_aliases={i: i for i in range(2 * n)},
                 compiler_params=pltpu.CompilerParams(has_side_effects=pltpu.SideEffectType.DATAFLOW_SIDE_EFFECTING))(
                     *args, *send_sems, *recv_sems, after)
    return res[:n], res[n:]


def _sum_pieces(land, g, wa, k1, c1, name):
    s, rp, cp = land.shape
    tr = _row_tile(rp, (s + 1) * cp * 4, 8 << 20)
    nb = rp // tr
    if wa == 1:
        own = pl.BlockSpec((tr, cp), lambda i, k, c: (c[0] * nb + i, k[0]))
    else:
        own = pl.BlockSpec((tr, cp), lambda i, k, c: ((2 * k[0] + c[0]) * nb + i, 0))

    def body(k_ref, c_ref, l_ref, g_ref, o_ref):
        acc = l_ref[0].astype(F32)
        for j in range(1, s):
            acc = acc + l_ref[j].astype(F32)
        o_ref[...] = acc + g_ref[...].astype(F32)

    gs = pltpu.PrefetchScalarGridSpec(
        num_scalar_prefetch=2, grid=(nb,),
        in_specs=[pl.BlockSpec((s, tr, cp), lambda i, k, c: (0, i, 0)), own],
        out_specs=pl.BlockSpec((tr, cp), lambda i, k, c: (c[0] * nb + i, 0)))
    return _pcall(body, name=name, grid_spec=gs, out_shape=jax.ShapeDtypeStruct((2 * rp, cp), F32),
                  compiler_params=_params())(k1, c1, land, g)


SWAP_CHUNKS = 4


def _sibling_swap(fulls, name):
    n = len(fulls)
    out_shape = [jax.ShapeDtypeStruct(a.shape, a.dtype) for a in fulls]

    def body(*refs):
        outs = refs[n:2 * n]
        send_sems, recv_sems = refs[2 * n:]
        x, y, c = _coords()

        def chunk(i, which, q):
            hl, rc = fulls[i].shape[0] // 2, fulls[i].shape[1] // SWAP_CHUNKS
            ref = outs[i].at[pl.ds(which * hl, hl), pl.ds(q * rc, rc)]
            return pltpu.make_async_remote_copy(src_ref=ref, dst_ref=ref, send_sem=send_sems.at[i, q],
                                                recv_sem=recv_sems.at[i, q], device_id=(x, y, 1 - c),
                                                device_id_type=MESH_T)

        started = []
        for i in range(n):
            for q in range(SWAP_CHUNKS):
                cp = chunk(i, c, q)
                cp.start()
                started.append(cp)
        for cp in started:
            cp.wait_send()
        for i in range(n):
            for q in range(SWAP_CHUNKS):
                chunk(i, 1 - c, q).wait_recv()

    for a in fulls:
        assert a.shape[1] % (8 * SWAP_CHUNKS) == 0, a.shape
    return _pcall(body, name=name, in_specs=[ANY] * n, out_specs=[ANY] * n, out_shape=out_shape,
                  input_output_aliases={i: i for i in range(n)},
                  scratch_shapes=[pltpu.SemaphoreType.DMA((n, SWAP_CHUNKS)), pltpu.SemaphoreType.DMA((n, SWAP_CHUNKS))],
                  compiler_params=pltpu.CompilerParams(has_side_effects=True))(*fulls)


def _scatter8(pk, name):
    r, cdim = pk.shape
    pr = r // 8
    assert pr % 8 == 0

    def body(p_ref, o_ref, send_sems, recv_sems, loc_sem):
        x, y, c = _coords()

        def piece(px, py, pc):
            return p_ref.at[pl.ds(pl.multiple_of((4 * px + 2 * py + pc) * pr, 8), pr)]

        lc = pltpu.make_async_copy(piece(x, y, c), o_ref.at[7], loc_sem)
        lc.start()
        started = []
        for mask in range(1, 8):
            px, py, pc = _peer(x, y, c, mask)
            cp = pltpu.make_async_remote_copy(src_ref=piece(px, py, pc), dst_ref=o_ref.at[mask - 1],
                                              send_sem=send_sems.at[mask - 1], recv_sem=recv_sems.at[mask - 1],
                                              device_id=(px, py, pc), device_id_type=MESH_T)
            cp.start()
            started.append(cp)
        for cp in started:
            cp.wait()
        lc.wait()

    return _pcall(body, name=name, in_specs=[ANY], out_specs=ANY, out_shape=jax.ShapeDtypeStruct((8, pr, cdim), F32),
                  scratch_shapes=[pltpu.SemaphoreType.DMA((7,)), pltpu.SemaphoreType.DMA((7,)),
                                  pltpu.SemaphoreType.DMA],
                  compiler_params=pltpu.CompilerParams(has_side_effects=True))(pk)


def _gather_all(pk, name):
    r, cdim = pk.shape

    def body(p_ref, o_ref, send_sems, recv_sems, loc_sem):
        x, y, c = _coords()
        lc = pltpu.make_async_copy(p_ref, o_ref.at[4 * x + 2 * y + c], loc_sem)
        lc.start()
        started = []
        for mask in range(1, 8):
            px, py, pc = _peer(x, y, c, mask)
            cp = pltpu.make_async_remote_copy(src_ref=p_ref, dst_ref=o_ref.at[4 * x + 2 * y + c],
                                              send_sem=send_sems.at[mask - 1], recv_sem=recv_sems.at[mask - 1],
                                              device_id=(px, py, pc), device_id_type=MESH_T)
            cp.start()
            started.append(cp)
        for cp in started:
            cp.wait()
        lc.wait()

    return _pcall(body, name=name, in_specs=[ANY], out_specs=ANY, out_shape=jax.ShapeDtypeStruct((8, r, cdim), F32),
                  scratch_shapes=[pltpu.SemaphoreType.DMA((7,)), pltpu.SemaphoreType.DMA((7,)),
                                  pltpu.SemaphoreType.DMA],
                  compiler_params=pltpu.CompilerParams(has_side_effects=True))(pk)


PACK_C = 1024


def _pack(arrs, row_mult):
    parts = []
    for a in arrs:
        flat = a.reshape(-1)
        parts.append(jnp.pad(flat, (0, (-flat.shape[0]) % PACK_C)))
    flat = jnp.concatenate(parts)
    flat = jnp.pad(flat, (0, (-flat.shape[0]) % (PACK_C * row_mult)))
    return flat.reshape(-1, PACK_C)


def _unpack(pk, shapes):
    flat = pk.reshape(-1)
    out, off = [], 0
    for s in shapes:
        size = 1
        for dd in s:
            size *= dd
        out.append(flat[off:off + size].reshape(s))
        off += size + (-size) % PACK_C
    return out


def _layer_params(full, l):
    p = {}
    for k in ['g_pre_mix', 'lru_conv_b', 'lru_ba', 'lru_bx', 'lru_lambda', 'conv_b', 'conv_ln_g', 'conv_ln_b', 'g_out_lru',
              'g_out_conv', 'g_post_mix', 'g_pre_ffn', 'ffn_conv_b', 'g_post_ffn']:
        p[k] = full[k][l][None, :]
    for k in ['lru_conv_w', 'conv_w', 'ffn_conv_w', 'w_in', 'w_out', 'w_up', 'w_down']:
        p[k] = full[k][l]
    p['lru_wa'] = full['lru_wa_bf'][l]
    p['lru_wx'] = full['lru_wx_bf'][l]
    return p


def _step(x, loss_target, w, m, v):
    depth = w['w_in'].shape[0]
    d = x.shape[2]
    xk, yk, _ = _coords()
    k_me = 2 * xk + yk

    k1 = jnp.reshape(k_me, (1,)).astype(jnp.int32)
    big_bf = {k: _cast_into_window(w[k], BIG_AXIS[k], k1, "cast_" + k) for k in BIG}
    sh_pad = [w['meta_tokens']] + [jnp.pad(w[k], ((0, 0), (0, (-w[k].shape[1]) % 8), (0, 0))) for k in SH_SMALL[1:]]
    gath = _gather_xy([big_bf[k] for k in BIG] + sh_pad, [BIG_AXIS[k] for k in BIG] + [1, 2, 2, 2], len(BIG),
                      "gather_weights")
    full = dict(w)
    full.update(dict(zip(BIG, gath[:len(BIG)])))
    full['meta_tokens'] = gath[len(BIG)]
    for k, a in zip(SH_SMALL[1:], gath[len(BIG) + 1:]):
        full[k] = a[:, :w[k].shape[1]]
    for k in ('lru_wa', 'lru_wx'):
        full[k + '_bf'] = _cast_bf16(w[k].reshape(-1, LANES), "cast_" + k).reshape(w[k].shape)

    h = jnp.concatenate([full['meta_tokens'], x[0]], axis=0)
    tgt = jnp.pad(loss_target[0], ((N_META, 0), (0, 0)))
    pending = {}

    def send_big_grads(l, keys, g):
        res = _scatter_start([g[k] for k in keys], [BIG_AXIS[k] - 1 for k in keys], "scatter_start_%d_%s" % (l, keys[-1]))
        pending[(l, keys)] = res[:4]
        return res[4]

    sq, dh, gl = _fwd_bwd(h, tgt, full, depth, send_big_grads)
    loss = lax.psum(0.5 * jnp.sum(sq) / d, ("x", "y", "c"))
    return _reduce_update(loss, dh, gl, pending, w, m, v, depth, k_me)


def _fwd_bwd(h, tgt, full, depth, on_grads=None):
    saved = []
    for l in range(depth):
        p = _layer_params(full, l)
        proj, zb1 = _rms_matmul(h, p['g_pre_mix'], p['w_in'], "in_proj")
        y, hs, c1 = _mix_fwd(proj, p, "mix_fwd")
        o, h1 = _matmul_rms_res(y, p['w_out'], h, p['g_post_mix'], "out_proj")
        u0, zb2 = _rms_matmul(h1, p['g_pre_ffn'], p['w_up'], "up_proj")
        act = _ffn_act_fwd(u0, p['ffn_conv_w'], p['ffn_conv_b'], "ffn_act_fwd")
        f, h2 = _matmul_rms_res(act, p['w_down'], h1, p['g_post_ffn'], "down_proj")
        saved.append((p, h, zb1, proj, y, hs, c1, o, h1, zb2, u0, act, f))
        h = h2

    dh, sq = _loss_head(h, tgt, "loss_head")

    def tied(a, token):
        return a if token is None else a + token[0:1, 0:1]

    gl = [None] * depth
    for l in reversed(range(depth)):
        p, h0, zb1, proj, y, hs, c1, o, h1, zb2, u0, act, f = saved[l]
        g = {}
        dact, dfb, g['g_post_ffn'] = _rmsbwd_matmul_nt(f, p['g_post_ffn'], dh, p['w_down'], "down_bwd")
        g['w_down'] = _matmul_tn(act, dfb, "down_dw")
        du0, g['ffn_conv_w'], g['ffn_conv_b'] = _ffn_act_bwd(dact, u0, p['ffn_conv_w'], p['ffn_conv_b'], "ffn_act_bwd")
        g['w_up'] = _matmul_tn(zb2, du0, "up_dw")
        dh1, g['g_pre_ffn'] = _matmul_nt_rmsbwd_res(du0, p['w_up'], h1, p['g_pre_ffn'], dh, "up_bwd")
        dy, dob, g['g_post_mix'] = _rmsbwd_matmul_nt(o, p['g_post_mix'], dh1, p['w_out'], "out_bwd")
        g['w_out'] = _matmul_tn(y, dob, "out_dw")
        token = on_grads(l, ('w_down', 'w_up', 'w_out'), g) if on_grads is not None else None
        pm = dict(p)
        pm['g_out_conv'] = tied(p['g_out_conv'], token)
        dproj, gm = _mix_bwd(dy, proj, hs, c1, pm, "mix_bwd")
        g.update(gm)
        g['w_in'] = _matmul_tn(zb1, dproj, "in_dw")
        token = on_grads(l, ('w_in',), g) if on_grads is not None else None
        dh, g['g_pre_mix'] = _matmul_nt_rmsbwd_res(dproj, p['w_in'], h0, tied(p['g_pre_mix'], token), dh1, "in_bwd")
        gl[l] = g
    return sq, dh, gl


def _reduce_update(loss, dh, gl, pending, w, m, v, depth, k_me):
    grad_x = dh[N_META:][None]

    def stacked(k):
        return jnp.stack([gl[l][k].reshape(w[k].shape[1:]) if k not in SH_SMALL + BIG else gl[l][k]
                          for l in range(depth)])

    k1 = jnp.reshape(k_me, (1,)).astype(jnp.int32)
    c1 = jnp.reshape(lax.axis_index("c"), (1,)).astype(jnp.int32)
    order, halves = [], []
    for (l, keys), (send_sems, recv_sems, g_thru, lands) in pending.items():
        was = [BIG_AXIS[k] - 1 for k in keys]
        g_own, lands = _scatter_wait(send_sems, recv_sems, g_thru, lands, was, dh, "scatter_wait_%d_%s" % (l, keys[-1]))
        for k, wa, land, g_k in zip(keys, was, lands, g_own):
            red = _sum_pieces(land, g_k, wa, k1, c1, "grad_sum")
            order.append((l, k))
            halves.append(red.reshape(2, land.shape[1], land.shape[2]))
    swapped = dict(zip(order, _sibling_swap(halves, "grad_swap")))
    big_red = [jnp.stack([swapped[(l, k)].reshape(w[k].shape[1:]) for l in range(depth)]) for k in BIG]

    out = {}
    for k, gk in zip(BIG, big_red):
        c2 = gk.shape[-1]
        dl, m2, v2 = _adamw(gk.reshape(-1, c2), w[k].reshape(-1, c2), m[k].reshape(-1, c2), v[k].reshape(-1, c2),
                            "adamw_big")
        out[k] = (gk, dl.reshape(gk.shape), m2.reshape(gk.shape), v2.reshape(gk.shape))

    rep_g = [stacked(k) for k in REP_SMALL]
    sh_g = [dh[:N_META]] + [stacked(k) for k in SH_SMALL[1:]]
    n_rep_rows = _pack(rep_g, 1).shape[0]
    pk = jnp.concatenate([_pack(rep_g, 1), _pack(sh_g, 1)])
    pk = jnp.pad(pk, ((0, (-pk.shape[0]) % 256), (0, 0)))
    part = _sum_slots(_scatter8(pk, "small_scatter"), "small_sum")
    red = _gather_all(part, "small_gather").reshape(pk.shape)
    rep_red = _unpack(red[:n_rep_rows], [a.shape for a in rep_g])
    sh_red = []
    for a in _unpack(red[n_rep_rows:], [a.shape for a in sh_g]):
        wd = a.shape[-1] // 4
        sh_red.append(lax.dynamic_slice_in_dim(a, k_me * wd, wd, axis=a.ndim - 1))

    for names, grads_ in ((REP_SMALL, rep_red), (SH_SMALL, sh_red)):
        res = _adamw(_pack(grads_, 16), _pack([w[k] for k in names], 16), _pack([m[k] for k in names], 16),
                     _pack([v[k] for k in names], 16), "adamw_small")
        shapes = [w[k].shape for k in names]
        un = [_unpack(r, shapes) for r in res]
        for j, k in enumerate(names):
            out[k] = (grads_[j].reshape(w[k].shape), un[0][j], un[1][j], un[2][j])

    return (loss, grad_x, *[out[k][0] for k in WEIGHTS], *[out[k][1] for k in WEIGHTS],
            *[out[k][2] for k in WEIGHTS], *[out[k][3] for k in WEIGHTS])


def kernel(x, meta_tokens, g_pre_mix, w_in, lru_conv_w, lru_conv_b, lru_wa, lru_ba, lru_wx, lru_bx, lru_lambda, conv_w, conv_b, conv_ln_g, conv_ln_b, g_out_lru, g_out_conv, w_out, g_post_mix, g_pre_ffn, w_up, ffn_conv_w, ffn_conv_b, w_down, g_post_ffn, loss_target, m_meta_tokens, m_g_pre_mix, m_w_in, m_lru_conv_w, m_lru_conv_b, m_lru_wa, m_lru_ba, m_lru_wx, m_lru_bx, m_lru_lambda, m_conv_w, m_conv_b, m_conv_ln_g, m_conv_ln_b, m_g_out_lru, m_g_out_conv, m_w_out, m_g_post_mix, m_g_pre_ffn, m_w_up, m_ffn_conv_w, m_ffn_conv_b, m_w_down, m_g_post_ffn, v_meta_tokens, v_g_pre_mix, v_w_in, v_lru_conv_w, v_lru_conv_b, v_lru_wa, v_lru_ba, v_lru_wx, v_lru_bx, v_lru_lambda, v_conv_w, v_conv_b, v_conv_ln_g, v_conv_ln_b, v_g_out_lru, v_g_out_conv, v_w_out, v_g_post_mix, v_g_pre_ffn, v_w_up, v_ffn_conv_w, v_ffn_conv_b, v_w_down, v_g_post_ffn):
    w = dict(meta_tokens=meta_tokens, g_pre_mix=g_pre_mix, w_in=w_in, lru_conv_w=lru_conv_w, lru_conv_b=lru_conv_b,
             lru_wa=lru_wa, lru_ba=lru_ba, lru_wx=lru_wx, lru_bx=lru_bx, lru_lambda=lru_lambda, conv_w=conv_w,
             conv_b=conv_b, conv_ln_g=conv_ln_g, conv_ln_b=conv_ln_b, g_out_lru=g_out_lru, g_out_conv=g_out_conv,
             w_out=w_out, g_post_mix=g_post_mix, g_pre_ffn=g_pre_ffn, w_up=w_up, ffn_conv_w=ffn_conv_w,
             ffn_conv_b=ffn_conv_b, w_down=w_down, g_post_ffn=g_post_ffn)
    m = dict(meta_tokens=m_meta_tokens, g_pre_mix=m_g_pre_mix, w_in=m_w_in, lru_conv_w=m_lru_conv_w,
             lru_conv_b=m_lru_conv_b, lru_wa=m_lru_wa, lru_ba=m_lru_ba, lru_wx=m_lru_wx, lru_bx=m_lru_bx,
             lru_lambda=m_lru_lambda, conv_w=m_conv_w, conv_b=m_conv_b, conv_ln_g=m_conv_ln_g, conv_ln_b=m_conv_ln_b,
             g_out_lru=m_g_out_lru, g_out_conv=m_g_out_conv, w_out=m_w_out, g_post_mix=m_g_post_mix,
             g_pre_ffn=m_g_pre_ffn, w_up=m_w_up, ffn_conv_w=m_ffn_conv_w, ffn_conv_b=m_ffn_conv_b, w_down=m_w_down,
             g_post_ffn=m_g_post_ffn)
    v = dict(meta_tokens=v_meta_tokens, g_pre_mix=v_g_pre_mix, w_in=v_w_in, lru_conv_w=v_lru_conv_w,
             lru_conv_b=v_lru_conv_b, lru_wa=v_lru_wa, lru_ba=v_lru_ba, lru_wx=v_lru_wx, lru_bx=v_lru_bx,
             lru_lambda=v_lru_lambda, conv_w=v_conv_w, conv_b=v_conv_b, conv_ln_g=v_conv_ln_g, conv_ln_b=v_conv_ln_b,
             g_out_lru=v_g_out_lru, g_out_conv=v_g_out_conv, w_out=v_w_out, g_post_mix=v_g_post_mix,
             g_pre_ffn=v_g_pre_ffn, w_up=v_w_up, ffn_conv_w=v_ffn_conv_w, ffn_conv_b=v_ffn_conv_b, w_down=v_w_down,
             g_post_ffn=v_g_post_ffn)
    return _step(x, loss_target, w, m, v)
```

```python
import functools

import jax
import jax.numpy as jnp
from jax import lax
from jax.experimental import pallas as pl
from jax.experimental.pallas import tpu as pltpu

F32 = jnp.float32
BF16 = jnp.bfloat16
EPS = 1e-6
N_META = 16
LRU_C = 8.0
CONV_K = 31
LRU_K = 4
FFN_K = 3
CONV_ROWS = 24
SYNC_LAYERS = 2
SCAN_U = 3
RSCAN_U = 2
LANES = 128
VMEM_LIMIT = 56 * 1024 * 1024
ADAM_LR, ADAM_B1, ADAM_B2, ADAM_EPS, ADAM_WD, ADAM_STEP = 0.001, 0.9, 0.999, 1e-08, 0.01, 10
MESH_T = pl.DeviceIdType.MESH
NT_DIMS = (((1,), (1,)), ((), ()))
TN_DIMS = (((0,), (0,)), ((), ()))

REP_SMALL = ['g_pre_mix', 'lru_conv_b', 'lru_wa', 'lru_ba', 'lru_wx', 'lru_bx', 'lru_lambda', 'conv_b', 'conv_ln_g',
             'conv_ln_b', 'g_out_lru', 'g_out_conv', 'g_post_mix', 'g_pre_ffn', 'ffn_conv_b', 'g_post_ffn']
SH_SMALL = ['meta_tokens', 'lru_conv_w', 'conv_w', 'ffn_conv_w']
BIG = ['w_in', 'w_out', 'w_up', 'w_down']
BIG_AXIS = {'w_in': 2, 'w_out': 1, 'w_up': 2, 'w_down': 1}
WEIGHTS = ['meta_tokens', 'g_pre_mix', 'w_in', 'lru_conv_w', 'lru_conv_b', 'lru_wa', 'lru_ba', 'lru_wx', 'lru_bx',
           'lru_lambda', 'conv_w', 'conv_b', 'conv_ln_g', 'conv_ln_b', 'g_out_lru', 'g_out_conv', 'w_out',
           'g_post_mix', 'g_pre_ffn', 'w_up', 'ffn_conv_w', 'ffn_conv_b', 'w_down', 'g_post_ffn']


def _pcall(body, **kw):
    return pl.pallas_call(body, **kw)


def _params(n_grid=1):
    return pltpu.CompilerParams(dimension_semantics=("arbitrary",) * n_grid, vmem_limit_bytes=VMEM_LIMIT)


def _tiles(t):
    if t % 432 == 0:
        return 432, 144
    assert t % 48 == 0
    return 48, 48


def _row(tm, n):
    return pl.BlockSpec((tm, n), lambda i: (i, 0))


def _rrow(tm, n, nt):
    return pl.BlockSpec((tm, n), lambda i: (nt - 1 - i, 0))


def _halo(hb, n, tm, nt):
    return pl.BlockSpec((hb, n), lambda i: (jnp.maximum((nt - 1 - i) * (tm // hb) - 1, 0), 0))


def _const(shape):
    nd = len(shape)
    return pl.BlockSpec(shape, lambda *_: (0,) * nd, pipeline_mode=pl.Buffered(1))


def _const_out(shape):
    nd = len(shape)
    return pl.BlockSpec(shape, lambda *_: (0,) * nd)


def _sigmoid(x):
    return 1.0 / (1.0 + jnp.exp(-x))


def _gelu(x):
    return 0.5 * x * (1.0 + jnp.tanh(0.7978845608028654 * (x + 0.044715 * (x * x * x))))


def _gelu_and_grad(x):
    k = 0.7978845608028654
    x2 = x * x
    th = jnp.tanh(k * (x + 0.044715 * (x2 * x)))
    return 0.5 * x * (1.0 + th), 0.5 * (1.0 + th) + 0.5 * x * (1.0 - th * th) * (k * (1.0 + 0.134145 * x2))


def _expm1(x):
    return jnp.where(jnp.abs(x) < 1e-2, x * (1.0 + x * (0.5 + x * (1.0 / 6.0 + x * (1.0 / 24.0)))), jnp.exp(x) - 1.0)


def _softplus(x):
    e = jnp.exp(-jnp.abs(x))
    return jnp.maximum(x, 0.0) + jnp.where(e < 1e-4, e * (1.0 - 0.5 * e), jnp.log(1.0 + e))


def _lru_gates(pa, px, sp):
    r = _sigmoid(pa)
    ig = _sigmoid(px)
    la = (-LRU_C * r) * sp
    return r, ig, jnp.exp(la), jnp.sqrt(-_expm1(2.0 * la))


def _rms(x):
    return lax.rsqrt(jnp.mean(x * x, axis=-1, keepdims=True) + EPS)


def _rms_bwd(x, g, dy):
    r = _rms(x)
    xr = x * r
    dyg = dy * g
    return r * (dyg - xr * jnp.mean(dyg * xr, axis=-1, keepdims=True)), xr


def _col_chunk(n):
    return 1536 if n % 1536 == 0 else 1024


def _rms_matmul(h, g, w, name):
    t, d = h.shape
    n = w.shape[1]
    tm, _ = _tiles(t)
    cn = _col_chunk(n)

    def body(h_ref, g_ref, w_ref, p_ref, zb_ref):
        x = h_ref[...]
        zb = ((x * _rms(x)) * g_ref[...]).astype(BF16)
        zb_ref[...] = zb
        for c in range(n // cn):
            p_ref[:, c * cn:(c + 1) * cn] = jnp.dot(zb, w_ref[:, c * cn:(c + 1) * cn], preferred_element_type=F32)

    return _pcall(body, name=name, grid=(t // tm,),
                  in_specs=[_row(tm, d), _const((1, d)), _const((d, n))],
                  out_specs=[_row(tm, n), _row(tm, d)],
                  out_shape=[jax.ShapeDtypeStruct((t, n), F32), jax.ShapeDtypeStruct((t, d), BF16)],
                  compiler_params=_params())(h, g, w)


def _matmul_rms_res(a, w, h, g, name):
    t, k = a.shape
    d = w.shape[1]
    tm, _ = _tiles(t)

    def body(a_ref, w_ref, h_ref, g_ref, o_ref, hn_ref):
        o = jnp.dot(a_ref[...], w_ref[...], preferred_element_type=F32)
        o_ref[...] = o
        hn_ref[...] = h_ref[...] + (o * _rms(o)) * g_ref[...]

    return _pcall(body, name=name, grid=(t // tm,),
                  in_specs=[_row(tm, k), _const((k, d)), _row(tm, d), _const((1, d))],
                  out_specs=[_row(tm, d), _row(tm, d)],
                  out_shape=[jax.ShapeDtypeStruct((t, d), F32), jax.ShapeDtypeStruct((t, d), F32)],
                  compiler_params=_params())(a, w, h, g)


def _rmsbwd_matmul_nt(x, g, dy, w, name):
    t, d = x.shape
    n = w.shape[0]
    tm, _ = _tiles(t)
    cn = _col_chunk(n)

    def body(x_ref, g_ref, dy_ref, w_ref, da_ref, dxb_ref, dg_ref):
        @pl.when(pl.program_id(0) == 0)
        def _():
            dg_ref[...] = jnp.zeros((1, d), F32)

        dy = dy_ref[...]
        dx, xr = _rms_bwd(x_ref[...], g_ref[...], dy)
        dg_ref[...] += jnp.sum(dy * xr, axis=0, keepdims=True)
        dxb = dx.astype(BF16)
        dxb_ref[...] = dxb
        for c in range(n // cn):
            da_ref[:, c * cn:(c + 1) * cn] = lax.dot_general(dxb, w_ref[c * cn:(c + 1) * cn, :], NT_DIMS,
                                                             preferred_element_type=F32)

    return _pcall(body, name=name, grid=(t // tm,),
                  in_specs=[_row(tm, d), _const((1, d)), _row(tm, d), _const((n, d))],
                  out_specs=[_row(tm, n), _row(tm, d), _const_out((1, d))],
                  out_shape=[jax.ShapeDtypeStruct((t, n), F32), jax.ShapeDtypeStruct((t, d), BF16),
                             jax.ShapeDtypeStruct((1, d), F32)],
                  compiler_params=_params())(x, g, dy, w)


def _matmul_nt_rmsbwd_res(dp, w, h, g, dh, name):
    t, n = dp.shape
    d = w.shape[0]
    tm, _ = _tiles(t)

    def body(dp_ref, w_ref, h_ref, g_ref, dh_ref, out_ref, dg_ref):
        @pl.when(pl.program_id(0) == 0)
        def _():
            dg_ref[...] = jnp.zeros((1, d), F32)

        dz = lax.dot_general(dp_ref[...], w_ref[...], NT_DIMS, preferred_element_type=F32)
        dx, xr = _rms_bwd(h_ref[...], g_ref[...], dz)
        dg_ref[...] += jnp.sum(dz * xr, axis=0, keepdims=True)
        out_ref[...] = dh_ref[...] + dx

    return _pcall(body, name=name, grid=(t // tm,),
                  in_specs=[_row(tm, n), _const((d, n)), _row(tm, d), _const((1, d)), _row(tm, d)],
                  out_specs=[_row(tm, d), _const_out((1, d))],
                  out_shape=[jax.ShapeDtypeStruct((t, d), F32), jax.ShapeDtypeStruct((1, d), F32)],
                  compiler_params=_params())(dp, w, h, g, dh)


def _matmul_tn(a, b, name):
    t, k = a.shape
    n = b.shape[1]
    tm, _ = _tiles(t)
    nt = t // tm
    bn = min(n, (1536 * 1024) // k)
    assert n % bn == 0 and bn % LANES == 0

    def body(a_ref, b_ref, o_ref, acc):
        @pl.when(pl.program_id(1) == 0)
        def _():
            acc[...] = jnp.zeros((k, bn), F32)

        acc[...] += lax.dot_general(a_ref[...], b_ref[...], TN_DIMS, preferred_element_type=F32)

        @pl.when(pl.program_id(1) == nt - 1)
        def _():
            o_ref[...] = acc[...].astype(BF16)

    return _pcall(body, name=name, grid=(n // bn, nt),
                  in_specs=[pl.BlockSpec((tm, k), lambda j, i: (i, 0)), pl.BlockSpec((tm, bn), lambda j, i: (i, j))],
                  out_specs=pl.BlockSpec((k, bn), lambda j, i: (0, j)),
                  out_shape=jax.ShapeDtypeStruct((k, n), BF16),
                  scratch_shapes=[pltpu.VMEM((k, bn), F32)],
                  compiler_params=_params(2))(a, b)


MIX_PARAMS = ['lru_conv_w', 'lru_conv_b', 'lru_wa', 'lru_ba', 'lru_wx', 'lru_bx', 'lru_lambda', 'conv_w', 'conv_b',
              'conv_ln_g', 'conv_ln_b', 'g_out_lru', 'g_out_conv']


def _head_gates(xcb, wa_ref, wx_ref, pa_s, px_s, ba, bx, heads):
    for hd in range(heads):
        sl = slice(LANES * hd, LANES * (hd + 1))
        pa_s[:, sl] = jnp.dot(xcb[:, sl], wa_ref[hd], preferred_element_type=F32) + ba[:, sl]
        px_s[:, sl] = jnp.dot(xcb[:, sl], wx_ref[hd], preferred_element_type=F32) + bx[:, sl]


def _lru_conv_chunk(exta, cw4_ref, cb4, r0):
    win = exta[pl.ds(r0, 16), :]
    acc = cw4_ref[3:4, :] * win[8:16]
    for k in range(LRU_K - 1):
        acc = acc + cw4_ref[k:k + 1, :] * pltpu.roll(win, LRU_K - 1 - k, 0)[8:16]
    return acc + cb4


def _mix_fwd(proj, p, name):
    t = proj.shape[0]
    w = p['lru_conv_b'].shape[1]
    cw = p['conv_b'].shape[1]
    heads = p['lru_wa'].shape[0]
    _, tm = _tiles(t)
    nch = tm // 8

    def body(proj_ref, cw4_ref, cb4_ref, wa_ref, ba_ref, wx_ref, bx_ref, lam_ref, cw31_ref, cb31_ref, lng_ref,
             lnb_ref, ga_ref, gb_ref, y_ref, hs_ref, c1_ref, exta, xc_s, pa_s, px_s, extb, nbuf, hcar):
        @pl.when(pl.program_id(0) == 0)
        def _():
            exta[0:8, :] = jnp.zeros((8, w), F32)
            extb[0:32, :] = jnp.zeros((32, cw), F32)
            hcar[...] = jnp.zeros((8, w), F32)

        exta[8:8 + tm, :] = proj_ref[:, 0:w]
        cb4 = cb4_ref[...]

        def conv_a(c, carry):
            r0 = pl.multiple_of(c * 8, 8)
            xc_s[pl.ds(r0, 8), :] = _lru_conv_chunk(exta, cw4_ref, cb4, r0)
            return carry

        lax.fori_loop(0, nch, conv_a, 0, unroll=3)
        _head_gates(xc_s[...].astype(BF16), wa_ref, wx_ref, pa_s, px_s, ba_ref[...], bx_ref[...], heads)

        sp = _softplus(-lam_ref[...])
        ga = ga_ref[...]
        row = lax.broadcasted_iota(jnp.int32, (8, w), 0)

        def scan_c(cg, hprev):
            part = []
            for u in range(SCAN_U):
                r0 = pl.multiple_of((cg * SCAN_U + u) * 8, 8)
                xc = xc_s[pl.ds(r0, 8), :]
                _, ig, a, m = _lru_gates(pa_s[pl.ds(r0, 8), :], px_s[pl.ds(r0, 8), :], sp)
                aa, bb = a, m * (ig * xc)
                for d in (1, 2, 4):
                    a_s = pltpu.roll(aa, d, 0)
                    b_s = pltpu.roll(bb, d, 0)
                    msk = row >= d
                    bb = jnp.where(msk, aa * b_s + bb, bb)
                    aa = jnp.where(msk, aa * a_s, aa)
                part.append((r0, aa, bb, _gelu(proj_ref[pl.ds(r0, 8), w:2 * w])))
            for r0, aa, bb, ge in part:
                hs = aa * hprev + bb
                hs_ref[pl.ds(r0, 8), :] = hs
                ya = hs * ge
                nbuf[pl.ds(r0, 8), 0:w] = (ya * _rms(ya)) * ga
                hprev = hs[7:8, :]
            return hprev

        hcar[0:1, :] = lax.fori_loop(0, nch // SCAN_U, scan_c, hcar[0:1, :])

        extb[32:32 + tm, :] = proj_ref[:, 2 * w:2 * w + cw] * _sigmoid(proj_ref[:, 2 * w + cw:2 * w + 2 * cw])

        def conv_b(c, carry):
            r0 = pl.multiple_of(c * CONV_ROWS, 8)
            ybs = []
            ssq = jnp.zeros((CONV_ROWS, 1), F32)
            for lb in range(cw // LANES):
                sl = slice(LANES * lb, LANES * (lb + 1))
                win = extb[pl.ds(r0, CONV_ROWS + 32), sl]
                rolled = [win] + [pltpu.roll(win, rr, 0) for rr in range(1, 8)]
                parts = [None] * 4
                for k in range(CONV_K):
                    q, rr = divmod(CONV_K - 1 - k, 8)
                    term = cw31_ref[k:k + 1, sl] * rolled[rr][32 - 8 * q:32 - 8 * q + CONV_ROWS]
                    parts[k % 4] = term if parts[k % 4] is None else parts[k % 4] + term
                acc = ((parts[0] + parts[1]) + (parts[2] + parts[3])) + cb31_ref[:, sl]
                c1_ref[pl.ds(r0, CONV_ROWS), sl] = acc
                dlt = acc - jnp.mean(acc, axis=-1, keepdims=True)
                c2 = dlt * lax.rsqrt(jnp.mean(dlt * dlt, axis=-1, keepdims=True) + EPS)
                yb0 = c2 * lng_ref[:, sl] + lnb_ref[:, sl]
                yb = yb0 * _sigmoid(yb0)
                ybs.append(yb)
                ssq = ssq + jnp.sum(yb * yb, axis=-1, keepdims=True)
            rb = lax.rsqrt(ssq / cw + EPS)
            for lb in range(cw // LANES):
                sl = slice(LANES * lb, LANES * (lb + 1))
                nbuf[pl.ds(r0, CONV_ROWS), w + LANES * lb:w + LANES * (lb + 1)] = (ybs[lb] * rb) * gb_ref[:, sl]
            return carry

        lax.fori_loop(0, tm // CONV_ROWS, conv_b, 0, unroll=2)
        exta[0:8, :] = exta[tm:tm + 8, :]
        extb[0:32, :] = extb[tm:tm + 32, :]
        y_ref[...] = nbuf[...].astype(BF16)

    consts = [p[k] for k in MIX_PARAMS]
    return _pcall(body, name=name, grid=(t // tm,),
                  in_specs=[_row(tm, 2 * w + 2 * cw)] + [_const(c.shape) for c in consts],
                  out_specs=[_row(tm, w + cw), _row(tm, w), _row(tm, cw)],
                  out_shape=[jax.ShapeDtypeStruct((t, w + cw), BF16), jax.ShapeDtypeStruct((t, w), F32),
                             jax.ShapeDtypeStruct((t, cw), F32)],
                  scratch_shapes=[pltpu.VMEM((8 + tm, w), F32), pltpu.VMEM((tm, w), F32), pltpu.VMEM((tm, w), F32),
                                  pltpu.VMEM((tm, w), F32), pltpu.VMEM((32 + tm, cw), F32),
                                  pltpu.VMEM((tm, w + cw), F32), pltpu.VMEM((8, w), F32)],
                  compiler_params=_params())(proj, *consts)


def _mix_bwd(dy, proj, hs, c1, p, name):
    t = proj.shape[0]
    w = p['lru_conv_b'].shape[1]
    cw = p['conv_b'].shape[1]
    heads = p['lru_wa'].shape[0]
    _, tm = _tiles(t)
    nt = t // tm
    nch = tm // 8
    nlb = cw // LANES
    G_CB4, G_CW4, G_BA, G_BX, G_SP, G_GA, NGW = 0, 1, 5, 6, 7, 8, 9
    G_CB31, G_LNG, G_LNB, G_GB, G_CW31, NGC = 0, 1, 2, 3, 4, 4 + CONV_K

    def body(dy_ref, proj_ref, projh_ref, hs_ref, hsh_ref, c1_ref, cw4_ref, cb4_ref, wa_ref, ba_ref, wx_ref, bx_ref,
             lam_ref, cw31_ref, cb31_ref, lng_ref, lnb_ref, ga_ref, gb_ref,
             dproj_ref, dcw4_ref, dcb4_ref, dwa_ref, dba_ref, dwx_ref, dbx_ref, dlam_ref, dcw31_ref, dcb31_ref,
             dlng_ref, dlnb_ref, dga_ref, dgb_ref,
             exta, exth, xc_s, pa_s, px_s, dpa_s, dpx_s, dxce, extb, dc1e, dpf, cp_s, acc_w, acc_c):
        i = pl.program_id(0)

        @pl.when(i == 0)
        def _():
            acc_w[...] = jnp.zeros((8 * NGW, w), F32)
            acc_c[...] = jnp.zeros((8 * NGC, cw), F32)
            dwa_ref[...] = jnp.zeros(dwa_ref.shape, F32)
            dwx_ref[...] = jnp.zeros(dwx_ref.shape, F32)
            cp_s[...] = jnp.zeros((8, w), F32)
            dxce[tm:tm + 8, :] = jnp.zeros((8, w), F32)
            dc1e[tm:tm + 32, :] = jnp.zeros((32, cw), F32)

        nf = jnp.where(i < nt - 1, 1.0, 0.0).astype(F32)
        exta[0:8, :] = projh_ref[40:48, 0:w] * nf
        exta[8:8 + tm, :] = proj_ref[:, 0:w]
        exth[0:8, :] = hsh_ref[...] * nf
        exth[8:8 + tm, :] = hs_ref[...]
        extb[0:48, :] = (projh_ref[:, 2 * w:2 * w + cw] * _sigmoid(projh_ref[:, 2 * w + cw:2 * w + 2 * cw])) * nf
        extb[48:48 + tm, :] = proj_ref[:, 2 * w:2 * w + cw] * _sigmoid(proj_ref[:, 2 * w + cw:2 * w + 2 * cw])
        cb4 = cb4_ref[...]

        def conv_a(c, carry):
            r0 = pl.multiple_of(c * 8, 8)
            xc_s[pl.ds(r0, 8), :] = _lru_conv_chunk(exta, cw4_ref, cb4, r0)
            return carry

        lax.fori_loop(0, nch, conv_a, 0, unroll=3)
        xcb = xc_s[...].astype(BF16)
        _head_gates(xcb, wa_ref, wx_ref, pa_s, px_s, ba_ref[...], bx_ref[...], heads)

        sp = _softplus(-lam_ref[...])
        ga = ga_ref[...]
        row = lax.broadcasted_iota(jnp.int32, (8, w), 0)

        def acc_add(ref, g, val, sl=slice(None)):
            for j in range(val.shape[0] // 8):
                ref[8 * g:8 * g + 8, sl] = ref[8 * g:8 * g + 8, sl] + val[8 * j:8 * j + 8]

        def rscan(cg, cp):
            part = []
            for u in range(RSCAN_U):
                r0 = pl.multiple_of((nch - 1 - (cg * RSCAN_U + u)) * 8, 8)
                xc = xc_s[pl.ds(r0, 8), :]
                r, ig, a, m = _lru_gates(pa_s[pl.ds(r0, 8), :], px_s[pl.ds(r0, 8), :], sp)
                hwin = exth[pl.ds(r0, 16), :]
                hcur = hwin[8:16]
                hprev = pltpu.roll(hwin, 1, 0)[8:16]
                ge, dge = _gelu_and_grad(proj_ref[pl.ds(r0, 8), w:2 * w])
                dna = dy_ref[pl.ds(r0, 8), 0:w]
                dya, yar = _rms_bwd(hcur * ge, ga, dna)
                acc_add(acc_w, G_GA, dna * yar)
                dpf[pl.ds(r0, 8), w:2 * w] = (dya * hcur) * dge
                aa = jnp.where(row == 7, 1.0, pltpu.roll(a, 7, 0))
                bb = dya * ge
                for d in (1, 2, 4):
                    a_s = pltpu.roll(aa, 8 - d, 0)
                    b_s = pltpu.roll(bb, 8 - d, 0)
                    msk = row < 8 - d
                    bb = jnp.where(msk, aa * b_s + bb, bb)
                    aa = jnp.where(msk, aa * a_s, aa)
                part.append((r0, aa, bb, xc, r, ig, a, m, hprev))
            for r0, aa, bb, xc, r, ig, a, m, hprev in part:
                lamb = bb + aa * cp
                cp = a[0:1, :] * lamb[0:1, :]
                dm = lamb * (ig * xc)
                di = lamb * (m * xc)
                dxce[pl.ds(r0, 8), :] = lamb * (m * ig)
                dla = a * (lamb * hprev - dm * (a / m))
                acc_add(acc_w, G_SP, dla * (-LRU_C * r))
                dpa = (dla * (-LRU_C * sp)) * (r * (1.0 - r))
                dpx = di * (ig * (1.0 - ig))
                acc_add(acc_w, G_BA, dpa)
                acc_add(acc_w, G_BX, dpx)
                dpa_s[pl.ds(r0, 8), :] = dpa
                dpx_s[pl.ds(r0, 8), :] = dpx
            return cp

        cp_s[0:1, :] = lax.fori_loop(0, nch // RSCAN_U, rscan, cp_s[0:1, :])

        dpab = dpa_s[...].astype(BF16)
        dpxb = dpx_s[...].astype(BF16)
        for hd in range(heads):
            sl = slice(LANES * hd, LANES * (hd + 1))
            dxce[0:tm, sl] = (dxce[0:tm, sl]
                              + lax.dot_general(dpab[:, sl], wa_ref[hd], NT_DIMS, preferred_element_type=F32)
                              + lax.dot_general(dpxb[:, sl], wx_ref[hd], NT_DIMS, preferred_element_type=F32))
            dwa_ref[hd] = dwa_ref[hd] + lax.dot_general(xcb[:, sl], dpab[:, sl], TN_DIMS, preferred_element_type=F32)
            dwx_ref[hd] = dwx_ref[hd] + lax.dot_general(xcb[:, sl], dpxb[:, sl], TN_DIMS, preferred_element_type=F32)

        def conv_a_bwd(c, carry):
            r0 = pl.multiple_of(c * 8, 8)
            win = dxce[pl.ds(r0, 16), :]
            dxc = win[0:8]
            xwin = exta[pl.ds(r0, 16), :]
            dxl = cw4_ref[3:4, :] * dxc
            acc_add(acc_w, G_CB4, dxc)
            acc_add(acc_w, G_CW4 + 3, dxc * xwin[8:16])
            for k in range(LRU_K - 1):
                s = LRU_K - 1 - k
                dxl = dxl + cw4_ref[k:k + 1, :] * pltpu.roll(win, 16 - s, 0)[0:8]
                acc_add(acc_w, G_CW4 + k, dxc * pltpu.roll(xwin, s, 0)[8:16])
            dpf[pl.ds(r0, 8), 0:w] = dxl
            return carry

        lax.fori_loop(0, nch, conv_a_bwd, 0, unroll=3)
        dxce[tm:tm + 8, :] = dxce[0:8, :]

        def mixb(c, carry):
            r0 = pl.multiple_of(c * 8, 8)
            st = []
            ssq = jnp.zeros((8, 1), F32)
            for lb in range(nlb):
                sl = slice(LANES * lb, LANES * (lb + 1))
                c1v = c1_ref[pl.ds(r0, 8), sl]
                dlt = c1v - jnp.mean(c1v, axis=-1, keepdims=True)
                rs = lax.rsqrt(jnp.mean(dlt * dlt, axis=-1, keepdims=True) + EPS)
                c2 = dlt * rs
                yb0 = c2 * lng_ref[:, sl] + lnb_ref[:, sl]
                sg = _sigmoid(yb0)
                yb = yb0 * sg
                ssq = ssq + jnp.sum(yb * yb, axis=-1, keepdims=True)
                st.append((rs, c2, yb0, sg, yb))
            rb = lax.rsqrt(ssq / cw + EPS)
            tsum = jnp.zeros((8, 1), F32)
            dngs = []
            for lb in range(nlb):
                sl = slice(LANES * lb, LANES * (lb + 1))
                dnb = dy_ref[pl.ds(r0, 8), w + LANES * lb:w + LANES * (lb + 1)]
                ybr = st[lb][4] * rb
                acc_add(acc_c, G_GB, dnb * ybr, sl)
                dng = dnb * gb_ref[:, sl]
                dngs.append((dng, ybr))
                tsum = tsum + jnp.sum(dng * ybr, axis=-1, keepdims=True)
            tsum = tsum / cw
            for lb in range(nlb):
                sl = slice(LANES * lb, LANES * (lb + 1))
                rs, c2, yb0, sg, _ = st[lb]
                dng, ybr = dngs[lb]
                dyb0 = (rb * (dng - ybr * tsum)) * (sg * (1.0 + yb0 * (1.0 - sg)))
                acc_add(acc_c, G_LNG, dyb0 * c2, sl)
                acc_add(acc_c, G_LNB, dyb0, sl)
                dc2 = dyb0 * lng_ref[:, sl]
                dc1 = rs * (dc2 - jnp.mean(dc2, axis=-1, keepdims=True)
                            - c2 * jnp.mean(dc2 * c2, axis=-1, keepdims=True))
                acc_add(acc_c, G_CB31, dc1, sl)
                dc1e[pl.ds(r0, 8), sl] = dc1
            return carry

        lax.fori_loop(0, nch, mixb, 0, unroll=6)

        def conv_b_bwd(c, carry):
            r0 = pl.multiple_of(c * CONV_ROWS, 8)
            nwin = CONV_ROWS + 32
            for lb in range(nlb):
                sl = slice(LANES * lb, LANES * (lb + 1))
                win = dc1e[pl.ds(r0, nwin), sl]
                ups = [win] + [pltpu.roll(win, nwin - rr, 0) for rr in range(1, 8)]
                dc1 = win[0:CONV_ROWS]
                parts = [None] * 4
                for k in range(CONV_K):
                    q, rr = divmod(CONV_K - 1 - k, 8)
                    term = cw31_ref[k:k + 1, sl] * ups[rr][8 * q:8 * q + CONV_ROWS]
                    parts[k % 4] = term if parts[k % 4] is None else parts[k % 4] + term
                dc0 = (parts[0] + parts[1]) + (parts[2] + parts[3])
                cav = proj_ref[pl.ds(r0, CONV_ROWS), 2 * w + LANES * lb:2 * w + LANES * (lb + 1)]
                sg = _sigmoid(proj_ref[pl.ds(r0, CONV_ROWS), 2 * w + cw + LANES * lb:2 * w + cw + LANES * (lb + 1)])
                dpf[pl.ds(r0, CONV_ROWS), 2 * w + LANES * lb:2 * w + LANES * (lb + 1)] = dc0 * sg
                dpf[pl.ds(r0, CONV_ROWS), 2 * w + cw + LANES * lb:2 * w + cw + LANES * (lb + 1)] = (
                    (dc0 * cav) * (sg * (1.0 - sg)))
                xwin = extb[pl.ds(pl.multiple_of(r0 + 16, 8), nwin), sl]
                xr = [xwin] + [pltpu.roll(xwin, rr, 0) for rr in range(1, 8)]
                for k in range(CONV_K):
                    q, rr = divmod(CONV_K - 1 - k, 8)
                    acc_add(acc_c, G_CW31 + k, dc1 * xr[rr][32 - 8 * q:32 - 8 * q + CONV_ROWS], sl)
            return carry

        lax.fori_loop(0, tm // CONV_ROWS, conv_b_bwd, 0, unroll=2)
        dc1e[tm:tm + 32, :] = dc1e[0:32, :]
        dproj_ref[...] = dpf[...].astype(BF16)

        @pl.when(i == nt - 1)
        def _():
            def fold(ref, g):
                return jnp.sum(ref[8 * g:8 * g + 8, :], axis=0, keepdims=True)

            dcb4_ref[...] = fold(acc_w, G_CB4)
            for k in range(LRU_K):
                dcw4_ref[k:k + 1, :] = fold(acc_w, G_CW4 + k)
            dba_ref[...] = fold(acc_w, G_BA)
            dbx_ref[...] = fold(acc_w, G_BX)
            dlam_ref[...] = fold(acc_w, G_SP) * (-_sigmoid(-lam_ref[...]))
            dga_ref[...] = fold(acc_w, G_GA)
            dcb31_ref[...] = fold(acc_c, G_CB31)
            dlng_ref[...] = fold(acc_c, G_LNG)
            dlnb_ref[...] = fold(acc_c, G_LNB)
            dgb_ref[...] = fold(acc_c, G_GB)
            for k in range(CONV_K):
                dcw31_ref[k:k + 1, :] = fold(acc_c, G_CW31 + k)

    consts = [p[k] for k in MIX_PARAMS]
    outs = _pcall(body, name=name, grid=(nt,),
                  in_specs=[_rrow(tm, w + cw, nt), _rrow(tm, 2 * w + 2 * cw, nt), _halo(48, 2 * w + 2 * cw, tm, nt),
                            _rrow(tm, w, nt), _halo(8, w, tm, nt), _rrow(tm, cw, nt)] + [_const(c.shape) for c in consts],
                  out_specs=[_rrow(tm, 2 * w + 2 * cw, nt)] + [_const_out(c.shape) for c in consts],
                  out_shape=[jax.ShapeDtypeStruct((t, 2 * w + 2 * cw), BF16)]
                  + [jax.ShapeDtypeStruct(c.shape, F32) for c in consts],
                  scratch_shapes=[pltpu.VMEM((8 + tm, w), F32), pltpu.VMEM((8 + tm, w), F32), pltpu.VMEM((tm, w), F32),
                                  pltpu.VMEM((tm, w), F32), pltpu.VMEM((tm, w), F32), pltpu.VMEM((tm, w), F32),
                                  pltpu.VMEM((tm, w), F32), pltpu.VMEM((tm + 8, w), F32),
                                  pltpu.VMEM((48 + tm, cw), F32), pltpu.VMEM((tm + 32, cw), F32),
                                  pltpu.VMEM((tm, 2 * w + 2 * cw), F32), pltpu.VMEM((8, w), F32),
                                  pltpu.VMEM((8 * NGW, w), F32), pltpu.VMEM((8 * NGC, cw), F32)],
                  compiler_params=_params())(dy, proj, proj, hs, hs, c1, *consts)
    return outs[0], dict(zip(MIX_PARAMS, outs[1:]))


def _ffn_window(u_ref, halo, c, col):
    if isinstance(c, int) and c == 0:
        return jnp.concatenate([halo[:, col:col + LANES], u_ref[0:16, col:col + LANES]], axis=0)
    return u_ref[pl.ds(pl.multiple_of(c * 16 - 8, 8), 24), col:col + LANES]


def _ffn_conv(win, w3_ref, b3_ref, col):
    sl = slice(col, col + LANES)
    x1 = pltpu.roll(win, 1, 0)[8:24]
    x2 = pltpu.roll(win, 2, 0)[8:24]
    u = w3_ref[2:3, sl] * win[8:24] + w3_ref[1:2, sl] * x1 + w3_ref[0:1, sl] * x2 + b3_ref[:, sl]
    return u, (x2, x1, win[8:24])


def _ffn_act_fwd(u0, w3, b3, name):
    t, f2 = u0.shape
    ff = f2 // 2
    _, tm = _tiles(t)
    nch = tm // 16

    def body(u_ref, w3_ref, b3_ref, act_ref, car):
        @pl.when(pl.program_id(0) == 0)
        def _():
            car[...] = jnp.zeros((8, f2), F32)

        def chunk(c):
            halo = car[...] if isinstance(c, int) else None
            r0 = 0 if isinstance(c, int) else pl.multiple_of(c * 16, 16)
            for j in range(ff // LANES):
                gate, _ = _ffn_conv(_ffn_window(u_ref, halo, c, LANES * j), w3_ref, b3_ref, LANES * j)
                up, _ = _ffn_conv(_ffn_window(u_ref, halo, c, ff + LANES * j), w3_ref, b3_ref, ff + LANES * j)
                act_ref[pl.ds(r0, 16), LANES * j:LANES * (j + 1)] = (_gelu(gate) * up).astype(BF16)

        chunk(0)

        def loop(c, carry):
            chunk(c)
            return carry

        lax.fori_loop(1, nch, loop, 0)
        car[...] = u_ref[tm - 8:tm, :]

    return _pcall(body, name=name, grid=(t // tm,),
                  in_specs=[_row(tm, f2), _const(w3.shape), _const(b3.shape)],
                  out_specs=_row(tm, ff), out_shape=jax.ShapeDtypeStruct((t, ff), BF16),
                  scratch_shapes=[pltpu.VMEM((8, f2), F32)],
                  compiler_params=_params())(u0, w3, b3)


def _ffn_act_bwd(dact, u0, w3, b3, name):
    t, f2 = u0.shape
    ff = f2 // 2
    _, tm = _tiles(t)
    nt = t // tm
    nch = tm // 16

    def body(dact_ref, u_ref, uh_ref, w3_ref, b3_ref, du0_ref, dw3_ref, db3_ref, dub, acc):
        i = pl.program_id(0)

        @pl.when(i == 0)
        def _():
            dub[tm:tm + 8, :] = jnp.zeros((8, f2), F32)
            acc[...] = jnp.zeros((32, f2), F32)

        nf = jnp.where(i < nt - 1, 1.0, 0.0).astype(F32)

        def acc_add(g, val, sl):
            acc[8 * g:8 * g + 8, sl] = acc[8 * g:8 * g + 8, sl] + (val[0:8] + val[8:16])

        def chunk(c):
            halo = uh_ref[...] * nf if isinstance(c, int) else None
            r0 = 0 if isinstance(c, int) else pl.multiple_of(c * 16, 16)
            for j in range(ff // LANES):
                cg, cu = LANES * j, ff + LANES * j
                gate, xg = _ffn_conv(_ffn_window(u_ref, halo, c, cg), w3_ref, b3_ref, cg)
                up, xu = _ffn_conv(_ffn_window(u_ref, halo, c, cu), w3_ref, b3_ref, cu)
                ge, dge = _gelu_and_grad(gate)
                da = dact_ref[pl.ds(r0, 16), cg:cg + LANES]
                for col, du, xs in ((cg, (da * up) * dge, xg), (cu, da * ge, xu)):
                    sl = slice(col, col + LANES)
                    dub[pl.ds(r0, 16), sl] = du
                    acc_add(0, du, sl)
                    for k in range(FFN_K):
                        acc_add(1 + k, du * xs[k], sl)

        chunk(0)

        def loop1(c, carry):
            chunk(c)
            return carry

        lax.fori_loop(1, nch, loop1, 0)

        def loop2(c, carry):
            r0 = pl.multiple_of(c * 16, 16)
            for j in range(f2 // LANES):
                sl = slice(LANES * j, LANES * (j + 1))
                win = dub[pl.ds(r0, 24), sl]
                du0 = (w3_ref[2:3, sl] * win[0:16] + w3_ref[1:2, sl] * pltpu.roll(win, 23, 0)[0:16]
                       + w3_ref[0:1, sl] * pltpu.roll(win, 22, 0)[0:16])
                du0_ref[pl.ds(r0, 16), sl] = du0.astype(BF16)
            return carry

        lax.fori_loop(0, nch, loop2, 0)
        dub[tm:tm + 8, :] = dub[0:8, :]

        @pl.when(i == nt - 1)
        def _():
            db3_ref[...] = jnp.sum(acc[0:8, :], axis=0, keepdims=True)
            for k in range(FFN_K):
                dw3_ref[k:k + 1, :] = jnp.sum(acc[8 + 8 * k:16 + 8 * k, :], axis=0, keepdims=True)

    return _pcall(body, name=name, grid=(nt,),
                  in_specs=[_rrow(tm, ff, nt), _rrow(tm, f2, nt), _halo(8, f2, tm, nt), _const(w3.shape),
                            _const(b3.shape)],
                  out_specs=[_rrow(tm, f2, nt), _const_out(w3.shape), _const_out(b3.shape)],
                  out_shape=[jax.ShapeDtypeStruct((t, f2), BF16), jax.ShapeDtypeStruct(w3.shape, F32),
                             jax.ShapeDtypeStruct(b3.shape, F32)],
                  scratch_shapes=[pltpu.VMEM((tm + 8, f2), F32), pltpu.VMEM((32, f2), F32)],
                  compiler_params=_params())(dact, u0, u0, w3, b3)


def _loss_head(h, tgt, name):
    t, d = h.shape
    tm, _ = _tiles(t)

    def body(h_ref, t_ref, dh_ref, s_ref):
        i = pl.program_id(0)

        @pl.when(i == 0)
        def _():
            s_ref[...] = jnp.zeros((1, d), F32)

        row = lax.broadcasted_iota(jnp.int32, (tm, d), 0) + i * tm
        err = jnp.where(row >= N_META, h_ref[...] - t_ref[...], 0.0)
        dh_ref[...] = err / d
        s_ref[...] += jnp.sum(err * err, axis=0, keepdims=True)

    return _pcall(body, name=name, grid=(t // tm,), in_specs=[_row(tm, d), _row(tm, d)],
                  out_specs=[_row(tm, d), _const_out((1, d))],
                  out_shape=[jax.ShapeDtypeStruct((t, d), F32), jax.ShapeDtypeStruct((1, d), F32)],
                  compiler_params=_params())(h, tgt)


def _row_tile(rows, row_bytes, budget):
    best = None
    for tr in range(16, rows + 1, 16):
        if rows % tr == 0 and tr * row_bytes <= budget:
            best = tr
    assert best is not None, (rows, row_bytes)
    return best


def _cast_bf16(a, name):
    r, c = a.shape
    tr = _row_tile(r, c * 4, 4 << 20)

    def body(a_ref, o_ref):
        o_ref[...] = a_ref[...].astype(BF16)

    return _pcall(body, name=name, grid=(r // tr,), in_specs=[_row(tr, c)], out_specs=_row(tr, c),
                  out_shape=jax.ShapeDtypeStruct((r, c), BF16), compiler_params=_params())(a)


def _sum_slots(r, name):
    s, rows, c = r.shape
    tr = _row_tile(rows, s * c * 4, 8 << 20)

    def body(r_ref, o_ref):
        acc = r_ref[0].astype(F32)
        for k in range(1, s):
            acc = acc + r_ref[k].astype(F32)
        o_ref[...] = acc

    return _pcall(body, name=name, grid=(rows // tr,),
                  in_specs=[pl.BlockSpec((s, tr, c), lambda i: (0, i, 0))], out_specs=_row(tr, c),
                  out_shape=jax.ShapeDtypeStruct((rows, c), F32), compiler_params=_params())(r)


def _cast_into_window(a, axis, k1, l0, l, name):
    _, r, c = a.shape
    shape = (l, 4 * r, c) if axis == 1 else (l, r, 4 * c)

    def body(k_ref, a_ref, o_ref):
        o_ref[...] = a_ref[...].astype(BF16)

    omap = (lambda i, k: (i, k[0], 0)) if axis == 1 else (lambda i, k: (i, 0, k[0]))
    gs = pltpu.PrefetchScalarGridSpec(num_scalar_prefetch=1, grid=(l,),
                                      in_specs=[pl.BlockSpec((1, r, c), lambda i, k: (i + l0, 0, 0))],
                                      out_specs=pl.BlockSpec((1, r, c), omap))
    return _pcall(body, name=name, grid_spec=gs, out_shape=jax.ShapeDtypeStruct(shape, BF16),
                  compiler_params=_params())(k1, a)


def _adamw(g, w, m, v, name):
    r, c = g.shape
    tr = _row_tile(r, c * 4, 1 << 20)

    def body(g_ref, w_ref, m_ref, v_ref, d_ref, m2_ref, v2_ref):
        gv = g_ref[...]
        m2 = ADAM_B1 * m_ref[...] + (1.0 - ADAM_B1) * gv
        v2 = ADAM_B2 * v_ref[...] + (1.0 - ADAM_B2) * (gv * gv)
        m_hat = m2 / (1.0 - ADAM_B1 ** ADAM_STEP)
        v_hat = v2 / (1.0 - ADAM_B2 ** ADAM_STEP)
        d_ref[...] = -ADAM_LR * (m_hat / (jnp.sqrt(v_hat) + ADAM_EPS) + ADAM_WD * w_ref[...])
        m2_ref[...] = m2
        v2_ref[...] = v2

    return _pcall(body, name=name, grid=(r // tr,), in_specs=[_row(tr, c)] * 4, out_specs=[_row(tr, c)] * 3,
                  out_shape=[jax.ShapeDtypeStruct((r, c), F32)] * 3, compiler_params=_params())(g, w, m, v)


ANY = pl.BlockSpec(memory_space=pl.ANY)


def _coords():
    return lax.axis_index("x"), lax.axis_index("y"), lax.axis_index("c")


def _window(ref, lead, axis, k, width):
    idx = [slice(None)] * len(ref.shape)
    idx[0] = lead
    idx[axis] = pl.ds(pl.multiple_of(k * width, LANES if axis == len(ref.shape) - 1 else 8), width)
    return ref.at[tuple(idx)]


def _gather_xy(arrs, axes, n_inplace, name):
    n = len(arrs)
    out_shape, widths = [], []
    for i, (a, ax) in enumerate(zip(arrs, axes)):
        s = list(a.shape)
        if i < n_inplace:
            widths.append(s[ax] // 4)
        else:
            widths.append(s[ax])
            s[ax] *= 4
        out_shape.append(jax.ShapeDtypeStruct(tuple(s), a.dtype))

    def body(*refs):
        ins, outs = refs[:n], refs[n:2 * n]
        send_sems, recv_sems, loc_sems = refs[2 * n:]
        x, y, c = _coords()
        k_me = 2 * x + y
        chips = [(1 - x, y), (x, 1 - y), (1 - x, 1 - y)]
        sib = (x, y, 1 - c)

        def half(i, which):
            hl = arrs[i].shape[0] // 2
            return pl.ds(which * hl, hl)

        def win(i, kk, which):
            return _window(outs[i], half(i, which), axes[i], kk, widths[i])

        def copy(i, s, src, dst, to):
            return pltpu.make_async_remote_copy(src_ref=src, dst_ref=dst, send_sem=send_sems.at[i, s],
                                                recv_sem=recv_sems.at[i, s], device_id=to, device_id_type=MESH_T)

        locs = []
        for i in range(n_inplace, n):
            lc = pltpu.make_async_copy(ins[i], _window(outs[i], slice(None), axes[i], k_me, widths[i]), loc_sems.at[i])
            lc.start()
            locs.append(lc)
        started = []
        for i in range(n):
            for j, chip in enumerate(chips):
                src = win(i, k_me, c) if i < n_inplace else ins[i].at[half(i, c)]
                cp = copy(i, j, src, win(i, k_me, c), (*chip, c))
                cp.start()
                started.append(cp)
        for i in range(n):
            for j, chip in enumerate(chips):
                kk = 2 * chip[0] + chip[1]
                copy(i, j, win(i, kk, c), win(i, kk, c), (*chip, c)).wait_recv()
                fw = copy(i, 3 + j, win(i, kk, c), win(i, kk, c), sib)
                fw.start()
                started.append(fw)
        for i in range(n):
            for j, chip in enumerate(chips):
                kk = 2 * chip[0] + chip[1]
                copy(i, 3 + j, win(i, kk, 1 - c), win(i, kk, 1 - c), sib).wait_recv()
        for cp in started:
            cp.wait_send()
        for lc in locs:
            lc.wait()

    return _pcall(body, name=name, in_specs=[ANY] * n, out_specs=[ANY] * n, out_shape=out_shape,
                  input_output_aliases={i: i for i in range(n_inplace)},
                  scratch_shapes=[pltpu.SemaphoreType.DMA((n, 6)), pltpu.SemaphoreType.DMA((n, 6)),
                                  pltpu.SemaphoreType.DMA((n,))],
                  compiler_params=pltpu.CompilerParams(has_side_effects=True))(*arrs)


def _peer(x, y, c, mask):
    bx, by, bc = (mask >> 2) & 1, (mask >> 1) & 1, mask & 1
    return (1 - x if bx else x, 1 - y if by else y, 1 - c if bc else c)


HBM = pl.BlockSpec(memory_space=pltpu.HBM)
SEM = pl.BlockSpec(memory_space=pltpu.SEMAPHORE)


def _piece_shape(shape, wa):
    r, c = shape
    return (r // 2, c // 4) if wa == 1 else (r // 8, c)


def _piece(ref, wa, k, h):
    r, c = ref.shape
    if wa == 1:
        return ref.at[pl.ds(pl.multiple_of(h * (r // 2), 16), r // 2), pl.ds(pl.multiple_of(k * (c // 4), LANES), c // 4)]
    return ref.at[pl.ds(pl.multiple_of((2 * k + h) * (r // 8), 16), r // 8), :]


def _scatter_start(grads, was, name):
    n = len(grads)
    lands = [lax.empty((7, *_piece_shape(g.shape, wa)), g.dtype) for g, wa in zip(grads, was)]

    def body(*refs):
        g_in, land_in = refs[:n], refs[n:2 * n]
        send_sems, recv_sems = refs[2 * n:2 * n + 7 * n], refs[2 * n + 7 * n:2 * n + 14 * n]
        token = refs[-1]
        x, y, c = _coords()
        for i in range(n):
            for mask in range(1, 8):
                px, py, pc = _peer(x, y, c, mask)
                pltpu.make_async_remote_copy(src_ref=_piece(g_in[i], was[i], 2 * px + py, pc),
                                             dst_ref=land_in[i].at[mask - 1], send_sem=send_sems[7 * i + mask - 1],
                                             recv_sem=recv_sems[7 * i + mask - 1], device_id=(px, py, pc),
                                             device_id_type=MESH_T).start()
        token[...] = jnp.zeros(token.shape, F32)

    args = [pltpu.with_memory_space_constraint(a, pltpu.HBM) for a in list(grads) + lands]
    res = _pcall(body, name=name, in_specs=[HBM] * (2 * n),
                 out_specs=[SEM] * (14 * n) + [HBM] * (2 * n) + [pl.BlockSpec(memory_space=pltpu.VMEM)],
                 out_shape=[pltpu.SemaphoreType.DMA(())] * (14 * n)
                 + [pltpu.HBM(a.shape, a.dtype) for a in args] + [jax.ShapeDtypeStruct((8, LANES), F32)],
                 input_output_aliases={i: 14 * n + i for i in range(2 * n)},
                 compiler_params=pltpu.CompilerParams(has_side_effects=pltpu.SideEffectType.DATAFLOW_SIDE_EFFECTING))(*args)
    return res[:7 * n], res[7 * n:14 * n], res[14 * n:15 * n], res[15 * n:16 * n], res[-1]


def _scatter_wait(send_sems, recv_sems, grads, lands, was, after, name):
    n = len(grads)

    def body(*refs):
        g_in, land_in = refs[:n], refs[n:2 * n]
        s_sems, r_sems = refs[2 * n:2 * n + 7 * n], refs[2 * n + 7 * n:2 * n + 14 * n]
        x, y, c = _coords()
        for i in range(n):
            for mask in range(1, 8):
                px, py, pc = _peer(x, y, c, mask)
                cp = pltpu.make_async_remote_copy(src_ref=_piece(g_in[i], was[i], 2 * px + py, pc),
                                                  dst_ref=land_in[i].at[mask - 1], send_sem=s_sems[7 * i + mask - 1],
                                                  recv_sem=r_sems[7 * i + mask - 1], device_id=(px, py, pc),
                                                  device_id_type=MESH_T)
                cp.wait_send()
                cp.wait_recv()

    args = list(grads) + list(lands)
    res = _pcall(body, name=name, in_specs=[HBM] * (2 * n) + [SEM] * (14 * n) + [ANY], out_specs=[HBM] * (2 * n),
                 out_shape=[pltpu.HBM(a.shape, a.dtype) for a in args],
                 input_output_aliases={i: i for i in range(2 * n)},
                 compiler_params=pltpu.CompilerParams(has_side_effects=pltpu.SideEffectType.DATAFLOW_SIDE_EFFECTING))(
                     *args, *send_sems, *recv_sems, after)
    return res[:n], res[n:]


def _chip_window(ref, wa, k):
    r, c = ref.shape
    if wa == 1:
        return ref.at[:, pl.ds(pl.multiple_of(k * (c // 4), LANES), c // 4)]
    return ref.at[pl.ds(pl.multiple_of(k * (r // 4), 16), r // 4), :]


def _gather_copies(bufs, was, send_sems, recv_sems):
    x, y, c = _coords()
    out = []
    for i, (buf, wa) in enumerate(zip(bufs, was)):
        for j, chip in enumerate([(1 - x, y), (x, 1 - y), (1 - x, 1 - y)]):
            def copy(k, i=i, j=j, chip=chip, buf=buf, wa=wa):
                win = _chip_window(buf, wa, k)
                return pltpu.make_async_remote_copy(src_ref=win, dst_ref=win, send_sem=send_sems[3 * i + j],
                                                    recv_sem=recv_sems[3 * i + j], device_id=(*chip, c),
                                                    device_id_type=MESH_T)
            out.append((copy(2 * x + y), copy(2 * chip[0] + chip[1])))
    return out


def _gather_start(bufs, was, name):
    n = len(bufs)
    ns = 3 * n

    def body(*refs):
        for mine, _ in _gather_copies(refs[:n], was, refs[n:n + ns], refs[n + ns:n + 2 * ns]):
            mine.start()
        refs[-1][...] = jnp.zeros(refs[-1].shape, F32)

    args = [pltpu.with_memory_space_constraint(a, pltpu.HBM) for a in bufs]
    res = _pcall(body, name=name, in_specs=[HBM] * n,
                 out_specs=[SEM] * (2 * ns) + [HBM] * n + [pl.BlockSpec(memory_space=pltpu.VMEM)],
                 out_shape=[pltpu.SemaphoreType.DMA(())] * (2 * ns) + [pltpu.HBM(a.shape, a.dtype) for a in args]
                 + [jax.ShapeDtypeStruct((8, LANES), F32)],
                 input_output_aliases={i: 2 * ns + i for i in range(n)},
                 compiler_params=pltpu.CompilerParams(has_side_effects=pltpu.SideEffectType.DATAFLOW_SIDE_EFFECTING))(*args)
    return res[:ns], res[ns:2 * ns], res[2 * ns:2 * ns + n], res[-1]


def _gather_wait(send_sems, recv_sems, bufs, was, after, name):
    n = len(bufs)
    ns = 3 * n

    def body(*refs):
        for mine, theirs in _gather_copies(refs[:n], was, refs[n:n + ns], refs[n + ns:n + 2 * ns]):
            mine.wait_send()
            theirs.wait_recv()

    return _pcall(body, name=name, in_specs=[HBM] * n + [SEM] * (2 * ns) + [ANY], out_specs=[HBM] * n,
                  out_shape=[pltpu.HBM(a.shape, a.dtype) for a in bufs], input_output_aliases={i: i for i in range(n)},
                  compiler_params=pltpu.CompilerParams(has_side_effects=pltpu.SideEffectType.DATAFLOW_SIDE_EFFECTING))(
                      *bufs, *send_sems, *recv_sems, after)


def _sum_pieces(land, g, wa, k1, c1, name):
    s, rp, cp = land.shape
    tr = _row_tile(rp, (s + 1) * cp * 4, 8 << 20)
    nb = rp // tr
    if wa == 1:
        own = pl.BlockSpec((tr, cp), lambda i, k, c: (c[0] * nb + i, k[0]))
    else:
        own = pl.BlockSpec((tr, cp), lambda i, k, c: ((2 * k[0] + c[0]) * nb + i, 0))

    def body(k_ref, c_ref, l_ref, g_ref, o_ref):
        acc = l_ref[0].astype(F32)
        for j in range(1, s):
            acc = acc + l_ref[j].astype(F32)
        o_ref[...] = acc + g_ref[...].astype(F32)

    gs = pltpu.PrefetchScalarGridSpec(
        num_scalar_prefetch=2, grid=(nb,),
        in_specs=[pl.BlockSpec((s, tr, cp), lambda i, k, c: (0, i, 0)), own],
        out_specs=pl.BlockSpec((tr, cp), lambda i, k, c: (c[0] * nb + i, 0)))
    return _pcall(body, name=name, grid_spec=gs, out_shape=jax.ShapeDtypeStruct((2 * rp, cp), F32),
                  compiler_params=_params())(k1, c1, land, g)


SWAP_CHUNKS = 4


def _sibling_swap(fulls, name):
    n = len(fulls)
    out_shape = [jax.ShapeDtypeStruct(a.shape, a.dtype) for a in fulls]

    def body(*refs):
        outs = refs[n:2 * n]
        send_sems, recv_sems = refs[2 * n:]
        x, y, c = _coords()

        def chunk(i, which, q):
            hl, rc = fulls[i].shape[0] // 2, fulls[i].shape[1] // SWAP_CHUNKS
            ref = outs[i].at[pl.ds(which * hl, hl), pl.ds(q * rc, rc)]
            return pltpu.make_async_remote_copy(src_ref=ref, dst_ref=ref, send_sem=send_sems.at[i, q],
                                                recv_sem=recv_sems.at[i, q], device_id=(x, y, 1 - c),
                                                device_id_type=MESH_T)

        started = []
        for i in range(n):
            for q in range(SWAP_CHUNKS):
                cp = chunk(i, c, q)
                cp.start()
                started.append(cp)
        for cp in started:
            cp.wait_send()
        for i in range(n):
            for q in range(SWAP_CHUNKS):
                chunk(i, 1 - c, q).wait_recv()

    for a in fulls:
        assert a.shape[1] % (8 * SWAP_CHUNKS) == 0, a.shape
    return _pcall(body, name=name, in_specs=[ANY] * n, out_specs=[ANY] * n, out_shape=out_shape,
                  input_output_aliases={i: i for i in range(n)},
                  scratch_shapes=[pltpu.SemaphoreType.DMA((n, SWAP_CHUNKS)), pltpu.SemaphoreType.DMA((n, SWAP_CHUNKS))],
                  compiler_params=pltpu.CompilerParams(has_side_effects=True))(*fulls)


def _scatter8(pk, name):
    r, cdim = pk.shape
    pr = r // 8
    assert pr % 8 == 0

    def body(p_ref, o_ref, send_sems, recv_sems, loc_sem):
        x, y, c = _coords()

        def piece(px, py, pc):
            return p_ref.at[pl.ds(pl.multiple_of((4 * px + 2 * py + pc) * pr, 8), pr)]

        lc = pltpu.make_async_copy(piece(x, y, c), o_ref.at[7], loc_sem)
        lc.start()
        started = []
        for mask in range(1, 8):
            px, py, pc = _peer(x, y, c, mask)
            cp = pltpu.make_async_remote_copy(src_ref=piece(px, py, pc), dst_ref=o_ref.at[mask - 1],
                                              send_sem=send_sems.at[mask - 1], recv_sem=recv_sems.at[mask - 1],
                                              device_id=(px, py, pc), device_id_type=MESH_T)
            cp.start()
            started.append(cp)
        for cp in started:
            cp.wait()
        lc.wait()

    return _pcall(body, name=name, in_specs=[ANY], out_specs=ANY, out_shape=jax.ShapeDtypeStruct((8, pr, cdim), F32),
                  scratch_shapes=[pltpu.SemaphoreType.DMA((7,)), pltpu.SemaphoreType.DMA((7,)),
                                  pltpu.SemaphoreType.DMA],
                  compiler_params=pltpu.CompilerParams(has_side_effects=True))(pk)


def _gather_all(pk, name):
    r, cdim = pk.shape

    def body(p_ref, o_ref, send_sems, recv_sems, loc_sem):
        x, y, c = _coords()
        lc = pltpu.make_async_copy(p_ref, o_ref.at[4 * x + 2 * y + c], loc_sem)
        lc.start()
        started = []
        for mask in range(1, 8):
            px, py, pc = _peer(x, y, c, mask)
            cp = pltpu.make_async_remote_copy(src_ref=p_ref, dst_ref=o_ref.at[4 * x + 2 * y + c],
                                              send_sem=send_sems.at[mask - 1], recv_sem=recv_sems.at[mask - 1],
                                              device_id=(px, py, pc), device_id_type=MESH_T)
            cp.start()
            started.append(cp)
        for cp in started:
            cp.wait()
        lc.wait()

    return _pcall(body, name=name, in_specs=[ANY], out_specs=ANY, out_shape=jax.ShapeDtypeStruct((8, r, cdim), F32),
                  scratch_shapes=[pltpu.SemaphoreType.DMA((7,)), pltpu.SemaphoreType.DMA((7,)),
                                  pltpu.SemaphoreType.DMA],
                  compiler_params=pltpu.CompilerParams(has_side_effects=True))(pk)


PACK_C = 1024


def _pack(arrs, row_mult):
    parts = []
    for a in arrs:
        flat = a.reshape(-1)
        parts.append(jnp.pad(flat, (0, (-flat.shape[0]) % PACK_C)))
    flat = jnp.concatenate(parts)
    flat = jnp.pad(flat, (0, (-flat.shape[0]) % (PACK_C * row_mult)))
    return flat.reshape(-1, PACK_C)


def _unpack(pk, shapes):
    flat = pk.reshape(-1)
    out, off = [], 0
    for s in shapes:
        size = 1
        for dd in s:
            size *= dd
        out.append(flat[off:off + size].reshape(s))
        off += size + (-size) % PACK_C
    return out


def _layer_params(full, l):
    p = {}
    for k in ['g_pre_mix', 'lru_conv_b', 'lru_ba', 'lru_bx', 'lru_lambda', 'conv_b', 'conv_ln_g', 'conv_ln_b', 'g_out_lru',
              'g_out_conv', 'g_post_mix', 'g_pre_ffn', 'ffn_conv_b', 'g_post_ffn']:
        p[k] = full[k][l][None, :]
    for k in ['lru_conv_w', 'conv_w', 'ffn_conv_w'] + [k for k in BIG if k in full]:
        p[k] = full[k][l]
    p['lru_wa'] = full['lru_wa_bf'][l]
    p['lru_wx'] = full['lru_wx_bf'][l]
    return p


def _step(x, loss_target, w, m, v):
    depth = w['w_in'].shape[0]
    d = x.shape[2]
    xk, yk, _ = _coords()
    k_me = 2 * xk + yk

    k1 = jnp.reshape(k_me, (1,)).astype(jnp.int32)
    big_bf = {k: _cast_into_window(w[k], BIG_AXIS[k], k1, 0, SYNC_LAYERS, "cast_" + k) for k in BIG}
    sh_pad = [w['meta_tokens']] + [jnp.pad(w[k], ((0, 0), (0, (-w[k].shape[1]) % 8), (0, 0))) for k in SH_SMALL[1:]]
    gath = _gather_xy([big_bf[k] for k in BIG] + sh_pad, [BIG_AXIS[k] for k in BIG] + [1, 2, 2, 2], len(BIG),
                      "gather_weights")
    full = {k: a for k, a in w.items() if k not in BIG}
    early = dict(zip(BIG, gath[:len(BIG)]))
    was = [BIG_AXIS[k] - 1 for k in BIG]
    in_flight = {}
    for l in range(SYNC_LAYERS, depth):
        bufs = [_cast_into_window(w[k], BIG_AXIS[k], k1, l, 1, "cast_%s_%d" % (k, l))[0] for k in BIG]
        send_sems, recv_sems, bufs, token = _gather_start(bufs, was, "gather_start_%d" % l)
        in_flight[l] = (send_sems, recv_sems, bufs)
        full['g_pre_mix'] = full['g_pre_mix'] + token[0, 0]

    def big_weights(l, h_in):
        if l < SYNC_LAYERS:
            return {k: early[k][l] for k in BIG}
        send_sems, recv_sems, bufs = in_flight[l]
        return dict(zip(BIG, _gather_wait(send_sems, recv_sems, bufs, was, h_in, "gather_wait_%d" % l)))

    full['meta_tokens'] = gath[len(BIG)]
    for k, a in zip(SH_SMALL[1:], gath[len(BIG) + 1:]):
        full[k] = a[:, :w[k].shape[1]]
    for k in ('lru_wa', 'lru_wx'):
        full[k + '_bf'] = _cast_bf16(w[k].reshape(-1, LANES), "cast_" + k).reshape(w[k].shape)

    h = jnp.concatenate([full['meta_tokens'], x[0]], axis=0)
    tgt = jnp.pad(loss_target[0], ((N_META, 0), (0, 0)))
    pending = {}

    def send_big_grads(l, keys, g):
        res = _scatter_start([g[k] for k in keys], [BIG_AXIS[k] - 1 for k in keys], "scatter_start_%d_%s" % (l, keys[-1]))
        pending[(l, keys)] = res[:4]
        return res[4]

    sq, dh, gl = _fwd_bwd(h, tgt, full, depth, send_big_grads, big_weights)
    loss = lax.psum(0.5 * jnp.sum(sq) / d, ("x", "y", "c"))
    return _reduce_update(loss, dh, gl, pending, w, m, v, depth, k_me)


def _fwd_bwd(h, tgt, full, depth, on_grads=None, big_weights=None):
    saved = []
    for l in range(depth):
        p = _layer_params(full, l)
        if big_weights is not None:
            p.update(big_weights(l, h))
        proj, zb1 = _rms_matmul(h, p['g_pre_mix'], p['w_in'], "in_proj")
        y, hs, c1 = _mix_fwd(proj, p, "mix_fwd")
        o, h1 = _matmul_rms_res(y, p['w_out'], h, p['g_post_mix'], "out_proj")
        u0, zb2 = _rms_matmul(h1, p['g_pre_ffn'], p['w_up'], "up_proj")
        act = _ffn_act_fwd(u0, p['ffn_conv_w'], p['ffn_conv_b'], "ffn_act_fwd")
        f, h2 = _matmul_rms_res(act, p['w_down'], h1, p['g_post_ffn'], "down_proj")
        saved.append((p, h, zb1, proj, y, hs, c1, o, h1, zb2, u0, act, f))
        h = h2

    dh, sq = _loss_head(h, tgt, "loss_head")

    def tied(a, token):
        return a if token is None else a + token[0:1, 0:1]

    gl = [None] * depth
    for l in reversed(range(depth)):
        p, h0, zb1, proj, y, hs, c1, o, h1, zb2, u0, act, f = saved[l]
        g = {}
        dact, dfb, g['g_post_ffn'] = _rmsbwd_matmul_nt(f, p['g_post_ffn'], dh, p['w_down'], "down_bwd")
        g['w_down'] = _matmul_tn(act, dfb, "down_dw")
        du0, g['ffn_conv_w'], g['ffn_conv_b'] = _ffn_act_bwd(dact, u0, p['ffn_conv_w'], p['ffn_conv_b'], "ffn_act_bwd")
        g['w_up'] = _matmul_tn(zb2, du0, "up_dw")
        dh1, g['g_pre_ffn'] = _matmul_nt_rmsbwd_res(du0, p['w_up'], h1, p['g_pre_ffn'], dh, "up_bwd")
        dy, dob, g['g_post_mix'] = _rmsbwd_matmul_nt(o, p['g_post_mix'], dh1, p['w_out'], "out_bwd")
        g['w_out'] = _matmul_tn(y, dob, "out_dw")
        token = on_grads(l, ('w_down', 'w_up', 'w_out'), g) if on_grads is not None else None
        pm = dict(p)
        pm['g_out_conv'] = tied(p['g_out_conv'], token)
        dproj, gm = _mix_bwd(dy, proj, hs, c1, pm, "mix_bwd")
        g.update(gm)
        g['w_in'] = _matmul_tn(zb1, dproj, "in_dw")
        token = on_grads(l, ('w_in',), g) if on_grads is not None else None
        dh, g['g_pre_mix'] = _matmul_nt_rmsbwd_res(dproj, p['w_in'], h0, tied(p['g_pre_mix'], token), dh1, "in_bwd")
        gl[l] = g
    return sq, dh, gl


def _reduce_update(loss, dh, gl, pending, w, m, v, depth, k_me):
    grad_x = dh[N_META:][None]

    def stacked(k):
        return jnp.stack([gl[l][k].reshape(w[k].shape[1:]) if k not in SH_SMALL + BIG else gl[l][k]
                          for l in range(depth)])

    k1 = jnp.reshape(k_me, (1,)).astype(jnp.int32)
    c1 = jnp.reshape(lax.axis_index("c"), (1,)).astype(jnp.int32)
    order, halves = [], []
    for (l, keys), (send_sems, recv_sems, g_thru, lands) in pending.items():
        was = [BIG_AXIS[k] - 1 for k in keys]
        g_own, lands = _scatter_wait(send_sems, recv_sems, g_thru, lands, was, dh, "scatter_wait_%d_%s" % (l, keys[-1]))
        for k, wa, land, g_k in zip(keys, was, lands, g_own):
            red = _sum_pieces(land, g_k, wa, k1, c1, "grad_sum")
            order.append((l, k))
            halves.append(red.reshape(2, land.shape[1], land.shape[2]))
    swapped = dict(zip(order, _sibling_swap(halves, "grad_swap")))
    big_red = [jnp.stack([swapped[(l, k)].reshape(w[k].shape[1:]) for l in range(depth)]) for k in BIG]

    out = {}
    for k, gk in zip(BIG, big_red):
        c2 = gk.shape[-1]
        dl, m2, v2 = _adamw(gk.reshape(-1, c2), w[k].reshape(-1, c2), m[k].reshape(-1, c2), v[k].reshape(-1, c2),
                            "adamw_big")
        out[k] = (gk, dl.reshape(gk.shape), m2.reshape(gk.shape), v2.reshape(gk.shape))

    rep_g = [stacked(k) for k in REP_SMALL]
    sh_g = [dh[:N_META]] + [stacked(k) for k in SH_SMALL[1:]]
    n_rep_rows = _pack(rep_g, 1).shape[0]
    pk = jnp.concatenate([_pack(rep_g, 1), _pack(sh_g, 1)])
    pk = jnp.pad(pk, ((0, (-pk.shape[0]) % 256), (0, 0)))
    part = _sum_slots(_scatter8(pk, "small_scatter"), "small_sum")
    red = _gather_all(part, "small_gather").reshape(pk.shape)
    rep_red = _unpack(red[:n_rep_rows], [a.shape for a in rep_g])
    sh_red = []
    for a in _unpack(red[n_rep_rows:], [a.shape for a in sh_g]):
        wd = a.shape[-1] // 4
        sh_red.append(lax.dynamic_slice_in_dim(a, k_me * wd, wd, axis=a.ndim - 1))

    for names, grads_ in ((REP_SMALL, rep_red), (SH_SMALL, sh_red)):
        res = _adamw(_pack(grads_, 16), _pack([w[k] for k in names], 16), _pack([m[k] for k in names], 16),
                     _pack([v[k] for k in names], 16), "adamw_small")
        shapes = [w[k].shape for k in names]
        un = [_unpack(r, shapes) for r in res]
        for j, k in enumerate(names):
            out[k] = (grads_[j].reshape(w[k].shape), un[0][j], un[1][j], un[2][j])

    return (loss, grad_x, *[out[k][0] for k in WEIGHTS], *[out[k][1] for k in WEIGHTS],
            *[out[k][2] for k in WEIGHTS], *[out[k][3] for k in WEIGHTS])


def kernel(x, meta_tokens, g_pre_mix, w_in, lru_conv_w, lru_conv_b, lru_wa, lru_ba, lru_wx, lru_bx, lru_lambda, conv_w, conv_b, conv_ln_g, conv_ln_b, g_out_lru, g_out_conv, w_out, g_post_mix, g_pre_ffn, w_up, ffn_conv_w, ffn_conv_b, w_down, g_post_ffn, loss_target, m_meta_tokens, m_g_pre_mix, m_w_in, m_lru_conv_w, m_lru_conv_b, m_lru_wa, m_lru_ba, m_lru_wx, m_lru_bx, m_lru_lambda, m_conv_w, m_conv_b, m_conv_ln_g, m_conv_ln_b, m_g_out_lru, m_g_out_conv, m_w_out, m_g_post_mix, m_g_pre_ffn, m_w_up, m_ffn_conv_w, m_ffn_conv_b, m_w_down, m_g_post_ffn, v_meta_tokens, v_g_pre_mix, v_w_in, v_lru_conv_w, v_lru_conv_b, v_lru_wa, v_lru_ba, v_lru_wx, v_lru_bx, v_lru_lambda, v_conv_w, v_conv_b, v_conv_ln_g, v_conv_ln_b, v_g_out_lru, v_g_out_conv, v_w_out, v_g_post_mix, v_g_pre_ffn, v_w_up, v_ffn_conv_w, v_ffn_conv_b, v_w_down, v_g_post_ffn):
    w = dict(meta_tokens=meta_tokens, g_pre_mix=g_pre_mix, w_in=w_in, lru_conv_w=lru_conv_w, lru_conv_b=lru_conv_b,
             lru_wa=lru_wa, lru_ba=lru_ba, lru_wx=lru_wx, lru_bx=lru_bx, lru_lambda=lru_lambda, conv_w=conv_w,
             conv_b=conv_b, conv_ln_g=conv_ln_g, conv_ln_b=conv_ln_b, g_out_lru=g_out_lru, g_out_conv=g_out_conv,
             w_out=w_out, g_post_mix=g_post_mix, g_pre_ffn=g_pre_ffn, w_up=w_up, ffn_conv_w=ffn_conv_w,
             ffn_conv_b=ffn_conv_b, w_down=w_down, g_post_ffn=g_post_ffn)
    m = dict(meta_tokens=m_meta_tokens, g_pre_mix=m_g_pre_mix, w_in=m_w_in, lru_conv_w=m_lru_conv_w,
             lru_conv_b=m_lru_conv_b, lru_wa=m_lru_wa, lru_ba=m_lru_ba, lru_wx=m_lru_wx, lru_bx=m_lru_bx,
             lru_lambda=m_lru_lambda, conv_w=m_conv_w, conv_b=m_conv_b, conv_ln_g=m_conv_ln_g, conv_ln_b=m_conv_ln_b,
             g_out_lru=m_g_out_lru, g_out_conv=m_g_out_conv, w_out=m_w_out, g_post_mix=m_g_post_mix,
             g_pre_ffn=m_g_pre_ffn, w_up=m_w_up, ffn_conv_w=m_ffn_conv_w, ffn_conv_b=m_ffn_conv_b, w_down=m_w_down,
             g_post_ffn=m_g_post_ffn)
    v = dict(meta_tokens=v_meta_tokens, g_pre_mix=v_g_pre_mix, w_in=v_w_in, lru_conv_w=v_lru_conv_w,
             lru_conv_b=v_lru_conv_b, lru_wa=v_lru_wa, lru_ba=v_lru_ba, lru_wx=v_lru_wx, lru_bx=v_lru_bx,
             lru_lambda=v_lru_lambda, conv_w=v_conv_w, conv_b=v_conv_b, conv_ln_g=v_conv_ln_g, conv_ln_b=v_conv_ln_b,
             g_out_lru=v_g_out_lru, g_out_conv=v_g_out_conv, w_out=v_w_out, g_post_mix=v_g_post_mix,
             g_pre_ffn=v_g_pre_ffn, w_up=v_w_up, ffn_conv_w=v_ffn_conv_w, ffn_conv_b=v_ffn_conv_b, w_down=v_w_down,
             g_post_ffn=v_g_post_ffn)
    return _step(x, loss_target, w, m, v)
```

```python
import functools

import jax
import jax.numpy as jnp
from jax import lax
from jax.experimental import pallas as pl
from jax.experimental.pallas import tpu as pltpu

F32 = jnp.float32
BF16 = jnp.bfloat16
EPS = 1e-6
N_META = 16
LRU_C = 8.0
CONV_K = 31
LRU_K = 4
FFN_K = 3
CONV_ROWS = 24
TN_ROWS = 912
SYNC_LAYERS = 2
SCAN_U = 3
RSCAN_U = 2
LANES = 128
VMEM_LIMIT = 56 * 1024 * 1024
ADAM_LR, ADAM_B1, ADAM_B2, ADAM_EPS, ADAM_WD, ADAM_STEP = 0.001, 0.9, 0.999, 1e-08, 0.01, 10
MESH_T = pl.DeviceIdType.MESH
NT_DIMS = (((1,), (1,)), ((), ()))
TN_DIMS = (((0,), (0,)), ((), ()))

REP_SMALL = ['g_pre_mix', 'lru_conv_b', 'lru_wa', 'lru_ba', 'lru_wx', 'lru_bx', 'lru_lambda', 'conv_b', 'conv_ln_g',
             'conv_ln_b', 'g_out_lru', 'g_out_conv', 'g_post_mix', 'g_pre_ffn', 'ffn_conv_b', 'g_post_ffn']
SH_SMALL = ['meta_tokens', 'lru_conv_w', 'conv_w', 'ffn_conv_w']
BIG = ['w_in', 'w_out', 'w_up', 'w_down']
BIG_AXIS = {'w_in': 2, 'w_out': 1, 'w_up': 2, 'w_down': 1}
WEIGHTS = ['meta_tokens', 'g_pre_mix', 'w_in', 'lru_conv_w', 'lru_conv_b', 'lru_wa', 'lru_ba', 'lru_wx', 'lru_bx',
           'lru_lambda', 'conv_w', 'conv_b', 'conv_ln_g', 'conv_ln_b', 'g_out_lru', 'g_out_conv', 'w_out',
           'g_post_mix', 'g_pre_ffn', 'w_up', 'ffn_conv_w', 'ffn_conv_b', 'w_down', 'g_post_ffn']


def _pcall(body, **kw):
    return pl.pallas_call(body, **kw)


def _params(n_grid=1):
    return pltpu.CompilerParams(dimension_semantics=("arbitrary",) * n_grid, vmem_limit_bytes=VMEM_LIMIT)


def _tiles(t):
    if t % 432 == 0:
        return 432, 144
    assert t % 48 == 0
    return 48, 48


def _row(tm, n):
    return pl.BlockSpec((tm, n), lambda i: (i, 0))


def _rrow(tm, n, nt):
    return pl.BlockSpec((tm, n), lambda i: (nt - 1 - i, 0))


def _halo(hb, n, tm, nt):
    return pl.BlockSpec((hb, n), lambda i: (jnp.maximum((nt - 1 - i) * (tm // hb) - 1, 0), 0))


def _const(shape):
    nd = len(shape)
    return pl.BlockSpec(shape, lambda *_: (0,) * nd, pipeline_mode=pl.Buffered(1))


def _const_out(shape):
    nd = len(shape)
    return pl.BlockSpec(shape, lambda *_: (0,) * nd)


def _sigmoid(x):
    return 1.0 / (1.0 + jnp.exp(-x))


def _gelu(x):
    return 0.5 * x * (1.0 + jnp.tanh(0.7978845608028654 * (x + 0.044715 * (x * x * x))))


def _gelu_and_grad(x):
    k = 0.7978845608028654
    x2 = x * x
    th = jnp.tanh(k * (x + 0.044715 * (x2 * x)))
    return 0.5 * x * (1.0 + th), 0.5 * (1.0 + th) + 0.5 * x * (1.0 - th * th) * (k * (1.0 + 0.134145 * x2))


def _expm1(x):
    return jnp.where(jnp.abs(x) < 1e-2, x * (1.0 + x * (0.5 + x * (1.0 / 6.0 + x * (1.0 / 24.0)))), jnp.exp(x) - 1.0)


def _softplus(x):
    e = jnp.exp(-jnp.abs(x))
    return jnp.maximum(x, 0.0) + jnp.where(e < 1e-4, e * (1.0 - 0.5 * e), jnp.log(1.0 + e))


def _lru_gates(pa, px, sp):
    r = _sigmoid(pa)
    ig = _sigmoid(px)
    la = (-LRU_C * r) * sp
    return r, ig, jnp.exp(la), jnp.sqrt(-_expm1(2.0 * la))


def _rms(x):
    return lax.rsqrt(jnp.mean(x * x, axis=-1, keepdims=True) + EPS)


def _rms_bwd(x, g, dy):
    r = _rms(x)
    xr = x * r
    dyg = dy * g
    return r * (dyg - xr * jnp.mean(dyg * xr, axis=-1, keepdims=True)), xr


def _col_chunk(n):
    return 1536 if n % 1536 == 0 else 1024


def _rms_matmul(h, g, w, name):
    t, d = h.shape
    n = w.shape[1]
    tm, _ = _tiles(t)
    cn = _col_chunk(n)

    def body(h_ref, g_ref, w_ref, p_ref, zb_ref):
        x = h_ref[...]
        zb = ((x * _rms(x)) * g_ref[...]).astype(BF16)
        zb_ref[...] = zb
        for c in range(n // cn):
            p_ref[:, c * cn:(c + 1) * cn] = jnp.dot(zb, w_ref[:, c * cn:(c + 1) * cn], preferred_element_type=F32)

    return _pcall(body, name=name, grid=(t // tm,),
                  in_specs=[_row(tm, d), _const((1, d)), _const((d, n))],
                  out_specs=[_row(tm, n), _row(tm, d)],
                  out_shape=[jax.ShapeDtypeStruct((t, n), F32), jax.ShapeDtypeStruct((t, d), BF16)],
                  compiler_params=_params())(h, g, w)


def _matmul_rms_res(a, w, h, g, name):
    t, k = a.shape
    d = w.shape[1]
    tm, _ = _tiles(t)

    def body(a_ref, w_ref, h_ref, g_ref, o_ref, hn_ref):
        o = jnp.dot(a_ref[...], w_ref[...], preferred_element_type=F32)
        o_ref[...] = o
        hn_ref[...] = h_ref[...] + (o * _rms(o)) * g_ref[...]

    return _pcall(body, name=name, grid=(t // tm,),
                  in_specs=[_row(tm, k), _const((k, d)), _row(tm, d), _const((1, d))],
                  out_specs=[_row(tm, d), _row(tm, d)],
                  out_shape=[jax.ShapeDtypeStruct((t, d), F32), jax.ShapeDtypeStruct((t, d), F32)],
                  compiler_params=_params())(a, w, h, g)


def _rmsbwd_matmul_nt(x, g, dy, w, name):
    t, d = x.shape
    n = w.shape[0]
    tm, _ = _tiles(t)
    cn = _col_chunk(n)

    def body(x_ref, g_ref, dy_ref, w_ref, da_ref, dxb_ref, dg_ref):
        @pl.when(pl.program_id(0) == 0)
        def _():
            dg_ref[...] = jnp.zeros((1, d), F32)

        dy = dy_ref[...]
        dx, xr = _rms_bwd(x_ref[...], g_ref[...], dy)
        dg_ref[...] += jnp.sum(dy * xr, axis=0, keepdims=True)
        dxb = dx.astype(BF16)
        dxb_ref[...] = dxb
        for c in range(n // cn):
            da_ref[:, c * cn:(c + 1) * cn] = lax.dot_general(dxb, w_ref[c * cn:(c + 1) * cn, :], NT_DIMS,
                                                             preferred_element_type=F32)

    return _pcall(body, name=name, grid=(t // tm,),
                  in_specs=[_row(tm, d), _const((1, d)), _row(tm, d), _const((n, d))],
                  out_specs=[_row(tm, n), _row(tm, d), _const_out((1, d))],
                  out_shape=[jax.ShapeDtypeStruct((t, n), F32), jax.ShapeDtypeStruct((t, d), BF16),
                             jax.ShapeDtypeStruct((1, d), F32)],
                  compiler_params=_params())(x, g, dy, w)


def _matmul_nt_rmsbwd_res(dp, w, h, g, dh, name):
    t, n = dp.shape
    d = w.shape[0]
    tm, _ = _tiles(t)

    def body(dp_ref, w_ref, h_ref, g_ref, dh_ref, out_ref, dg_ref):
        @pl.when(pl.program_id(0) == 0)
        def _():
            dg_ref[...] = jnp.zeros((1, d), F32)

        dz = lax.dot_general(dp_ref[...], w_ref[...], NT_DIMS, preferred_element_type=F32)
        dx, xr = _rms_bwd(h_ref[...], g_ref[...], dz)
        dg_ref[...] += jnp.sum(dz * xr, axis=0, keepdims=True)
        out_ref[...] = dh_ref[...] + dx

    return _pcall(body, name=name, grid=(t // tm,),
                  in_specs=[_row(tm, n), _const((d, n)), _row(tm, d), _const((1, d)), _row(tm, d)],
                  out_specs=[_row(tm, d), _const_out((1, d))],
                  out_shape=[jax.ShapeDtypeStruct((t, d), F32), jax.ShapeDtypeStruct((1, d), F32)],
                  compiler_params=_params())(dp, w, h, g, dh)


def _matmul_tn(a, b, name):
    t, k = a.shape
    n = b.shape[1]
    tm, _ = _tiles(t)
    if t % TN_ROWS == 0:
        tm = TN_ROWS
    nt = t // tm
    bn = min(n, (1536 * 1024) // k)
    assert n % bn == 0 and bn % LANES == 0

    def body(a_ref, b_ref, o_ref, acc):
        @pl.when(pl.program_id(1) == 0)
        def _():
            acc[...] = jnp.zeros((k, bn), F32)

        acc[...] += lax.dot_general(a_ref[...], b_ref[...], TN_DIMS, preferred_element_type=F32)

        @pl.when(pl.program_id(1) == nt - 1)
        def _():
            o_ref[...] = acc[...].astype(BF16)

    return _pcall(body, name=name, grid=(n // bn, nt),
                  in_specs=[pl.BlockSpec((tm, k), lambda j, i: (i, 0)), pl.BlockSpec((tm, bn), lambda j, i: (i, j))],
                  out_specs=pl.BlockSpec((k, bn), lambda j, i: (0, j)),
                  out_shape=jax.ShapeDtypeStruct((k, n), BF16),
                  scratch_shapes=[pltpu.VMEM((k, bn), F32)],
                  compiler_params=_params(2))(a, b)


MIX_PARAMS = ['lru_conv_w', 'lru_conv_b', 'lru_wa', 'lru_ba', 'lru_wx', 'lru_bx', 'lru_lambda', 'conv_w', 'conv_b',
              'conv_ln_g', 'conv_ln_b', 'g_out_lru', 'g_out_conv']


def _head_gates(xcb, wa_ref, wx_ref, pa_s, px_s, ba, bx, heads):
    for hd in range(heads):
        sl = slice(LANES * hd, LANES * (hd + 1))
        pa_s[:, sl] = jnp.dot(xcb[:, sl], wa_ref[hd], preferred_element_type=F32) + ba[:, sl]
        px_s[:, sl] = jnp.dot(xcb[:, sl], wx_ref[hd], preferred_element_type=F32) + bx[:, sl]


def _lru_conv_chunk(exta, cw4_ref, cb4, r0):
    win = exta[pl.ds(r0, 16), :]
    acc = cw4_ref[3:4, :] * win[8:16]
    for k in range(LRU_K - 1):
        acc = acc + cw4_ref[k:k + 1, :] * pltpu.roll(win, LRU_K - 1 - k, 0)[8:16]
    return acc + cb4


def _mix_fwd(proj, p, name):
    t = proj.shape[0]
    w = p['lru_conv_b'].shape[1]
    cw = p['conv_b'].shape[1]
    heads = p['lru_wa'].shape[0]
    _, tm = _tiles(t)
    nch = tm // 8

    def body(proj_ref, cw4_ref, cb4_ref, wa_ref, ba_ref, wx_ref, bx_ref, lam_ref, cw31_ref, cb31_ref, lng_ref,
             lnb_ref, ga_ref, gb_ref, y_ref, hs_ref, c1_ref, exta, xc_s, pa_s, px_s, extb, nbuf, hcar):
        @pl.when(pl.program_id(0) == 0)
        def _():
            exta[0:8, :] = jnp.zeros((8, w), F32)
            extb[0:32, :] = jnp.zeros((32, cw), F32)
            hcar[...] = jnp.zeros((8, w), F32)

        exta[8:8 + tm, :] = proj_ref[:, 0:w]
        cb4 = cb4_ref[...]

        def conv_a(c, carry):
            r0 = pl.multiple_of(c * 8, 8)
            xc_s[pl.ds(r0, 8), :] = _lru_conv_chunk(exta, cw4_ref, cb4, r0)
            return carry

        lax.fori_loop(0, nch, conv_a, 0, unroll=3)
        _head_gates(xc_s[...].astype(BF16), wa_ref, wx_ref, pa_s, px_s, ba_ref[...], bx_ref[...], heads)

        sp = _softplus(-lam_ref[...])
        ga = ga_ref[...]
        row = lax.broadcasted_iota(jnp.int32, (8, w), 0)

        def scan_c(cg, hprev):
            part = []
            for u in range(SCAN_U):
                r0 = pl.multiple_of((cg * SCAN_U + u) * 8, 8)
                xc = xc_s[pl.ds(r0, 8), :]
                _, ig, a, m = _lru_gates(pa_s[pl.ds(r0, 8), :], px_s[pl.ds(r0, 8), :], sp)
                aa, bb = a, m * (ig * xc)
                for d in (1, 2, 4):
                    a_s = pltpu.roll(aa, d, 0)
                    b_s = pltpu.roll(bb, d, 0)
                    msk = row >= d
                    bb = jnp.where(msk, aa * b_s + bb, bb)
                    aa = jnp.where(msk, aa * a_s, aa)
                part.append((r0, aa, bb, _gelu(proj_ref[pl.ds(r0, 8), w:2 * w])))
            for r0, aa, bb, ge in part:
                hs = aa * hprev + bb
                hs_ref[pl.ds(r0, 8), :] = hs
                ya = hs * ge
                nbuf[pl.ds(r0, 8), 0:w] = (ya * _rms(ya)) * ga
                hprev = hs[7:8, :]
            return hprev

        hcar[0:1, :] = lax.fori_loop(0, nch // SCAN_U, scan_c, hcar[0:1, :])

        extb[32:32 + tm, :] = proj_ref[:, 2 * w:2 * w + cw] * _sigmoid(proj_ref[:, 2 * w + cw:2 * w + 2 * cw])

        def conv_b(c, carry):
            r0 = pl.multiple_of(c * CONV_ROWS, 8)
            ybs = []
            ssq = jnp.zeros((CONV_ROWS, 1), F32)
            for lb in range(cw // LANES):
                sl = slice(LANES * lb, LANES * (lb + 1))
                win = extb[pl.ds(r0, CONV_ROWS + 32), sl]
                rolled = [win] + [pltpu.roll(win, rr, 0) for rr in range(1, 8)]
                parts = [None] * 4
                for k in range(CONV_K):
                    q, rr = divmod(CONV_K - 1 - k, 8)
                    term = cw31_ref[k:k + 1, sl] * rolled[rr][32 - 8 * q:32 - 8 * q + CONV_ROWS]
                    parts[k % 4] = term if parts[k % 4] is None else parts[k % 4] + term
                acc = ((parts[0] + parts[1]) + (parts[2] + parts[3])) + cb31_ref[:, sl]
                c1_ref[pl.ds(r0, CONV_ROWS), sl] = acc
                dlt = acc - jnp.mean(acc, axis=-1, keepdims=True)
                c2 = dlt * lax.rsqrt(jnp.mean(dlt * dlt, axis=-1, keepdims=True) + EPS)
                yb0 = c2 * lng_ref[:, sl] + lnb_ref[:, sl]
                yb = yb0 * _sigmoid(yb0)
                ybs.append(yb)
                ssq = ssq + jnp.sum(yb * yb, axis=-1, keepdims=True)
            rb = lax.rsqrt(ssq / cw + EPS)
            for lb in range(cw // LANES):
                sl = slice(LANES * lb, LANES * (lb + 1))
                nbuf[pl.ds(r0, CONV_ROWS), w + LANES * lb:w + LANES * (lb + 1)] = (ybs[lb] * rb) * gb_ref[:, sl]
            return carry

        lax.fori_loop(0, tm // CONV_ROWS, conv_b, 0, unroll=2)
        exta[0:8, :] = exta[tm:tm + 8, :]
        extb[0:32, :] = extb[tm:tm + 32, :]
        y_ref[...] = nbuf[...].astype(BF16)

    consts = [p[k] for k in MIX_PARAMS]
    return _pcall(body, name=name, grid=(t // tm,),
                  in_specs=[_row(tm, 2 * w + 2 * cw)] + [_const(c.shape) for c in consts],
                  out_specs=[_row(tm, w + cw), _row(tm, w), _row(tm, cw)],
                  out_shape=[jax.ShapeDtypeStruct((t, w + cw), BF16), jax.ShapeDtypeStruct((t, w), F32),
                             jax.ShapeDtypeStruct((t, cw), F32)],
                  scratch_shapes=[pltpu.VMEM((8 + tm, w), F32), pltpu.VMEM((tm, w), F32), pltpu.VMEM((tm, w), F32),
                                  pltpu.VMEM((tm, w), F32), pltpu.VMEM((32 + tm, cw), F32),
                                  pltpu.VMEM((tm, w + cw), F32), pltpu.VMEM((8, w), F32)],
                  compiler_params=_params())(proj, *consts)


def _mix_bwd(dy, proj, hs, c1, p, name):
    t = proj.shape[0]
    w = p['lru_conv_b'].shape[1]
    cw = p['conv_b'].shape[1]
    heads = p['lru_wa'].shape[0]
    _, tm = _tiles(t)
    nt = t // tm
    nch = tm // 8
    nlb = cw // LANES
    G_CB4, G_CW4, G_BA, G_BX, G_SP, G_GA, NGW = 0, 1, 5, 6, 7, 8, 9
    G_CB31, G_LNG, G_LNB, G_GB, G_CW31, NGC = 0, 1, 2, 3, 4, 4 + CONV_K

    def body(dy_ref, proj_ref, projh_ref, hs_ref, hsh_ref, c1_ref, cw4_ref, cb4_ref, wa_ref, ba_ref, wx_ref, bx_ref,
             lam_ref, cw31_ref, cb31_ref, lng_ref, lnb_ref, ga_ref, gb_ref,
             dproj_ref, dcw4_ref, dcb4_ref, dwa_ref, dba_ref, dwx_ref, dbx_ref, dlam_ref, dcw31_ref, dcb31_ref,
             dlng_ref, dlnb_ref, dga_ref, dgb_ref,
             exta, exth, xc_s, pa_s, px_s, dpa_s, dpx_s, dxce, extb, dc1e, dpf, cp_s, acc_w, acc_c):
        i = pl.program_id(0)

        @pl.when(i == 0)
        def _():
            acc_w[...] = jnp.zeros((8 * NGW, w), F32)
            acc_c[...] = jnp.zeros((8 * NGC, cw), F32)
            dwa_ref[...] = jnp.zeros(dwa_ref.shape, F32)
            dwx_ref[...] = jnp.zeros(dwx_ref.shape, F32)
            cp_s[...] = jnp.zeros((8, w), F32)
            dxce[tm:tm + 8, :] = jnp.zeros((8, w), F32)
            dc1e[tm:tm + 32, :] = jnp.zeros((32, cw), F32)

        nf = jnp.where(i < nt - 1, 1.0, 0.0).astype(F32)
        exta[0:8, :] = projh_ref[40:48, 0:w] * nf
        exta[8:8 + tm, :] = proj_ref[:, 0:w]
        exth[0:8, :] = hsh_ref[...] * nf
        exth[8:8 + tm, :] = hs_ref[...]
        extb[0:48, :] = (projh_ref[:, 2 * w:2 * w + cw] * _sigmoid(projh_ref[:, 2 * w + cw:2 * w + 2 * cw])) * nf
        extb[48:48 + tm, :] = proj_ref[:, 2 * w:2 * w + cw] * _sigmoid(proj_ref[:, 2 * w + cw:2 * w + 2 * cw])
        cb4 = cb4_ref[...]

        def conv_a(c, carry):
            r0 = pl.multiple_of(c * 8, 8)
            xc_s[pl.ds(r0, 8), :] = _lru_conv_chunk(exta, cw4_ref, cb4, r0)
            return carry

        lax.fori_loop(0, nch, conv_a, 0, unroll=3)
        xcb = xc_s[...].astype(BF16)
        _head_gates(xcb, wa_ref, wx_ref, pa_s, px_s, ba_ref[...], bx_ref[...], heads)

        sp = _softplus(-lam_ref[...])
        ga = ga_ref[...]
        row = lax.broadcasted_iota(jnp.int32, (8, w), 0)

        def acc_add(ref, g, val, sl=slice(None)):
            for j in range(val.shape[0] // 8):
                ref[8 * g:8 * g + 8, sl] = ref[8 * g:8 * g + 8, sl] + val[8 * j:8 * j + 8]

        def rscan(cg, cp):
            part = []
            for u in range(RSCAN_U):
                r0 = pl.multiple_of((nch - 1 - (cg * RSCAN_U + u)) * 8, 8)
                xc = xc_s[pl.ds(r0, 8), :]
                r, ig, a, m = _lru_gates(pa_s[pl.ds(r0, 8), :], px_s[pl.ds(r0, 8), :], sp)
                hwin = exth[pl.ds(r0, 16), :]
                hcur = hwin[8:16]
                hprev = pltpu.roll(hwin, 1, 0)[8:16]
                ge, dge = _gelu_and_grad(proj_ref[pl.ds(r0, 8), w:2 * w])
                dna = dy_ref[pl.ds(r0, 8), 0:w]
                dya, yar = _rms_bwd(hcur * ge, ga, dna)
                acc_add(acc_w, G_GA, dna * yar)
                dpf[pl.ds(r0, 8), w:2 * w] = (dya * hcur) * dge
                aa = jnp.where(row == 7, 1.0, pltpu.roll(a, 7, 0))
                bb = dya * ge
                for d in (1, 2, 4):
                    a_s = pltpu.roll(aa, 8 - d, 0)
                    b_s = pltpu.roll(bb, 8 - d, 0)
                    msk = row < 8 - d
                    bb = jnp.where(msk, aa * b_s + bb, bb)
                    aa = jnp.where(msk, aa * a_s, aa)
                part.append((r0, aa, bb, xc, r, ig, a, m, hprev))
            for r0, aa, bb, xc, r, ig, a, m, hprev in part:
                lamb = bb + aa * cp
                cp = a[0:1, :] * lamb[0:1, :]
                dm = lamb * (ig * xc)
                di = lamb * (m * xc)
                dxce[pl.ds(r0, 8), :] = lamb * (m * ig)
                dla = a * (lamb * hprev - dm * (a / m))
                acc_add(acc_w, G_SP, dla * (-LRU_C * r))
                dpa = (dla * (-LRU_C * sp)) * (r * (1.0 - r))
                dpx = di * (ig * (1.0 - ig))
                acc_add(acc_w, G_BA, dpa)
                acc_add(acc_w, G_BX, dpx)
                dpa_s[pl.ds(r0, 8), :] = dpa
                dpx_s[pl.ds(r0, 8), :] = dpx
            return cp

        cp_s[0:1, :] = lax.fori_loop(0, nch // RSCAN_U, rscan, cp_s[0:1, :])

        dpab = dpa_s[...].astype(BF16)
        dpxb = dpx_s[...].astype(BF16)
        for hd in range(heads):
            sl = slice(LANES * hd, LANES * (hd + 1))
            dxce[0:tm, sl] = (dxce[0:tm, sl]
                              + lax.dot_general(dpab[:, sl], wa_ref[hd], NT_DIMS, preferred_element_type=F32)
                              + lax.dot_general(dpxb[:, sl], wx_ref[hd], NT_DIMS, preferred_element_type=F32))
            dwa_ref[hd] = dwa_ref[hd] + lax.dot_general(xcb[:, sl], dpab[:, sl], TN_DIMS, preferred_element_type=F32)
            dwx_ref[hd] = dwx_ref[hd] + lax.dot_general(xcb[:, sl], dpxb[:, sl], TN_DIMS, preferred_element_type=F32)

        def conv_a_bwd(c, carry):
            r0 = pl.multiple_of(c * 8, 8)
            win = dxce[pl.ds(r0, 16), :]
            dxc = win[0:8]
            xwin = exta[pl.ds(r0, 16), :]
            dxl = cw4_ref[3:4, :] * dxc
            acc_add(acc_w, G_CB4, dxc)
            acc_add(acc_w, G_CW4 + 3, dxc * xwin[8:16])
            for k in range(LRU_K - 1):
                s = LRU_K - 1 - k
                dxl = dxl + cw4_ref[k:k + 1, :] * pltpu.roll(win, 16 - s, 0)[0:8]
                acc_add(acc_w, G_CW4 + k, dxc * pltpu.roll(xwin, s, 0)[8:16])
            dpf[pl.ds(r0, 8), 0:w] = dxl
            return carry

        lax.fori_loop(0, nch, conv_a_bwd, 0, unroll=3)
        dxce[tm:tm + 8, :] = dxce[0:8, :]

        def mixb(c, carry):
            r0 = pl.multiple_of(c * 8, 8)
            st = []
            ssq = jnp.zeros((8, 1), F32)
            for lb in range(nlb):
                sl = slice(LANES * lb, LANES * (lb + 1))
                c1v = c1_ref[pl.ds(r0, 8), sl]
                dlt = c1v - jnp.mean(c1v, axis=-1, keepdims=True)
                rs = lax.rsqrt(jnp.mean(dlt * dlt, axis=-1, keepdims=True) + EPS)
                c2 = dlt * rs
                yb0 = c2 * lng_ref[:, sl] + lnb_ref[:, sl]
                sg = _sigmoid(yb0)
                yb = yb0 * sg
                ssq = ssq + jnp.sum(yb * yb, axis=-1, keepdims=True)
                st.append((rs, c2, yb0, sg, yb))
            rb = lax.rsqrt(ssq / cw + EPS)
            tsum = jnp.zeros((8, 1), F32)
            dngs = []
            for lb in range(nlb):
                sl = slice(LANES * lb, LANES * (lb + 1))
                dnb = dy_ref[pl.ds(r0, 8), w + LANES * lb:w + LANES * (lb + 1)]
                ybr = st[lb][4] * rb
                acc_add(acc_c, G_GB, dnb * ybr, sl)
                dng = dnb * gb_ref[:, sl]
                dngs.append((dng, ybr))
                tsum = tsum + jnp.sum(dng * ybr, axis=-1, keepdims=True)
            tsum = tsum / cw
            for lb in range(nlb):
                sl = slice(LANES * lb, LANES * (lb + 1))
                rs, c2, yb0, sg, _ = st[lb]
                dng, ybr = dngs[lb]
                dyb0 = (rb * (dng - ybr * tsum)) * (sg * (1.0 + yb0 * (1.0 - sg)))
                acc_add(acc_c, G_LNG, dyb0 * c2, sl)
                acc_add(acc_c, G_LNB, dyb0, sl)
                dc2 = dyb0 * lng_ref[:, sl]
                dc1 = rs * (dc2 - jnp.mean(dc2, axis=-1, keepdims=True)
                            - c2 * jnp.mean(dc2 * c2, axis=-1, keepdims=True))
                acc_add(acc_c, G_CB31, dc1, sl)
                dc1e[pl.ds(r0, 8), sl] = dc1
            return carry

        lax.fori_loop(0, nch, mixb, 0, unroll=6)

        def conv_b_bwd(c, carry):
            r0 = pl.multiple_of(c * CONV_ROWS, 8)
            nwin = CONV_ROWS + 32
            for lb in range(nlb):
                sl = slice(LANES * lb, LANES * (lb + 1))
                win = dc1e[pl.ds(r0, nwin), sl]
                ups = [win] + [pltpu.roll(win, nwin - rr, 0) for rr in range(1, 8)]
                dc1 = win[0:CONV_ROWS]
                parts = [None] * 4
                for k in range(CONV_K):
                    q, rr = divmod(CONV_K - 1 - k, 8)
                    term = cw31_ref[k:k + 1, sl] * ups[rr][8 * q:8 * q + CONV_ROWS]
                    parts[k % 4] = term if parts[k % 4] is None else parts[k % 4] + term
                dc0 = (parts[0] + parts[1]) + (parts[2] + parts[3])
                cav = proj_ref[pl.ds(r0, CONV_ROWS), 2 * w + LANES * lb:2 * w + LANES * (lb + 1)]
                sg = _sigmoid(proj_ref[pl.ds(r0, CONV_ROWS), 2 * w + cw + LANES * lb:2 * w + cw + LANES * (lb + 1)])
                dpf[pl.ds(r0, CONV_ROWS), 2 * w + LANES * lb:2 * w + LANES * (lb + 1)] = dc0 * sg
                dpf[pl.ds(r0, CONV_ROWS), 2 * w + cw + LANES * lb:2 * w + cw + LANES * (lb + 1)] = (
                    (dc0 * cav) * (sg * (1.0 - sg)))
                xwin = extb[pl.ds(pl.multiple_of(r0 + 16, 8), nwin), sl]
                xr = [xwin] + [pltpu.roll(xwin, rr, 0) for rr in range(1, 8)]
                for k in range(CONV_K):
                    q, rr = divmod(CONV_K - 1 - k, 8)
                    acc_add(acc_c, G_CW31 + k, dc1 * xr[rr][32 - 8 * q:32 - 8 * q + CONV_ROWS], sl)
            return carry

        lax.fori_loop(0, tm // CONV_ROWS, conv_b_bwd, 0, unroll=2)
        dc1e[tm:tm + 32, :] = dc1e[0:32, :]
        dproj_ref[...] = dpf[...].astype(BF16)

        @pl.when(i == nt - 1)
        def _():
            def fold(ref, g):
                return jnp.sum(ref[8 * g:8 * g + 8, :], axis=0, keepdims=True)

            dcb4_ref[...] = fold(acc_w, G_CB4)
            for k in range(LRU_K):
                dcw4_ref[k:k + 1, :] = fold(acc_w, G_CW4 + k)
            dba_ref[...] = fold(acc_w, G_BA)
            dbx_ref[...] = fold(acc_w, G_BX)
            dlam_ref[...] = fold(acc_w, G_SP) * (-_sigmoid(-lam_ref[...]))
            dga_ref[...] = fold(acc_w, G_GA)
            dcb31_ref[...] = fold(acc_c, G_CB31)
            dlng_ref[...] = fold(acc_c, G_LNG)
            dlnb_ref[...] = fold(acc_c, G_LNB)
            dgb_ref[...] = fold(acc_c, G_GB)
            for k in range(CONV_K):
                dcw31_ref[k:k + 1, :] = fold(acc_c, G_CW31 + k)

    consts = [p[k] for k in MIX_PARAMS]
    outs = _pcall(body, name=name, grid=(nt,),
                  in_specs=[_rrow(tm, w + cw, nt), _rrow(tm, 2 * w + 2 * cw, nt), _halo(48, 2 * w + 2 * cw, tm, nt),
                            _rrow(tm, w, nt), _halo(8, w, tm, nt), _rrow(tm, cw, nt)] + [_const(c.shape) for c in consts],
                  out_specs=[_rrow(tm, 2 * w + 2 * cw, nt)] + [_const_out(c.shape) for c in consts],
                  out_shape=[jax.ShapeDtypeStruct((t, 2 * w + 2 * cw), BF16)]
                  + [jax.ShapeDtypeStruct(c.shape, F32) for c in consts],
                  scratch_shapes=[pltpu.VMEM((8 + tm, w), F32), pltpu.VMEM((8 + tm, w), F32), pltpu.VMEM((tm, w), F32),
                                  pltpu.VMEM((tm, w), F32), pltpu.VMEM((tm, w), F32), pltpu.VMEM((tm, w), F32),
                                  pltpu.VMEM((tm, w), F32), pltpu.VMEM((tm + 8, w), F32),
                                  pltpu.VMEM((48 + tm, cw), F32), pltpu.VMEM((tm + 32, cw), F32),
                                  pltpu.VMEM((tm, 2 * w + 2 * cw), F32), pltpu.VMEM((8, w), F32),
                                  pltpu.VMEM((8 * NGW, w), F32), pltpu.VMEM((8 * NGC, cw), F32)],
                  compiler_params=_params())(dy, proj, proj, hs, hs, c1, *consts)
    return outs[0], dict(zip(MIX_PARAMS, outs[1:]))


def _ffn_window(u_ref, halo, c, col):
    if isinstance(c, int) and c == 0:
        return jnp.concatenate([halo[:, col:col + LANES], u_ref[0:16, col:col + LANES]], axis=0)
    return u_ref[pl.ds(pl.multiple_of(c * 16 - 8, 8), 24), col:col + LANES]


def _ffn_conv(win, w3_ref, b3_ref, col):
    sl = slice(col, col + LANES)
    x1 = pltpu.roll(win, 1, 0)[8:24]
    x2 = pltpu.roll(win, 2, 0)[8:24]
    u = w3_ref[2:3, sl] * win[8:24] + w3_ref[1:2, sl] * x1 + w3_ref[0:1, sl] * x2 + b3_ref[:, sl]
    return u, (x2, x1, win[8:24])


def _ffn_act_fwd(u0, w3, b3, name):
    t, f2 = u0.shape
    ff = f2 // 2
    _, tm = _tiles(t)
    nch = tm // 16

    def body(u_ref, w3_ref, b3_ref, act_ref, car):
        @pl.when(pl.program_id(0) == 0)
        def _():
            car[...] = jnp.zeros((8, f2), F32)

        def chunk(c):
            halo = car[...] if isinstance(c, int) else None
            r0 = 0 if isinstance(c, int) else pl.multiple_of(c * 16, 16)
            for j in range(ff // LANES):
                gate, _ = _ffn_conv(_ffn_window(u_ref, halo, c, LANES * j), w3_ref, b3_ref, LANES * j)
                up, _ = _ffn_conv(_ffn_window(u_ref, halo, c, ff + LANES * j), w3_ref, b3_ref, ff + LANES * j)
                act_ref[pl.ds(r0, 16), LANES * j:LANES * (j + 1)] = (_gelu(gate) * up).astype(BF16)

        chunk(0)

        def loop(c, carry):
            chunk(c)
            return carry

        lax.fori_loop(1, nch, loop, 0)
        car[...] = u_ref[tm - 8:tm, :]

    return _pcall(body, name=name, grid=(t // tm,),
                  in_specs=[_row(tm, f2), _const(w3.shape), _const(b3.shape)],
                  out_specs=_row(tm, ff), out_shape=jax.ShapeDtypeStruct((t, ff), BF16),
                  scratch_shapes=[pltpu.VMEM((8, f2), F32)],
                  compiler_params=_params())(u0, w3, b3)


def _ffn_act_bwd(dact, u0, w3, b3, name):
    t, f2 = u0.shape
    ff = f2 // 2
    _, tm = _tiles(t)
    nt = t // tm
    nch = tm // 16

    def body(dact_ref, u_ref, uh_ref, w3_ref, b3_ref, du0_ref, dw3_ref, db3_ref, dub, acc):
        i = pl.program_id(0)

        @pl.when(i == 0)
        def _():
            dub[tm:tm + 8, :] = jnp.zeros((8, f2), F32)
            acc[...] = jnp.zeros((32, f2), F32)

        nf = jnp.where(i < nt - 1, 1.0, 0.0).astype(F32)

        def acc_add(g, val, sl):
            acc[8 * g:8 * g + 8, sl] = acc[8 * g:8 * g + 8, sl] + (val[0:8] + val[8:16])

        def chunk(c):
            halo = uh_ref[...] * nf if isinstance(c, int) else None
            r0 = 0 if isinstance(c, int) else pl.multiple_of(c * 16, 16)
            for j in range(ff // LANES):
                cg, cu = LANES * j, ff + LANES * j
                gate, xg = _ffn_conv(_ffn_window(u_ref, halo, c, cg), w3_ref, b3_ref, cg)
                up, xu = _ffn_conv(_ffn_window(u_ref, halo, c, cu), w3_ref, b3_ref, cu)
                ge, dge = _gelu_and_grad(gate)
                da = dact_ref[pl.ds(r0, 16), cg:cg + LANES]
                for col, du, xs in ((cg, (da * up) * dge, xg), (cu, da * ge, xu)):
                    sl = slice(col, col + LANES)
                    dub[pl.ds(r0, 16), sl] = du
                    acc_add(0, du, sl)
                    for k in range(FFN_K):
                        acc_add(1 + k, du * xs[k], sl)

        chunk(0)

        def loop1(c, carry):
            chunk(c)
            return carry

        lax.fori_loop(1, nch, loop1, 0)

        def loop2(c, carry):
            r0 = pl.multiple_of(c * 16, 16)
            for j in range(f2 // LANES):
                sl = slice(LANES * j, LANES * (j + 1))
                win = dub[pl.ds(r0, 24), sl]
                du0 = (w3_ref[2:3, sl] * win[0:16] + w3_ref[1:2, sl] * pltpu.roll(win, 23, 0)[0:16]
                       + w3_ref[0:1, sl] * pltpu.roll(win, 22, 0)[0:16])
                du0_ref[pl.ds(r0, 16), sl] = du0.astype(BF16)
            return carry

        lax.fori_loop(0, nch, loop2, 0)
        dub[tm:tm + 8, :] = dub[0:8, :]

        @pl.when(i == nt - 1)
        def _():
            db3_ref[...] = jnp.sum(acc[0:8, :], axis=0, keepdims=True)
            for k in range(FFN_K):
                dw3_ref[k:k + 1, :] = jnp.sum(acc[8 + 8 * k:16 + 8 * k, :], axis=0, keepdims=True)

    return _pcall(body, name=name, grid=(nt,),
                  in_specs=[_rrow(tm, ff, nt), _rrow(tm, f2, nt), _halo(8, f2, tm, nt), _const(w3.shape),
                            _const(b3.shape)],
                  out_specs=[_rrow(tm, f2, nt), _const_out(w3.shape), _const_out(b3.shape)],
                  out_shape=[jax.ShapeDtypeStruct((t, f2), BF16), jax.ShapeDtypeStruct(w3.shape, F32),
                             jax.ShapeDtypeStruct(b3.shape, F32)],
                  scratch_shapes=[pltpu.VMEM((tm + 8, f2), F32), pltpu.VMEM((32, f2), F32)],
                  compiler_params=_params())(dact, u0, u0, w3, b3)


def _loss_head(h, tgt, name):
    t, d = h.shape
    tm, _ = _tiles(t)

    def body(h_ref, t_ref, dh_ref, s_ref):
        i = pl.program_id(0)

        @pl.when(i == 0)
        def _():
            s_ref[...] = jnp.zeros((1, d), F32)

        row = lax.broadcasted_iota(jnp.int32, (tm, d), 0) + i * tm
        err = jnp.where(row >= N_META, h_ref[...] - t_ref[...], 0.0)
        dh_ref[...] = err / d
        s_ref[...] += jnp.sum(err * err, axis=0, keepdims=True)

    return _pcall(body, name=name, grid=(t // tm,), in_specs=[_row(tm, d), _row(tm, d)],
                  out_specs=[_row(tm, d), _const_out((1, d))],
                  out_shape=[jax.ShapeDtypeStruct((t, d), F32), jax.ShapeDtypeStruct((1, d), F32)],
                  compiler_params=_params())(h, tgt)


def _row_tile(rows, row_bytes, budget):
    best = None
    for tr in range(16, rows + 1, 16):
        if rows % tr == 0 and tr * row_bytes <= budget:
            best = tr
    assert best is not None, (rows, row_bytes)
    return best


def _cast_bf16(a, name):
    r, c = a.shape
    tr = _row_tile(r, c * 4, 4 << 20)

    def body(a_ref, o_ref):
        o_ref[...] = a_ref[...].astype(BF16)

    return _pcall(body, name=name, grid=(r // tr,), in_specs=[_row(tr, c)], out_specs=_row(tr, c),
                  out_shape=jax.ShapeDtypeStruct((r, c), BF16), compiler_params=_params())(a)


def _sum_slots(r, name):
    s, rows, c = r.shape
    tr = _row_tile(rows, s * c * 4, 8 << 20)

    def body(r_ref, o_ref):
        acc = r_ref[0].astype(F32)
        for k in range(1, s):
            acc = acc + r_ref[k].astype(F32)
        o_ref[...] = acc

    return _pcall(body, name=name, grid=(rows // tr,),
                  in_specs=[pl.BlockSpec((s, tr, c), lambda i: (0, i, 0))], out_specs=_row(tr, c),
                  out_shape=jax.ShapeDtypeStruct((rows, c), F32), compiler_params=_params())(r)


def _cast_into_window(a, axis, k1, l0, l, name):
    _, r, c = a.shape
    shape = (l, 4 * r, c) if axis == 1 else (l, r, 4 * c)

    def body(k_ref, a_ref, o_ref):
        o_ref[...] = a_ref[...].astype(BF16)

    omap = (lambda i, k: (i, k[0], 0)) if axis == 1 else (lambda i, k: (i, 0, k[0]))
    gs = pltpu.PrefetchScalarGridSpec(num_scalar_prefetch=1, grid=(l,),
                                      in_specs=[pl.BlockSpec((1, r, c), lambda i, k: (i + l0, 0, 0))],
                                      out_specs=pl.BlockSpec((1, r, c), omap))
    return _pcall(body, name=name, grid_spec=gs, out_shape=jax.ShapeDtypeStruct(shape, BF16),
                  compiler_params=_params())(k1, a)


def _adamw(g, w, m, v, name):
    r, c = g.shape
    tr = _row_tile(r, c * 4, 1 << 20)

    def body(g_ref, w_ref, m_ref, v_ref, d_ref, m2_ref, v2_ref):
        gv = g_ref[...]
        m2 = ADAM_B1 * m_ref[...] + (1.0 - ADAM_B1) * gv
        v2 = ADAM_B2 * v_ref[...] + (1.0 - ADAM_B2) * (gv * gv)
        m_hat = m2 / (1.0 - ADAM_B1 ** ADAM_STEP)
        v_hat = v2 / (1.0 - ADAM_B2 ** ADAM_STEP)
        d_ref[...] = -ADAM_LR * (m_hat / (jnp.sqrt(v_hat) + ADAM_EPS) + ADAM_WD * w_ref[...])
        m2_ref[...] = m2
        v2_ref[...] = v2

    return _pcall(body, name=name, grid=(r // tr,), in_specs=[_row(tr, c)] * 4, out_specs=[_row(tr, c)] * 3,
                  out_shape=[jax.ShapeDtypeStruct((r, c), F32)] * 3, compiler_params=_params())(g, w, m, v)


ANY = pl.BlockSpec(memory_space=pl.ANY)


def _coords():
    return lax.axis_index("x"), lax.axis_index("y"), lax.axis_index("c")


def _window(ref, lead, axis, k, width):
    idx = [slice(None)] * len(ref.shape)
    idx[0] = lead
    idx[axis] = pl.ds(pl.multiple_of(k * width, LANES if axis == len(ref.shape) - 1 else 8), width)
    return ref.at[tuple(idx)]


def _gather_xy(arrs, axes, n_inplace, name):
    n = len(arrs)
    out_shape, widths = [], []
    for i, (a, ax) in enumerate(zip(arrs, axes)):
        s = list(a.shape)
        if i < n_inplace:
            widths.append(s[ax] // 4)
        else:
            widths.append(s[ax])
            s[ax] *= 4
        out_shape.append(jax.ShapeDtypeStruct(tuple(s), a.dtype))

    def body(*refs):
        ins, outs = refs[:n], refs[n:2 * n]
        send_sems, recv_sems, loc_sems = refs[2 * n:]
        x, y, c = _coords()
        k_me = 2 * x + y
        chips = [(1 - x, y), (x, 1 - y), (1 - x, 1 - y)]
        sib = (x, y, 1 - c)

        def half(i, which):
            hl = arrs[i].shape[0] // 2
            return pl.ds(which * hl, hl)

        def win(i, kk, which):
            return _window(outs[i], half(i, which), axes[i], kk, widths[i])

        def copy(i, s, src, dst, to):
            return pltpu.make_async_remote_copy(src_ref=src, dst_ref=dst, send_sem=send_sems.at[i, s],
                                                recv_sem=recv_sems.at[i, s], device_id=to, device_id_type=MESH_T)

        locs = []
        for i in range(n_inplace, n):
            lc = pltpu.make_async_copy(ins[i], _window(outs[i], slice(None), axes[i], k_me, widths[i]), loc_sems.at[i])
            lc.start()
            locs.append(lc)
        started = []
        for i in range(n):
            for j, chip in enumerate(chips):
                src = win(i, k_me, c) if i < n_inplace else ins[i].at[half(i, c)]
                cp = copy(i, j, src, win(i, k_me, c), (*chip, c))
                cp.start()
                started.append(cp)
        for i in range(n):
            for j, chip in enumerate(chips):
                kk = 2 * chip[0] + chip[1]
                copy(i, j, win(i, kk, c), win(i, kk, c), (*chip, c)).wait_recv()
                fw = copy(i, 3 + j, win(i, kk, c), win(i, kk, c), sib)
                fw.start()
                started.append(fw)
        for i in range(n):
            for j, chip in enumerate(chips):
                kk = 2 * chip[0] + chip[1]
                copy(i, 3 + j, win(i, kk, 1 - c), win(i, kk, 1 - c), sib).wait_recv()
        for cp in started:
            cp.wait_send()
        for lc in locs:
            lc.wait()

    return _pcall(body, name=name, in_specs=[ANY] * n, out_specs=[ANY] * n, out_shape=out_shape,
                  input_output_aliases={i: i for i in range(n_inplace)},
                  scratch_shapes=[pltpu.SemaphoreType.DMA((n, 6)), pltpu.SemaphoreType.DMA((n, 6)),
                                  pltpu.SemaphoreType.DMA((n,))],
                  compiler_params=pltpu.CompilerParams(has_side_effects=True))(*arrs)


def _peer(x, y, c, mask):
    bx, by, bc = (mask >> 2) & 1, (mask >> 1) & 1, mask & 1
    return (1 - x if bx else x, 1 - y if by else y, 1 - c if bc else c)


HBM = pl.BlockSpec(memory_space=pltpu.HBM)
SEM = pl.BlockSpec(memory_space=pltpu.SEMAPHORE)


def _piece_shape(shape, wa):
    r, c = shape
    return (r // 2, c // 4) if wa == 1 else (r // 8, c)


def _piece(ref, wa, k, h):
    r, c = ref.shape
    if wa == 1:
        return ref.at[pl.ds(pl.multiple_of(h * (r // 2), 16), r // 2), pl.ds(pl.multiple_of(k * (c // 4), LANES), c // 4)]
    return ref.at[pl.ds(pl.multiple_of((2 * k + h) * (r // 8), 16), r // 8), :]


def _scatter_start(grads, was, name):
    n = len(grads)
    lands = [lax.empty((7, *_piece_shape(g.shape, wa)), g.dtype) for g, wa in zip(grads, was)]

    def body(*refs):
        g_in, land_in = refs[:n], refs[n:2 * n]
        send_sems, recv_sems = refs[2 * n:2 * n + 7 * n], refs[2 * n + 7 * n:2 * n + 14 * n]
        token = refs[-1]
        x, y, c = _coords()
        for i in range(n):
            for mask in range(1, 8):
                px, py, pc = _peer(x, y, c, mask)
                pltpu.make_async_remote_copy(src_ref=_piece(g_in[i], was[i], 2 * px + py, pc),
                                             dst_ref=land_in[i].at[mask - 1], send_sem=send_sems[7 * i + mask - 1],
                                             recv_sem=recv_sems[7 * i + mask - 1], device_id=(px, py, pc),
                                             device_id_type=MESH_T).start()
        token[...] = jnp.zeros(token.shape, F32)

    args = [pltpu.with_memory_space_constraint(a, pltpu.HBM) for a in list(grads) + lands]
    res = _pcall(body, name=name, in_specs=[HBM] * (2 * n),
                 out_specs=[SEM] * (14 * n) + [HBM] * (2 * n) + [pl.BlockSpec(memory_space=pltpu.VMEM)],
                 out_shape=[pltpu.SemaphoreType.DMA(())] * (14 * n)
                 + [pltpu.HBM(a.shape, a.dtype) for a in args] + [jax.ShapeDtypeStruct((8, LANES), F32)],
                 input_output_aliases={i: 14 * n + i for i in range(2 * n)},
                 compiler_params=pltpu.CompilerParams(has_side_effects=pltpu.SideEffectType.DATAFLOW_SIDE_EFFECTING))(*args)
    return res[:7 * n], res[7 * n:14 * n], res[14 * n:15 * n], res[15 * n:16 * n], res[-1]


def _scatter_wait(send_sems, recv_sems, grads, lands, was, after, name):
    n = len(grads)

    def body(*refs):
        g_in, land_in = refs[:n], refs[n:2 * n]
        s_sems, r_sems = refs[2 * n:2 * n + 7 * n], refs[2 * n + 7 * n:2 * n + 14 * n]
        x, y, c = _coords()
        for i in range(n):
            for mask in range(1, 8):
                px, py, pc = _peer(x, y, c, mask)
                cp = pltpu.make_async_remote_copy(src_ref=_piece(g_in[i], was[i], 2 * px + py, pc),
                                                  dst_ref=land_in[i].at[mask - 1], send_sem=s_sems[7 * i + mask - 1],
                                                  recv_sem=r_sems[7 * i + mask - 1], device_id=(px, py, pc),
                                                  device_id_type=MESH_T)
                cp.wait_send()
                cp.wait_recv()

    args = list(grads) + list(lands)
    res = _pcall(body, name=name, in_specs=[HBM] * (2 * n) + [SEM] * (14 * n) + [ANY], out_specs=[HBM] * (2 * n),
                 out_shape=[pltpu.HBM(a.shape, a.dtype) for a in args],
                 input_output_aliases={i: i for i in range(2 * n)},
                 compiler_params=pltpu.CompilerParams(has_side_effects=pltpu.SideEffectType.DATAFLOW_SIDE_EFFECTING))(
                     *args, *send_sems, *recv_sems, after)
    return res[:n], res[n:]


def _chip_window(ref, wa, k):
    r, c = ref.shape
    if wa == 1:
        return ref.at[:, pl.ds(pl.multiple_of(k * (c // 4), LANES), c // 4)]
    return ref.at[pl.ds(pl.multiple_of(k * (r // 4), 16), r // 4), :]


def _gather_copies(bufs, was, send_sems, recv_sems):
    x, y, c = _coords()
    out = []
    for i, (buf, wa) in enumerate(zip(bufs, was)):
        for j, chip in enumerate([(1 - x, y), (x, 1 - y), (1 - x, 1 - y)]):
            def copy(k, i=i, j=j, chip=chip, buf=buf, wa=wa):
                win = _chip_window(buf, wa, k)
                return pltpu.make_async_remote_copy(src_ref=win, dst_ref=win, send_sem=send_sems[3 * i + j],
                                                    recv_sem=recv_sems[3 * i + j], device_id=(*chip, c),
                                                    device_id_type=MESH_T)
            out.append((copy(2 * x + y), copy(2 * chip[0] + chip[1])))
    return out


def _gather_start(bufs, was, name):
    n = len(bufs)
    ns = 3 * n

    def body(*refs):
        for mine, _ in _gather_copies(refs[:n], was, refs[n:n + ns], refs[n + ns:n + 2 * ns]):
            mine.start()
        refs[-1][...] = jnp.zeros(refs[-1].shape, F32)

    args = [pltpu.with_memory_space_constraint(a, pltpu.HBM) for a in bufs]
    res = _pcall(body, name=name, in_specs=[HBM] * n,
                 out_specs=[SEM] * (2 * ns) + [HBM] * n + [pl.BlockSpec(memory_space=pltpu.VMEM)],
                 out_shape=[pltpu.SemaphoreType.DMA(())] * (2 * ns) + [pltpu.HBM(a.shape, a.dtype) for a in args]
                 + [jax.ShapeDtypeStruct((8, LANES), F32)],
                 input_output_aliases={i: 2 * ns + i for i in range(n)},
                 compiler_params=pltpu.CompilerParams(has_side_effects=pltpu.SideEffectType.DATAFLOW_SIDE_EFFECTING))(*args)
    return res[:ns], res[ns:2 * ns], res[2 * ns:2 * ns + n], res[-1]


def _gather_wait(send_sems, recv_sems, bufs, was, after, name):
    n = len(bufs)
    ns = 3 * n

    def body(*refs):
        for mine, theirs in _gather_copies(refs[:n], was, refs[n:n + ns], refs[n + ns:n + 2 * ns]):
            mine.wait_send()
            theirs.wait_recv()

    return _pcall(body, name=name, in_specs=[HBM] * n + [SEM] * (2 * ns) + [ANY], out_specs=[HBM] * n,
                  out_shape=[pltpu.HBM(a.shape, a.dtype) for a in bufs], input_output_aliases={i: i for i in range(n)},
                  compiler_params=pltpu.CompilerParams(has_side_effects=pltpu.SideEffectType.DATAFLOW_SIDE_EFFECTING))(
                      *bufs, *send_sems, *recv_sems, after)


def _sum_pieces(land, g, wa, k1, c1, name):
    s, rp, cp = land.shape
    tr = _row_tile(rp, (s + 1) * cp * 4, 8 << 20)
    nb = rp // tr
    if wa == 1:
        own = pl.BlockSpec((tr, cp), lambda i, k, c: (c[0] * nb + i, k[0]))
    else:
        own = pl.BlockSpec((tr, cp), lambda i, k, c: ((2 * k[0] + c[0]) * nb + i, 0))

    def body(k_ref, c_ref, l_ref, g_ref, o_ref):
        acc = l_ref[0].astype(F32)
        for j in range(1, s):
            acc = acc + l_ref[j].astype(F32)
        o_ref[...] = acc + g_ref[...].astype(F32)

    gs = pltpu.PrefetchScalarGridSpec(
        num_scalar_prefetch=2, grid=(nb,),
        in_specs=[pl.BlockSpec((s, tr, cp), lambda i, k, c: (0, i, 0)), own],
        out_specs=pl.BlockSpec((tr, cp), lambda i, k, c: (c[0] * nb + i, 0)))
    return _pcall(body, name=name, grid_spec=gs, out_shape=jax.ShapeDtypeStruct((2 * rp, cp), F32),
                  compiler_params=_params())(k1, c1, land, g)


SWAP_CHUNKS = 4


def _sibling_swap(fulls, name):
    n = len(fulls)
    out_shape = [jax.ShapeDtypeStruct(a.shape, a.dtype) for a in fulls]

    def body(*refs):
        outs = refs[n:2 * n]
        send_sems, recv_sems = refs[2 * n:]
        x, y, c = _coords()

        def chunk(i, which, q):
            hl, rc = fulls[i].shape[0] // 2, fulls[i].shape[1] // SWAP_CHUNKS
            ref = outs[i].at[pl.ds(which * hl, hl), pl.ds(q * rc, rc)]
            return pltpu.make_async_remote_copy(src_ref=ref, dst_ref=ref, send_sem=send_sems.at[i, q],
                                                recv_sem=recv_sems.at[i, q], device_id=(x, y, 1 - c),
                                                device_id_type=MESH_T)

        started = []
        for i in range(n):
            for q in range(SWAP_CHUNKS):
                cp = chunk(i, c, q)
                cp.start()
                started.append(cp)
        for cp in started:
            cp.wait_send()
        for i in range(n):
            for q in range(SWAP_CHUNKS):
                chunk(i, 1 - c, q).wait_recv()

    for a in fulls:
        assert a.shape[1] % (8 * SWAP_CHUNKS) == 0, a.shape
    return _pcall(body, name=name, in_specs=[ANY] * n, out_specs=[ANY] * n, out_shape=out_shape,
                  input_output_aliases={i: i for i in range(n)},
                  scratch_shapes=[pltpu.SemaphoreType.DMA((n, SWAP_CHUNKS)), pltpu.SemaphoreType.DMA((n, SWAP_CHUNKS))],
                  compiler_params=pltpu.CompilerParams(has_side_effects=True))(*fulls)


def _scatter8(pk, name):
    r, cdim = pk.shape
    pr = r // 8
    assert pr % 8 == 0

    def body(p_ref, o_ref, send_sems, recv_sems, loc_sem):
        x, y, c = _coords()

        def piece(px, py, pc):
            return p_ref.at[pl.ds(pl.multiple_of((4 * px + 2 * py + pc) * pr, 8), pr)]

        lc = pltpu.make_async_copy(piece(x, y, c), o_ref.at[7], loc_sem)
        lc.start()
        started = []
        for mask in range(1, 8):
            px, py, pc = _peer(x, y, c, mask)
            cp = pltpu.make_async_remote_copy(src_ref=piece(px, py, pc), dst_ref=o_ref.at[mask - 1],
                                              send_sem=send_sems.at[mask - 1], recv_sem=recv_sems.at[mask - 1],
                                              device_id=(px, py, pc), device_id_type=MESH_T)
            cp.start()
            started.append(cp)
        for cp in started:
            cp.wait()
        lc.wait()

    return _pcall(body, name=name, in_specs=[ANY], out_specs=ANY, out_shape=jax.ShapeDtypeStruct((8, pr, cdim), F32),
                  scratch_shapes=[pltpu.SemaphoreType.DMA((7,)), pltpu.SemaphoreType.DMA((7,)),
                                  pltpu.SemaphoreType.DMA],
                  compiler_params=pltpu.CompilerParams(has_side_effects=True))(pk)


def _gather_all(pk, name):
    r, cdim = pk.shape

    def body(p_ref, o_ref, send_sems, recv_sems, loc_sem):
        x, y, c = _coords()
        lc = pltpu.make_async_copy(p_ref, o_ref.at[4 * x + 2 * y + c], loc_sem)
        lc.start()
        started = []
        for mask in range(1, 8):
            px, py, pc = _peer(x, y, c, mask)
            cp = pltpu.make_async_remote_copy(src_ref=p_ref, dst_ref=o_ref.at[4 * x + 2 * y + c],
                                              send_sem=send_sems.at[mask - 1], recv_sem=recv_sems.at[mask - 1],
                                              device_id=(px, py, pc), device_id_type=MESH_T)
            cp.start()
            started.append(cp)
        for cp in started:
            cp.wait()
        lc.wait()

    return _pcall(body, name=name, in_specs=[ANY], out_specs=ANY, out_shape=jax.ShapeDtypeStruct((8, r, cdim), F32),
                  scratch_shapes=[pltpu.SemaphoreType.DMA((7,)), pltpu.SemaphoreType.DMA((7,)),
                                  pltpu.SemaphoreType.DMA],
                  compiler_params=pltpu.CompilerParams(has_side_effects=True))(pk)


PACK_C = 1024


def _pack(arrs, row_mult):
    parts = []
    for a in arrs:
        flat = a.reshape(-1)
        parts.append(jnp.pad(flat, (0, (-flat.shape[0]) % PACK_C)))
    flat = jnp.concatenate(parts)
    flat = jnp.pad(flat, (0, (-flat.shape[0]) % (PACK_C * row_mult)))
    return flat.reshape(-1, PACK_C)


def _unpack(pk, shapes):
    flat = pk.reshape(-1)
    out, off = [], 0
    for s in shapes:
        size = 1
        for dd in s:
            size *= dd
        out.append(flat[off:off + size].reshape(s))
        off += size + (-size) % PACK_C
    return out


def _layer_params(full, l):
    p = {}
    for k in ['g_pre_mix', 'lru_conv_b', 'lru_ba', 'lru_bx', 'lru_lambda', 'conv_b', 'conv_ln_g', 'conv_ln_b', 'g_out_lru',
              'g_out_conv', 'g_post_mix', 'g_pre_ffn', 'ffn_conv_b', 'g_post_ffn']:
        p[k] = full[k][l][None, :]
    for k in ['lru_conv_w', 'conv_w', 'ffn_conv_w'] + [k for k in BIG if k in full]:
        p[k] = full[k][l]
    p['lru_wa'] = full['lru_wa_bf'][l]
    p['lru_wx'] = full['lru_wx_bf'][l]
    return p


def _step(x, loss_target, w, m, v):
    depth = w['w_in'].shape[0]
    d = x.shape[2]
    xk, yk, _ = _coords()
    k_me = 2 * xk + yk

    k1 = jnp.reshape(k_me, (1,)).astype(jnp.int32)
    big_bf = {k: _cast_into_window(w[k], BIG_AXIS[k], k1, 0, SYNC_LAYERS, "cast_" + k) for k in BIG}
    sh_pad = [w['meta_tokens']] + [jnp.pad(w[k], ((0, 0), (0, (-w[k].shape[1]) % 8), (0, 0))) for k in SH_SMALL[1:]]
    gath = _gather_xy([big_bf[k] for k in BIG] + sh_pad, [BIG_AXIS[k] for k in BIG] + [1, 2, 2, 2], len(BIG),
                      "gather_weights")
    full = {k: a for k, a in w.items() if k not in BIG}
    early = dict(zip(BIG, gath[:len(BIG)]))
    was = [BIG_AXIS[k] - 1 for k in BIG]
    in_flight = {}
    for l in range(SYNC_LAYERS, depth):
        bufs = [_cast_into_window(w[k], BIG_AXIS[k], k1, l, 1, "cast_%s_%d" % (k, l))[0] for k in BIG]
        send_sems, recv_sems, bufs, token = _gather_start(bufs, was, "gather_start_%d" % l)
        in_flight[l] = (send_sems, recv_sems, bufs)
        full['g_pre_mix'] = full['g_pre_mix'] + token[0, 0]

    def big_weights(l, h_in):
        if l < SYNC_LAYERS:
            return {k: early[k][l] for k in BIG}
        send_sems, recv_sems, bufs = in_flight[l]
        return dict(zip(BIG, _gather_wait(send_sems, recv_sems, bufs, was, h_in, "gather_wait_%d" % l)))

    full['meta_tokens'] = gath[len(BIG)]
    for k, a in zip(SH_SMALL[1:], gath[len(BIG) + 1:]):
        full[k] = a[:, :w[k].shape[1]]
    for k in ('lru_wa', 'lru_wx'):
        full[k + '_bf'] = _cast_bf16(w[k].reshape(-1, LANES), "cast_" + k).reshape(w[k].shape)

    h = jnp.concatenate([full['meta_tokens'], x[0]], axis=0)
    tgt = jnp.pad(loss_target[0], ((N_META, 0), (0, 0)))
    pending = {}

    def send_big_grads(l, keys, g):
        res = _scatter_start([g[k] for k in keys], [BIG_AXIS[k] - 1 for k in keys], "scatter_start_%d_%s" % (l, keys[-1]))
        pending[(l, keys)] = res[:4]
        return res[4]

    sq, dh, gl = _fwd_bwd(h, tgt, full, depth, send_big_grads, big_weights)
    loss = lax.psum(0.5 * jnp.sum(sq) / d, ("x", "y", "c"))
    return _reduce_update(loss, dh, gl, pending, w, m, v, depth, k_me)


def _fwd_bwd(h, tgt, full, depth, on_grads=None, big_weights=None):
    saved = []
    for l in range(depth):
        p = _layer_params(full, l)
        if big_weights is not None:
            p.update(big_weights(l, h))
        proj, zb1 = _rms_matmul(h, p['g_pre_mix'], p['w_in'], "in_proj")
        y, hs, c1 = _mix_fwd(proj, p, "mix_fwd")
        o, h1 = _matmul_rms_res(y, p['w_out'], h, p['g_post_mix'], "out_proj")
        u0, zb2 = _rms_matmul(h1, p['g_pre_ffn'], p['w_up'], "up_proj")
        act = _ffn_act_fwd(u0, p['ffn_conv_w'], p['ffn_conv_b'], "ffn_act_fwd")
        f, h2 = _matmul_rms_res(act, p['w_down'], h1, p['g_post_ffn'], "down_proj")
        saved.append((p, h, zb1, proj, y, hs, c1, o, h1, zb2, u0, act, f))
        h = h2

    dh, sq = _loss_head(h, tgt, "loss_head")

    def tied(a, token):
        return a if token is None else a + token[0:1, 0:1]

    gl = [None] * depth
    for l in reversed(range(depth)):
        p, h0, zb1, proj, y, hs, c1, o, h1, zb2, u0, act, f = saved[l]
        g = {}
        dact, dfb, g['g_post_ffn'] = _rmsbwd_matmul_nt(f, p['g_post_ffn'], dh, p['w_down'], "down_bwd")
        g['w_down'] = _matmul_tn(act, dfb, "down_dw")
        du0, g['ffn_conv_w'], g['ffn_conv_b'] = _ffn_act_bwd(dact, u0, p['ffn_conv_w'], p['ffn_conv_b'], "ffn_act_bwd")
        g['w_up'] = _matmul_tn(zb2, du0, "up_dw")
        dh1, g['g_pre_ffn'] = _matmul_nt_rmsbwd_res(du0, p['w_up'], h1, p['g_pre_ffn'], dh, "up_bwd")
        dy, dob, g['g_post_mix'] = _rmsbwd_matmul_nt(o, p['g_post_mix'], dh1, p['w_out'], "out_bwd")
        g['w_out'] = _matmul_tn(y, dob, "out_dw")
        token = on_grads(l, ('w_down', 'w_up', 'w_out'), g) if on_grads is not None else None
        pm = dict(p)
        pm['g_out_conv'] = tied(p['g_out_conv'], token)
        dproj, gm = _mix_bwd(dy, proj, hs, c1, pm, "mix_bwd")
        g.update(gm)
        g['w_in'] = _matmul_tn(zb1, dproj, "in_dw")
        token = on_grads(l, ('w_in',), g) if on_grads is not None else None
        dh, g['g_pre_mix'] = _matmul_nt_rmsbwd_res(dproj, p['w_in'], h0, tied(p['g_pre_mix'], token), dh1, "in_bwd")
        gl[l] = g
    return sq, dh, gl


def _reduce_update(loss, dh, gl, pending, w, m, v, depth, k_me):
    grad_x = dh[N_META:][None]

    def stacked(k):
        return jnp.stack([gl[l][k].reshape(w[k].shape[1:]) if k not in SH_SMALL + BIG else gl[l][k]
                          for l in range(depth)])

    k1 = jnp.reshape(k_me, (1,)).astype(jnp.int32)
    c1 = jnp.reshape(lax.axis_index("c"), (1,)).astype(jnp.int32)
    order, halves = [], []
    for (l, keys), (send_sems, recv_sems, g_thru, lands) in pending.items():
        was = [BIG_AXIS[k] - 1 for k in keys]
        g_own, lands = _scatter_wait(send_sems, recv_sems, g_thru, lands, was, dh, "scatter_wait_%d_%s" % (l, keys[-1]))
        for k, wa, land, g_k in zip(keys, was, lands, g_own):
            red = _sum_pieces(land, g_k, wa, k1, c1, "grad_sum")
            order.append((l, k))
            halves.append(red.reshape(2, land.shape[1], land.shape[2]))
    swapped = dict(zip(order, _sibling_swap(halves, "grad_swap")))
    big_red = [jnp.stack([swapped[(l, k)].reshape(w[k].shape[1:]) for l in range(depth)]) for k in BIG]

    out = {}
    for k, gk in zip(BIG, big_red):
        c2 = gk.shape[-1]
        dl, m2, v2 = _adamw(gk.reshape(-1, c2), w[k].reshape(-1, c2), m[k].reshape(-1, c2), v[k].reshape(-1, c2),
                            "adamw_big")
        out[k] = (gk, dl.reshape(gk.shape), m2.reshape(gk.shape), v2.reshape(gk.shape))

    rep_g = [stacked(k) for k in REP_SMALL]
    sh_g = [dh[:N_META]] + [stacked(k) for k in SH_SMALL[1:]]
    n_rep_rows = _pack(rep_g, 1).shape[0]
    pk = jnp.concatenate([_pack(rep_g, 1), _pack(sh_g, 1)])
    pk = jnp.pad(pk, ((0, (-pk.shape[0]) % 256), (0, 0)))
    part = _sum_slots(_scatter8(pk, "small_scatter"), "small_sum")
    red = _gather_all(part, "small_gather").reshape(pk.shape)
    rep_red = _unpack(red[:n_rep_rows], [a.shape for a in rep_g])
    sh_red = []
    for a in _unpack(red[n_rep_rows:], [a.shape for a in sh_g]):
        wd = a.shape[-1] // 4
        sh_red.append(lax.dynamic_slice_in_dim(a, k_me * wd, wd, axis=a.ndim - 1))

    for names, grads_ in ((REP_SMALL, rep_red), (SH_SMALL, sh_red)):
        res = _adamw(_pack(grads_, 16), _pack([w[k] for k in names], 16), _pack([m[k] for k in names], 16),
                     _pack([v[k] for k in names], 16), "adamw_small")
        shapes = [w[k].shape for k in names]
        un = [_unpack(r, shapes) for r in res]
        for j, k in enumerate(names):
            out[k] = (grads_[j].reshape(w[k].shape), un[0][j], un[1][j], un[2][j])

    return (loss, grad_x, *[out[k][0] for k in WEIGHTS], *[out[k][1] for k in WEIGHTS],
            *[out[k][2] for k in WEIGHTS], *[out[k][3] for k in WEIGHTS])


def kernel(x, meta_tokens, g_pre_mix, w_in, lru_conv_w, lru_conv_b, lru_wa, lru_ba, lru_wx, lru_bx, lru_lambda, conv_w, conv_b, conv_ln_g, conv_ln_b, g_out_lru, g_out_conv, w_out, g_post_mix, g_pre_ffn, w_up, ffn_conv_w, ffn_conv_b, w_down, g_post_ffn, loss_target, m_meta_tokens, m_g_pre_mix, m_w_in, m_lru_conv_w, m_lru_conv_b, m_lru_wa, m_lru_ba, m_lru_wx, m_lru_bx, m_lru_lambda, m_conv_w, m_conv_b, m_conv_ln_g, m_conv_ln_b, m_g_out_lru, m_g_out_conv, m_w_out, m_g_post_mix, m_g_pre_ffn, m_w_up, m_ffn_conv_w, m_ffn_conv_b, m_w_down, m_g_post_ffn, v_meta_tokens, v_g_pre_mix, v_w_in, v_lru_conv_w, v_lru_conv_b, v_lru_wa, v_lru_ba, v_lru_wx, v_lru_bx, v_lru_lambda, v_conv_w, v_conv_b, v_conv_ln_g, v_conv_ln_b, v_g_out_lru, v_g_out_conv, v_w_out, v_g_post_mix, v_g_pre_ffn, v_w_up, v_ffn_conv_w, v_ffn_conv_b, v_w_down, v_g_post_ffn):
    w = dict(meta_tokens=meta_tokens, g_pre_mix=g_pre_mix, w_in=w_in, lru_conv_w=lru_conv_w, lru_conv_b=lru_conv_b,
             lru_wa=lru_wa, lru_ba=lru_ba, lru_wx=lru_wx, lru_bx=lru_bx, lru_lambda=lru_lambda, conv_w=conv_w,
             conv_b=conv_b, conv_ln_g=conv_ln_g, conv_ln_b=conv_ln_b, g_out_lru=g_out_lru, g_out_conv=g_out_conv,
             w_out=w_out, g_post_mix=g_post_mix, g_pre_ffn=g_pre_ffn, w_up=w_up, ffn_conv_w=ffn_conv_w,
             ffn_conv_b=ffn_conv_b, w_down=w_down, g_post_ffn=g_post_ffn)
    m = dict(meta_tokens=m_meta_tokens, g_pre_mix=m_g_pre_mix, w_in=m_w_in, lru_conv_w=m_lru_conv_w,
             lru_conv_b=m_lru_conv_b, lru_wa=m_lru_wa, lru_ba=m_lru_ba, lru_wx=m_lru_wx, lru_bx=m_lru_bx,
             lru_lambda=m_lru_lambda, conv_w=m_conv_w, conv_b=m_conv_b, conv_ln_g=m_conv_ln_g, conv_ln_b=m_conv_ln_b,
             g_out_lru=m_g_out_lru, g_out_conv=m_g_out_conv, w_out=m_w_out, g_post_mix=m_g_post_mix,
             g_pre_ffn=m_g_pre_ffn, w_up=m_w_up, ffn_conv_w=m_ffn_conv_w, ffn_conv_b=m_ffn_conv_b, w_down=m_w_down,
             g_post_ffn=m_g_post_ffn)
    v = dict(meta_tokens=v_meta_tokens, g_pre_mix=v_g_pre_mix, w_in=v_w_in, lru_conv_w=v_lru_conv_w,
             lru_conv_b=v_lru_conv_b, lru_wa=v_lru_wa, lru_ba=v_lru_ba, lru_wx=v_lru_wx, lru_bx=v_lru_bx,
             lru_lambda=v_lru_lambda, conv_w=v_conv_w, conv_b=v_conv_b, conv_ln_g=v_conv_ln_g, conv_ln_b=v_conv_ln_b,
             g_out_lru=v_g_out_lru, g_out_conv=v_g_out_conv, w_out=v_w_out, g_post_mix=v_g_post_mix,
             g_pre_ffn=v_g_pre_ffn, w_up=v_w_up, ffn_conv_w=v_ffn_conv_w, ffn_conv_b=v_ffn_conv_b, w_down=v_w_down,
             g_post_ffn=v_g_post_ffn)
    return _step(x, loss_target, w, m, v)
```

```python
import functools

import jax
import jax.numpy as jnp
from jax import lax
from jax.experimental import pallas as pl
from jax.experimental.pallas import tpu as pltpu

F32 = jnp.float32
BF16 = jnp.bfloat16
EPS = 1e-6
N_META = 16
LRU_C = 8.0
CONV_K = 31
LRU_K = 4
FFN_K = 3
CONV_ROWS = 24
TN_ROWS = 912
SYNC_LAYERS = 2
SCAN_U = 3
RSCAN_U = 2
LANES = 128
VMEM_LIMIT = 56 * 1024 * 1024
ADAM_LR, ADAM_B1, ADAM_B2, ADAM_EPS, ADAM_WD, ADAM_STEP = 0.001, 0.9, 0.999, 1e-08, 0.01, 10
MESH_T = pl.DeviceIdType.MESH
NT_DIMS = (((1,), (1,)), ((), ()))
TN_DIMS = (((0,), (0,)), ((), ()))

REP_SMALL = ['g_pre_mix', 'lru_conv_b', 'lru_wa', 'lru_ba', 'lru_wx', 'lru_bx', 'lru_lambda', 'conv_b', 'conv_ln_g',
             'conv_ln_b', 'g_out_lru', 'g_out_conv', 'g_post_mix', 'g_pre_ffn', 'ffn_conv_b', 'g_post_ffn']
SH_SMALL = ['meta_tokens', 'lru_conv_w', 'conv_w', 'ffn_conv_w']
BIG = ['w_in', 'w_out', 'w_up', 'w_down']
BIG_AXIS = {'w_in': 2, 'w_out': 1, 'w_up': 2, 'w_down': 1}
WEIGHTS = ['meta_tokens', 'g_pre_mix', 'w_in', 'lru_conv_w', 'lru_conv_b', 'lru_wa', 'lru_ba', 'lru_wx', 'lru_bx',
           'lru_lambda', 'conv_w', 'conv_b', 'conv_ln_g', 'conv_ln_b', 'g_out_lru', 'g_out_conv', 'w_out',
           'g_post_mix', 'g_pre_ffn', 'w_up', 'ffn_conv_w', 'ffn_conv_b', 'w_down', 'g_post_ffn']


def _pcall(body, **kw):
    return pl.pallas_call(body, **kw)


def _params(n_grid=1):
    return pltpu.CompilerParams(dimension_semantics=("arbitrary",) * n_grid, vmem_limit_bytes=VMEM_LIMIT)


def _tiles(t):
    if t % 432 == 0:
        return 432, 144
    assert t % 48 == 0
    return 48, 48


def _row(tm, n):
    return pl.BlockSpec((tm, n), lambda i: (i, 0))


def _rrow(tm, n, nt):
    return pl.BlockSpec((tm, n), lambda i: (nt - 1 - i, 0))


def _halo(hb, n, tm, nt):
    return pl.BlockSpec((hb, n), lambda i: (jnp.maximum((nt - 1 - i) * (tm // hb) - 1, 0), 0))


def _const(shape):
    nd = len(shape)
    return pl.BlockSpec(shape, lambda *_: (0,) * nd, pipeline_mode=pl.Buffered(1))


def _const_out(shape):
    nd = len(shape)
    return pl.BlockSpec(shape, lambda *_: (0,) * nd)


def _sigmoid(x):
    return 1.0 / (1.0 + jnp.exp(-x))


def _gelu(x):
    return 0.5 * x * (1.0 + jnp.tanh(0.7978845608028654 * (x + 0.044715 * (x * x * x))))


def _gelu_and_grad(x):
    k = 0.7978845608028654
    x2 = x * x
    th = jnp.tanh(k * (x + 0.044715 * (x2 * x)))
    return 0.5 * x * (1.0 + th), 0.5 * (1.0 + th) + 0.5 * x * (1.0 - th * th) * (k * (1.0 + 0.134145 * x2))


def _expm1(x):
    return jnp.where(jnp.abs(x) < 1e-2, x * (1.0 + x * (0.5 + x * (1.0 / 6.0 + x * (1.0 / 24.0)))), jnp.exp(x) - 1.0)


def _softplus(x):
    e = jnp.exp(-jnp.abs(x))
    return jnp.maximum(x, 0.0) + jnp.where(e < 1e-4, e * (1.0 - 0.5 * e), jnp.log(1.0 + e))


def _lru_gates(pa, px, sp):
    r = _sigmoid(pa)
    ig = _sigmoid(px)
    la = (-LRU_C * r) * sp
    return r, ig, jnp.exp(la), jnp.sqrt(-_expm1(2.0 * la))


def _rms(x):
    return lax.rsqrt(jnp.mean(x * x, axis=-1, keepdims=True) + EPS)


def _rms_bwd(x, g, dy):
    r = _rms(x)
    xr = x * r
    dyg = dy * g
    return r * (dyg - xr * jnp.mean(dyg * xr, axis=-1, keepdims=True)), xr


def _col_chunk(n):
    return 1536 if n % 1536 == 0 else 1024


def _rms_matmul(h, g, w, name):
    t, d = h.shape
    n = w.shape[1]
    tm, _ = _tiles(t)
    cn = _col_chunk(n)

    def body(h_ref, g_ref, w_ref, p_ref, zb_ref):
        x = h_ref[...]
        zb = ((x * _rms(x)) * g_ref[...]).astype(BF16)
        zb_ref[...] = zb
        for c in range(n // cn):
            p_ref[:, c * cn:(c + 1) * cn] = jnp.dot(zb, w_ref[:, c * cn:(c + 1) * cn], preferred_element_type=F32)

    return _pcall(body, name=name, grid=(t // tm,),
                  in_specs=[_row(tm, d), _const((1, d)), _const((d, n))],
                  out_specs=[_row(tm, n), _row(tm, d)],
                  out_shape=[jax.ShapeDtypeStruct((t, n), F32), jax.ShapeDtypeStruct((t, d), BF16)],
                  compiler_params=_params())(h, g, w)


def _matmul_rms_res(a, w, h, g, name):
    t, k = a.shape
    d = w.shape[1]
    tm, _ = _tiles(t)

    def body(a_ref, w_ref, h_ref, g_ref, o_ref, hn_ref):
        o = jnp.dot(a_ref[...], w_ref[...], preferred_element_type=F32)
        o_ref[...] = o
        hn_ref[...] = h_ref[...] + (o * _rms(o)) * g_ref[...]

    return _pcall(body, name=name, grid=(t // tm,),
                  in_specs=[_row(tm, k), _const((k, d)), _row(tm, d), _const((1, d))],
                  out_specs=[_row(tm, d), _row(tm, d)],
                  out_shape=[jax.ShapeDtypeStruct((t, d), F32), jax.ShapeDtypeStruct((t, d), F32)],
                  compiler_params=_params())(a, w, h, g)


def _rmsbwd_matmul_nt(x, g, dy, w, name):
    t, d = x.shape
    n = w.shape[0]
    tm, _ = _tiles(t)
    cn = _col_chunk(n)

    def body(x_ref, g_ref, dy_ref, w_ref, da_ref, dxb_ref, dg_ref):
        @pl.when(pl.program_id(0) == 0)
        def _():
            dg_ref[...] = jnp.zeros((1, d), F32)

        dy = dy_ref[...]
        dx, xr = _rms_bwd(x_ref[...], g_ref[...], dy)
        dg_ref[...] += jnp.sum(dy * xr, axis=0, keepdims=True)
        dxb = dx.astype(BF16)
        dxb_ref[...] = dxb
        for c in range(n // cn):
            da_ref[:, c * cn:(c + 1) * cn] = lax.dot_general(dxb, w_ref[c * cn:(c + 1) * cn, :], NT_DIMS,
                                                             preferred_element_type=F32)

    return _pcall(body, name=name, grid=(t // tm,),
                  in_specs=[_row(tm, d), _const((1, d)), _row(tm, d), _const((n, d))],
                  out_specs=[_row(tm, n), _row(tm, d), _const_out((1, d))],
                  out_shape=[jax.ShapeDtypeStruct((t, n), F32), jax.ShapeDtypeStruct((t, d), BF16),
                             jax.ShapeDtypeStruct((1, d), F32)],
                  compiler_params=_params())(x, g, dy, w)


def _matmul_nt_rmsbwd_res(dp, w, h, g, dh, name):
    t, n = dp.shape
    d = w.shape[0]
    tm, _ = _tiles(t)

    def body(dp_ref, w_ref, h_ref, g_ref, dh_ref, out_ref, dg_ref):
        @pl.when(pl.program_id(0) == 0)
        def _():
            dg_ref[...] = jnp.zeros((1, d), F32)

        dz = lax.dot_general(dp_ref[...], w_ref[...], NT_DIMS, preferred_element_type=F32)
        dx, xr = _rms_bwd(h_ref[...], g_ref[...], dz)
        dg_ref[...] += jnp.sum(dz * xr, axis=0, keepdims=True)
        out_ref[...] = dh_ref[...] + dx

    return _pcall(body, name=name, grid=(t // tm,),
                  in_specs=[_row(tm, n), _const((d, n)), _row(tm, d), _const((1, d)), _row(tm, d)],
                  out_specs=[_row(tm, d), _const_out((1, d))],
                  out_shape=[jax.ShapeDtypeStruct((t, d), F32), jax.ShapeDtypeStruct((1, d), F32)],
                  compiler_params=_params())(dp, w, h, g, dh)


def _matmul_tn(a, b, name):
    t, k = a.shape
    n = b.shape[1]
    tm, _ = _tiles(t)
    if t % TN_ROWS == 0:
        tm = TN_ROWS
    nt = t // tm
    bn = min(n, (1536 * 1024) // k)
    assert n % bn == 0 and bn % LANES == 0

    def body(a_ref, b_ref, o_ref, acc):
        @pl.when(pl.program_id(1) == 0)
        def _():
            acc[...] = jnp.zeros((k, bn), F32)

        acc[...] += lax.dot_general(a_ref[...], b_ref[...], TN_DIMS, preferred_element_type=F32)

        @pl.when(pl.program_id(1) == nt - 1)
        def _():
            o_ref[...] = acc[...].astype(BF16)

    return _pcall(body, name=name, grid=(n // bn, nt),
                  in_specs=[pl.BlockSpec((tm, k), lambda j, i: (i, 0)), pl.BlockSpec((tm, bn), lambda j, i: (i, j))],
                  out_specs=pl.BlockSpec((k, bn), lambda j, i: (0, j)),
                  out_shape=jax.ShapeDtypeStruct((k, n), BF16),
                  scratch_shapes=[pltpu.VMEM((k, bn), F32)],
                  compiler_params=_params(2))(a, b)


MIX_PARAMS = ['lru_conv_w', 'lru_conv_b', 'lru_wa', 'lru_ba', 'lru_wx', 'lru_bx', 'lru_lambda', 'conv_w', 'conv_b',
              'conv_ln_g', 'conv_ln_b', 'g_out_lru', 'g_out_conv']


def _pair_heads(a):
    hh, hd, _ = a.shape
    a = a.reshape(hh // 2, 2, hd, hd)
    z = jnp.zeros_like(a[:, 0])
    return jnp.concatenate([jnp.concatenate([a[:, 0], z], axis=-1), jnp.concatenate([z, a[:, 1]], axis=-1)], axis=-2)


def _head_gates(xcb, wa_ref, wx_ref, pa_s, px_s, ba, bx, heads):
    for j in range(heads // 2):
        sl = slice(2 * LANES * j, 2 * LANES * (j + 1))
        pa_s[:, sl] = jnp.dot(xcb[:, sl], wa_ref[j], preferred_element_type=F32) + ba[:, sl]
        px_s[:, sl] = jnp.dot(xcb[:, sl], wx_ref[j], preferred_element_type=F32) + bx[:, sl]


def _lru_conv_chunk(exta, cw4_ref, cb4, r0):
    win = exta[pl.ds(r0, 16), :]
    acc = cw4_ref[3:4, :] * win[8:16]
    for k in range(LRU_K - 1):
        acc = acc + cw4_ref[k:k + 1, :] * pltpu.roll(win, LRU_K - 1 - k, 0)[8:16]
    return acc + cb4


def _mix_fwd(proj, p, name):
    t = proj.shape[0]
    w = p['lru_conv_b'].shape[1]
    cw = p['conv_b'].shape[1]
    heads = w // LANES
    _, tm = _tiles(t)
    nch = tm // 8

    def body(proj_ref, cw4_ref, cb4_ref, wa_ref, ba_ref, wx_ref, bx_ref, lam_ref, cw31_ref, cb31_ref, lng_ref,
             lnb_ref, ga_ref, gb_ref, y_ref, hs_ref, c1_ref, exta, xc_s, pa_s, px_s, extb, nbuf, hcar):
        @pl.when(pl.program_id(0) == 0)
        def _():
            exta[0:8, :] = jnp.zeros((8, w), F32)
            extb[0:32, :] = jnp.zeros((32, cw), F32)
            hcar[...] = jnp.zeros((8, w), F32)

        exta[8:8 + tm, :] = proj_ref[:, 0:w]
        cb4 = cb4_ref[...]

        def conv_a(c, carry):
            r0 = pl.multiple_of(c * 8, 8)
            xc_s[pl.ds(r0, 8), :] = _lru_conv_chunk(exta, cw4_ref, cb4, r0)
            return carry

        lax.fori_loop(0, nch, conv_a, 0, unroll=3)
        _head_gates(xc_s[...].astype(BF16), wa_ref, wx_ref, pa_s, px_s, ba_ref[...], bx_ref[...], heads)

        sp = _softplus(-lam_ref[...])
        ga = ga_ref[...]
        row = lax.broadcasted_iota(jnp.int32, (8, w), 0)

        def scan_c(cg, hprev):
            part = []
            for u in range(SCAN_U):
                r0 = pl.multiple_of((cg * SCAN_U + u) * 8, 8)
                xc = xc_s[pl.ds(r0, 8), :]
                _, ig, a, m = _lru_gates(pa_s[pl.ds(r0, 8), :], px_s[pl.ds(r0, 8), :], sp)
                aa, bb = a, m * (ig * xc)
                for d in (1, 2, 4):
                    a_s = pltpu.roll(aa, d, 0)
                    b_s = pltpu.roll(bb, d, 0)
                    msk = row >= d
                    bb = jnp.where(msk, aa * b_s + bb, bb)
                    aa = jnp.where(msk, aa * a_s, aa)
                part.append((r0, aa, bb, _gelu(proj_ref[pl.ds(r0, 8), w:2 * w])))
            for r0, aa, bb, ge in part:
                hs = aa * hprev + bb
                hs_ref[pl.ds(r0, 8), :] = hs
                ya = hs * ge
                nbuf[pl.ds(r0, 8), 0:w] = (ya * _rms(ya)) * ga
                hprev = hs[7:8, :]
            return hprev

        hcar[0:1, :] = lax.fori_loop(0, nch // SCAN_U, scan_c, hcar[0:1, :])

        extb[32:32 + tm, :] = proj_ref[:, 2 * w:2 * w + cw] * _sigmoid(proj_ref[:, 2 * w + cw:2 * w + 2 * cw])

        def conv_b(c, carry):
            r0 = pl.multiple_of(c * CONV_ROWS, 8)
            ybs = []
            ssq = jnp.zeros((CONV_ROWS, 1), F32)
            for lb in range(cw // LANES):
                sl = slice(LANES * lb, LANES * (lb + 1))
                win = extb[pl.ds(r0, CONV_ROWS + 32), sl]
                rolled = [win] + [pltpu.roll(win, rr, 0) for rr in range(1, 8)]
                parts = [None] * 4
                for k in range(CONV_K):
                    q, rr = divmod(CONV_K - 1 - k, 8)
                    term = cw31_ref[k:k + 1, sl] * rolled[rr][32 - 8 * q:32 - 8 * q + CONV_ROWS]
                    parts[k % 4] = term if parts[k % 4] is None else parts[k % 4] + term
                acc = ((parts[0] + parts[1]) + (parts[2] + parts[3])) + cb31_ref[:, sl]
                c1_ref[pl.ds(r0, CONV_ROWS), sl] = acc
                dlt = acc - jnp.mean(acc, axis=-1, keepdims=True)
                c2 = dlt * lax.rsqrt(jnp.mean(dlt * dlt, axis=-1, keepdims=True) + EPS)
                yb0 = c2 * lng_ref[:, sl] + lnb_ref[:, sl]
                yb = yb0 * _sigmoid(yb0)
                ybs.append(yb)
                ssq = ssq + jnp.sum(yb * yb, axis=-1, keepdims=True)
            rb = lax.rsqrt(ssq / cw + EPS)
            for lb in range(cw // LANES):
                sl = slice(LANES * lb, LANES * (lb + 1))
                nbuf[pl.ds(r0, CONV_ROWS), w + LANES * lb:w + LANES * (lb + 1)] = (ybs[lb] * rb) * gb_ref[:, sl]
            return carry

        lax.fori_loop(0, tm // CONV_ROWS, conv_b, 0, unroll=2)
        exta[0:8, :] = exta[tm:tm + 8, :]
        extb[0:32, :] = extb[tm:tm + 32, :]
        y_ref[...] = nbuf[...].astype(BF16)

    consts = [p[k] for k in MIX_PARAMS]
    return _pcall(body, name=name, grid=(t // tm,),
                  in_specs=[_row(tm, 2 * w + 2 * cw)] + [_const(c.shape) for c in consts],
                  out_specs=[_row(tm, w + cw), _row(tm, w), _row(tm, cw)],
                  out_shape=[jax.ShapeDtypeStruct((t, w + cw), BF16), jax.ShapeDtypeStruct((t, w), F32),
                             jax.ShapeDtypeStruct((t, cw), F32)],
                  scratch_shapes=[pltpu.VMEM((8 + tm, w), F32), pltpu.VMEM((tm, w), F32), pltpu.VMEM((tm, w), F32),
                                  pltpu.VMEM((tm, w), F32), pltpu.VMEM((32 + tm, cw), F32),
                                  pltpu.VMEM((tm, w + cw), F32), pltpu.VMEM((8, w), F32)],
                  compiler_params=_params())(proj, *consts)


def _mix_bwd(dy, proj, hs, c1, p, name):
    t = proj.shape[0]
    w = p['lru_conv_b'].shape[1]
    cw = p['conv_b'].shape[1]
    heads = w // LANES
    _, tm = _tiles(t)
    nt = t // tm
    nch = tm // 8
    nlb = cw // LANES
    G_CB4, G_CW4, G_BA, G_BX, G_SP, G_GA, NGW = 0, 1, 5, 6, 7, 8, 9
    G_CB31, G_LNG, G_LNB, G_GB, G_CW31, NGC = 0, 1, 2, 3, 4, 4 + CONV_K

    def body(dy_ref, proj_ref, projh_ref, hs_ref, hsh_ref, c1_ref, cw4_ref, cb4_ref, wa_ref, ba_ref, wx_ref, bx_ref,
             lam_ref, cw31_ref, cb31_ref, lng_ref, lnb_ref, ga_ref, gb_ref,
             dproj_ref, dcw4_ref, dcb4_ref, dwa_ref, dba_ref, dwx_ref, dbx_ref, dlam_ref, dcw31_ref, dcb31_ref,
             dlng_ref, dlnb_ref, dga_ref, dgb_ref,
             exta, exth, xc_s, pa_s, px_s, dpa_s, dpx_s, dxce, extb, dc1e, dpf, cp_s, acc_w, acc_c):
        i = pl.program_id(0)

        @pl.when(i == 0)
        def _():
            acc_w[...] = jnp.zeros((8 * NGW, w), F32)
            acc_c[...] = jnp.zeros((8 * NGC, cw), F32)
            dwa_ref[...] = jnp.zeros(dwa_ref.shape, F32)
            dwx_ref[...] = jnp.zeros(dwx_ref.shape, F32)
            cp_s[...] = jnp.zeros((8, w), F32)
            dxce[tm:tm + 8, :] = jnp.zeros((8, w), F32)
            dc1e[tm:tm + 32, :] = jnp.zeros((32, cw), F32)

        nf = jnp.where(i < nt - 1, 1.0, 0.0).astype(F32)
        exta[0:8, :] = projh_ref[40:48, 0:w] * nf
        exta[8:8 + tm, :] = proj_ref[:, 0:w]
        exth[0:8, :] = hsh_ref[...] * nf
        exth[8:8 + tm, :] = hs_ref[...]
        extb[0:48, :] = (projh_ref[:, 2 * w:2 * w + cw] * _sigmoid(projh_ref[:, 2 * w + cw:2 * w + 2 * cw])) * nf
        extb[48:48 + tm, :] = proj_ref[:, 2 * w:2 * w + cw] * _sigmoid(proj_ref[:, 2 * w + cw:2 * w + 2 * cw])
        cb4 = cb4_ref[...]

        def conv_a(c, carry):
            r0 = pl.multiple_of(c * 8, 8)
            xc_s[pl.ds(r0, 8), :] = _lru_conv_chunk(exta, cw4_ref, cb4, r0)
            return carry

        lax.fori_loop(0, nch, conv_a, 0, unroll=3)
        xcb = xc_s[...].astype(BF16)
        _head_gates(xcb, wa_ref, wx_ref, pa_s, px_s, ba_ref[...], bx_ref[...], heads)

        sp = _softplus(-lam_ref[...])
        ga = ga_ref[...]
        row = lax.broadcasted_iota(jnp.int32, (8, w), 0)

        def acc_add(ref, g, val, sl=slice(None)):
            for j in range(val.shape[0] // 8):
                ref[8 * g:8 * g + 8, sl] = ref[8 * g:8 * g + 8, sl] + val[8 * j:8 * j + 8]

        def rscan(cg, cp):
            part = []
            for u in range(RSCAN_U):
                r0 = pl.multiple_of((nch - 1 - (cg * RSCAN_U + u)) * 8, 8)
                xc = xc_s[pl.ds(r0, 8), :]
                r, ig, a, m = _lru_gates(pa_s[pl.ds(r0, 8), :], px_s[pl.ds(r0, 8), :], sp)
                hwin = exth[pl.ds(r0, 16), :]
                hcur = hwin[8:16]
                hprev = pltpu.roll(hwin, 1, 0)[8:16]
                ge, dge = _gelu_and_grad(proj_ref[pl.ds(r0, 8), w:2 * w])
                dna = dy_ref[pl.ds(r0, 8), 0:w]
                dya, yar = _rms_bwd(hcur * ge, ga, dna)
                acc_add(acc_w, G_GA, dna * yar)
                dpf[pl.ds(r0, 8), w:2 * w] = (dya * hcur) * dge
                aa = jnp.where(row == 7, 1.0, pltpu.roll(a, 7, 0))
                bb = dya * ge
                for d in (1, 2, 4):
                    a_s = pltpu.roll(aa, 8 - d, 0)
                    b_s = pltpu.roll(bb, 8 - d, 0)
                    msk = row < 8 - d
                    bb = jnp.where(msk, aa * b_s + bb, bb)
                    aa = jnp.where(msk, aa * a_s, aa)
                part.append((r0, aa, bb, xc, r, ig, a, m, hprev))
            for r0, aa, bb, xc, r, ig, a, m, hprev in part:
                lamb = bb + aa * cp
                cp = a[0:1, :] * lamb[0:1, :]
                dm = lamb * (ig * xc)
                di = lamb * (m * xc)
                dxce[pl.ds(r0, 8), :] = lamb * (m * ig)
                dla = a * (lamb * hprev - dm * (a / m))
                acc_add(acc_w, G_SP, dla * (-LRU_C * r))
                dpa = (dla * (-LRU_C * sp)) * (r * (1.0 - r))
                dpx = di * (ig * (1.0 - ig))
                acc_add(acc_w, G_BA, dpa)
                acc_add(acc_w, G_BX, dpx)
                dpa_s[pl.ds(r0, 8), :] = dpa
                dpx_s[pl.ds(r0, 8), :] = dpx
            return cp

        cp_s[0:1, :] = lax.fori_loop(0, nch // RSCAN_U, rscan, cp_s[0:1, :])

        dpab = dpa_s[...].astype(BF16)
        dpxb = dpx_s[...].astype(BF16)
        for j in range(heads // 2):
            sl = slice(2 * LANES * j, 2 * LANES * (j + 1))
            dxce[0:tm, sl] = (dxce[0:tm, sl]
                              + lax.dot_general(dpab[:, sl], wa_ref[j], NT_DIMS, preferred_element_type=F32)
                              + lax.dot_general(dpxb[:, sl], wx_ref[j], NT_DIMS, preferred_element_type=F32))
            ga2 = lax.dot_general(xcb[:, sl], dpab[:, sl], TN_DIMS, preferred_element_type=F32)
            gx2 = lax.dot_general(xcb[:, sl], dpxb[:, sl], TN_DIMS, preferred_element_type=F32)
            for e in range(2):
                blk = slice(LANES * e, LANES * (e + 1))
                dwa_ref[2 * j + e] = dwa_ref[2 * j + e] + ga2[blk, blk]
                dwx_ref[2 * j + e] = dwx_ref[2 * j + e] + gx2[blk, blk]

        def conv_a_bwd(c, carry):
            r0 = pl.multiple_of(c * 8, 8)
            win = dxce[pl.ds(r0, 16), :]
            dxc = win[0:8]
            xwin = exta[pl.ds(r0, 16), :]
            dxl = cw4_ref[3:4, :] * dxc
            acc_add(acc_w, G_CB4, dxc)
            acc_add(acc_w, G_CW4 + 3, dxc * xwin[8:16])
            for k in range(LRU_K - 1):
                s = LRU_K - 1 - k
                dxl = dxl + cw4_ref[k:k + 1, :] * pltpu.roll(win, 16 - s, 0)[0:8]
                acc_add(acc_w, G_CW4 + k, dxc * pltpu.roll(xwin, s, 0)[8:16])
            dpf[pl.ds(r0, 8), 0:w] = dxl
            return carry

        lax.fori_loop(0, nch, conv_a_bwd, 0, unroll=3)
        dxce[tm:tm + 8, :] = dxce[0:8, :]

        def mixb(c, carry):
            r0 = pl.multiple_of(c * 8, 8)
            st = []
            ssq = jnp.zeros((8, 1), F32)
            for lb in range(nlb):
                sl = slice(LANES * lb, LANES * (lb + 1))
                c1v = c1_ref[pl.ds(r0, 8), sl]
                dlt = c1v - jnp.mean(c1v, axis=-1, keepdims=True)
                rs = lax.rsqrt(jnp.mean(dlt * dlt, axis=-1, keepdims=True) + EPS)
                c2 = dlt * rs
                yb0 = c2 * lng_ref[:, sl] + lnb_ref[:, sl]
                sg = _sigmoid(yb0)
                yb = yb0 * sg
                ssq = ssq + jnp.sum(yb * yb, axis=-1, keepdims=True)
                st.append((rs, c2, yb0, sg, yb))
            rb = lax.rsqrt(ssq / cw + EPS)
            tsum = jnp.zeros((8, 1), F32)
            dngs = []
            for lb in range(nlb):
                sl = slice(LANES * lb, LANES * (lb + 1))
                dnb = dy_ref[pl.ds(r0, 8), w + LANES * lb:w + LANES * (lb + 1)]
                ybr = st[lb][4] * rb
                acc_add(acc_c, G_GB, dnb * ybr, sl)
                dng = dnb * gb_ref[:, sl]
                dngs.append((dng, ybr))
                tsum = tsum + jnp.sum(dng * ybr, axis=-1, keepdims=True)
            tsum = tsum / cw
            for lb in range(nlb):
                sl = slice(LANES * lb, LANES * (lb + 1))
                rs, c2, yb0, sg, _ = st[lb]
                dng, ybr = dngs[lb]
                dyb0 = (rb * (dng - ybr * tsum)) * (sg * (1.0 + yb0 * (1.0 - sg)))
                acc_add(acc_c, G_LNG, dyb0 * c2, sl)
                acc_add(acc_c, G_LNB, dyb0, sl)
                dc2 = dyb0 * lng_ref[:, sl]
                dc1 = rs * (dc2 - jnp.mean(dc2, axis=-1, keepdims=True)
                            - c2 * jnp.mean(dc2 * c2, axis=-1, keepdims=True))
                acc_add(acc_c, G_CB31, dc1, sl)
                dc1e[pl.ds(r0, 8), sl] = dc1
            return carry

        lax.fori_loop(0, nch, mixb, 0, unroll=6)

        def conv_b_bwd(c, carry):
            r0 = pl.multiple_of(c * CONV_ROWS, 8)
            nwin = CONV_ROWS + 32
            for lb in range(nlb):
                sl = slice(LANES * lb, LANES * (lb + 1))
                win = dc1e[pl.ds(r0, nwin), sl]
                ups = [win] + [pltpu.roll(win, nwin - rr, 0) for rr in range(1, 8)]
                dc1 = win[0:CONV_ROWS]
                parts = [None] * 4
                for k in range(CONV_K):
                    q, rr = divmod(CONV_K - 1 - k, 8)
                    term = cw31_ref[k:k + 1, sl] * ups[rr][8 * q:8 * q + CONV_ROWS]
                    parts[k % 4] = term if parts[k % 4] is None else parts[k % 4] + term
                dc0 = (parts[0] + parts[1]) + (parts[2] + parts[3])
                cav = proj_ref[pl.ds(r0, CONV_ROWS), 2 * w + LANES * lb:2 * w + LANES * (lb + 1)]
                sg = _sigmoid(proj_ref[pl.ds(r0, CONV_ROWS), 2 * w + cw + LANES * lb:2 * w + cw + LANES * (lb + 1)])
                dpf[pl.ds(r0, CONV_ROWS), 2 * w + LANES * lb:2 * w + LANES * (lb + 1)] = dc0 * sg
                dpf[pl.ds(r0, CONV_ROWS), 2 * w + cw + LANES * lb:2 * w + cw + LANES * (lb + 1)] = (
                    (dc0 * cav) * (sg * (1.0 - sg)))
                xwin = extb[pl.ds(pl.multiple_of(r0 + 16, 8), nwin), sl]
                xr = [xwin] + [pltpu.roll(xwin, rr, 0) for rr in range(1, 8)]
                for k in range(CONV_K):
                    q, rr = divmod(CONV_K - 1 - k, 8)
                    acc_add(acc_c, G_CW31 + k, dc1 * xr[rr][32 - 8 * q:32 - 8 * q + CONV_ROWS], sl)
            return carry

        lax.fori_loop(0, tm // CONV_ROWS, conv_b_bwd, 0, unroll=2)
        dc1e[tm:tm + 32, :] = dc1e[0:32, :]
        dproj_ref[...] = dpf[...].astype(BF16)

        @pl.when(i == nt - 1)
        def _():
            def fold(ref, g):
                return jnp.sum(ref[8 * g:8 * g + 8, :], axis=0, keepdims=True)

            dcb4_ref[...] = fold(acc_w, G_CB4)
            for k in range(LRU_K):
                dcw4_ref[k:k + 1, :] = fold(acc_w, G_CW4 + k)
            dba_ref[...] = fold(acc_w, G_BA)
            dbx_ref[...] = fold(acc_w, G_BX)
            dlam_ref[...] = fold(acc_w, G_SP) * (-_sigmoid(-lam_ref[...]))
            dga_ref[...] = fold(acc_w, G_GA)
            dcb31_ref[...] = fold(acc_c, G_CB31)
            dlng_ref[...] = fold(acc_c, G_LNG)
            dlnb_ref[...] = fold(acc_c, G_LNB)
            dgb_ref[...] = fold(acc_c, G_GB)
            for k in range(CONV_K):
                dcw31_ref[k:k + 1, :] = fold(acc_c, G_CW31 + k)

    consts = [p[k] for k in MIX_PARAMS]
    gshapes = [(heads, LANES, LANES) if k in ('lru_wa', 'lru_wx') else p[k].shape for k in MIX_PARAMS]
    outs = _pcall(body, name=name, grid=(nt,),
                  in_specs=[_rrow(tm, w + cw, nt), _rrow(tm, 2 * w + 2 * cw, nt), _halo(48, 2 * w + 2 * cw, tm, nt),
                            _rrow(tm, w, nt), _halo(8, w, tm, nt), _rrow(tm, cw, nt)] + [_const(c.shape) for c in consts],
                  out_specs=[_rrow(tm, 2 * w + 2 * cw, nt)] + [_const_out(s) for s in gshapes],
                  out_shape=[jax.ShapeDtypeStruct((t, 2 * w + 2 * cw), BF16)]
                  + [jax.ShapeDtypeStruct(s, F32) for s in gshapes],
                  scratch_shapes=[pltpu.VMEM((8 + tm, w), F32), pltpu.VMEM((8 + tm, w), F32), pltpu.VMEM((tm, w), F32),
                                  pltpu.VMEM((tm, w), F32), pltpu.VMEM((tm, w), F32), pltpu.VMEM((tm, w), F32),
                                  pltpu.VMEM((tm, w), F32), pltpu.VMEM((tm + 8, w), F32),
                                  pltpu.VMEM((48 + tm, cw), F32), pltpu.VMEM((tm + 32, cw), F32),
                                  pltpu.VMEM((tm, 2 * w + 2 * cw), F32), pltpu.VMEM((8, w), F32),
                                  pltpu.VMEM((8 * NGW, w), F32), pltpu.VMEM((8 * NGC, cw), F32)],
                  compiler_params=_params())(dy, proj, proj, hs, hs, c1, *consts)
    return outs[0], dict(zip(MIX_PARAMS, outs[1:]))


def _ffn_window(u_ref, halo, c, col):
    if isinstance(c, int) and c == 0:
        return jnp.concatenate([halo[:, col:col + LANES], u_ref[0:16, col:col + LANES]], axis=0)
    return u_ref[pl.ds(pl.multiple_of(c * 16 - 8, 8), 24), col:col + LANES]


def _ffn_conv(win, w3_ref, b3_ref, col):
    sl = slice(col, col + LANES)
    x1 = pltpu.roll(win, 1, 0)[8:24]
    x2 = pltpu.roll(win, 2, 0)[8:24]
    u = w3_ref[2:3, sl] * win[8:24] + w3_ref[1:2, sl] * x1 + w3_ref[0:1, sl] * x2 + b3_ref[:, sl]
    return u, (x2, x1, win[8:24])


def _ffn_act_fwd(u0, w3, b3, name):
    t, f2 = u0.shape
    ff = f2 // 2
    _, tm = _tiles(t)
    nch = tm // 16

    def body(u_ref, w3_ref, b3_ref, act_ref, car):
        @pl.when(pl.program_id(0) == 0)
        def _():
            car[...] = jnp.zeros((8, f2), F32)

        def chunk(c):
            halo = car[...] if isinstance(c, int) else None
            r0 = 0 if isinstance(c, int) else pl.multiple_of(c * 16, 16)
            for j in range(ff // LANES):
                gate, _ = _ffn_conv(_ffn_window(u_ref, halo, c, LANES * j), w3_ref, b3_ref, LANES * j)
                up, _ = _ffn_conv(_ffn_window(u_ref, halo, c, ff + LANES * j), w3_ref, b3_ref, ff + LANES * j)
                act_ref[pl.ds(r0, 16), LANES * j:LANES * (j + 1)] = (_gelu(gate) * up).astype(BF16)

        chunk(0)

        def loop(c, carry):
            chunk(c)
            return carry

        lax.fori_loop(1, nch, loop, 0)
        car[...] = u_ref[tm - 8:tm, :]

    return _pcall(body, name=name, grid=(t // tm,),
                  in_specs=[_row(tm, f2), _const(w3.shape), _const(b3.shape)],
                  out_specs=_row(tm, ff), out_shape=jax.ShapeDtypeStruct((t, ff), BF16),
                  scratch_shapes=[pltpu.VMEM((8, f2), F32)],
                  compiler_params=_params())(u0, w3, b3)


def _ffn_act_bwd(dact, u0, w3, b3, name):
    t, f2 = u0.shape
    ff = f2 // 2
    _, tm = _tiles(t)
    nt = t // tm
    nch = tm // 16

    def body(dact_ref, u_ref, uh_ref, w3_ref, b3_ref, du0_ref, dw3_ref, db3_ref, dub, acc):
        i = pl.program_id(0)

        @pl.when(i == 0)
        def _():
            dub[tm:tm + 8, :] = jnp.zeros((8, f2), F32)
            acc[...] = jnp.zeros((32, f2), F32)

        nf = jnp.where(i < nt - 1, 1.0, 0.0).astype(F32)

        def acc_add(g, val, sl):
            acc[8 * g:8 * g + 8, sl] = acc[8 * g:8 * g + 8, sl] + (val[0:8] + val[8:16])

        def chunk(c):
            halo = uh_ref[...] * nf if isinstance(c, int) else None
            r0 = 0 if isinstance(c, int) else pl.multiple_of(c * 16, 16)
            for j in range(ff // LANES):
                cg, cu = LANES * j, ff + LANES * j
                gate, xg = _ffn_conv(_ffn_window(u_ref, halo, c, cg), w3_ref, b3_ref, cg)
                up, xu = _ffn_conv(_ffn_window(u_ref, halo, c, cu), w3_ref, b3_ref, cu)
                ge, dge = _gelu_and_grad(gate)
                da = dact_ref[pl.ds(r0, 16), cg:cg + LANES]
                for col, du, xs in ((cg, (da * up) * dge, xg), (cu, da * ge, xu)):
                    sl = slice(col, col + LANES)
                    dub[pl.ds(r0, 16), sl] = du
                    acc_add(0, du, sl)
                    for k in range(FFN_K):
                        acc_add(1 + k, du * xs[k], sl)

        chunk(0)

        def loop1(c, carry):
            chunk(c)
            return carry

        lax.fori_loop(1, nch, loop1, 0)

        def loop2(c, carry):
            r0 = pl.multiple_of(c * 16, 16)
            for j in range(f2 // LANES):
                sl = slice(LANES * j, LANES * (j + 1))
                win = dub[pl.ds(r0, 24), sl]
                du0 = (w3_ref[2:3, sl] * win[0:16] + w3_ref[1:2, sl] * pltpu.roll(win, 23, 0)[0:16]
                       + w3_ref[0:1, sl] * pltpu.roll(win, 22, 0)[0:16])
                du0_ref[pl.ds(r0, 16), sl] = du0.astype(BF16)
            return carry

        lax.fori_loop(0, nch, loop2, 0)
        dub[tm:tm + 8, :] = dub[0:8, :]

        @pl.when(i == nt - 1)
        def _():
            db3_ref[...] = jnp.sum(acc[0:8, :], axis=0, keepdims=True)
            for k in range(FFN_K):
                dw3_ref[k:k + 1, :] = jnp.sum(acc[8 + 8 * k:16 + 8 * k, :], axis=0, keepdims=True)

    return _pcall(body, name=name, grid=(nt,),
                  in_specs=[_rrow(tm, ff, nt), _rrow(tm, f2, nt), _halo(8, f2, tm, nt), _const(w3.shape),
                            _const(b3.shape)],
                  out_specs=[_rrow(tm, f2, nt), _const_out(w3.shape), _const_out(b3.shape)],
                  out_shape=[jax.ShapeDtypeStruct((t, f2), BF16), jax.ShapeDtypeStruct(w3.shape, F32),
                             jax.ShapeDtypeStruct(b3.shape, F32)],
                  scratch_shapes=[pltpu.VMEM((tm + 8, f2), F32), pltpu.VMEM((32, f2), F32)],
                  compiler_params=_params())(dact, u0, u0, w3, b3)


def _loss_head(h, tgt, name):
    t, d = h.shape
    tm, _ = _tiles(t)

    def body(h_ref, t_ref, dh_ref, s_ref):
        i = pl.program_id(0)

        @pl.when(i == 0)
        def _():
            s_ref[...] = jnp.zeros((1, d), F32)

        row = lax.broadcasted_iota(jnp.int32, (tm, d), 0) + i * tm
        err = jnp.where(row >= N_META, h_ref[...] - t_ref[...], 0.0)
        dh_ref[...] = err / d
        s_ref[...] += jnp.sum(err * err, axis=0, keepdims=True)

    return _pcall(body, name=name, grid=(t // tm,), in_specs=[_row(tm, d), _row(tm, d)],
                  out_specs=[_row(tm, d), _const_out((1, d))],
                  out_shape=[jax.ShapeDtypeStruct((t, d), F32), jax.ShapeDtypeStruct((1, d), F32)],
                  compiler_params=_params())(h, tgt)


def _row_tile(rows, row_bytes, budget):
    best = None
    for tr in range(16, rows + 1, 16):
        if rows % tr == 0 and tr * row_bytes <= budget:
            best = tr
    assert best is not None, (rows, row_bytes)
    return best


def _cast_bf16(a, name):
    r, c = a.shape
    tr = _row_tile(r, c * 4, 4 << 20)

    def body(a_ref, o_ref):
        o_ref[...] = a_ref[...].astype(BF16)

    return _pcall(body, name=name, grid=(r // tr,), in_specs=[_row(tr, c)], out_specs=_row(tr, c),
                  out_shape=jax.ShapeDtypeStruct((r, c), BF16), compiler_params=_params())(a)


def _sum_slots(r, name):
    s, rows, c = r.shape
    tr = _row_tile(rows, s * c * 4, 8 << 20)

    def body(r_ref, o_ref):
        acc = r_ref[0].astype(F32)
        for k in range(1, s):
            acc = acc + r_ref[k].astype(F32)
        o_ref[...] = acc

    return _pcall(body, name=name, grid=(rows // tr,),
                  in_specs=[pl.BlockSpec((s, tr, c), lambda i: (0, i, 0))], out_specs=_row(tr, c),
                  out_shape=jax.ShapeDtypeStruct((rows, c), F32), compiler_params=_params())(r)


def _cast_into_window(a, axis, k1, l0, l, name):
    _, r, c = a.shape
    shape = (l, 4 * r, c) if axis == 1 else (l, r, 4 * c)

    def body(k_ref, a_ref, o_ref):
        o_ref[...] = a_ref[...].astype(BF16)

    omap = (lambda i, k: (i, k[0], 0)) if axis == 1 else (lambda i, k: (i, 0, k[0]))
    gs = pltpu.PrefetchScalarGridSpec(num_scalar_prefetch=1, grid=(l,),
                                      in_specs=[pl.BlockSpec((1, r, c), lambda i, k: (i + l0, 0, 0))],
                                      out_specs=pl.BlockSpec((1, r, c), omap))
    return _pcall(body, name=name, grid_spec=gs, out_shape=jax.ShapeDtypeStruct(shape, BF16),
                  compiler_params=_params())(k1, a)


def _adamw(g, w, m, v, name):
    r, c = g.shape
    tr = _row_tile(r, c * 4, 1 << 20)

    def body(g_ref, w_ref, m_ref, v_ref, d_ref, m2_ref, v2_ref):
        gv = g_ref[...]
        m2 = ADAM_B1 * m_ref[...] + (1.0 - ADAM_B1) * gv
        v2 = ADAM_B2 * v_ref[...] + (1.0 - ADAM_B2) * (gv * gv)
        m_hat = m2 / (1.0 - ADAM_B1 ** ADAM_STEP)
        v_hat = v2 / (1.0 - ADAM_B2 ** ADAM_STEP)
        d_ref[...] = -ADAM_LR * (m_hat / (jnp.sqrt(v_hat) + ADAM_EPS) + ADAM_WD * w_ref[...])
        m2_ref[...] = m2
        v2_ref[...] = v2

    return _pcall(body, name=name, grid=(r // tr,), in_specs=[_row(tr, c)] * 4, out_specs=[_row(tr, c)] * 3,
                  out_shape=[jax.ShapeDtypeStruct((r, c), F32)] * 3, compiler_params=_params())(g, w, m, v)


ANY = pl.BlockSpec(memory_space=pl.ANY)


def _coords():
    return lax.axis_index("x"), lax.axis_index("y"), lax.axis_index("c")


def _window(ref, lead, axis, k, width):
    idx = [slice(None)] * len(ref.shape)
    idx[0] = lead
    idx[axis] = pl.ds(pl.multiple_of(k * width, LANES if axis == len(ref.shape) - 1 else 8), width)
    return ref.at[tuple(idx)]


def _gather_xy(arrs, axes, n_inplace, name):
    n = len(arrs)
    out_shape, widths = [], []
    for i, (a, ax) in enumerate(zip(arrs, axes)):
        s = list(a.shape)
        if i < n_inplace:
            widths.append(s[ax] // 4)
        else:
            widths.append(s[ax])
            s[ax] *= 4
        out_shape.append(jax.ShapeDtypeStruct(tuple(s), a.dtype))

    def body(*refs):
        ins, outs = refs[:n], refs[n:2 * n]
        send_sems, recv_sems, loc_sems = refs[2 * n:]
        x, y, c = _coords()
        k_me = 2 * x + y
        chips = [(1 - x, y), (x, 1 - y), (1 - x, 1 - y)]
        sib = (x, y, 1 - c)

        def half(i, which):
            hl = arrs[i].shape[0] // 2
            return pl.ds(which * hl, hl)

        def win(i, kk, which):
            return _window(outs[i], half(i, which), axes[i], kk, widths[i])

        def copy(i, s, src, dst, to):
            return pltpu.make_async_remote_copy(src_ref=src, dst_ref=dst, send_sem=send_sems.at[i, s],
                                                recv_sem=recv_sems.at[i, s], device_id=to, device_id_type=MESH_T)

        locs = []
        for i in range(n_inplace, n):
            lc = pltpu.make_async_copy(ins[i], _window(outs[i], slice(None), axes[i], k_me, widths[i]), loc_sems.at[i])
            lc.start()
            locs.append(lc)
        started = []
        for i in range(n):
            for j, chip in enumerate(chips):
                src = win(i, k_me, c) if i < n_inplace else ins[i].at[half(i, c)]
                cp = copy(i, j, src, win(i, k_me, c), (*chip, c))
                cp.start()
                started.append(cp)
        for i in range(n):
            for j, chip in enumerate(chips):
                kk = 2 * chip[0] + chip[1]
                copy(i, j, win(i, kk, c), win(i, kk, c), (*chip, c)).wait_recv()
                fw = copy(i, 3 + j, win(i, kk, c), win(i, kk, c), sib)
                fw.start()
                started.append(fw)
        for i in range(n):
            for j, chip in enumerate(chips):
                kk = 2 * chip[0] + chip[1]
                copy(i, 3 + j, win(i, kk, 1 - c), win(i, kk, 1 - c), sib).wait_recv()
        for cp in started:
            cp.wait_send()
        for lc in locs:
            lc.wait()

    return _pcall(body, name=name, in_specs=[ANY] * n, out_specs=[ANY] * n, out_shape=out_shape,
                  input_output_aliases={i: i for i in range(n_inplace)},
                  scratch_shapes=[pltpu.SemaphoreType.DMA((n, 6)), pltpu.SemaphoreType.DMA((n, 6)),
                                  pltpu.SemaphoreType.DMA((n,))],
                  compiler_params=pltpu.CompilerParams(has_side_effects=True))(*arrs)


def _peer(x, y, c, mask):
    bx, by, bc = (mask >> 2) & 1, (mask >> 1) & 1, mask & 1
    return (1 - x if bx else x, 1 - y if by else y, 1 - c if bc else c)


HBM = pl.BlockSpec(memory_space=pltpu.HBM)
SEM = pl.BlockSpec(memory_space=pltpu.SEMAPHORE)


def _piece_shape(shape, wa):
    r, c = shape
    return (r // 2, c // 4) if wa == 1 else (r // 8, c)


def _piece(ref, wa, k, h):
    r, c = ref.shape
    if wa == 1:
        return ref.at[pl.ds(pl.multiple_of(h * (r // 2), 16), r // 2), pl.ds(pl.multiple_of(k * (c // 4), LANES), c // 4)]
    return ref.at[pl.ds(pl.multiple_of((2 * k + h) * (r // 8), 16), r // 8), :]


def _scatter_start(grads, was, name):
    n = len(grads)
    lands = [lax.empty((7, *_piece_shape(g.shape, wa)), g.dtype) for g, wa in zip(grads, was)]

    def body(*refs):
        g_in, land_in = refs[:n], refs[n:2 * n]
        send_sems, recv_sems = refs[2 * n:2 * n + 7 * n], refs[2 * n + 7 * n:2 * n + 14 * n]
        token = refs[-1]
        x, y, c = _coords()
        for i in range(n):
            for mask in range(1, 8):
                px, py, pc = _peer(x, y, c, mask)
                pltpu.make_async_remote_copy(src_ref=_piece(g_in[i], was[i], 2 * px + py, pc),
                                             dst_ref=land_in[i].at[mask - 1], send_sem=send_sems[7 * i + mask - 1],
                                             recv_sem=recv_sems[7 * i + mask - 1], device_id=(px, py, pc),
                                             device_id_type=MESH_T).start()
        token[...] = jnp.zeros(token.shape, F32)

    args = [pltpu.with_memory_space_constraint(a, pltpu.HBM) for a in list(grads) + lands]
    res = _pcall(body, name=name, in_specs=[HBM] * (2 * n),
                 out_specs=[SEM] * (14 * n) + [HBM] * (2 * n) + [pl.BlockSpec(memory_space=pltpu.VMEM)],
                 out_shape=[pltpu.SemaphoreType.DMA(())] * (14 * n)
                 + [pltpu.HBM(a.shape, a.dtype) for a in args] + [jax.ShapeDtypeStruct((8, LANES), F32)],
                 input_output_aliases={i: 14 * n + i for i in range(2 * n)},
                 compiler_params=pltpu.CompilerParams(has_side_effects=pltpu.SideEffectType.DATAFLOW_SIDE_EFFECTING))(*args)
    return res[:7 * n], res[7 * n:14 * n], res[14 * n:15 * n], res[15 * n:16 * n], res[-1]


def _scatter_wait(send_sems, recv_sems, grads, lands, was, after, name):
    n = len(grads)

    def body(*refs):
        g_in, land_in = refs[:n], refs[n:2 * n]
        s_sems, r_sems = refs[2 * n:2 * n + 7 * n], refs[2 * n + 7 * n:2 * n + 14 * n]
        x, y, c = _coords()
        for i in range(n):
            for mask in range(1, 8):
                px, py, pc = _peer(x, y, c, mask)
                cp = pltpu.make_async_remote_copy(src_ref=_piece(g_in[i], was[i], 2 * px + py, pc),
                                                  dst_ref=land_in[i].at[mask - 1], send_sem=s_sems[7 * i + mask - 1],
                                                  recv_sem=r_sems[7 * i + mask - 1], device_id=(px, py, pc),
                                                  device_id_type=MESH_T)
                cp.wait_send()
                cp.wait_recv()

    args = list(grads) + list(lands)
    res = _pcall(body, name=name, in_specs=[HBM] * (2 * n) + [SEM] * (14 * n) + [ANY], out_specs=[HBM] * (2 * n),
                 out_shape=[pltpu.HBM(a.shape, a.dtype) for a in args],
                 input_output_aliases={i: i for i in range(2 * n)},
                 compiler_params=pltpu.CompilerParams(has_side_effects=pltpu.SideEffectType.DATAFLOW_SIDE_EFFECTING))(
                     *args, *send_sems, *recv_sems, after)
    return res[:n], res[n:]


def _chip_window(ref, wa, k):
    r, c = ref.shape
    if wa == 1:
        return ref.at[:, pl.ds(pl.multiple_of(k * (c // 4), LANES), c // 4)]
    return ref.at[pl.ds(pl.multiple_of(k * (r // 4), 16), r // 4), :]


def _gather_copies(bufs, was, send_sems, recv_sems):
    x, y, c = _coords()
    out = []
    for i, (buf, wa) in enumerate(zip(bufs, was)):
        for j, chip in enumerate([(1 - x, y), (x, 1 - y), (1 - x, 1 - y)]):
            def copy(k, i=i, j=j, chip=chip, buf=buf, wa=wa):
                win = _chip_window(buf, wa, k)
                return pltpu.make_async_remote_copy(src_ref=win, dst_ref=win, send_sem=send_sems[3 * i + j],
                                                    recv_sem=recv_sems[3 * i + j], device_id=(*chip, c),
                                                    device_id_type=MESH_T)
            out.append((copy(2 * x + y), copy(2 * chip[0] + chip[1])))
    return out


def _gather_start(bufs, was, name):
    n = len(bufs)
    ns = 3 * n

    def body(*refs):
        for mine, _ in _gather_copies(refs[:n], was, refs[n:n + ns], refs[n + ns:n + 2 * ns]):
            mine.start()
        refs[-1][...] = jnp.zeros(refs[-1].shape, F32)

    args = [pltpu.with_memory_space_constraint(a, pltpu.HBM) for a in bufs]
    res = _pcall(body, name=name, in_specs=[HBM] * n,
                 out_specs=[SEM] * (2 * ns) + [HBM] * n + [pl.BlockSpec(memory_space=pltpu.VMEM)],
                 out_shape=[pltpu.SemaphoreType.DMA(())] * (2 * ns) + [pltpu.HBM(a.shape, a.dtype) for a in args]
                 + [jax.ShapeDtypeStruct((8, LANES), F32)],
                 input_output_aliases={i: 2 * ns + i for i in range(n)},
                 compiler_params=pltpu.CompilerParams(has_side_effects=pltpu.SideEffectType.DATAFLOW_SIDE_EFFECTING))(*args)
    return res[:ns], res[ns:2 * ns], res[2 * ns:2 * ns + n], res[-1]


def _gather_wait(send_sems, recv_sems, bufs, was, after, name):
    n = len(bufs)
    ns = 3 * n

    def body(*refs):
        for mine, theirs in _gather_copies(refs[:n], was, refs[n:n + ns], refs[n + ns:n + 2 * ns]):
            mine.wait_send()
            theirs.wait_recv()

    return _pcall(body, name=name, in_specs=[HBM] * n + [SEM] * (2 * ns) + [ANY], out_specs=[HBM] * n,
                  out_shape=[pltpu.HBM(a.shape, a.dtype) for a in bufs], input_output_aliases={i: i for i in range(n)},
                  compiler_params=pltpu.CompilerParams(has_side_effects=pltpu.SideEffectType.DATAFLOW_SIDE_EFFECTING))(
                      *bufs, *send_sems, *recv_sems, after)


def _sum_pieces(land, g, wa, k1, c1, name):
    s, rp, cp = land.shape
    tr = _row_tile(rp, (s + 1) * cp * 4, 8 << 20)
    nb = rp // tr
    if wa == 1:
        own = pl.BlockSpec((tr, cp), lambda i, k, c: (c[0] * nb + i, k[0]))
    else:
        own = pl.BlockSpec((tr, cp), lambda i, k, c: ((2 * k[0] + c[0]) * nb + i, 0))

    def body(k_ref, c_ref, l_ref, g_ref, o_ref):
        acc = l_ref[0].astype(F32)
        for j in range(1, s):
            acc = acc + l_ref[j].astype(F32)
        o_ref[...] = acc + g_ref[...].astype(F32)

    gs = pltpu.PrefetchScalarGridSpec(
        num_scalar_prefetch=2, grid=(nb,),
        in_specs=[pl.BlockSpec((s, tr, cp), lambda i, k, c: (0, i, 0)), own],
        out_specs=pl.BlockSpec((tr, cp), lambda i, k, c: (c[0] * nb + i, 0)))
    return _pcall(body, name=name, grid_spec=gs, out_shape=jax.ShapeDtypeStruct((2 * rp, cp), F32),
                  compiler_params=_params())(k1, c1, land, g)


SWAP_CHUNKS = 4


def _sibling_swap(fulls, name):
    n = len(fulls)
    out_shape = [jax.ShapeDtypeStruct(a.shape, a.dtype) for a in fulls]

    def body(*refs):
        outs = refs[n:2 * n]
        send_sems, recv_sems = refs[2 * n:]
        x, y, c = _coords()

        def chunk(i, which, q):
            hl, rc = fulls[i].shape[0] // 2, fulls[i].shape[1] // SWAP_CHUNKS
            ref = outs[i].at[pl.ds(which * hl, hl), pl.ds(q * rc, rc)]
            return pltpu.make_async_remote_copy(src_ref=ref, dst_ref=ref, send_sem=send_sems.at[i, q],
                                                recv_sem=recv_sems.at[i, q], device_id=(x, y, 1 - c),
                                                device_id_type=MESH_T)

        started = []
        for i in range(n):
            for q in range(SWAP_CHUNKS):
                cp = chunk(i, c, q)
                cp.start()
                started.append(cp)
        for cp in started:
            cp.wait_send()
        for i in range(n):
            for q in range(SWAP_CHUNKS):
                chunk(i, 1 - c, q).wait_recv()

    for a in fulls:
        assert a.shape[1] % (8 * SWAP_CHUNKS) == 0, a.shape
    return _pcall(body, name=name, in_specs=[ANY] * n, out_specs=[ANY] * n, out_shape=out_shape,
                  input_output_aliases={i: i for i in range(n)},
                  scratch_shapes=[pltpu.SemaphoreType.DMA((n, SWAP_CHUNKS)), pltpu.SemaphoreType.DMA((n, SWAP_CHUNKS))],
                  compiler_params=pltpu.CompilerParams(has_side_effects=True))(*fulls)


def _scatter8(pk, name):
    r, cdim = pk.shape
    pr = r // 8
    assert pr % 8 == 0

    def body(p_ref, o_ref, send_sems, recv_sems, loc_sem):
        x, y, c = _coords()

        def piece(px, py, pc):
            return p_ref.at[pl.ds(pl.multiple_of((4 * px + 2 * py + pc) * pr, 8), pr)]

        lc = pltpu.make_async_copy(piece(x, y, c), o_ref.at[7], loc_sem)
        lc.start()
        started = []
        for mask in range(1, 8):
            px, py, pc = _peer(x, y, c, mask)
            cp = pltpu.make_async_remote_copy(src_ref=piece(px, py, pc), dst_ref=o_ref.at[mask - 1],
                                              send_sem=send_sems.at[mask - 1], recv_sem=recv_sems.at[mask - 1],
                                              device_id=(px, py, pc), device_id_type=MESH_T)
            cp.start()
            started.append(cp)
        for cp in started:
            cp.wait()
        lc.wait()

    return _pcall(body, name=name, in_specs=[ANY], out_specs=ANY, out_shape=jax.ShapeDtypeStruct((8, pr, cdim), F32),
                  scratch_shapes=[pltpu.SemaphoreType.DMA((7,)), pltpu.SemaphoreType.DMA((7,)),
                                  pltpu.SemaphoreType.DMA],
                  compiler_params=pltpu.CompilerParams(has_side_effects=True))(pk)


def _gather_all(pk, name):
    r, cdim = pk.shape

    def body(p_ref, o_ref, send_sems, recv_sems, loc_sem):
        x, y, c = _coords()
        lc = pltpu.make_async_copy(p_ref, o_ref.at[4 * x + 2 * y + c], loc_sem)
        lc.start()
        started = []
        for mask in range(1, 8):
            px, py, pc = _peer(x, y, c, mask)
            cp = pltpu.make_async_remote_copy(src_ref=p_ref, dst_ref=o_ref.at[4 * x + 2 * y + c],
                                              send_sem=send_sems.at[mask - 1], recv_sem=recv_sems.at[mask - 1],
                                              device_id=(px, py, pc), device_id_type=MESH_T)
            cp.start()
            started.append(cp)
        for cp in started:
            cp.wait()
        lc.wait()

    return _pcall(body, name=name, in_specs=[ANY], out_specs=ANY, out_shape=jax.ShapeDtypeStruct((8, r, cdim), F32),
                  scratch_shapes=[pltpu.SemaphoreType.DMA((7,)), pltpu.SemaphoreType.DMA((7,)),
                                  pltpu.SemaphoreType.DMA],
                  compiler_params=pltpu.CompilerParams(has_side_effects=True))(pk)


PACK_C = 1024


def _pack(arrs, row_mult):
    parts = []
    for a in arrs:
        flat = a.reshape(-1)
        parts.append(jnp.pad(flat, (0, (-flat.shape[0]) % PACK_C)))
    flat = jnp.concatenate(parts)
    flat = jnp.pad(flat, (0, (-flat.shape[0]) % (PACK_C * row_mult)))
    return flat.reshape(-1, PACK_C)


def _unpack(pk, shapes):
    flat = pk.reshape(-1)
    out, off = [], 0
    for s in shapes:
        size = 1
        for dd in s:
            size *= dd
        out.append(flat[off:off + size].reshape(s))
        off += size + (-size) % PACK_C
    return out


def _layer_params(full, l):
    p = {}
    for k in ['g_pre_mix', 'lru_conv_b', 'lru_ba', 'lru_bx', 'lru_lambda', 'conv_b', 'conv_ln_g', 'conv_ln_b', 'g_out_lru',
              'g_out_conv', 'g_post_mix', 'g_pre_ffn', 'ffn_conv_b', 'g_post_ffn']:
        p[k] = full[k][l][None, :]
    for k in ['lru_conv_w', 'conv_w', 'ffn_conv_w'] + [k for k in BIG if k in full]:
        p[k] = full[k][l]
    p['lru_wa'] = _pair_heads(full['lru_wa_bf'][l])
    p['lru_wx'] = _pair_heads(full['lru_wx_bf'][l])
    return p


def _step(x, loss_target, w, m, v):
    depth = w['w_in'].shape[0]
    d = x.shape[2]
    xk, yk, _ = _coords()
    k_me = 2 * xk + yk

    k1 = jnp.reshape(k_me, (1,)).astype(jnp.int32)
    big_bf = {k: _cast_into_window(w[k], BIG_AXIS[k], k1, 0, SYNC_LAYERS, "cast_" + k) for k in BIG}
    sh_pad = [w['meta_tokens']] + [jnp.pad(w[k], ((0, 0), (0, (-w[k].shape[1]) % 8), (0, 0))) for k in SH_SMALL[1:]]
    gath = _gather_xy([big_bf[k] for k in BIG] + sh_pad, [BIG_AXIS[k] for k in BIG] + [1, 2, 2, 2], len(BIG),
                      "gather_weights")
    full = {k: a for k, a in w.items() if k not in BIG}
    early = dict(zip(BIG, gath[:len(BIG)]))
    was = [BIG_AXIS[k] - 1 for k in BIG]
    in_flight = {}
    for l in range(SYNC_LAYERS, depth):
        bufs = [_cast_into_window(w[k], BIG_AXIS[k], k1, l, 1, "cast_%s_%d" % (k, l))[0] for k in BIG]
        send_sems, recv_sems, bufs, token = _gather_start(bufs, was, "gather_start_%d" % l)
        in_flight[l] = (send_sems, recv_sems, bufs)
        full['g_pre_mix'] = full['g_pre_mix'] + token[0, 0]

    def big_weights(l, h_in):
        if l < SYNC_LAYERS:
            return {k: early[k][l] for k in BIG}
        send_sems, recv_sems, bufs = in_flight[l]
        return dict(zip(BIG, _gather_wait(send_sems, recv_sems, bufs, was, h_in, "gather_wait_%d" % l)))

    full['meta_tokens'] = gath[len(BIG)]
    for k, a in zip(SH_SMALL[1:], gath[len(BIG) + 1:]):
        full[k] = a[:, :w[k].shape[1]]
    for k in ('lru_wa', 'lru_wx'):
        full[k + '_bf'] = _cast_bf16(w[k].reshape(-1, LANES), "cast_" + k).reshape(w[k].shape)

    h = jnp.concatenate([full['meta_tokens'], x[0]], axis=0)
    tgt = jnp.pad(loss_target[0], ((N_META, 0), (0, 0)))
    pending = {}

    def send_big_grads(l, keys, g):
        res = _scatter_start([g[k] for k in keys], [BIG_AXIS[k] - 1 for k in keys], "scatter_start_%d_%s" % (l, keys[-1]))
        pending[(l, keys)] = res[:4]
        return res[4]

    sq, dh, gl = _fwd_bwd(h, tgt, full, depth, send_big_grads, big_weights)
    loss = lax.psum(0.5 * jnp.sum(sq) / d, ("x", "y", "c"))
    return _reduce_update(loss, dh, gl, pending, w, m, v, depth, k_me)


def _fwd_bwd(h, tgt, full, depth, on_grads=None, big_weights=None):
    saved = []
    for l in range(depth):
        p = _layer_params(full, l)
        if big_weights is not None:
            p.update(big_weights(l, h))
        proj, zb1 = _rms_matmul(h, p['g_pre_mix'], p['w_in'], "in_proj")
        y, hs, c1 = _mix_fwd(proj, p, "mix_fwd")
        o, h1 = _matmul_rms_res(y, p['w_out'], h, p['g_post_mix'], "out_proj")
        u0, zb2 = _rms_matmul(h1, p['g_pre_ffn'], p['w_up'], "up_proj")
        act = _ffn_act_fwd(u0, p['ffn_conv_w'], p['ffn_conv_b'], "ffn_act_fwd")
        f, h2 = _matmul_rms_res(act, p['w_down'], h1, p['g_post_ffn'], "down_proj")
        saved.append((p, h, zb1, proj, y, hs, c1, o, h1, zb2, u0, act, f))
        h = h2

    dh, sq = _loss_head(h, tgt, "loss_head")

    def tied(a, token):
        return a if token is None else a + token[0:1, 0:1]

    gl = [None] * depth
    for l in reversed(range(depth)):
        p, h0, zb1, proj, y, hs, c1, o, h1, zb2, u0, act, f = saved[l]
        g = {}
        dact, dfb, g['g_post_ffn'] = _rmsbwd_matmul_nt(f, p['g_post_ffn'], dh, p['w_down'], "down_bwd")
        g['w_down'] = _matmul_tn(act, dfb, "down_dw")
        du0, g['ffn_conv_w'], g['ffn_conv_b'] = _ffn_act_bwd(dact, u0, p['ffn_conv_w'], p['ffn_conv_b'], "ffn_act_bwd")
        g['w_up'] = _matmul_tn(zb2, du0, "up_dw")
        dh1, g['g_pre_ffn'] = _matmul_nt_rmsbwd_res(du0, p['w_up'], h1, p['g_pre_ffn'], dh, "up_bwd")
        dy, dob, g['g_post_mix'] = _rmsbwd_matmul_nt(o, p['g_post_mix'], dh1, p['w_out'], "out_bwd")
        g['w_out'] = _matmul_tn(y, dob, "out_dw")
        token = on_grads(l, ('w_down', 'w_up', 'w_out'), g) if on_grads is not None else None
        pm = dict(p)
        pm['g_out_conv'] = tied(p['g_out_conv'], token)
        dproj, gm = _mix_bwd(dy, proj, hs, c1, pm, "mix_bwd")
        g.update(gm)
        g['w_in'] = _matmul_tn(zb1, dproj, "in_dw")
        token = on_grads(l, ('w_in',), g) if on_grads is not None else None
        dh, g['g_pre_mix'] = _matmul_nt_rmsbwd_res(dproj, p['w_in'], h0, tied(p['g_pre_mix'], token), dh1, "in_bwd")
        gl[l] = g
    return sq, dh, gl


def _reduce_update(loss, dh, gl, pending, w, m, v, depth, k_me):
    grad_x = dh[N_META:][None]

    def stacked(k):
        return jnp.stack([gl[l][k].reshape(w[k].shape[1:]) if k not in SH_SMALL + BIG else gl[l][k]
                          for l in range(depth)])

    k1 = jnp.reshape(k_me, (1,)).astype(jnp.int32)
    c1 = jnp.reshape(lax.axis_index("c"), (1,)).astype(jnp.int32)
    order, halves = [], []
    for (l, keys), (send_sems, recv_sems, g_thru, lands) in pending.items():
        was = [BIG_AXIS[k] - 1 for k in keys]
        g_own, lands = _scatter_wait(send_sems, recv_sems, g_thru, lands, was, dh, "scatter_wait_%d_%s" % (l, keys[-1]))
        for k, wa, land, g_k in zip(keys, was, lands, g_own):
            red = _sum_pieces(land, g_k, wa, k1, c1, "grad_sum")
            order.append((l, k))
            halves.append(red.reshape(2, land.shape[1], land.shape[2]))
    swapped = dict(zip(order, _sibling_swap(halves, "grad_swap")))
    big_red = [jnp.stack([swapped[(l, k)].reshape(w[k].shape[1:]) for l in range(depth)]) for k in BIG]

    out = {}
    for k, gk in zip(BIG, big_red):
        c2 = gk.shape[-1]
        dl, m2, v2 = _adamw(gk.reshape(-1, c2), w[k].reshape(-1, c2), m[k].reshape(-1, c2), v[k].reshape(-1, c2),
                            "adamw_big")
        out[k] = (gk, dl.reshape(gk.shape), m2.reshape(gk.shape), v2.reshape(gk.shape))

    rep_g = [stacked(k) for k in REP_SMALL]
    sh_g = [dh[:N_META]] + [stacked(k) for k in SH_SMALL[1:]]
    n_rep_rows = _pack(rep_g, 1).shape[0]
    pk = jnp.concatenate([_pack(rep_g, 1), _pack(sh_g, 1)])
    pk = jnp.pad(pk, ((0, (-pk.shape[0]) % 256), (0, 0)))
    part = _sum_slots(_scatter8(pk, "small_scatter"), "small_sum")
    red = _gather_all(part, "small_gather").reshape(pk.shape)
    rep_red = _unpack(red[:n_rep_rows], [a.shape for a in rep_g])
    sh_red = []
    for a in _unpack(red[n_rep_rows:], [a.shape for a in sh_g]):
        wd = a.shape[-1] // 4
        sh_red.append(lax.dynamic_slice_in_dim(a, k_me * wd, wd, axis=a.ndim - 1))

    for names, grads_ in ((REP_SMALL, rep_red), (SH_SMALL, sh_red)):
        res = _adamw(_pack(grads_, 16), _pack([w[k] for k in names], 16), _pack([m[k] for k in names], 16),
                     _pack([v[k] for k in names], 16), "adamw_small")
        shapes = [w[k].shape for k in names]
        un = [_unpack(r, shapes) for r in res]
        for j, k in enumerate(names):
            out[k] = (grads_[j].reshape(w[k].shape), un[0][j], un[1][j], un[2][j])

    return (loss, grad_x, *[out[k][0] for k in WEIGHTS], *[out[k][1] for k in WEIGHTS],
            *[out[k][2] for k in WEIGHTS], *[out[k][3] for k in WEIGHTS])


def kernel(x, meta_tokens, g_pre_mix, w_in, lru_conv_w, lru_conv_b, lru_wa, lru_ba, lru_wx, lru_bx, lru_lambda, conv_w, conv_b, conv_ln_g, conv_ln_b, g_out_lru, g_out_conv, w_out, g_post_mix, g_pre_ffn, w_up, ffn_conv_w, ffn_conv_b, w_down, g_post_ffn, loss_target, m_meta_tokens, m_g_pre_mix, m_w_in, m_lru_conv_w, m_lru_conv_b, m_lru_wa, m_lru_ba, m_lru_wx, m_lru_bx, m_lru_lambda, m_conv_w, m_conv_b, m_conv_ln_g, m_conv_ln_b, m_g_out_lru, m_g_out_conv, m_w_out, m_g_post_mix, m_g_pre_ffn, m_w_up, m_ffn_conv_w, m_ffn_conv_b, m_w_down, m_g_post_ffn, v_meta_tokens, v_g_pre_mix, v_w_in, v_lru_conv_w, v_lru_conv_b, v_lru_wa, v_lru_ba, v_lru_wx, v_lru_bx, v_lru_lambda, v_conv_w, v_conv_b, v_conv_ln_g, v_conv_ln_b, v_g_out_lru, v_g_out_conv, v_w_out, v_g_post_mix, v_g_pre_ffn, v_w_up, v_ffn_conv_w, v_ffn_conv_b, v_w_down, v_g_post_ffn):
    w = dict(meta_tokens=meta_tokens, g_pre_mix=g_pre_mix, w_in=w_in, lru_conv_w=lru_conv_w, lru_conv_b=lru_conv_b,
             lru_wa=lru_wa, lru_ba=lru_ba, lru_wx=lru_wx, lru_bx=lru_bx, lru_lambda=lru_lambda, conv_w=conv_w,
             conv_b=conv_b, conv_ln_g=conv_ln_g, conv_ln_b=conv_ln_b, g_out_lru=g_out_lru, g_out_conv=g_out_conv,
             w_out=w_out, g_post_mix=g_post_mix, g_pre_ffn=g_pre_ffn, w_up=w_up, ffn_conv_w=ffn_conv_w,
             ffn_conv_b=ffn_conv_b, w_down=w_down, g_post_ffn=g_post_ffn)
    m = dict(meta_tokens=m_meta_tokens, g_pre_mix=m_g_pre_mix, w_in=m_w_in, lru_conv_w=m_lru_conv_w,
             lru_conv_b=m_lru_conv_b, lru_wa=m_lru_wa, lru_ba=m_lru_ba, lru_wx=m_lru_wx, lru_bx=m_lru_bx,
             lru_lambda=m_lru_lambda, conv_w=m_conv_w, conv_b=m_conv_b, conv_ln_g=m_conv_ln_g, conv_ln_b=m_conv_ln_b,
             g_out_lru=m_g_out_lru, g_out_conv=m_g_out_conv, w_out=m_w_out, g_post_mix=m_g_post_mix,
             g_pre_ffn=m_g_pre_ffn, w_up=m_w_up, ffn_conv_w=m_ffn_conv_w, ffn_conv_b=m_ffn_conv_b, w_down=m_w_down,
             g_post_ffn=m_g_post_ffn)
    v = dict(meta_tokens=v_meta_tokens, g_pre_mix=v_g_pre_mix, w_in=v_w_in, lru_conv_w=v_lru_conv_w,
             lru_conv_b=v_lru_conv_b, lru_wa=v_lru_wa, lru_ba=v_lru_ba, lru_wx=v_lru_wx, lru_bx=v_lru_bx,
             lru_lambda=v_lru_lambda, conv_w=v_conv_w, conv_b=v_conv_b, conv_ln_g=v_conv_ln_g, conv_ln_b=v_conv_ln_b,
             g_out_lru=v_g_out_lru, g_out_conv=v_g_out_conv, w_out=v_w_out, g_post_mix=v_g_post_mix,
             g_pre_ffn=v_g_pre_ffn, w_up=v_w_up, ffn_conv_w=v_ffn_conv_w, ffn_conv_b=v_ffn_conv_b, w_down=v_w_down,
             g_post_ffn=v_g_post_ffn)
    return _step(x, loss_target, w, m, v)
```

```python
import functools

import jax
import jax.numpy as jnp
from jax import lax
from jax.experimental import pallas as pl
from jax.experimental.pallas import tpu as pltpu

F32 = jnp.float32
BF16 = jnp.bfloat16
EPS = 1e-6
N_META = 16
LRU_C = 8.0
CONV_K = 31
LRU_K = 4
FFN_K = 3
CONV_ROWS = 24
TN_ROWS = 912
SYNC_LAYERS = 2
SCAN_U = 3
RSCAN_U = 2
LANES = 128
VMEM_LIMIT = 56 * 1024 * 1024
ADAM_LR, ADAM_B1, ADAM_B2, ADAM_EPS, ADAM_WD, ADAM_STEP = 0.001, 0.9, 0.999, 1e-08, 0.01, 10
MESH_T = pl.DeviceIdType.MESH
NT_DIMS = (((1,), (1,)), ((), ()))
TN_DIMS = (((0,), (0,)), ((), ()))

REP_SMALL = ['g_pre_mix', 'lru_conv_b', 'lru_wa', 'lru_ba', 'lru_wx', 'lru_bx', 'lru_lambda', 'conv_b', 'conv_ln_g',
             'conv_ln_b', 'g_out_lru', 'g_out_conv', 'g_post_mix', 'g_pre_ffn', 'ffn_conv_b', 'g_post_ffn']
SH_SMALL = ['meta_tokens', 'lru_conv_w', 'conv_w', 'ffn_conv_w']
BIG = ['w_in', 'w_out', 'w_up', 'w_down']
BIG_AXIS = {'w_in': 2, 'w_out': 1, 'w_up': 2, 'w_down': 1}
WEIGHTS = ['meta_tokens', 'g_pre_mix', 'w_in', 'lru_conv_w', 'lru_conv_b', 'lru_wa', 'lru_ba', 'lru_wx', 'lru_bx',
           'lru_lambda', 'conv_w', 'conv_b', 'conv_ln_g', 'conv_ln_b', 'g_out_lru', 'g_out_conv', 'w_out',
           'g_post_mix', 'g_pre_ffn', 'w_up', 'ffn_conv_w', 'ffn_conv_b', 'w_down', 'g_post_ffn']


def _pcall(body, **kw):
    return pl.pallas_call(body, **kw)


def _params(n_grid=1):
    return pltpu.CompilerParams(dimension_semantics=("arbitrary",) * n_grid, vmem_limit_bytes=VMEM_LIMIT)


def _tiles(t):
    if t % 432 == 0:
        return 432, 144
    assert t % 48 == 0
    return 48, 48


def _row(tm, n):
    return pl.BlockSpec((tm, n), lambda i: (i, 0))


def _rrow(tm, n, nt):
    return pl.BlockSpec((tm, n), lambda i: (nt - 1 - i, 0))


def _halo(hb, n, tm, nt):
    return pl.BlockSpec((hb, n), lambda i: (jnp.maximum((nt - 1 - i) * (tm // hb) - 1, 0), 0))


def _const(shape):
    nd = len(shape)
    return pl.BlockSpec(shape, lambda *_: (0,) * nd, pipeline_mode=pl.Buffered(1))


def _const_out(shape):
    nd = len(shape)
    return pl.BlockSpec(shape, lambda *_: (0,) * nd)


def _sigmoid(x):
    return 1.0 / (1.0 + jnp.exp(-x))


def _gelu(x):
    return 0.5 * x * (1.0 + jnp.tanh(0.7978845608028654 * (x + 0.044715 * (x * x * x))))


def _gelu_and_grad(x):
    k = 0.7978845608028654
    x2 = x * x
    th = jnp.tanh(k * (x + 0.044715 * (x2 * x)))
    return 0.5 * x * (1.0 + th), 0.5 * (1.0 + th) + 0.5 * x * (1.0 - th * th) * (k * (1.0 + 0.134145 * x2))


def _expm1(x):
    return jnp.where(jnp.abs(x) < 1e-2, x * (1.0 + x * (0.5 + x * (1.0 / 6.0 + x * (1.0 / 24.0)))), jnp.exp(x) - 1.0)


def _softplus(x):
    e = jnp.exp(-jnp.abs(x))
    return jnp.maximum(x, 0.0) + jnp.where(e < 1e-4, e * (1.0 - 0.5 * e), jnp.log(1.0 + e))


def _lru_gates(pa, px, sp):
    r = _sigmoid(pa)
    ig = _sigmoid(px)
    la = (-LRU_C * r) * sp
    return r, ig, jnp.exp(la), jnp.sqrt(-_expm1(2.0 * la))


def _rms(x):
    return lax.rsqrt(jnp.mean(x * x, axis=-1, keepdims=True) + EPS)


def _rms_bwd(x, g, dy):
    r = _rms(x)
    xr = x * r
    dyg = dy * g
    return r * (dyg - xr * jnp.mean(dyg * xr, axis=-1, keepdims=True)), xr


def _col_chunk(n):
    return 1536 if n % 1536 == 0 else 1024


def _rms_matmul(h, g, w, name):
    t, d = h.shape
    n = w.shape[1]
    tm, _ = _tiles(t)
    cn = _col_chunk(n)

    def body(h_ref, g_ref, w_ref, p_ref, zb_ref):
        x = h_ref[...]
        zb = ((x * _rms(x)) * g_ref[...]).astype(BF16)
        zb_ref[...] = zb
        for c in range(n // cn):
            p_ref[:, c * cn:(c + 1) * cn] = jnp.dot(zb, w_ref[:, c * cn:(c + 1) * cn], preferred_element_type=F32)

    return _pcall(body, name=name, grid=(t // tm,),
                  in_specs=[_row(tm, d), _const((1, d)), _const((d, n))],
                  out_specs=[_row(tm, n), _row(tm, d)],
                  out_shape=[jax.ShapeDtypeStruct((t, n), F32), jax.ShapeDtypeStruct((t, d), BF16)],
                  compiler_params=_params())(h, g, w)


def _matmul_rms_res(a, w, h, g, name):
    t, k = a.shape
    d = w.shape[1]
    tm, _ = _tiles(t)

    def body(a_ref, w_ref, h_ref, g_ref, o_ref, hn_ref):
        o = jnp.dot(a_ref[...], w_ref[...], preferred_element_type=F32)
        o_ref[...] = o
        hn_ref[...] = h_ref[...] + (o * _rms(o)) * g_ref[...]

    return _pcall(body, name=name, grid=(t // tm,),
                  in_specs=[_row(tm, k), _const((k, d)), _row(tm, d), _const((1, d))],
                  out_specs=[_row(tm, d), _row(tm, d)],
                  out_shape=[jax.ShapeDtypeStruct((t, d), F32), jax.ShapeDtypeStruct((t, d), F32)],
                  compiler_params=_params())(a, w, h, g)


def _rmsbwd_matmul_nt(x, g, dy, w, name):
    t, d = x.shape
    n = w.shape[0]
    tm, _ = _tiles(t)
    cn = _col_chunk(n)

    def body(x_ref, g_ref, dy_ref, w_ref, da_ref, dxb_ref, dg_ref):
        @pl.when(pl.program_id(0) == 0)
        def _():
            dg_ref[...] = jnp.zeros((1, d), F32)

        dy = dy_ref[...]
        dx, xr = _rms_bwd(x_ref[...], g_ref[...], dy)
        dg_ref[...] += jnp.sum(dy * xr, axis=0, keepdims=True)
        dxb = dx.astype(BF16)
        dxb_ref[...] = dxb
        for c in range(n // cn):
            da_ref[:, c * cn:(c + 1) * cn] = lax.dot_general(dxb, w_ref[c * cn:(c + 1) * cn, :], NT_DIMS,
                                                             preferred_element_type=F32)

    return _pcall(body, name=name, grid=(t // tm,),
                  in_specs=[_row(tm, d), _const((1, d)), _row(tm, d), _const((n, d))],
                  out_specs=[_row(tm, n), _row(tm, d), _const_out((1, d))],
                  out_shape=[jax.ShapeDtypeStruct((t, n), F32), jax.ShapeDtypeStruct((t, d), BF16),
                             jax.ShapeDtypeStruct((1, d), F32)],
                  compiler_params=_params())(x, g, dy, w)


def _matmul_nt_rmsbwd_res(dp, w, h, g, dh, name):
    t, n = dp.shape
    d = w.shape[0]
    tm, _ = _tiles(t)

    def body(dp_ref, w_ref, h_ref, g_ref, dh_ref, out_ref, dg_ref):
        @pl.when(pl.program_id(0) == 0)
        def _():
            dg_ref[...] = jnp.zeros((1, d), F32)

        dz = lax.dot_general(dp_ref[...], w_ref[...], NT_DIMS, preferred_element_type=F32)
        dx, xr = _rms_bwd(h_ref[...], g_ref[...], dz)
        dg_ref[...] += jnp.sum(dz * xr, axis=0, keepdims=True)
        out_ref[...] = dh_ref[...] + dx

    return _pcall(body, name=name, grid=(t // tm,),
                  in_specs=[_row(tm, n), _const((d, n)), _row(tm, d), _const((1, d)), _row(tm, d)],
                  out_specs=[_row(tm, d), _const_out((1, d))],
                  out_shape=[jax.ShapeDtypeStruct((t, d), F32), jax.ShapeDtypeStruct((1, d), F32)],
                  compiler_params=_params())(dp, w, h, g, dh)


def _matmul_tn(a, b, name):
    t, k = a.shape
    n = b.shape[1]
    tm, _ = _tiles(t)
    if t % TN_ROWS == 0:
        tm = TN_ROWS
    nt = t // tm
    bn = min(n, (1536 * 1024) // k)
    assert n % bn == 0 and bn % LANES == 0

    def body(a_ref, b_ref, o_ref, acc):
        @pl.when(pl.program_id(1) == 0)
        def _():
            acc[...] = jnp.zeros((k, bn), F32)

        acc[...] += lax.dot_general(a_ref[...], b_ref[...], TN_DIMS, preferred_element_type=F32)

        @pl.when(pl.program_id(1) == nt - 1)
        def _():
            o_ref[...] = acc[...].astype(BF16)

    return _pcall(body, name=name, grid=(n // bn, nt),
                  in_specs=[pl.BlockSpec((tm, k), lambda j, i: (i, 0)), pl.BlockSpec((tm, bn), lambda j, i: (i, j))],
                  out_specs=pl.BlockSpec((k, bn), lambda j, i: (0, j)),
                  out_shape=jax.ShapeDtypeStruct((k, n), BF16),
                  scratch_shapes=[pltpu.VMEM((k, bn), F32)],
                  compiler_params=_params(2))(a, b)


MIX_PARAMS = ['lru_conv_w', 'lru_conv_b', 'lru_wa', 'lru_ba', 'lru_wx', 'lru_bx', 'lru_lambda', 'conv_w', 'conv_b',
              'conv_ln_g', 'conv_ln_b', 'g_out_lru', 'g_out_conv']


def _pair_heads(a):
    hh, hd, _ = a.shape
    a = a.reshape(hh // 2, 2, hd, hd)
    z = jnp.zeros_like(a[:, 0])
    return jnp.concatenate([jnp.concatenate([a[:, 0], z], axis=-1), jnp.concatenate([z, a[:, 1]], axis=-1)], axis=-2)


def _head_gates(xcb, wa_ref, wx_ref, pa_s, px_s, ba, bx, heads):
    for j in range(heads // 2):
        sl = slice(2 * LANES * j, 2 * LANES * (j + 1))
        pa_s[:, sl] = jnp.dot(xcb[:, sl], wa_ref[j], preferred_element_type=F32) + ba[:, sl]
        px_s[:, sl] = jnp.dot(xcb[:, sl], wx_ref[j], preferred_element_type=F32) + bx[:, sl]


def _lru_conv_chunk(exta, cw4_ref, cb4, r0):
    win = exta[pl.ds(r0, 16), :]
    acc = cw4_ref[3:4, :] * win[8:16]
    for k in range(LRU_K - 1):
        acc = acc + cw4_ref[k:k + 1, :] * pltpu.roll(win, LRU_K - 1 - k, 0)[8:16]
    return acc + cb4


def _mix_fwd(proj, p, name):
    t = proj.shape[0]
    w = p['lru_conv_b'].shape[1]
    cw = p['conv_b'].shape[1]
    heads = w // LANES
    _, tm = _tiles(t)
    nch = tm // 8

    def body(proj_ref, cw4_ref, cb4_ref, wa_ref, ba_ref, wx_ref, bx_ref, lam_ref, cw31_ref, cb31_ref, lng_ref,
             lnb_ref, ga_ref, gb_ref, y_ref, hs_ref, c1_ref, exta, xc_s, pa_s, px_s, extb, nbuf, hcar):
        @pl.when(pl.program_id(0) == 0)
        def _():
            exta[0:8, :] = jnp.zeros((8, w), F32)
            extb[0:32, :] = jnp.zeros((32, cw), F32)
            hcar[...] = jnp.zeros((8, w), F32)

        exta[8:8 + tm, :] = proj_ref[:, 0:w]
        cb4 = cb4_ref[...]

        def conv_a(c, carry):
            r0 = pl.multiple_of(c * 8, 8)
            xc_s[pl.ds(r0, 8), :] = _lru_conv_chunk(exta, cw4_ref, cb4, r0)
            return carry

        lax.fori_loop(0, nch, conv_a, 0, unroll=3)
        _head_gates(xc_s[...].astype(BF16), wa_ref, wx_ref, pa_s, px_s, ba_ref[...], bx_ref[...], heads)

        sp = _softplus(-lam_ref[...])
        ga = ga_ref[...]
        row = lax.broadcasted_iota(jnp.int32, (8, w), 0)

        def scan_c(cg, hprev):
            part = []
            for u in range(SCAN_U):
                r0 = pl.multiple_of((cg * SCAN_U + u) * 8, 8)
                xc = xc_s[pl.ds(r0, 8), :]
                _, ig, a, m = _lru_gates(pa_s[pl.ds(r0, 8), :], px_s[pl.ds(r0, 8), :], sp)
                aa, bb = a, m * (ig * xc)
                for d in (1, 2, 4):
                    a_s = pltpu.roll(aa, d, 0)
                    b_s = pltpu.roll(bb, d, 0)
                    msk = row >= d
                    bb = jnp.where(msk, aa * b_s + bb, bb)
                    aa = jnp.where(msk, aa * a_s, aa)
                part.append((r0, aa, bb, _gelu(proj_ref[pl.ds(r0, 8), w:2 * w])))
            for r0, aa, bb, ge in part:
                hs = aa * hprev + bb
                hs_ref[pl.ds(r0, 8), :] = hs
                ya = hs * ge
                nbuf[pl.ds(r0, 8), 0:w] = (ya * _rms(ya)) * ga
                hprev = hs[7:8, :]
            return hprev

        hcar[0:1, :] = lax.fori_loop(0, nch // SCAN_U, scan_c, hcar[0:1, :])

        extb[32:32 + tm, :] = proj_ref[:, 2 * w:2 * w + cw] * _sigmoid(proj_ref[:, 2 * w + cw:2 * w + 2 * cw])

        def conv_b(c, carry):
            r0 = pl.multiple_of(c * CONV_ROWS, 8)
            ybs = []
            ssq = jnp.zeros((CONV_ROWS, 1), F32)
            for lb in range(cw // LANES):
                sl = slice(LANES * lb, LANES * (lb + 1))
                win = extb[pl.ds(r0, CONV_ROWS + 32), sl]
                rolled = [win] + [pltpu.roll(win, rr, 0) for rr in range(1, 8)]
                parts = [None] * 4
                for k in range(CONV_K):
                    q, rr = divmod(CONV_K - 1 - k, 8)
                    term = cw31_ref[k:k + 1, sl] * rolled[rr][32 - 8 * q:32 - 8 * q + CONV_ROWS]
                    parts[k % 4] = term if parts[k % 4] is None else parts[k % 4] + term
                acc = ((parts[0] + parts[1]) + (parts[2] + parts[3])) + cb31_ref[:, sl]
                c1_ref[pl.ds(r0, CONV_ROWS), sl] = acc
                dlt = acc - jnp.mean(acc, axis=-1, keepdims=True)
                c2 = dlt * lax.rsqrt(jnp.mean(dlt * dlt, axis=-1, keepdims=True) + EPS)
                yb0 = c2 * lng_ref[:, sl] + lnb_ref[:, sl]
                yb = yb0 * _sigmoid(yb0)
                ybs.append(yb)
                ssq = ssq + jnp.sum(yb * yb, axis=-1, keepdims=True)
            rb = lax.rsqrt(ssq / cw + EPS)
            for lb in range(cw // LANES):
                sl = slice(LANES * lb, LANES * (lb + 1))
                nbuf[pl.ds(r0, CONV_ROWS), w + LANES * lb:w + LANES * (lb + 1)] = (ybs[lb] * rb) * gb_ref[:, sl]
            return carry

        lax.fori_loop(0, tm // CONV_ROWS, conv_b, 0, unroll=2)
        exta[0:8, :] = exta[tm:tm + 8, :]
        extb[0:32, :] = extb[tm:tm + 32, :]
        y_ref[...] = nbuf[...].astype(BF16)

    consts = [p[k] for k in MIX_PARAMS]
    return _pcall(body, name=name, grid=(t // tm,),
                  in_specs=[_row(tm, 2 * w + 2 * cw)] + [_const(c.shape) for c in consts],
                  out_specs=[_row(tm, w + cw), _row(tm, w), _row(tm, cw)],
                  out_shape=[jax.ShapeDtypeStruct((t, w + cw), BF16), jax.ShapeDtypeStruct((t, w), F32),
                             jax.ShapeDtypeStruct((t, cw), F32)],
                  scratch_shapes=[pltpu.VMEM((8 + tm, w), F32), pltpu.VMEM((tm, w), F32), pltpu.VMEM((tm, w), F32),
                                  pltpu.VMEM((tm, w), F32), pltpu.VMEM((32 + tm, cw), F32),
                                  pltpu.VMEM((tm, w + cw), F32), pltpu.VMEM((8, w), F32)],
                  compiler_params=_params())(proj, *consts)


def _mix_bwd(dy, proj, hs, c1, p, name):
    t = proj.shape[0]
    w = p['lru_conv_b'].shape[1]
    cw = p['conv_b'].shape[1]
    heads = w // LANES
    _, tm = _tiles(t)
    nt = t // tm
    nch = tm // 8
    nlb = cw // LANES
    G_CB4, G_CW4, G_BA, G_BX, G_SP, G_GA, NGW = 0, 1, 5, 6, 7, 8, 9
    G_CB31, G_LNG, G_LNB, G_GB, G_CW31, NGC = 0, 1, 2, 3, 4, 4 + CONV_K

    def body(dy_ref, proj_ref, projh_ref, hs_ref, hsh_ref, c1_ref, cw4_ref, cb4_ref, wa_ref, ba_ref, wx_ref, bx_ref,
             lam_ref, cw31_ref, cb31_ref, lng_ref, lnb_ref, ga_ref, gb_ref,
             dproj_ref, dcw4_ref, dcb4_ref, dwa_ref, dba_ref, dwx_ref, dbx_ref, dlam_ref, dcw31_ref, dcb31_ref,
             dlng_ref, dlnb_ref, dga_ref, dgb_ref,
             exta, exth, xc_s, pa_s, px_s, dpa_s, dpx_s, dxce, extb, dc1e, dpf, cp_s, acc_w, acc_c):
        i = pl.program_id(0)

        @pl.when(i == 0)
        def _():
            acc_w[...] = jnp.zeros((8 * NGW, w), F32)
            acc_c[...] = jnp.zeros((8 * NGC, cw), F32)
            dwa_ref[...] = jnp.zeros(dwa_ref.shape, F32)
            dwx_ref[...] = jnp.zeros(dwx_ref.shape, F32)
            cp_s[...] = jnp.zeros((8, w), F32)
            dxce[tm:tm + 8, :] = jnp.zeros((8, w), F32)
            dc1e[tm:tm + 32, :] = jnp.zeros((32, cw), F32)

        nf = jnp.where(i < nt - 1, 1.0, 0.0).astype(F32)
        exta[0:8, :] = projh_ref[40:48, 0:w] * nf
        exta[8:8 + tm, :] = proj_ref[:, 0:w]
        exth[0:8, :] = hsh_ref[...] * nf
        exth[8:8 + tm, :] = hs_ref[...]
        extb[0:48, :] = (projh_ref[:, 2 * w:2 * w + cw] * _sigmoid(projh_ref[:, 2 * w + cw:2 * w + 2 * cw])) * nf
        extb[48:48 + tm, :] = proj_ref[:, 2 * w:2 * w + cw] * _sigmoid(proj_ref[:, 2 * w + cw:2 * w + 2 * cw])
        cb4 = cb4_ref[...]

        def conv_a(c, carry):
            r0 = pl.multiple_of(c * 8, 8)
            xc_s[pl.ds(r0, 8), :] = _lru_conv_chunk(exta, cw4_ref, cb4, r0)
            return carry

        lax.fori_loop(0, nch, conv_a, 0, unroll=3)
        xcb = xc_s[...].astype(BF16)
        _head_gates(xcb, wa_ref, wx_ref, pa_s, px_s, ba_ref[...], bx_ref[...], heads)

        sp = _softplus(-lam_ref[...])
        ga = ga_ref[...]
        row = lax.broadcasted_iota(jnp.int32, (8, w), 0)

        def acc_add(ref, g, val, sl=slice(None)):
            for j in range(val.shape[0] // 8):
                ref[8 * g:8 * g + 8, sl] = ref[8 * g:8 * g + 8, sl] + val[8 * j:8 * j + 8]

        def rscan(cg, cp):
            part = []
            for u in range(RSCAN_U):
                r0 = pl.multiple_of((nch - 1 - (cg * RSCAN_U + u)) * 8, 8)
                xc = xc_s[pl.ds(r0, 8), :]
                r, ig, a, m = _lru_gates(pa_s[pl.ds(r0, 8), :], px_s[pl.ds(r0, 8), :], sp)
                hwin = exth[pl.ds(r0, 16), :]
                hcur = hwin[8:16]
                hprev = pltpu.roll(hwin, 1, 0)[8:16]
                ge, dge = _gelu_and_grad(proj_ref[pl.ds(r0, 8), w:2 * w])
                dna = dy_ref[pl.ds(r0, 8), 0:w]
                dya, yar = _rms_bwd(hcur * ge, ga, dna)
                acc_add(acc_w, G_GA, dna * yar)
                dpf[pl.ds(r0, 8), w:2 * w] = (dya * hcur) * dge
                aa = jnp.where(row == 7, 1.0, pltpu.roll(a, 7, 0))
                bb = dya * ge
                for d in (1, 2, 4):
                    a_s = pltpu.roll(aa, 8 - d, 0)
                    b_s = pltpu.roll(bb, 8 - d, 0)
                    msk = row < 8 - d
                    bb = jnp.where(msk, aa * b_s + bb, bb)
                    aa = jnp.where(msk, aa * a_s, aa)
                part.append((r0, aa, bb, xc, r, ig, a, m, hprev))
            for r0, aa, bb, xc, r, ig, a, m, hprev in part:
                lamb = bb + aa * cp
                cp = a[0:1, :] * lamb[0:1, :]
                dm = lamb * (ig * xc)
                di = lamb * (m * xc)
                dxce[pl.ds(r0, 8), :] = lamb * (m * ig)
                dla = a * (lamb * hprev - dm * (a / m))
                acc_add(acc_w, G_SP, dla * (-LRU_C * r))
                dpa = (dla * (-LRU_C * sp)) * (r * (1.0 - r))
                dpx = di * (ig * (1.0 - ig))
                acc_add(acc_w, G_BA, dpa)
                acc_add(acc_w, G_BX, dpx)
                dpa_s[pl.ds(r0, 8), :] = dpa
                dpx_s[pl.ds(r0, 8), :] = dpx
            return cp

        cp_s[0:1, :] = lax.fori_loop(0, nch // RSCAN_U, rscan, cp_s[0:1, :])

        dpab = dpa_s[...].astype(BF16)
        dpxb = dpx_s[...].astype(BF16)
        for j in range(heads // 2):
            sl = slice(2 * LANES * j, 2 * LANES * (j + 1))
            dxce[0:tm, sl] = (dxce[0:tm, sl]
                              + lax.dot_general(dpab[:, sl], wa_ref[j], NT_DIMS, preferred_element_type=F32)
                              + lax.dot_general(dpxb[:, sl], wx_ref[j], NT_DIMS, preferred_element_type=F32))
            ga2 = lax.dot_general(xcb[:, sl], dpab[:, sl], TN_DIMS, preferred_element_type=F32)
            gx2 = lax.dot_general(xcb[:, sl], dpxb[:, sl], TN_DIMS, preferred_element_type=F32)
            for e in range(2):
                blk = slice(LANES * e, LANES * (e + 1))
                dwa_ref[2 * j + e] = dwa_ref[2 * j + e] + ga2[blk, blk]
                dwx_ref[2 * j + e] = dwx_ref[2 * j + e] + gx2[blk, blk]

        def conv_a_bwd(c, carry):
            r0 = pl.multiple_of(c * 8, 8)
            win = dxce[pl.ds(r0, 16), :]
            dxc = win[0:8]
            xwin = exta[pl.ds(r0, 16), :]
            dxl = cw4_ref[3:4, :] * dxc
            acc_add(acc_w, G_CB4, dxc)
            acc_add(acc_w, G_CW4 + 3, dxc * xwin[8:16])
            for k in range(LRU_K - 1):
                s = LRU_K - 1 - k
                dxl = dxl + cw4_ref[k:k + 1, :] * pltpu.roll(win, 16 - s, 0)[0:8]
                acc_add(acc_w, G_CW4 + k, dxc * pltpu.roll(xwin, s, 0)[8:16])
            dpf[pl.ds(r0, 8), 0:w] = dxl
            return carry

        lax.fori_loop(0, nch, conv_a_bwd, 0, unroll=3)
        dxce[tm:tm + 8, :] = dxce[0:8, :]

        def mixb(c, carry):
            r0 = pl.multiple_of(c * 8, 8)
            st = []
            ssq = jnp.zeros((8, 1), F32)
            for lb in range(nlb):
                sl = slice(LANES * lb, LANES * (lb + 1))
                c1v = c1_ref[pl.ds(r0, 8), sl]
                dlt = c1v - jnp.mean(c1v, axis=-1, keepdims=True)
                rs = lax.rsqrt(jnp.mean(dlt * dlt, axis=-1, keepdims=True) + EPS)
                c2 = dlt * rs
                yb0 = c2 * lng_ref[:, sl] + lnb_ref[:, sl]
                sg = _sigmoid(yb0)
                yb = yb0 * sg
                ssq = ssq + jnp.sum(yb * yb, axis=-1, keepdims=True)
                st.append((rs, c2, yb0, sg, yb))
            rb = lax.rsqrt(ssq / cw + EPS)
            tsum = jnp.zeros((8, 1), F32)
            dngs = []
            for lb in range(nlb):
                sl = slice(LANES * lb, LANES * (lb + 1))
                dnb = dy_ref[pl.ds(r0, 8), w + LANES * lb:w + LANES * (lb + 1)]
                ybr = st[lb][4] * rb
                acc_add(acc_c, G_GB, dnb * ybr, sl)
                dng = dnb * gb_ref[:, sl]
                dngs.append((dng, ybr))
                tsum = tsum + jnp.sum(dng * ybr, axis=-1, keepdims=True)
            tsum = tsum / cw
            for lb in range(nlb):
                sl = slice(LANES * lb, LANES * (lb + 1))
                rs, c2, yb0, sg, _ = st[lb]
                dng, ybr = dngs[lb]
                dyb0 = (rb * (dng - ybr * tsum)) * (sg * (1.0 + yb0 * (1.0 - sg)))
                acc_add(acc_c, G_LNG, dyb0 * c2, sl)
                acc_add(acc_c, G_LNB, dyb0, sl)
                dc2 = dyb0 * lng_ref[:, sl]
                dc1 = rs * (dc2 - jnp.mean(dc2, axis=-1, keepdims=True)
                            - c2 * jnp.mean(dc2 * c2, axis=-1, keepdims=True))
                acc_add(acc_c, G_CB31, dc1, sl)
                dc1e[pl.ds(r0, 8), sl] = dc1
            return carry

        lax.fori_loop(0, nch, mixb, 0, unroll=6)

        def conv_b_bwd(c, carry):
            r0 = pl.multiple_of(c * CONV_ROWS, 8)
            nwin = CONV_ROWS + 32
            for lb in range(nlb):
                sl = slice(LANES * lb, LANES * (lb + 1))
                win = dc1e[pl.ds(r0, nwin), sl]
                ups = [win] + [pltpu.roll(win, nwin - rr, 0) for rr in range(1, 8)]
                dc1 = win[0:CONV_ROWS]
                parts = [None] * 4
                for k in range(CONV_K):
                    q, rr = divmod(CONV_K - 1 - k, 8)
                    term = cw31_ref[k:k + 1, sl] * ups[rr][8 * q:8 * q + CONV_ROWS]
                    parts[k % 4] = term if parts[k % 4] is None else parts[k % 4] + term
                dc0 = (parts[0] + parts[1]) + (parts[2] + parts[3])
                cav = proj_ref[pl.ds(r0, CONV_ROWS), 2 * w + LANES * lb:2 * w + LANES * (lb + 1)]
                sg = _sigmoid(proj_ref[pl.ds(r0, CONV_ROWS), 2 * w + cw + LANES * lb:2 * w + cw + LANES * (lb + 1)])
                dpf[pl.ds(r0, CONV_ROWS), 2 * w + LANES * lb:2 * w + LANES * (lb + 1)] = dc0 * sg
                dpf[pl.ds(r0, CONV_ROWS), 2 * w + cw + LANES * lb:2 * w + cw + LANES * (lb + 1)] = (
                    (dc0 * cav) * (sg * (1.0 - sg)))
                xwin = extb[pl.ds(pl.multiple_of(r0 + 16, 8), nwin), sl]
                xr = [xwin] + [pltpu.roll(xwin, rr, 0) for rr in range(1, 8)]
                for k in range(CONV_K):
                    q, rr = divmod(CONV_K - 1 - k, 8)
                    acc_add(acc_c, G_CW31 + k, dc1 * xr[rr][32 - 8 * q:32 - 8 * q + CONV_ROWS], sl)
            return carry

        lax.fori_loop(0, tm // CONV_ROWS, conv_b_bwd, 0, unroll=2)
        dc1e[tm:tm + 32, :] = dc1e[0:32, :]
        dproj_ref[...] = dpf[...].astype(BF16)

        @pl.when(i == nt - 1)
        def _():
            def fold(ref, g):
                return jnp.sum(ref[8 * g:8 * g + 8, :], axis=0, keepdims=True)

            dcb4_ref[...] = fold(acc_w, G_CB4)
            for k in range(LRU_K):
                dcw4_ref[k:k + 1, :] = fold(acc_w, G_CW4 + k)
            dba_ref[...] = fold(acc_w, G_BA)
            dbx_ref[...] = fold(acc_w, G_BX)
            dlam_ref[...] = fold(acc_w, G_SP) * (-_sigmoid(-lam_ref[...]))
            dga_ref[...] = fold(acc_w, G_GA)
            dcb31_ref[...] = fold(acc_c, G_CB31)
            dlng_ref[...] = fold(acc_c, G_LNG)
            dlnb_ref[...] = fold(acc_c, G_LNB)
            dgb_ref[...] = fold(acc_c, G_GB)
            for k in range(CONV_K):
                dcw31_ref[k:k + 1, :] = fold(acc_c, G_CW31 + k)

    consts = [p[k] for k in MIX_PARAMS]
    gshapes = [(heads, LANES, LANES) if k in ('lru_wa', 'lru_wx') else p[k].shape for k in MIX_PARAMS]
    outs = _pcall(body, name=name, grid=(nt,),
                  in_specs=[_rrow(tm, w + cw, nt), _rrow(tm, 2 * w + 2 * cw, nt), _halo(48, 2 * w + 2 * cw, tm, nt),
                            _rrow(tm, w, nt), _halo(8, w, tm, nt), _rrow(tm, cw, nt)] + [_const(c.shape) for c in consts],
                  out_specs=[_rrow(tm, 2 * w + 2 * cw, nt)] + [_const_out(s) for s in gshapes],
                  out_shape=[jax.ShapeDtypeStruct((t, 2 * w + 2 * cw), BF16)]
                  + [jax.ShapeDtypeStruct(s, F32) for s in gshapes],
                  scratch_shapes=[pltpu.VMEM((8 + tm, w), F32), pltpu.VMEM((8 + tm, w), F32), pltpu.VMEM((tm, w), F32),
                                  pltpu.VMEM((tm, w), F32), pltpu.VMEM((tm, w), F32), pltpu.VMEM((tm, w), F32),
                                  pltpu.VMEM((tm, w), F32), pltpu.VMEM((tm + 8, w), F32),
                                  pltpu.VMEM((48 + tm, cw), F32), pltpu.VMEM((tm + 32, cw), F32),
                                  pltpu.VMEM((tm, 2 * w + 2 * cw), F32), pltpu.VMEM((8, w), F32),
                                  pltpu.VMEM((8 * NGW, w), F32), pltpu.VMEM((8 * NGC, cw), F32)],
                  compiler_params=_params())(dy, proj, proj, hs, hs, c1, *consts)
    return outs[0], dict(zip(MIX_PARAMS, outs[1:]))


def _ffn_window(u_ref, halo, c, col):
    if isinstance(c, int) and c == 0:
        return jnp.concatenate([halo[:, col:col + LANES], u_ref[0:16, col:col + LANES]], axis=0)
    return u_ref[pl.ds(pl.multiple_of(c * 16 - 8, 8), 24), col:col + LANES]


def _ffn_conv(win, w3_ref, b3_ref, col):
    sl = slice(col, col + LANES)
    x1 = pltpu.roll(win, 1, 0)[8:24]
    x2 = pltpu.roll(win, 2, 0)[8:24]
    u = w3_ref[2:3, sl] * win[8:24] + w3_ref[1:2, sl] * x1 + w3_ref[0:1, sl] * x2 + b3_ref[:, sl]
    return u, (x2, x1, win[8:24])


def _ffn_act_fwd(u0, w3, b3, name):
    t, f2 = u0.shape
    ff = f2 // 2
    _, tm = _tiles(t)
    nch = tm // 16

    def body(u_ref, w3_ref, b3_ref, act_ref, car):
        @pl.when(pl.program_id(0) == 0)
        def _():
            car[...] = jnp.zeros((8, f2), F32)

        def chunk(c):
            halo = car[...] if isinstance(c, int) else None
            r0 = 0 if isinstance(c, int) else pl.multiple_of(c * 16, 16)
            for j in range(ff // LANES):
                gate, _ = _ffn_conv(_ffn_window(u_ref, halo, c, LANES * j), w3_ref, b3_ref, LANES * j)
                up, _ = _ffn_conv(_ffn_window(u_ref, halo, c, ff + LANES * j), w3_ref, b3_ref, ff + LANES * j)
                act_ref[pl.ds(r0, 16), LANES * j:LANES * (j + 1)] = (_gelu(gate) * up).astype(BF16)

        chunk(0)

        def loop(c, carry):
            chunk(c)
            return carry

        lax.fori_loop(1, nch, loop, 0)
        car[...] = u_ref[tm - 8:tm, :]

    return _pcall(body, name=name, grid=(t // tm,),
                  in_specs=[_row(tm, f2), _const(w3.shape), _const(b3.shape)],
                  out_specs=_row(tm, ff), out_shape=jax.ShapeDtypeStruct((t, ff), BF16),
                  scratch_shapes=[pltpu.VMEM((8, f2), F32)],
                  compiler_params=_params())(u0, w3, b3)


def _ffn_act_bwd(dact, u0, w3, b3, name):
    t, f2 = u0.shape
    ff = f2 // 2
    _, tm = _tiles(t)
    nt = t // tm
    nch = tm // 16

    def body(dact_ref, u_ref, uh_ref, w3_ref, b3_ref, du0_ref, dw3_ref, db3_ref, dub, acc):
        i = pl.program_id(0)

        @pl.when(i == 0)
        def _():
            dub[tm:tm + 8, :] = jnp.zeros((8, f2), F32)
            acc[...] = jnp.zeros((32, f2), F32)

        nf = jnp.where(i < nt - 1, 1.0, 0.0).astype(F32)

        def acc_add(g, val, sl):
            acc[8 * g:8 * g + 8, sl] = acc[8 * g:8 * g + 8, sl] + (val[0:8] + val[8:16])

        def chunk(c):
            halo = uh_ref[...] * nf if isinstance(c, int) else None
            r0 = 0 if isinstance(c, int) else pl.multiple_of(c * 16, 16)
            for j in range(ff // LANES):
                cg, cu = LANES * j, ff + LANES * j
                gate, xg = _ffn_conv(_ffn_window(u_ref, halo, c, cg), w3_ref, b3_ref, cg)
                up, xu = _ffn_conv(_ffn_window(u_ref, halo, c, cu), w3_ref, b3_ref, cu)
                ge, dge = _gelu_and_grad(gate)
                da = dact_ref[pl.ds(r0, 16), cg:cg + LANES]
                for col, du, xs in ((cg, (da * up) * dge, xg), (cu, da * ge, xu)):
                    sl = slice(col, col + LANES)
                    dub[pl.ds(r0, 16), sl] = du
                    acc_add(0, du, sl)
                    for k in range(FFN_K):
                        acc_add(1 + k, du * xs[k], sl)

        chunk(0)

        def loop1(c, carry):
            chunk(c)
            return carry

        lax.fori_loop(1, nch, loop1, 0)

        def loop2(c, carry):
            r0 = pl.multiple_of(c * 16, 16)
            for j in range(f2 // LANES):
                sl = slice(LANES * j, LANES * (j + 1))
                win = dub[pl.ds(r0, 24), sl]
                du0 = (w3_ref[2:3, sl] * win[0:16] + w3_ref[1:2, sl] * pltpu.roll(win, 23, 0)[0:16]
                       + w3_ref[0:1, sl] * pltpu.roll(win, 22, 0)[0:16])
                du0_ref[pl.ds(r0, 16), sl] = du0.astype(BF16)
            return carry

        lax.fori_loop(0, nch, loop2, 0)
        dub[tm:tm + 8, :] = dub[0:8, :]

        @pl.when(i == nt - 1)
        def _():
            db3_ref[...] = jnp.sum(acc[0:8, :], axis=0, keepdims=True)
            for k in range(FFN_K):
                dw3_ref[k:k + 1, :] = jnp.sum(acc[8 + 8 * k:16 + 8 * k, :], axis=0, keepdims=True)

    return _pcall(body, name=name, grid=(nt,),
                  in_specs=[_rrow(tm, ff, nt), _rrow(tm, f2, nt), _halo(8, f2, tm, nt), _const(w3.shape),
                            _const(b3.shape)],
                  out_specs=[_rrow(tm, f2, nt), _const_out(w3.shape), _const_out(b3.shape)],
                  out_shape=[jax.ShapeDtypeStruct((t, f2), BF16), jax.ShapeDtypeStruct(w3.shape, F32),
                             jax.ShapeDtypeStruct(b3.shape, F32)],
                  scratch_shapes=[pltpu.VMEM((tm + 8, f2), F32), pltpu.VMEM((32, f2), F32)],
                  compiler_params=_params())(dact, u0, u0, w3, b3)


def _loss_head(h, tgt, name):
    t, d = h.shape
    tm, _ = _tiles(t)

    def body(h_ref, t_ref, dh_ref, s_ref):
        i = pl.program_id(0)

        @pl.when(i == 0)
        def _():
            s_ref[...] = jnp.zeros((1, d), F32)

        row = lax.broadcasted_iota(jnp.int32, (tm, d), 0) + i * tm
        err = jnp.where(row >= N_META, h_ref[...] - t_ref[...], 0.0)
        dh_ref[...] = err / d
        s_ref[...] += jnp.sum(err * err, axis=0, keepdims=True)

    return _pcall(body, name=name, grid=(t // tm,), in_specs=[_row(tm, d), _row(tm, d)],
                  out_specs=[_row(tm, d), _const_out((1, d))],
                  out_shape=[jax.ShapeDtypeStruct((t, d), F32), jax.ShapeDtypeStruct((1, d), F32)],
                  compiler_params=_params())(h, tgt)


def _row_tile(rows, row_bytes, budget):
    best = None
    for tr in range(16, rows + 1, 16):
        if rows % tr == 0 and tr * row_bytes <= budget:
            best = tr
    assert best is not None, (rows, row_bytes)
    return best


def _cast_bf16(a, name):
    r, c = a.shape
    tr = _row_tile(r, c * 4, 4 << 20)

    def body(a_ref, o_ref):
        o_ref[...] = a_ref[...].astype(BF16)

    return _pcall(body, name=name, grid=(r // tr,), in_specs=[_row(tr, c)], out_specs=_row(tr, c),
                  out_shape=jax.ShapeDtypeStruct((r, c), BF16), compiler_params=_params())(a)


def _sum_slots(r, name):
    s, rows, c = r.shape
    tr = _row_tile(rows, s * c * 4, 8 << 20)

    def body(r_ref, o_ref):
        acc = r_ref[0].astype(F32)
        for k in range(1, s):
            acc = acc + r_ref[k].astype(F32)
        o_ref[...] = acc

    return _pcall(body, name=name, grid=(rows // tr,),
                  in_specs=[pl.BlockSpec((s, tr, c), lambda i: (0, i, 0))], out_specs=_row(tr, c),
                  out_shape=jax.ShapeDtypeStruct((rows, c), F32), compiler_params=_params())(r)


def _cast_into_window(a, axis, k1, l0, l, name):
    _, r, c = a.shape
    shape = (l, 4 * r, c) if axis == 1 else (l, r, 4 * c)

    def body(k_ref, a_ref, o_ref):
        o_ref[...] = a_ref[...].astype(BF16)

    omap = (lambda i, k: (i, k[0], 0)) if axis == 1 else (lambda i, k: (i, 0, k[0]))
    gs = pltpu.PrefetchScalarGridSpec(num_scalar_prefetch=1, grid=(l,),
                                      in_specs=[pl.BlockSpec((1, r, c), lambda i, k: (i + l0, 0, 0))],
                                      out_specs=pl.BlockSpec((1, r, c), omap))
    return _pcall(body, name=name, grid_spec=gs, out_shape=jax.ShapeDtypeStruct(shape, BF16),
                  compiler_params=_params())(k1, a)


def _adamw(g, w, m, v, name):
    r, c = g.shape
    tr = _row_tile(r, c * 4, 2 << 20)

    def body(g_ref, w_ref, m_ref, v_ref, d_ref, m2_ref, v2_ref):
        gv = g_ref[...]
        m2 = ADAM_B1 * m_ref[...] + (1.0 - ADAM_B1) * gv
        v2 = ADAM_B2 * v_ref[...] + (1.0 - ADAM_B2) * (gv * gv)
        m_hat = m2 / (1.0 - ADAM_B1 ** ADAM_STEP)
        v_hat = v2 / (1.0 - ADAM_B2 ** ADAM_STEP)
        d_ref[...] = -ADAM_LR * (m_hat / (jnp.sqrt(v_hat) + ADAM_EPS) + ADAM_WD * w_ref[...])
        m2_ref[...] = m2
        v2_ref[...] = v2

    return _pcall(body, name=name, grid=(r // tr,), in_specs=[_row(tr, c)] * 4, out_specs=[_row(tr, c)] * 3,
                  out_shape=[jax.ShapeDtypeStruct((r, c), F32)] * 3, compiler_params=_params())(g, w, m, v)


ANY = pl.BlockSpec(memory_space=pl.ANY)


def _coords():
    return lax.axis_index("x"), lax.axis_index("y"), lax.axis_index("c")


def _window(ref, lead, axis, k, width):
    idx = [slice(None)] * len(ref.shape)
    idx[0] = lead
    idx[axis] = pl.ds(pl.multiple_of(k * width, LANES if axis == len(ref.shape) - 1 else 8), width)
    return ref.at[tuple(idx)]


def _gather_xy(arrs, axes, n_inplace, name):
    n = len(arrs)
    out_shape, widths = [], []
    for i, (a, ax) in enumerate(zip(arrs, axes)):
        s = list(a.shape)
        if i < n_inplace:
            widths.append(s[ax] // 4)
        else:
            widths.append(s[ax])
            s[ax] *= 4
        out_shape.append(jax.ShapeDtypeStruct(tuple(s), a.dtype))

    def body(*refs):
        ins, outs = refs[:n], refs[n:2 * n]
        send_sems, recv_sems, loc_sems = refs[2 * n:]
        x, y, c = _coords()
        k_me = 2 * x + y
        chips = [(1 - x, y), (x, 1 - y), (1 - x, 1 - y)]
        sib = (x, y, 1 - c)

        def half(i, which):
            hl = arrs[i].shape[0] // 2
            return pl.ds(which * hl, hl)

        def win(i, kk, which):
            return _window(outs[i], half(i, which), axes[i], kk, widths[i])

        def copy(i, s, src, dst, to):
            return pltpu.make_async_remote_copy(src_ref=src, dst_ref=dst, send_sem=send_sems.at[i, s],
                                                recv_sem=recv_sems.at[i, s], device_id=to, device_id_type=MESH_T)

        locs = []
        for i in range(n_inplace, n):
            lc = pltpu.make_async_copy(ins[i], _window(outs[i], slice(None), axes[i], k_me, widths[i]), loc_sems.at[i])
            lc.start()
            locs.append(lc)
        started = []
        for i in range(n):
            for j, chip in enumerate(chips):
                src = win(i, k_me, c) if i < n_inplace else ins[i].at[half(i, c)]
                cp = copy(i, j, src, win(i, k_me, c), (*chip, c))
                cp.start()
                started.append(cp)
        for i in range(n):
            for j, chip in enumerate(chips):
                kk = 2 * chip[0] + chip[1]
                copy(i, j, win(i, kk, c), win(i, kk, c), (*chip, c)).wait_recv()
                fw = copy(i, 3 + j, win(i, kk, c), win(i, kk, c), sib)
                fw.start()
                started.append(fw)
        for i in range(n):
            for j, chip in enumerate(chips):
                kk = 2 * chip[0] + chip[1]
                copy(i, 3 + j, win(i, kk, 1 - c), win(i, kk, 1 - c), sib).wait_recv()
        for cp in started:
            cp.wait_send()
        for lc in locs:
            lc.wait()

    return _pcall(body, name=name, in_specs=[ANY] * n, out_specs=[ANY] * n, out_shape=out_shape,
                  input_output_aliases={i: i for i in range(n_inplace)},
                  scratch_shapes=[pltpu.SemaphoreType.DMA((n, 6)), pltpu.SemaphoreType.DMA((n, 6)),
                                  pltpu.SemaphoreType.DMA((n,))],
                  compiler_params=pltpu.CompilerParams(has_side_effects=True))(*arrs)


def _peer(x, y, c, mask):
    bx, by, bc = (mask >> 2) & 1, (mask >> 1) & 1, mask & 1
    return (1 - x if bx else x, 1 - y if by else y, 1 - c if bc else c)


HBM = pl.BlockSpec(memory_space=pltpu.HBM)
SEM = pl.BlockSpec(memory_space=pltpu.SEMAPHORE)


def _piece_shape(shape, wa):
    r, c = shape
    return (r // 2, c // 4) if wa == 1 else (r // 8, c)


def _piece(ref, wa, k, h):
    r, c = ref.shape
    if wa == 1:
        return ref.at[pl.ds(pl.multiple_of(h * (r // 2), 16), r // 2), pl.ds(pl.multiple_of(k * (c // 4), LANES), c // 4)]
    return ref.at[pl.ds(pl.multiple_of((2 * k + h) * (r // 8), 16), r // 8), :]


def _scatter_start(grads, was, name):
    n = len(grads)
    lands = [lax.empty((7, *_piece_shape(g.shape, wa)), g.dtype) for g, wa in zip(grads, was)]

    def body(*refs):
        g_in, land_in = refs[:n], refs[n:2 * n]
        send_sems, recv_sems = refs[2 * n:2 * n + 7 * n], refs[2 * n + 7 * n:2 * n + 14 * n]
        token = refs[-1]
        x, y, c = _coords()
        for i in range(n):
            for mask in range(1, 8):
                px, py, pc = _peer(x, y, c, mask)
                pltpu.make_async_remote_copy(src_ref=_piece(g_in[i], was[i], 2 * px + py, pc),
                                             dst_ref=land_in[i].at[mask - 1], send_sem=send_sems[7 * i + mask - 1],
                                             recv_sem=recv_sems[7 * i + mask - 1], device_id=(px, py, pc),
                                             device_id_type=MESH_T).start()
        token[...] = jnp.zeros(token.shape, F32)

    args = [pltpu.with_memory_space_constraint(a, pltpu.HBM) for a in list(grads) + lands]
    res = _pcall(body, name=name, in_specs=[HBM] * (2 * n),
                 out_specs=[SEM] * (14 * n) + [HBM] * (2 * n) + [pl.BlockSpec(memory_space=pltpu.VMEM)],
                 out_shape=[pltpu.SemaphoreType.DMA(())] * (14 * n)
                 + [pltpu.HBM(a.shape, a.dtype) for a in args] + [jax.ShapeDtypeStruct((8, LANES), F32)],
                 input_output_aliases={i: 14 * n + i for i in range(2 * n)},
                 compiler_params=pltpu.CompilerParams(has_side_effects=pltpu.SideEffectType.DATAFLOW_SIDE_EFFECTING))(*args)
    return res[:7 * n], res[7 * n:14 * n], res[14 * n:15 * n], res[15 * n:16 * n], res[-1]


def _scatter_wait(send_sems, recv_sems, grads, lands, was, after, name):
    n = len(grads)

    def body(*refs):
        g_in, land_in = refs[:n], refs[n:2 * n]
        s_sems, r_sems = refs[2 * n:2 * n + 7 * n], refs[2 * n + 7 * n:2 * n + 14 * n]
        x, y, c = _coords()
        for i in range(n):
            for mask in range(1, 8):
                px, py, pc = _peer(x, y, c, mask)
                cp = pltpu.make_async_remote_copy(src_ref=_piece(g_in[i], was[i], 2 * px + py, pc),
                                                  dst_ref=land_in[i].at[mask - 1], send_sem=s_sems[7 * i + mask - 1],
                                                  recv_sem=r_sems[7 * i + mask - 1], device_id=(px, py, pc),
                                                  device_id_type=MESH_T)
                cp.wait_send()
                cp.wait_recv()

    args = list(grads) + list(lands)
    res = _pcall(body, name=name, in_specs=[HBM] * (2 * n) + [SEM] * (14 * n) + [ANY], out_specs=[HBM] * (2 * n),
                 out_shape=[pltpu.HBM(a.shape, a.dtype) for a in args],
                 input_output_aliases={i: i for i in range(2 * n)},
                 compiler_params=pltpu.CompilerParams(has_side_effects=pltpu.SideEffectType.DATAFLOW_SIDE_EFFECTING))(
                     *args, *send_sems, *recv_sems, after)
    return res[:n], res[n:]


def _chip_window(ref, wa, k):
    r, c = ref.shape
    if wa == 1:
        return ref.at[:, pl.ds(pl.multiple_of(k * (c // 4), LANES), c // 4)]
    return ref.at[pl.ds(pl.multiple_of(k * (r // 4), 16), r // 4), :]


def _gather_copies(bufs, was, send_sems, recv_sems):
    x, y, c = _coords()
    out = []
    for i, (buf, wa) in enumerate(zip(bufs, was)):
        for j, chip in enumerate([(1 - x, y), (x, 1 - y), (1 - x, 1 - y)]):
            def copy(k, i=i, j=j, chip=chip, buf=buf, wa=wa):
                win = _chip_window(buf, wa, k)
                return pltpu.make_async_remote_copy(src_ref=win, dst_ref=win, send_sem=send_sems[3 * i + j],
                                                    recv_sem=recv_sems[3 * i + j], device_id=(*chip, c),
                                                    device_id_type=MESH_T)
            out.append((copy(2 * x + y), copy(2 * chip[0] + chip[1])))
    return out


def _gather_start(bufs, was, name):
    n = len(bufs)
    ns = 3 * n

    def body(*refs):
        for mine, _ in _gather_copies(refs[:n], was, refs[n:n + ns], refs[n + ns:n + 2 * ns]):
            mine.start()
        refs[-1][...] = jnp.zeros(refs[-1].shape, F32)

    args = [pltpu.with_memory_space_constraint(a, pltpu.HBM) for a in bufs]
    res = _pcall(body, name=name, in_specs=[HBM] * n,
                 out_specs=[SEM] * (2 * ns) + [HBM] * n + [pl.BlockSpec(memory_space=pltpu.VMEM)],
                 out_shape=[pltpu.SemaphoreType.DMA(())] * (2 * ns) + [pltpu.HBM(a.shape, a.dtype) for a in args]
                 + [jax.ShapeDtypeStruct((8, LANES), F32)],
                 input_output_aliases={i: 2 * ns + i for i in range(n)},
                 compiler_params=pltpu.CompilerParams(has_side_effects=pltpu.SideEffectType.DATAFLOW_SIDE_EFFECTING))(*args)
    return res[:ns], res[ns:2 * ns], res[2 * ns:2 * ns + n], res[-1]


def _gather_wait(send_sems, recv_sems, bufs, was, after, name):
    n = len(bufs)
    ns = 3 * n

    def body(*refs):
        for mine, theirs in _gather_copies(refs[:n], was, refs[n:n + ns], refs[n + ns:n + 2 * ns]):
            mine.wait_send()
            theirs.wait_recv()

    return _pcall(body, name=name, in_specs=[HBM] * n + [SEM] * (2 * ns) + [ANY], out_specs=[HBM] * n,
                  out_shape=[pltpu.HBM(a.shape, a.dtype) for a in bufs], input_output_aliases={i: i for i in range(n)},
                  compiler_params=pltpu.CompilerParams(has_side_effects=pltpu.SideEffectType.DATAFLOW_SIDE_EFFECTING))(
                      *bufs, *send_sems, *recv_sems, after)


def _sum_pieces(land, g, wa, k1, c1, name):
    s, rp, cp = land.shape
    tr = _row_tile(rp, (s + 1) * cp * 4, 8 << 20)
    nb = rp // tr
    if wa == 1:
        own = pl.BlockSpec((tr, cp), lambda i, k, c: (c[0] * nb + i, k[0]))
    else:
        own = pl.BlockSpec((tr, cp), lambda i, k, c: ((2 * k[0] + c[0]) * nb + i, 0))

    def body(k_ref, c_ref, l_ref, g_ref, o_ref):
        acc = l_ref[0].astype(F32)
        for j in range(1, s):
            acc = acc + l_ref[j].astype(F32)
        o_ref[...] = acc + g_ref[...].astype(F32)

    gs = pltpu.PrefetchScalarGridSpec(
        num_scalar_prefetch=2, grid=(nb,),
        in_specs=[pl.BlockSpec((s, tr, cp), lambda i, k, c: (0, i, 0)), own],
        out_specs=pl.BlockSpec((tr, cp), lambda i, k, c: (c[0] * nb + i, 0)))
    return _pcall(body, name=name, grid_spec=gs, out_shape=jax.ShapeDtypeStruct((2 * rp, cp), F32),
                  compiler_params=_params())(k1, c1, land, g)


SWAP_CHUNKS = 4


def _sibling_swap(fulls, name):
    n = len(fulls)
    out_shape = [jax.ShapeDtypeStruct(a.shape, a.dtype) for a in fulls]

    def body(*refs):
        outs = refs[n:2 * n]
        send_sems, recv_sems = refs[2 * n:]
        x, y, c = _coords()

        def chunk(i, which, q):
            hl, rc = fulls[i].shape[0] // 2, fulls[i].shape[1] // SWAP_CHUNKS
            ref = outs[i].at[pl.ds(which * hl, hl), pl.ds(q * rc, rc)]
            return pltpu.make_async_remote_copy(src_ref=ref, dst_ref=ref, send_sem=send_sems.at[i, q],
                                                recv_sem=recv_sems.at[i, q], device_id=(x, y, 1 - c),
                                                device_id_type=MESH_T)

        started = []
        for i in range(n):
            for q in range(SWAP_CHUNKS):
                cp = chunk(i, c, q)
                cp.start()
                started.append(cp)
        for cp in started:
            cp.wait_send()
        for i in range(n):
            for q in range(SWAP_CHUNKS):
                chunk(i, 1 - c, q).wait_recv()

    for a in fulls:
        assert a.shape[1] % (8 * SWAP_CHUNKS) == 0, a.shape
    return _pcall(body, name=name, in_specs=[ANY] * n, out_specs=[ANY] * n, out_shape=out_shape,
                  input_output_aliases={i: i for i in range(n)},
                  scratch_shapes=[pltpu.SemaphoreType.DMA((n, SWAP_CHUNKS)), pltpu.SemaphoreType.DMA((n, SWAP_CHUNKS))],
                  compiler_params=pltpu.CompilerParams(has_side_effects=True))(*fulls)


def _scatter8(pk, name):
    r, cdim = pk.shape
    pr = r // 8
    assert pr % 8 == 0

    def body(p_ref, o_ref, send_sems, recv_sems, loc_sem):
        x, y, c = _coords()

        def piece(px, py, pc):
            return p_ref.at[pl.ds(pl.multiple_of((4 * px + 2 * py + pc) * pr, 8), pr)]

        lc = pltpu.make_async_copy(piece(x, y, c), o_ref.at[7], loc_sem)
        lc.start()
        started = []
        for mask in range(1, 8):
            px, py, pc = _peer(x, y, c, mask)
            cp = pltpu.make_async_remote_copy(src_ref=piece(px, py, pc), dst_ref=o_ref.at[mask - 1],
                                              send_sem=send_sems.at[mask - 1], recv_sem=recv_sems.at[mask - 1],
                                              device_id=(px, py, pc), device_id_type=MESH_T)
            cp.start()
            started.append(cp)
        for cp in started:
            cp.wait()
        lc.wait()

    return _pcall(body, name=name, in_specs=[ANY], out_specs=ANY, out_shape=jax.ShapeDtypeStruct((8, pr, cdim), F32),
                  scratch_shapes=[pltpu.SemaphoreType.DMA((7,)), pltpu.SemaphoreType.DMA((7,)),
                                  pltpu.SemaphoreType.DMA],
                  compiler_params=pltpu.CompilerParams(has_side_effects=True))(pk)


def _gather_all(pk, name):
    r, cdim = pk.shape

    def body(p_ref, o_ref, send_sems, recv_sems, loc_sem):
        x, y, c = _coords()
        lc = pltpu.make_async_copy(p_ref, o_ref.at[4 * x + 2 * y + c], loc_sem)
        lc.start()
        started = []
        for mask in range(1, 8):
            px, py, pc = _peer(x, y, c, mask)
            cp = pltpu.make_async_remote_copy(src_ref=p_ref, dst_ref=o_ref.at[4 * x + 2 * y + c],
                                              send_sem=send_sems.at[mask - 1], recv_sem=recv_sems.at[mask - 1],
                                              device_id=(px, py, pc), device_id_type=MESH_T)
            cp.start()
            started.append(cp)
        for cp in started:
            cp.wait()
        lc.wait()

    return _pcall(body, name=name, in_specs=[ANY], out_specs=ANY, out_shape=jax.ShapeDtypeStruct((8, r, cdim), F32),
                  scratch_shapes=[pltpu.SemaphoreType.DMA((7,)), pltpu.SemaphoreType.DMA((7,)),
                                  pltpu.SemaphoreType.DMA],
                  compiler_params=pltpu.CompilerParams(has_side_effects=True))(pk)


PACK_C = 1024


def _pack(arrs, row_mult):
    parts = []
    for a in arrs:
        flat = a.reshape(-1)
        parts.append(jnp.pad(flat, (0, (-flat.shape[0]) % PACK_C)))
    flat = jnp.concatenate(parts)
    flat = jnp.pad(flat, (0, (-flat.shape[0]) % (PACK_C * row_mult)))
    return flat.reshape(-1, PACK_C)


def _unpack(pk, shapes):
    flat = pk.reshape(-1)
    out, off = [], 0
    for s in shapes:
        size = 1
        for dd in s:
            size *= dd
        out.append(flat[off:off + size].reshape(s))
        off += size + (-size) % PACK_C
    return out


def _layer_params(full, l):
    p = {}
    for k in ['g_pre_mix', 'lru_conv_b', 'lru_ba', 'lru_bx', 'lru_lambda', 'conv_b', 'conv_ln_g', 'conv_ln_b', 'g_out_lru',
              'g_out_conv', 'g_post_mix', 'g_pre_ffn', 'ffn_conv_b', 'g_post_ffn']:
        p[k] = full[k][l][None, :]
    for k in ['lru_conv_w', 'conv_w', 'ffn_conv_w'] + [k for k in BIG if k in full]:
        p[k] = full[k][l]
    p['lru_wa'] = _pair_heads(full['lru_wa_bf'][l])
    p['lru_wx'] = _pair_heads(full['lru_wx_bf'][l])
    return p


def _step(x, loss_target, w, m, v):
    depth = w['w_in'].shape[0]
    d = x.shape[2]
    xk, yk, _ = _coords()
    k_me = 2 * xk + yk

    k1 = jnp.reshape(k_me, (1,)).astype(jnp.int32)
    big_bf = {k: _cast_into_window(w[k], BIG_AXIS[k], k1, 0, SYNC_LAYERS, "cast_" + k) for k in BIG}
    sh_pad = [w['meta_tokens']] + [jnp.pad(w[k], ((0, 0), (0, (-w[k].shape[1]) % 8), (0, 0))) for k in SH_SMALL[1:]]
    gath = _gather_xy([big_bf[k] for k in BIG] + sh_pad, [BIG_AXIS[k] for k in BIG] + [1, 2, 2, 2], len(BIG),
                      "gather_weights")
    full = {k: a for k, a in w.items() if k not in BIG}
    early = dict(zip(BIG, gath[:len(BIG)]))
    was = [BIG_AXIS[k] - 1 for k in BIG]
    in_flight = {}
    for l in range(SYNC_LAYERS, depth):
        bufs = [_cast_into_window(w[k], BIG_AXIS[k], k1, l, 1, "cast_%s_%d" % (k, l))[0] for k in BIG]
        send_sems, recv_sems, bufs, token = _gather_start(bufs, was, "gather_start_%d" % l)
        in_flight[l] = (send_sems, recv_sems, bufs)
        full['g_pre_mix'] = full['g_pre_mix'] + token[0, 0]

    def big_weights(l, h_in):
        if l < SYNC_LAYERS:
            return {k: early[k][l] for k in BIG}
        send_sems, recv_sems, bufs = in_flight[l]
        return dict(zip(BIG, _gather_wait(send_sems, recv_sems, bufs, was, h_in, "gather_wait_%d" % l)))

    full['meta_tokens'] = gath[len(BIG)]
    for k, a in zip(SH_SMALL[1:], gath[len(BIG) + 1:]):
        full[k] = a[:, :w[k].shape[1]]
    for k in ('lru_wa', 'lru_wx'):
        full[k + '_bf'] = _cast_bf16(w[k].reshape(-1, LANES), "cast_" + k).reshape(w[k].shape)

    h = jnp.concatenate([full['meta_tokens'], x[0]], axis=0)
    tgt = jnp.pad(loss_target[0], ((N_META, 0), (0, 0)))
    pending = {}

    def send_big_grads(l, keys, g):
        res = _scatter_start([g[k] for k in keys], [BIG_AXIS[k] - 1 for k in keys], "scatter_start_%d_%s" % (l, keys[-1]))
        pending[(l, keys)] = res[:4]
        return res[4]

    sq, dh, gl = _fwd_bwd(h, tgt, full, depth, send_big_grads, big_weights)
    loss = lax.psum(0.5 * jnp.sum(sq) / d, ("x", "y", "c"))
    return _reduce_update(loss, dh, gl, pending, w, m, v, depth, k_me)


def _fwd_bwd(h, tgt, full, depth, on_grads=None, big_weights=None):
    saved = []
    for l in range(depth):
        p = _layer_params(full, l)
        if big_weights is not None:
            p.update(big_weights(l, h))
        proj, zb1 = _rms_matmul(h, p['g_pre_mix'], p['w_in'], "in_proj")
        y, hs, c1 = _mix_fwd(proj, p, "mix_fwd")
        o, h1 = _matmul_rms_res(y, p['w_out'], h, p['g_post_mix'], "out_proj")
        u0, zb2 = _rms_matmul(h1, p['g_pre_ffn'], p['w_up'], "up_proj")
        act = _ffn_act_fwd(u0, p['ffn_conv_w'], p['ffn_conv_b'], "ffn_act_fwd")
        f, h2 = _matmul_rms_res(act, p['w_down'], h1, p['g_post_ffn'], "down_proj")
        saved.append((p, h, zb1, proj, y, hs, c1, o, h1, zb2, u0, act, f))
        h = h2

    dh, sq = _loss_head(h, tgt, "loss_head")

    def tied(a, token):
        return a if token is None else a + token[0:1, 0:1]

    gl = [None] * depth
    for l in reversed(range(depth)):
        p, h0, zb1, proj, y, hs, c1, o, h1, zb2, u0, act, f = saved[l]
        g = {}
        dact, dfb, g['g_post_ffn'] = _rmsbwd_matmul_nt(f, p['g_post_ffn'], dh, p['w_down'], "down_bwd")
        g['w_down'] = _matmul_tn(act, dfb, "down_dw")
        du0, g['ffn_conv_w'], g['ffn_conv_b'] = _ffn_act_bwd(dact, u0, p['ffn_conv_w'], p['ffn_conv_b'], "ffn_act_bwd")
        g['w_up'] = _matmul_tn(zb2, du0, "up_dw")
        dh1, g['g_pre_ffn'] = _matmul_nt_rmsbwd_res(du0, p['w_up'], h1, p['g_pre_ffn'], dh, "up_bwd")
        dy, dob, g['g_post_mix'] = _rmsbwd_matmul_nt(o, p['g_post_mix'], dh1, p['w_out'], "out_bwd")
        g['w_out'] = _matmul_tn(y, dob, "out_dw")
        token = on_grads(l, ('w_down', 'w_up', 'w_out'), g) if on_grads is not None else None
        pm = dict(p)
        pm['g_out_conv'] = tied(p['g_out_conv'], token)
        dproj, gm = _mix_bwd(dy, proj, hs, c1, pm, "mix_bwd")
        g.update(gm)
        g['w_in'] = _matmul_tn(zb1, dproj, "in_dw")
        token = on_grads(l, ('w_in',), g) if on_grads is not None else None
        dh, g['g_pre_mix'] = _matmul_nt_rmsbwd_res(dproj, p['w_in'], h0, tied(p['g_pre_mix'], token), dh1, "in_bwd")
        gl[l] = g
    return sq, dh, gl


def _reduce_update(loss, dh, gl, pending, w, m, v, depth, k_me):
    grad_x = dh[N_META:][None]

    def stacked(k):
        return jnp.stack([gl[l][k].reshape(w[k].shape[1:]) if k not in SH_SMALL + BIG else gl[l][k]
                          for l in range(depth)])

    k1 = jnp.reshape(k_me, (1,)).astype(jnp.int32)
    c1 = jnp.reshape(lax.axis_index("c"), (1,)).astype(jnp.int32)
    order, halves = [], []
    for (l, keys), (send_sems, recv_sems, g_thru, lands) in pending.items():
        was = [BIG_AXIS[k] - 1 for k in keys]
        g_own, lands = _scatter_wait(send_sems, recv_sems, g_thru, lands, was, dh, "scatter_wait_%d_%s" % (l, keys[-1]))
        for k, wa, land, g_k in zip(keys, was, lands, g_own):
            red = _sum_pieces(land, g_k, wa, k1, c1, "grad_sum")
            order.append((l, k))
            halves.append(red.reshape(2, land.shape[1], land.shape[2]))
    swapped = dict(zip(order, _sibling_swap(halves, "grad_swap")))
    big_red = [jnp.stack([swapped[(l, k)].reshape(w[k].shape[1:]) for l in range(depth)]) for k in BIG]

    out = {}
    for k, gk in zip(BIG, big_red):
        c2 = gk.shape[-1]
        dl, m2, v2 = _adamw(gk.reshape(-1, c2), w[k].reshape(-1, c2), m[k].reshape(-1, c2), v[k].reshape(-1, c2),
                            "adamw_big")
        out[k] = (gk, dl.reshape(gk.shape), m2.reshape(gk.shape), v2.reshape(gk.shape))

    rep_g = [stacked(k) for k in REP_SMALL]
    sh_g = [dh[:N_META]] + [stacked(k) for k in SH_SMALL[1:]]
    n_rep_rows = _pack(rep_g, 1).shape[0]
    pk = jnp.concatenate([_pack(rep_g, 1), _pack(sh_g, 1)])
    pk = jnp.pad(pk, ((0, (-pk.shape[0]) % 256), (0, 0)))
    part = _sum_slots(_scatter8(pk, "small_scatter"), "small_sum")
    red = _gather_all(part, "small_gather").reshape(pk.shape)
    rep_red = _unpack(red[:n_rep_rows], [a.shape for a in rep_g])
    sh_red = []
    for a in _unpack(red[n_rep_rows:], [a.shape for a in sh_g]):
        wd = a.shape[-1] // 4
        sh_red.append(lax.dynamic_slice_in_dim(a, k_me * wd, wd, axis=a.ndim - 1))

    for names, grads_ in ((REP_SMALL, rep_red), (SH_SMALL, sh_red)):
        res = _adamw(_pack(grads_, 16), _pack([w[k] for k in names], 16), _pack([m[k] for k in names], 16),
                     _pack([v[k] for k in names], 16), "adamw_small")
        shapes = [w[k].shape for k in names]
        un = [_unpack(r, shapes) for r in res]
        for j, k in enumerate(names):
            out[k] = (grads_[j].reshape(w[k].shape), un[0][j], un[1][j], un[2][j])

    return (loss, grad_x, *[out[k][0] for k in WEIGHTS], *[out[k][1] for k in WEIGHTS],
            *[out[k][2] for k in WEIGHTS], *[out[k][3] for k in WEIGHTS])


def kernel(x, meta_tokens, g_pre_mix, w_in, lru_conv_w, lru_conv_b, lru_wa, lru_ba, lru_wx, lru_bx, lru_lambda, conv_w, conv_b, conv_ln_g, conv_ln_b, g_out_lru, g_out_conv, w_out, g_post_mix, g_pre_ffn, w_up, ffn_conv_w, ffn_conv_b, w_down, g_post_ffn, loss_target, m_meta_tokens, m_g_pre_mix, m_w_in, m_lru_conv_w, m_lru_conv_b, m_lru_wa, m_lru_ba, m_lru_wx, m_lru_bx, m_lru_lambda, m_conv_w, m_conv_b, m_conv_ln_g, m_conv_ln_b, m_g_out_lru, m_g_out_conv, m_w_out, m_g_post_mix, m_g_pre_ffn, m_w_up, m_ffn_conv_w, m_ffn_conv_b, m_w_down, m_g_post_ffn, v_meta_tokens, v_g_pre_mix, v_w_in, v_lru_conv_w, v_lru_conv_b, v_lru_wa, v_lru_ba, v_lru_wx, v_lru_bx, v_lru_lambda, v_conv_w, v_conv_b, v_conv_ln_g, v_conv_ln_b, v_g_out_lru, v_g_out_conv, v_w_out, v_g_post_mix, v_g_pre_ffn, v_w_up, v_ffn_conv_w, v_ffn_conv_b, v_w_down, v_g_post_ffn):
    w = dict(meta_tokens=meta_tokens, g_pre_mix=g_pre_mix, w_in=w_in, lru_conv_w=lru_conv_w, lru_conv_b=lru_conv_b,
             lru_wa=lru_wa, lru_ba=lru_ba, lru_wx=lru_wx, lru_bx=lru_bx, lru_lambda=lru_lambda, conv_w=conv_w,
             conv_b=conv_b, conv_ln_g=conv_ln_g, conv_ln_b=conv_ln_b, g_out_lru=g_out_lru, g_out_conv=g_out_conv,
             w_out=w_out, g_post_mix=g_post_mix, g_pre_ffn=g_pre_ffn, w_up=w_up, ffn_conv_w=ffn_conv_w,
             ffn_conv_b=ffn_conv_b, w_down=w_down, g_post_ffn=g_post_ffn)
    m = dict(meta_tokens=m_meta_tokens, g_pre_mix=m_g_pre_mix, w_in=m_w_in, lru_conv_w=m_lru_conv_w,
             lru_conv_b=m_lru_conv_b, lru_wa=m_lru_wa, lru_ba=m_lru_ba, lru_wx=m_lru_wx, lru_bx=m_lru_bx,
             lru_lambda=m_lru_lambda, conv_w=m_conv_w, conv_b=m_conv_b, conv_ln_g=m_conv_ln_g, conv_ln_b=m_conv_ln_b,
             g_out_lru=m_g_out_lru, g_out_conv=m_g_out_conv, w_out=m_w_out, g_post_mix=m_g_post_mix,
             g_pre_ffn=m_g_pre_ffn, w_up=m_w_up, ffn_conv_w=m_ffn_conv_w, ffn_conv_b=m_ffn_conv_b, w_down=m_w_down,
             g_post_ffn=m_g_post_ffn)
    v = dict(meta_tokens=v_meta_tokens, g_pre_mix=v_g_pre_mix, w_in=v_w_in, lru_conv_w=v_lru_conv_w,
             lru_conv_b=v_lru_conv_b, lru_wa=v_lru_wa, lru_ba=v_lru_ba, lru_wx=v_lru_wx, lru_bx=v_lru_bx,
             lru_lambda=v_lru_lambda, conv_w=v_conv_w, conv_b=v_conv_b, conv_ln_g=v_conv_ln_g, conv_ln_b=v_conv_ln_b,
             g_out_lru=v_g_out_lru, g_out_conv=v_g_out_conv, w_out=v_w_out, g_post_mix=v_g_post_mix,
             g_pre_ffn=v_g_pre_ffn, w_up=v_w_up, ffn_conv_w=v_ffn_conv_w, ffn_conv_b=v_ffn_conv_b, w_down=v_w_down,
             g_post_ffn=v_g_post_ffn)
    return _step(x, loss_target, w, m, v)
```
